```python
import jax
import jax.numpy as jnp
from jax import lax
import numpy as np

D_MODEL = 2048
BATCH = 1
SEQ = 8192
DEPTH = 1

CHUNK = 64
MEM_LEN = 256
FOX_HEADS = 8
FOX_HEAD_DIM = 128
FOX_WIDTH = FOX_HEADS * FOX_HEAD_DIM
LRU_WIDTH = 1024
LRU_BLOCKS = 8
LRU_BLOCK_DIM = LRU_WIDTH // LRU_BLOCKS
LRU_CONV = 4
LRU_C = 8.0
MEM_HEADS = 4
MEM_HEAD_DIM = 256
MEM_WIDTH = MEM_HEADS * MEM_HEAD_DIM
N_BRANCH = 3
BRANCH_WIDTH = 1024
FFN_HIDDEN = 5632
FFN_CONV = 3
Q_BLOCK = 128
EPS = 1e-6
IN_COLS = 3 * FOX_WIDTH + FOX_HEADS + 2 * LRU_WIDTH + MEM_WIDTH + N_BRANCH * D_MODEL

kernel_name = 'fox_rglru_mem_gated_hybrid'


def rmsnorm(x, g):
    xf = x.astype(jnp.float32)
    y = xf * lax.rsqrt(jnp.mean(xf * xf, axis=-1, keepdims=True) + EPS)
    return (y * g.astype(jnp.float32)).astype(x.dtype)


def causal_depthwise_conv(x, w, b):
    k_width = w.shape[0]
    y = lax.conv_general_dilated(
        x, w[:, None, :].astype(x.dtype), window_strides=(1,),
        padding=[(k_width - 1, 0)], dimension_numbers=('NWC', 'WIO', 'NWC'),
        feature_group_count=x.shape[-1])
    return y + b.astype(x.dtype)


def forgetting_attention(q, k, v, log_f):
    b, h, s, d = q.shape
    n_blocks = s // Q_BLOCK
    c = jnp.cumsum(log_f, axis=-1)
    scale = d ** -0.5
    q_blocks = q.reshape(b, h, n_blocks, Q_BLOCK, d).transpose(2, 0, 1, 3, 4)
    c_blocks = c.reshape(b, h, n_blocks, Q_BLOCK).transpose(2, 0, 1, 3)
    key_pos = jnp.arange(s)

    def block(args):
        q_i, c_i, i = args
        logits = jnp.einsum('bhqd,bhkd->bhqk', q_i, k, preferred_element_type=jnp.float32) * scale
        logits = logits + c_i[..., None] - c[:, :, None, :]
        query_pos = i * Q_BLOCK + jnp.arange(Q_BLOCK)
        mask = key_pos[None, :] <= query_pos[:, None]
        p = jax.nn.softmax(jnp.where(mask, logits, -jnp.inf), axis=-1)
        return jnp.einsum('bhqk,bhkd->bhqd', p.astype(v.dtype), v)

    out = lax.map(block, (q_blocks, c_blocks, jnp.arange(n_blocks)))
    return out.transpose(1, 0, 3, 2, 4).reshape(b, s, h * d)


def rg_lru(x, w_a, b_a, w_x, b_x, lam):
    b, s, w = x.shape
    xb = x.reshape(b, s, LRU_BLOCKS, LRU_BLOCK_DIM)
    r = jax.nn.sigmoid(jnp.einsum('bsni,nij->bsnj', xb, w_a) + b_a).reshape(b, s, w)
    i_gate = jax.nn.sigmoid(jnp.einsum('bsni,nij->bsnj', xb, w_x) + b_x).reshape(b, s, w)
    log_a = LRU_C * r.astype(jnp.float32) * jax.nn.log_sigmoid(lam.astype(jnp.float32))
    a = jnp.exp(log_a)
    u = jnp.sqrt(-jnp.expm1(2.0 * log_a)) * (i_gate * x).astype(jnp.float32)

    def combine(left, right):
        a_l, h_l = left
        a_r, h_r = right
        return a_l * a_r, a_r * h_l + h_r

    _, hs = lax.associative_scan(combine, (a, u), axis=1)
    return hs.astype(x.dtype)


def memory_attention(q, k, v):
    b, s, h, d = q.shape
    logits = jnp.einsum('bshd,bmhd->bhsm', q, k, preferred_element_type=jnp.float32) * (d ** -0.5)
    p = jax.nn.softmax(logits, axis=-1)
    out = jnp.einsum('bhsm,bmhd->bshd', p.astype(v.dtype), v)
    return out.reshape(b, s, h * d)


def hybrid_layer(x, mem, g_mix, w_in, b_f, g_q_fox, g_k_fox, w_lru_conv, b_lru_conv,
                 w_rg_a, b_rg_a, w_rg_x, b_rg_x, lru_lambda, g_mem, w_mem_kv, g_q_mem,
                 g_k_mem, b_gate, w_branch, w_out, g_ffn, w_ffn_up, w_ffn_conv,
                 b_ffn_conv, w_ffn_down):
    b, s, _ = x.shape
    h = rmsnorm(x, g_mix)
    proj = h @ w_in
    cuts = np.cumsum([FOX_WIDTH, FOX_WIDTH, FOX_WIDTH, FOX_HEADS,
                      LRU_WIDTH, LRU_WIDTH, MEM_WIDTH]).tolist()
    fq, fk, fv, ff, lx, lg, mq, gate_logits = jnp.split(proj, cuts, axis=-1)

    q = rmsnorm(fq.reshape(b, s, FOX_HEADS, FOX_HEAD_DIM), g_q_fox).transpose(0, 2, 1, 3)
    k = rmsnorm(fk.reshape(b, s, FOX_HEADS, FOX_HEAD_DIM), g_k_fox).transpose(0, 2, 1, 3)
    v = fv.reshape(b, s, FOX_HEADS, FOX_HEAD_DIM).transpose(0, 2, 1, 3)
    log_f = jax.nn.log_sigmoid(ff.astype(jnp.float32) + b_f.astype(jnp.float32)).transpose(0, 2, 1)
    y_fox = forgetting_attention(q, k, v, log_f)

    xr = causal_depthwise_conv(lx, w_lru_conv, b_lru_conv)
    y_lru = rg_lru(xr, w_rg_a, b_rg_a, w_rg_x, b_rg_x, lru_lambda) * jax.nn.gelu(lg)

    m = mem.shape[1]
    mk, mv = jnp.split(rmsnorm(mem, g_mem) @ w_mem_kv, 2, axis=-1)
    qm = rmsnorm(mq.reshape(b, s, MEM_HEADS, MEM_HEAD_DIM), g_q_mem)
    km = rmsnorm(mk.reshape(b, m, MEM_HEADS, MEM_HEAD_DIM), g_k_mem)
    vm = mv.reshape(b, m, MEM_HEADS, MEM_HEAD_DIM)
    y_mem = memory_attention(qm, km, vm)

    branches = jnp.stack([y_fox, y_lru, y_mem], axis=2)
    y_d = jnp.einsum('bsnc,ncd->bsnd', branches, w_branch)
    gates = jax.nn.sigmoid(gate_logits.reshape(b, s, N_BRANCH, D_MODEL) + b_gate)
    merged = jnp.sum(gates * y_d, axis=2)
    x = x + merged @ w_out

    up = causal_depthwise_conv(rmsnorm(x, g_ffn) @ w_ffn_up, w_ffn_conv, b_ffn_conv)
    u_act, u_val = jnp.split(up, 2, axis=-1)
    return x + (jax.nn.gelu(u_act) * u_val) @ w_ffn_down


def setup_inputs(seed: int = 0) -> dict:
    key = jax.random.key(seed)
    ks = jax.random.split(key, 26)
    L = DEPTH
    f32 = jnp.float32

    def nrm(k, shape, fan_in):
        return jax.random.normal(k, shape, f32) * (fan_in ** -0.5)

    def gain(k, shape):
        return 1.0 + 0.02 * jax.random.normal(k, shape, f32)

    def bias(k, shape):
        return 0.01 * jax.random.normal(k, shape, f32)

    a_c = jax.random.uniform(ks[13], (L, LRU_WIDTH), f32, minval=0.9, maxval=0.999)
    sig = a_c ** (1.0 / LRU_C)
    lru_lambda = jnp.log(sig) - jnp.log1p(-sig)
    return {
        'x': jax.random.normal(ks[0], (BATCH, SEQ, D_MODEL), f32),
        'mem': jax.random.normal(ks[1], (BATCH, MEM_LEN, D_MODEL), f32),
        'g_mix': gain(ks[2], (L, D_MODEL)),
        'w_in': nrm(ks[3], (L, D_MODEL, IN_COLS), D_MODEL),
        'b_f': jax.random.uniform(ks[4], (L, FOX_HEADS), f32, minval=1.0, maxval=6.0),
        'g_q_fox': gain(ks[5], (L, FOX_HEAD_DIM)),
        'g_k_fox': gain(ks[6], (L, FOX_HEAD_DIM)),
        'w_lru_conv': nrm(ks[7], (L, LRU_CONV, LRU_WIDTH), LRU_CONV),
        'b_lru_conv': bias(ks[8], (L, LRU_WIDTH)),
        'w_rg_a': nrm(ks[9], (L, LRU_BLOCKS, LRU_BLOCK_DIM, LRU_BLOCK_DIM), LRU_BLOCK_DIM),
        'b_rg_a': bias(ks[10], (L, LRU_BLOCKS, LRU_BLOCK_DIM)),
        'w_rg_x': nrm(ks[11], (L, LRU_BLOCKS, LRU_BLOCK_DIM, LRU_BLOCK_DIM), LRU_BLOCK_DIM),
        'b_rg_x': bias(ks[12], (L, LRU_BLOCKS, LRU_BLOCK_DIM)),
        'lru_lambda': lru_lambda,
        'g_mem': gain(ks[14], (L, D_MODEL)),
        'w_mem_kv': nrm(ks[15], (L, D_MODEL, 2 * MEM_WIDTH), D_MODEL),
        'g_q_mem': gain(ks[16], (L, MEM_HEAD_DIM)),
        'g_k_mem': gain(ks[17], (L, MEM_HEAD_DIM)),
        'b_gate': bias(ks[18], (L, N_BRANCH, D_MODEL)),
        'w_branch': nrm(ks[19], (L, N_BRANCH, BRANCH_WIDTH, D_MODEL), BRANCH_WIDTH),
        'w_out': nrm(ks[20], (L, D_MODEL, D_MODEL), D_MODEL),
        'g_ffn': gain(ks[21], (L, D_MODEL)),
        'w_ffn_up': nrm(ks[22], (L, D_MODEL, 2 * FFN_HIDDEN), D_MODEL),
        'w_ffn_conv': nrm(ks[23], (L, FFN_CONV, 2 * FFN_HIDDEN), FFN_CONV),
        'b_ffn_conv': bias(ks[24], (L, 2 * FFN_HIDDEN)),
        'w_ffn_down': nrm(ks[25], (L, FFN_HIDDEN, D_MODEL), FFN_HIDDEN),
    }


def reference(x, mem, g_mix, w_in, b_f, g_q_fox, g_k_fox, w_lru_conv, b_lru_conv,
              w_rg_a, b_rg_a, w_rg_x, b_rg_x, lru_lambda, g_mem, w_mem_kv, g_q_mem,
              g_k_mem, b_gate, w_branch, w_out, g_ffn, w_ffn_up, w_ffn_conv,
              b_ffn_conv, w_ffn_down):
    for l in range(DEPTH):
        x = hybrid_layer(
            x, mem, g_mix[l], w_in[l], b_f[l], g_q_fox[l], g_k_fox[l], w_lru_conv[l],
            b_lru_conv[l], w_rg_a[l], b_rg_a[l], w_rg_x[l], b_rg_x[l], lru_lambda[l],
            g_mem[l], w_mem_kv[l], g_q_mem[l], g_k_mem[l], b_gate[l], w_branch[l],
            w_out[l], g_ffn[l], w_ffn_up[l], w_ffn_conv[l], b_ffn_conv[l], w_ffn_down[l])
    return x
```

```python
import functools

import jax
import jax.numpy as jnp
from jax import lax
from jax.experimental import pallas as pl
from jax.experimental.pallas import tpu as pltpu

D_MODEL = 2048
FOX_HEADS = 8
FOX_HEAD_DIM = 128
FOX_WIDTH = FOX_HEADS * FOX_HEAD_DIM
LRU_WIDTH = 1024
LRU_BLOCKS = 8
LRU_BLOCK_DIM = LRU_WIDTH // LRU_BLOCKS
LRU_CONV = 4
LRU_C = 8.0
MEM_HEADS = 4
MEM_HEAD_DIM = 256
MEM_WIDTH = MEM_HEADS * MEM_HEAD_DIM
N_BRANCH = 3
FFN_HIDDEN = 5632
FFN_CONV = 3
EPS = 1e-6

V7X_SUBLANES = 8
V7X_LANES = 128
MASK_VALUE = -1e30

_BF16 = jnp.bfloat16
_F32 = jnp.float32


def _params(semantics, vmem_mib):
    return pltpu.CompilerParams(dimension_semantics=semantics,
                                vmem_limit_bytes=vmem_mib * 1024 * 1024)


def _rmsnorm_kernel(x_ref, g_ref, o_ref):
    x = x_ref[...]
    ms = jnp.mean(x * x, axis=-1, keepdims=True)
    o_ref[...] = (x * lax.rsqrt(ms + EPS) * g_ref[...]).astype(o_ref.dtype)


def _rmsnorm(x, g, tm):
    m, d = x.shape
    return pl.pallas_call(
        _rmsnorm_kernel,
        out_shape=jax.ShapeDtypeStruct((m, d), _BF16),
        grid=(m // tm,),
        in_specs=[pl.BlockSpec((tm, d), lambda i: (i, 0)),
                  pl.BlockSpec((1, d), lambda i: (0, 0))],
        out_specs=pl.BlockSpec((tm, d), lambda i: (i, 0)),
        compiler_params=_params(("parallel",), 32),
        name="rmsnorm",
    )(x, g.reshape(1, d))


def _mm_kernel(*refs, epilogue, group):
    a_ref, w_ref = refs[0], refs[1]
    o_ref = refs[-1]
    acc = jnp.dot(a_ref[...], w_ref[...], preferred_element_type=_F32)
    if epilogue == "gnorm":
        g_ref = refs[2]
        for gi in range(acc.shape[1] // group):
            sl = slice(gi * group, (gi + 1) * group)
            blk = acc[:, sl]
            ms = jnp.mean(blk * blk, axis=-1, keepdims=True)
            o_ref[:, sl] = (blk * lax.rsqrt(ms + EPS) * g_ref[:, sl]).astype(o_ref.dtype)
    elif epilogue == "residual":
        o_ref[...] = (refs[2][...] + acc).astype(o_ref.dtype)
    else:
        o_ref[...] = acc.astype(o_ref.dtype)


def _matmul(a, w, *, tm, tn, out_dtype, epilogue="plain", extra=None, group=None, name):
    m, k = a.shape
    n = w.shape[1]
    in_specs = [pl.BlockSpec((tm, k), lambda i, j: (i, 0)),
                pl.BlockSpec((k, tn), lambda i, j: (0, j))]
    args = [a, w]
    if epilogue == "gnorm":
        in_specs.append(pl.BlockSpec((1, tn), lambda i, j: (0, j)))
        args.append(extra)
    elif epilogue == "residual":
        in_specs.append(pl.BlockSpec((tm, tn), lambda i, j: (i, j)))
        args.append(extra)
    return pl.pallas_call(
        functools.partial(_mm_kernel, epilogue=epilogue, group=group),
        out_shape=jax.ShapeDtypeStruct((m, n), out_dtype),
        grid=(m // tm, n // tn),
        in_specs=in_specs,
        out_specs=pl.BlockSpec((tm, tn), lambda i, j: (i, j)),
        compiler_params=_params(("parallel", "arbitrary"), 56),
        name=name,
    )(*args)


def _forget_cumsum_kernel(h_ref, w_ref, b_ref, c_ref, carry_ref):
    @pl.when(pl.program_id(0) == 0)
    def _():
        carry_ref[...] = jnp.zeros_like(carry_ref)

    tc = h_ref.shape[0]
    z = lax.dot_general(w_ref[...], h_ref[...], (((1,), (1,)), ((), ())),
                        preferred_element_type=_F32) + b_ref[...]
    log_f = jnp.minimum(z, 0.0) - jnp.log1p(jnp.exp(-jnp.abs(z)))
    row = lax.broadcasted_iota(jnp.int32, (tc, tc), 0)
    col = lax.broadcasted_iota(jnp.int32, (tc, tc), 1)
    tri = jnp.where(row <= col, 1.0, 0.0).astype(_BF16)
    hi = log_f.astype(_BF16)
    r1 = log_f - hi.astype(_F32)
    mid = r1.astype(_BF16)
    lo = (r1 - mid.astype(_F32)).astype(_BF16)
    c = (jnp.dot(hi, tri, preferred_element_type=_F32)
         + jnp.dot(mid, tri, preferred_element_type=_F32)
         + jnp.dot(lo, tri, preferred_element_type=_F32)) + carry_ref[:, 0:1]
    c_ref[...] = c
    carry_ref[...] = jnp.broadcast_to(c[:, tc - 1:tc], carry_ref.shape)


def _forget_cumsum(h, w_t, b_f, tc):
    s, d = h.shape
    nh = w_t.shape[0]
    return pl.pallas_call(
        _forget_cumsum_kernel,
        out_shape=jax.ShapeDtypeStruct((nh, s), _F32),
        grid=(s // tc,),
        in_specs=[pl.BlockSpec((tc, d), lambda i: (i, 0)),
                  pl.BlockSpec((nh, d), lambda i: (0, 0)),
                  pl.BlockSpec((nh, 1), lambda i: (0, 0))],
        out_specs=pl.BlockSpec((nh, tc), lambda i: (0, i)),
        scratch_shapes=[pltpu.VMEM((nh, V7X_LANES), _F32)],
        compiler_params=_params(("arbitrary",), 32),
        name="forget_cumsum",
    )(h, w_t, b_f.reshape(nh, 1))


def _fox_kernel(q_ref, k_ref, v_ref, c_ref, o_ref, m_ref, l_ref, acc_ref, *, tq):
    qi = pl.program_id(1)
    q = q_ref[...]
    c0 = c_ref[:, pl.ds(pl.multiple_of(qi * tq, tq), V7X_LANES)][:, 0:1]
    m_ref[...] = jnp.full_like(m_ref, MASK_VALUE)
    l_ref[...] = jnp.zeros_like(l_ref)
    acc_ref[...] = jnp.zeros_like(acc_ref)

    def step(j, masked):
        ks = pl.multiple_of(j * tq, tq)
        k = k_ref[pl.ds(ks, tq), :]
        v = v_ref[pl.ds(ks, tq), :]
        s = lax.dot_general(q, k, (((1,), (1,)), ((), ())), preferred_element_type=_F32)
        s = s + (c0 - c_ref[:, pl.ds(ks, tq)])
        if masked:
            row = lax.broadcasted_iota(jnp.int32, s.shape, 0)
            col = lax.broadcasted_iota(jnp.int32, s.shape, 1)
            s = jnp.where(row >= col, s, MASK_VALUE)
        m_prev = m_ref[...]
        m_new = jnp.maximum(m_prev, jnp.max(s, axis=-1, keepdims=True))
        alpha = jnp.exp(m_prev - m_new)
        p = jnp.exp(s - m_new)
        l_ref[...] = alpha * l_ref[...] + jnp.sum(p, axis=-1, keepdims=True)
        acc_ref[...] = alpha * acc_ref[...] + jnp.dot(p.astype(_BF16), v,
                                                      preferred_element_type=_F32)
        m_ref[...] = m_new

    def body(j, carry):
        step(j, False)
        return carry

    lax.fori_loop(0, qi, body, 0)
    step(qi, True)
    o_ref[...] = (acc_ref[...] / l_ref[...]).astype(o_ref.dtype)


def _fox_attention(qk, v, c, tq):
    s = v.shape[0]
    return pl.pallas_call(
        functools.partial(_fox_kernel, tq=tq),
        out_shape=jax.ShapeDtypeStruct((s, FOX_WIDTH), _BF16),
        grid=(FOX_HEADS, s // tq),
        in_specs=[pl.BlockSpec((tq, FOX_HEAD_DIM), lambda h, i: (i, h)),
                  pl.BlockSpec((s, FOX_HEAD_DIM), lambda h, i: (0, FOX_HEADS + h)),
                  pl.BlockSpec((s, FOX_HEAD_DIM), lambda h, i: (0, h)),
                  pl.BlockSpec((None, 1, s), lambda h, i: (h, 0, 0))],
        out_specs=pl.BlockSpec((tq, FOX_HEAD_DIM), lambda h, i: (i, h)),
        scratch_shapes=[pltpu.VMEM((tq, 1), _F32), pltpu.VMEM((tq, 1), _F32),
                        pltpu.VMEM((tq, FOX_HEAD_DIM), _F32)],
        compiler_params=_params(("parallel", "arbitrary"), 32),
        name="fox_attention",
    )(qk, qk, v, c.reshape(FOX_HEADS, 1, s))


def _shift_rows(ext, d):
    return pltpu.roll(ext, d, axis=0)[V7X_SUBLANES:]


def _lru_kernel(lx_ref, lg_ref, wc_ref, bc_ref, wg_ref, bg_ref, lam_ref, o_ref,
                halo_ref, carry_ref):
    @pl.when(pl.program_id(1) == 0)
    def _():
        halo_ref[...] = jnp.zeros_like(halo_ref)
        carry_ref[...] = jnp.zeros_like(carry_ref)

    lx = lx_ref[...]
    ts = lx.shape[0]
    ext = jnp.concatenate([halo_ref[...], lx], axis=0)
    xr = (wc_ref[3:4, :] * lx + wc_ref[2:3, :] * _shift_rows(ext, 1)
          + wc_ref[1:2, :] * _shift_rows(ext, 2) + wc_ref[0:1, :] * _shift_rows(ext, 3)
          + bc_ref[...])
    halo_ref[...] = lx[ts - V7X_SUBLANES:]

    gates = jax.nn.sigmoid(jnp.dot(xr.astype(_BF16), wg_ref[...],
                                   preferred_element_type=_F32) + bg_ref[...])
    r = gates[:, :LRU_BLOCK_DIM]
    i_gate = gates[:, LRU_BLOCK_DIM:]
    lam = lam_ref[...]
    log_sig_lam = jnp.minimum(lam, 0.0) - jnp.log1p(jnp.exp(-jnp.abs(lam)))
    log_a = LRU_C * r * log_sig_lam
    a = jnp.exp(log_a)
    t = jnp.tanh(log_a)
    h = jnp.sqrt(-2.0 * t / (1.0 - t)) * (i_gate * xr)

    row = lax.broadcasted_iota(jnp.int32, (ts, 1), 0)
    d = 1
    while d < ts:
        valid = row >= d
        h = h + a * jnp.where(valid, pltpu.roll(h, d, axis=0), 0.0)
        a = a * jnp.where(valid, pltpu.roll(a, d, axis=0), 1.0)
        d *= 2
    h = h + a * carry_ref[0:1, :]
    carry_ref[...] = jnp.broadcast_to(h[ts - 1:ts, :], carry_ref.shape)
    o_ref[...] = (h * jax.nn.gelu(lg_ref[...])).astype(o_ref.dtype)


def _lru_branch(lxlg, w_conv, b_conv, w_gate, b_gate, lam, ts):
    s = lxlg.shape[0]
    bd = LRU_BLOCK_DIM
    return pl.pallas_call(
        _lru_kernel,
        out_shape=jax.ShapeDtypeStruct((s, LRU_WIDTH), _BF16),
        grid=(LRU_BLOCKS, s // ts),
        in_specs=[pl.BlockSpec((ts, bd), lambda n, i: (i, n)),
                  pl.BlockSpec((ts, bd), lambda n, i: (i, LRU_BLOCKS + n)),
                  pl.BlockSpec((LRU_CONV, bd), lambda n, i: (0, n)),
                  pl.BlockSpec((1, bd), lambda n, i: (0, n)),
                  pl.BlockSpec((None, bd, 2 * bd), lambda n, i: (n, 0, 0)),
                  pl.BlockSpec((None, 1, 2 * bd), lambda n, i: (n, 0, 0)),
                  pl.BlockSpec((1, bd), lambda n, i: (0, n))],
        out_specs=pl.BlockSpec((ts, bd), lambda n, i: (i, n)),
        scratch_shapes=[pltpu.VMEM((V7X_SUBLANES, bd), _F32),
                        pltpu.VMEM((V7X_SUBLANES, bd), _F32)],
        compiler_params=_params(("parallel", "arbitrary"), 32),
        name="conv_rglru",
    )(lxlg, lxlg, w_conv, b_conv.reshape(1, -1), w_gate, b_gate, lam.reshape(1, -1))


def _mem_attn_kernel(q_ref, k_ref, v_ref, o_ref):
    s = lax.dot_general(q_ref[...], k_ref[...], (((1,), (1,)), ((), ())),
                        preferred_element_type=_F32)
    m = jnp.max(s, axis=-1, keepdims=True)
    p = jnp.exp(s - m)
    l = jnp.sum(p, axis=-1, keepdims=True)
    acc = jnp.dot(p.astype(_BF16), v_ref[...], preferred_element_type=_F32)
    o_ref[...] = (acc / l).astype(o_ref.dtype)


def _mem_attention(q, k, v, ts):
    s = q.shape[0]
    m = k.shape[0]
    hd = MEM_HEAD_DIM
    return pl.pallas_call(
        _mem_attn_kernel,
        out_shape=jax.ShapeDtypeStruct((s, MEM_WIDTH), _BF16),
        grid=(s // ts, MEM_HEADS),
        in_specs=[pl.BlockSpec((ts, hd), lambda i, h: (i, h)),
                  pl.BlockSpec((m, hd), lambda i, h: (0, h)),
                  pl.BlockSpec((m, hd), lambda i, h: (0, h))],
        out_specs=pl.BlockSpec((ts, hd), lambda i, h: (i, h)),
        compiler_params=_params(("parallel", "arbitrary"), 32),
        name="mem_attention",
    )(q, k, v)


def _merge_kernel(h_ref, yf_ref, yl_ref, ym_ref, wg0_ref, wg1_ref, wg2_ref, wb_ref, bg_ref,
                  o_ref):
    h = h_ref[...]
    merged = None
    for n, (y_ref, wg_ref) in enumerate(((yf_ref, wg0_ref), (yl_ref, wg1_ref),
                                         (ym_ref, wg2_ref))):
        gate = jax.nn.sigmoid(jnp.dot(h, wg_ref[...], preferred_element_type=_F32)
                              + bg_ref[n:n + 1, :])
        term = gate * jnp.dot(y_ref[...], wb_ref[n], preferred_element_type=_F32)
        merged = term if merged is None else merged + term
    o_ref[...] = merged.astype(o_ref.dtype)


def _gated_merge(h, y_fox, y_lru, y_mem, w_gate, w_branch, b_gate, tm, tn):
    s, d = h.shape
    nj = d // tn
    bw = y_fox.shape[1]
    y_spec = pl.BlockSpec((tm, bw), lambda i, j: (i, 0))
    return pl.pallas_call(
        _merge_kernel,
        out_shape=jax.ShapeDtypeStruct((s, d), _BF16),
        grid=(s // tm, nj),
        in_specs=[pl.BlockSpec((tm, d), lambda i, j: (i, 0)), y_spec, y_spec, y_spec,
                  pl.BlockSpec((d, tn), lambda i, j: (0, j)),
                  pl.BlockSpec((d, tn), lambda i, j: (0, nj + j)),
                  pl.BlockSpec((d, tn), lambda i, j: (0, 2 * nj + j)),
                  pl.BlockSpec((N_BRANCH, bw, tn), lambda i, j: (0, 0, j)),
                  pl.BlockSpec((N_BRANCH, tn), lambda i, j: (0, j))],
        out_specs=pl.BlockSpec((tm, tn), lambda i, j: (i, j)),
        compiler_params=_params(("parallel", "arbitrary"), 56),
        name="gated_merge",
    )(h, y_fox, y_lru, y_mem, w_gate, w_gate, w_gate, w_branch, b_gate)


def _ffn_up_kernel(a_ref, wa_ref, wv_ref, wca_ref, wcv_ref, bca_ref, bcv_ref, o_ref,
                   halo_a_ref, halo_v_ref):
    @pl.when(pl.program_id(1) == 0)
    def _():
        halo_a_ref[...] = jnp.zeros_like(halo_a_ref)
        halo_v_ref[...] = jnp.zeros_like(halo_v_ref)

    a = a_ref[...]
    tm = a.shape[0]

    def conv(up, halo_ref, wc_ref, bc_ref):
        ext = jnp.concatenate([halo_ref[...], up], axis=0)
        halo_ref[...] = up[tm - V7X_SUBLANES:]
        return (wc_ref[2:3, :] * up + wc_ref[1:2, :] * _shift_rows(ext, 1)
                + wc_ref[0:1, :] * _shift_rows(ext, 2) + bc_ref[...])

    act = conv(jnp.dot(a, wa_ref[...], preferred_element_type=_F32), halo_a_ref, wca_ref,
               bca_ref)
    val = conv(jnp.dot(a, wv_ref[...], preferred_element_type=_F32), halo_v_ref, wcv_ref,
               bcv_ref)
    o_ref[...] = (jax.nn.gelu(act) * val).astype(o_ref.dtype)


def _ffn_up(h2, w_up, w_conv, b_conv, tm, tn):
    s, d = h2.shape
    f = FFN_HIDDEN
    nj = f // tn
    return pl.pallas_call(
        _ffn_up_kernel,
        out_shape=jax.ShapeDtypeStruct((s, f), _BF16),
        grid=(nj, s // tm),
        in_specs=[pl.BlockSpec((tm, d), lambda j, i: (i, 0)),
                  pl.BlockSpec((d, tn), lambda j, i: (0, j)),
                  pl.BlockSpec((d, tn), lambda j, i: (0, nj + j)),
                  pl.BlockSpec((FFN_CONV, tn), lambda j, i: (0, j)),
                  pl.BlockSpec((FFN_CONV, tn), lambda j, i: (0, nj + j)),
                  pl.BlockSpec((1, tn), lambda j, i: (0, j)),
                  pl.BlockSpec((1, tn), lambda j, i: (0, nj + j))],
        out_specs=pl.BlockSpec((tm, tn), lambda j, i: (i, j)),
        scratch_shapes=[pltpu.VMEM((V7X_SUBLANES, tn), _F32),
                        pltpu.VMEM((V7X_SUBLANES, tn), _F32)],
        compiler_params=_params(("parallel", "arbitrary"), 56),
        name="ffn_up_conv_geglu",
    )(h2, w_up, w_up, w_conv, w_conv, b_conv.reshape(1, -1), b_conv.reshape(1, -1))


def _layer(x, mem, g_mix, w_in, b_f, g_q_fox, g_k_fox, w_lru_conv, b_lru_conv, w_rg_a, b_rg_a,
           w_rg_x, b_rg_x, lru_lambda, g_mem, w_mem_kv, g_q_mem, g_k_mem, b_gate, w_branch,
           w_out, g_ffn, w_ffn_up, w_ffn_conv, b_ffn_conv, w_ffn_down):
    s = x.shape[0]
    c_k = 2 * FOX_WIDTH
    c_v = c_k + FOX_WIDTH
    c_f = c_v + FOX_HEADS
    c_l = c_f + 2 * LRU_WIDTH
    c_m = c_l + MEM_WIDTH

    h = _rmsnorm(x, g_mix, 512)

    gain_qk = jnp.concatenate([jnp.tile(g_q_fox, FOX_HEADS) * (FOX_HEAD_DIM ** -0.5),
                               jnp.tile(g_k_fox, FOX_HEADS)]).reshape(1, -1)
    qk = _matmul(h, w_in[:, :c_k].astype(_BF16), tm=1024, tn=512, out_dtype=_BF16,
                 epilogue="gnorm", extra=gain_qk, group=FOX_HEAD_DIM, name="proj_qk")
    v = _matmul(h, w_in[:, c_k:c_v].astype(_BF16), tm=1024, tn=512, out_dtype=_BF16,
                name="proj_v")
    c = _forget_cumsum(h, w_in[:, c_v:c_f].T.astype(_BF16), b_f, 256)
    y_fox = _fox_attention(qk, v, c, 512)

    lxlg = _matmul(h, w_in[:, c_f:c_l].astype(_BF16), tm=1024, tn=512, out_dtype=_F32,
                   name="proj_lru")
    w_gate_lru = jnp.concatenate([w_rg_a, w_rg_x], axis=-1).astype(_BF16)
    b_gate_lru = jnp.concatenate([b_rg_a, b_rg_x], axis=-1).reshape(LRU_BLOCKS, 1, -1)
    y_lru = _lru_branch(lxlg, w_lru_conv, b_lru_conv, w_gate_lru, b_gate_lru, lru_lambda, 512)

    gain_mq = (jnp.tile(g_q_mem, MEM_HEADS) * (MEM_HEAD_DIM ** -0.5)).reshape(1, -1)
    mq = _matmul(h, w_in[:, c_l:c_m].astype(_BF16), tm=1024, tn=512, out_dtype=_BF16,
                 epilogue="gnorm", extra=gain_mq, group=MEM_HEAD_DIM, name="proj_mq")
    hm = _rmsnorm(mem, g_mem, mem.shape[0])
    mk = _matmul(hm, w_mem_kv[:, :MEM_WIDTH].astype(_BF16), tm=mem.shape[0], tn=512,
                 out_dtype=_BF16, epilogue="gnorm",
                 extra=jnp.tile(g_k_mem, MEM_HEADS).reshape(1, -1), group=MEM_HEAD_DIM,
                 name="proj_mk")
    mv = _matmul(hm, w_mem_kv[:, MEM_WIDTH:].astype(_BF16), tm=mem.shape[0], tn=512,
                 out_dtype=_BF16, name="proj_mv")
    y_mem = _mem_attention(mq, mk, mv, 1024)

    merged = _gated_merge(h, y_fox, y_lru, y_mem, w_in[:, c_m:].astype(_BF16),
                          w_branch.astype(_BF16), b_gate, 1024, 256)
    x2 = _matmul(merged, w_out.astype(_BF16), tm=1024, tn=512, out_dtype=_F32,
                 epilogue="residual", extra=x, name="proj_out")

    h2 = _rmsnorm(x2, g_ffn, 512)
    g = _ffn_up(h2, w_ffn_up.astype(_BF16), w_ffn_conv, b_ffn_conv, 1024, 512)
    return _matmul(g, w_ffn_down.astype(_BF16), tm=512, tn=512, out_dtype=_F32,
                   epilogue="residual", extra=x2, name="ffn_down")


def kernel(x, mem, g_mix, w_in, b_f, g_q_fox, g_k_fox, w_lru_conv, b_lru_conv, w_rg_a, b_rg_a,
           w_rg_x, b_rg_x, lru_lambda, g_mem, w_mem_kv, g_q_mem, g_k_mem, b_gate, w_branch,
           w_out, g_ffn, w_ffn_up, w_ffn_conv, b_ffn_conv, w_ffn_down):
    depth = g_mix.shape[0]
    outs = []
    for b in range(x.shape[0]):
        xb = x[b]
        for l in range(depth):
            xb = _layer(xb, mem[b], g_mix[l], w_in[l], b_f[l], g_q_fox[l], g_k_fox[l],
                        w_lru_conv[l], b_lru_conv[l], w_rg_a[l], b_rg_a[l], w_rg_x[l],
                        b_rg_x[l], lru_lambda[l], g_mem[l], w_mem_kv[l], g_q_mem[l],
                        g_k_mem[l], b_gate[l], w_branch[l], w_out[l], g_ffn[l], w_ffn_up[l],
                        w_ffn_conv[l], b_ffn_conv[l], w_ffn_down[l])
        outs.append(xb)
    return outs[0][None] if len(outs) == 1 else jnp.stack(outs)
```

```python
import functools

import jax
import jax.numpy as jnp
from jax import lax
from jax.experimental import pallas as pl
from jax.experimental.pallas import tpu as pltpu

D_MODEL = 2048
FOX_HEADS = 8
FOX_HEAD_DIM = 128
FOX_WIDTH = FOX_HEADS * FOX_HEAD_DIM
LRU_WIDTH = 1024
LRU_BLOCKS = 8
LRU_BLOCK_DIM = LRU_WIDTH // LRU_BLOCKS
LRU_CONV = 4
LRU_C = 8.0
MEM_HEADS = 4
MEM_HEAD_DIM = 256
MEM_WIDTH = MEM_HEADS * MEM_HEAD_DIM
N_BRANCH = 3
FFN_HIDDEN = 5632
FFN_CONV = 3
EPS = 1e-6

V7X_SUBLANES = 8
V7X_LANES = 128
MASK_VALUE = -1e30
LOG2_E = 1.4426950408889634

_BF16 = jnp.bfloat16
_F32 = jnp.float32


def _params(semantics, vmem_mib):
    return pltpu.CompilerParams(dimension_semantics=semantics,
                                vmem_limit_bytes=vmem_mib * 1024 * 1024)


def _rmsnorm_kernel(x_ref, g_ref, o_ref):
    x = x_ref[...]
    ms = jnp.mean(x * x, axis=-1, keepdims=True)
    o_ref[...] = (x * lax.rsqrt(ms + EPS) * g_ref[...]).astype(o_ref.dtype)


def _rmsnorm(x, g, tm):
    m, d = x.shape
    return pl.pallas_call(
        _rmsnorm_kernel,
        out_shape=jax.ShapeDtypeStruct((m, d), _BF16),
        grid=(m // tm,),
        in_specs=[pl.BlockSpec((tm, d), lambda i: (i, 0)),
                  pl.BlockSpec((1, d), lambda i: (0, 0))],
        out_specs=pl.BlockSpec((tm, d), lambda i: (i, 0)),
        compiler_params=_params(("parallel",), 32),
        name="rmsnorm",
    )(x, g.reshape(1, d))


def _mm_kernel(*refs, epilogue, group):
    a_ref, w_ref = refs[0], refs[1]
    o_ref = refs[-1]
    acc = jnp.dot(a_ref[...], w_ref[...], preferred_element_type=_F32)
    if epilogue == "gnorm":
        g_ref = refs[2]
        for gi in range(acc.shape[1] // group):
            sl = slice(gi * group, (gi + 1) * group)
            blk = acc[:, sl]
            ms = jnp.mean(blk * blk, axis=-1, keepdims=True)
            o_ref[:, sl] = (blk * lax.rsqrt(ms + EPS) * g_ref[:, sl]).astype(o_ref.dtype)
    elif epilogue == "residual":
        o_ref[...] = (refs[2][...] + acc).astype(o_ref.dtype)
    else:
        o_ref[...] = acc.astype(o_ref.dtype)


def _matmul(a, w, *, tm, tn, out_dtype, epilogue="plain", extra=None, group=None, name):
    m, k = a.shape
    n = w.shape[1]
    in_specs = [pl.BlockSpec((tm, k), lambda i, j: (i, 0)),
                pl.BlockSpec((k, tn), lambda i, j: (0, j))]
    args = [a, w]
    if epilogue == "gnorm":
        in_specs.append(pl.BlockSpec((1, tn), lambda i, j: (0, j)))
        args.append(extra)
    elif epilogue == "residual":
        in_specs.append(pl.BlockSpec((tm, tn), lambda i, j: (i, j)))
        args.append(extra)
    return pl.pallas_call(
        functools.partial(_mm_kernel, epilogue=epilogue, group=group),
        out_shape=jax.ShapeDtypeStruct((m, n), out_dtype),
        grid=(m // tm, n // tn),
        in_specs=in_specs,
        out_specs=pl.BlockSpec((tm, tn), lambda i, j: (i, j)),
        compiler_params=_params(("parallel", "arbitrary"), 56),
        name=name,
    )(*args)


def _split3(x):
    hi = x.astype(_BF16)
    r1 = x - hi.astype(_F32)
    mid = r1.astype(_BF16)
    lo = (r1 - mid.astype(_F32)).astype(_BF16)
    return hi, mid, lo


def _forget_bias_kernel(h_ref, w_ref, b_ref, sel_ref, o_ref, carry_ref):
    @pl.when(pl.program_id(0) == 0)
    def _():
        carry_ref[...] = jnp.zeros_like(carry_ref)

    tc = h_ref.shape[0]
    z = jnp.dot(h_ref[...], w_ref[...], preferred_element_type=_F32) + b_ref[...]
    neg_log_f = (jnp.log1p(jnp.exp(-jnp.abs(z))) - jnp.minimum(z, 0.0)) * LOG2_E
    row = lax.broadcasted_iota(jnp.int32, (tc, tc), 0)
    col = lax.broadcasted_iota(jnp.int32, (tc, tc), 1)
    tri = jnp.where(col <= row, 1.0, 0.0).astype(_BF16)
    c = carry_ref[0:1, :]
    for part in _split3(neg_log_f):
        c = c + jnp.dot(tri, part, preferred_element_type=_F32)
    carry_ref[...] = jnp.broadcast_to(c[tc - 1:tc, :], carry_ref.shape)
    parts = jnp.concatenate(_split3(c), axis=1)
    o_ref[...] = jnp.dot(parts, sel_ref[...], preferred_element_type=_F32).astype(o_ref.dtype)


def _forget_bias(h, w_pad, b_pad, tc):
    s, d = h.shape
    lanes = V7X_LANES
    part = jnp.arange(3)[:, None]
    head = jnp.arange(FOX_HEADS)[None, :]
    sel = jnp.zeros((3 * lanes, FOX_HEADS * lanes), _BF16).at[
        (part * lanes + head).ravel(), (head * lanes + part).ravel()].set(1.0)
    return pl.pallas_call(
        _forget_bias_kernel,
        out_shape=jax.ShapeDtypeStruct((s, FOX_HEADS * lanes), _BF16),
        grid=(s // tc,),
        in_specs=[pl.BlockSpec((tc, d), lambda i: (i, 0)),
                  pl.BlockSpec((d, lanes), lambda i: (0, 0)),
                  pl.BlockSpec((1, lanes), lambda i: (0, 0)),
                  pl.BlockSpec((3 * lanes, FOX_HEADS * lanes), lambda i: (0, 0))],
        out_specs=pl.BlockSpec((tc, FOX_HEADS * lanes), lambda i: (i, 0)),
        scratch_shapes=[pltpu.VMEM((V7X_SUBLANES, lanes), _F32)],
        compiler_params=_params(("arbitrary",), 32),
        name="forget_bias",
    )(h, w_pad, b_pad, sel)


def _fox_kernel(q_ref, k_ref, kb_ref, v_ref, o_ref, acc_ref, *, tq):
    qi = pl.program_id(1)
    lane = lax.broadcasted_iota(jnp.int32, (tq, V7X_LANES), 1)
    ones3 = jnp.where(lane < 3, 1.0, 0.0).astype(_BF16)
    q_aug = jnp.concatenate([q_ref[...], ones3], axis=1)
    acc_ref[...] = jnp.zeros_like(acc_ref)

    def logits(j):
        ks = pl.multiple_of(j * tq, tq)
        k_aug = jnp.concatenate([k_ref[pl.ds(ks, tq), :], kb_ref[pl.ds(ks, tq), :]], axis=1)
        return lax.dot_general(k_aug, q_aug, (((1,), (1,)), ((), ())),
                               preferred_element_type=_F32)

    def accumulate(j, s, m_prev, l_prev):
        m_new = jnp.maximum(m_prev, jnp.max(s, axis=0, keepdims=True))
        alpha = jnp.exp2(m_prev - m_new)
        p = jnp.exp2(s - m_new)
        l_new = alpha * l_prev + jnp.sum(p, axis=0, keepdims=True)
        ks = pl.multiple_of(j * tq, tq)
        pv = lax.dot_general(v_ref[pl.ds(ks, tq), :], p.astype(_BF16),
                             (((0,), (0,)), ((), ())), preferred_element_type=_F32)
        acc_ref[...] = alpha * acc_ref[...] + pv
        return m_new, l_new

    def body(j, carry):
        m_prev, l_prev, s = carry
        s_next = logits(j + 1)
        m_new, l_new = accumulate(j, s, m_prev, l_prev)
        return m_new, l_new, s_next

    m0 = jnp.full((1, tq), MASK_VALUE, _F32)
    l0 = jnp.zeros((1, tq), _F32)
    m, l, s = lax.fori_loop(0, qi, body, (m0, l0, logits(0)))
    row = lax.broadcasted_iota(jnp.int32, s.shape, 0)
    col = lax.broadcasted_iota(jnp.int32, s.shape, 1)
    m, l = accumulate(qi, jnp.where(row <= col, s, MASK_VALUE), m, l)
    o_ref[...] = (acc_ref[...] / l).T.astype(o_ref.dtype)


def _fox_attention(qk, kb, v, tq):
    s = v.shape[0]
    hd = FOX_HEAD_DIM
    return pl.pallas_call(
        functools.partial(_fox_kernel, tq=tq),
        out_shape=jax.ShapeDtypeStruct((s, FOX_WIDTH), _BF16),
        grid=(FOX_HEADS, s // tq),
        in_specs=[pl.BlockSpec((tq, hd), lambda h, i: (i, h)),
                  pl.BlockSpec((s, hd), lambda h, i: (0, FOX_HEADS + h)),
                  pl.BlockSpec((s, V7X_LANES), lambda h, i: (0, h)),
                  pl.BlockSpec((s, hd), lambda h, i: (0, h))],
        out_specs=pl.BlockSpec((tq, hd), lambda h, i: (i, h)),
        scratch_shapes=[pltpu.VMEM((hd, tq), _F32)],
        compiler_params=_params(("parallel", "arbitrary"), 32),
        name="fox_attention",
    )(qk, qk, kb, v)


def _shift_rows(ext, d):
    return pltpu.roll(ext, d, axis=0)[V7X_SUBLANES:]


def _lru_kernel(lx_ref, lg_ref, wc_ref, bc_ref, wg_ref, bg_ref, lam_ref, o_ref,
                halo_ref, carry_ref):
    @pl.when(pl.program_id(1) == 0)
    def _():
        halo_ref[...] = jnp.zeros_like(halo_ref)
        carry_ref[...] = jnp.zeros_like(carry_ref)

    lx = lx_ref[...]
    ts = lx.shape[0]
    ext = jnp.concatenate([halo_ref[...], lx], axis=0)
    xr = (wc_ref[3:4, :] * lx + wc_ref[2:3, :] * _shift_rows(ext, 1)
          + wc_ref[1:2, :] * _shift_rows(ext, 2) + wc_ref[0:1, :] * _shift_rows(ext, 3)
          + bc_ref[...])
    halo_ref[...] = lx[ts - V7X_SUBLANES:]

    gates = jax.nn.sigmoid(jnp.dot(xr.astype(_BF16), wg_ref[...],
                                   preferred_element_type=_F32) + bg_ref[...])
    r = gates[:, :LRU_BLOCK_DIM]
    i_gate = gates[:, LRU_BLOCK_DIM:]
    lam = lam_ref[...]
    log_sig_lam = jnp.minimum(lam, 0.0) - jnp.log1p(jnp.exp(-jnp.abs(lam)))
    log_a = LRU_C * r * log_sig_lam
    a = jnp.exp(log_a)
    t = jnp.tanh(log_a)
    h = jnp.sqrt(-2.0 * t / (1.0 - t)) * (i_gate * xr)

    row = lax.broadcasted_iota(jnp.int32, (ts, 1), 0)
    d = 1
    while d < ts:
        valid = row >= d
        h = h + a * jnp.where(valid, pltpu.roll(h, d, axis=0), 0.0)
        a = a * jnp.where(valid, pltpu.roll(a, d, axis=0), 1.0)
        d *= 2
    h = h + a * carry_ref[0:1, :]
    carry_ref[...] = jnp.broadcast_to(h[ts - 1:ts, :], carry_ref.shape)
    o_ref[...] = (h * jax.nn.gelu(lg_ref[...])).astype(o_ref.dtype)


def _lru_branch(lxlg, w_conv, b_conv, w_gate, b_gate, lam, ts):
    s = lxlg.shape[0]
    bd = LRU_BLOCK_DIM
    return pl.pallas_call(
        _lru_kernel,
        out_shape=jax.ShapeDtypeStruct((s, LRU_WIDTH), _BF16),
        grid=(LRU_BLOCKS, s // ts),
        in_specs=[pl.BlockSpec((ts, bd), lambda n, i: (i, n)),
                  pl.BlockSpec((ts, bd), lambda n, i: (i, LRU_BLOCKS + n)),
                  pl.BlockSpec((LRU_CONV, bd), lambda n, i: (0, n)),
                  pl.BlockSpec((1, bd), lambda n, i: (0, n)),
                  pl.BlockSpec((None, bd, 2 * bd), lambda n, i: (n, 0, 0)),
                  pl.BlockSpec((None, 1, 2 * bd), lambda n, i: (n, 0, 0)),
                  pl.BlockSpec((1, bd), lambda n, i: (0, n))],
        out_specs=pl.BlockSpec((ts, bd), lambda n, i: (i, n)),
        scratch_shapes=[pltpu.VMEM((V7X_SUBLANES, bd), _F32),
                        pltpu.VMEM((V7X_SUBLANES, bd), _F32)],
        compiler_params=_params(("parallel", "arbitrary"), 32),
        name="conv_rglru",
    )(lxlg, lxlg, w_conv, b_conv.reshape(1, -1), w_gate, b_gate, lam.reshape(1, -1))


def _mem_attn_kernel(q_ref, k_ref, v_ref, o_ref):
    s = lax.dot_general(q_ref[...], k_ref[...], (((1,), (1,)), ((), ())),
                        preferred_element_type=_F32)
    m = jnp.max(s, axis=-1, keepdims=True)
    p = jnp.exp(s - m)
    l = jnp.sum(p, axis=-1, keepdims=True)
    acc = jnp.dot(p.astype(_BF16), v_ref[...], preferred_element_type=_F32)
    o_ref[...] = (acc / l).astype(o_ref.dtype)


def _mem_attention(q, k, v, ts):
    s = q.shape[0]
    m = k.shape[0]
    hd = MEM_HEAD_DIM
    return pl.pallas_call(
        _mem_attn_kernel,
        out_shape=jax.ShapeDtypeStruct((s, MEM_WIDTH), _BF16),
        grid=(s // ts, MEM_HEADS),
        in_specs=[pl.BlockSpec((ts, hd), lambda i, h: (i, h)),
                  pl.BlockSpec((m, hd), lambda i, h: (0, h)),
                  pl.BlockSpec((m, hd), lambda i, h: (0, h))],
        out_specs=pl.BlockSpec((ts, hd), lambda i, h: (i, h)),
        compiler_params=_params(("parallel", "arbitrary"), 32),
        name="mem_attention",
    )(q, k, v)


def _merge_kernel(h_ref, yf_ref, yl_ref, ym_ref, wg0_ref, wg1_ref, wg2_ref, wb_ref, bg_ref,
                  o_ref):
    h = h_ref[...]
    merged = None
    for n, (y_ref, wg_ref) in enumerate(((yf_ref, wg0_ref), (yl_ref, wg1_ref),
                                         (ym_ref, wg2_ref))):
        gate = jax.nn.sigmoid(jnp.dot(h, wg_ref[...], preferred_element_type=_F32)
                              + bg_ref[n:n + 1, :])
        term = gate * jnp.dot(y_ref[...], wb_ref[n], preferred_element_type=_F32)
        merged = term if merged is None else merged + term
    o_ref[...] = merged.astype(o_ref.dtype)


def _gated_merge(h, y_fox, y_lru, y_mem, w_gate, w_branch, b_gate, tm, tn):
    s, d = h.shape
    nj = d // tn
    bw = y_fox.shape[1]
    y_spec = pl.BlockSpec((tm, bw), lambda i, j: (i, 0))
    return pl.pallas_call(
        _merge_kernel,
        out_shape=jax.ShapeDtypeStruct((s, d), _BF16),
        grid=(s // tm, nj),
        in_specs=[pl.BlockSpec((tm, d), lambda i, j: (i, 0)), y_spec, y_spec, y_spec,
                  pl.BlockSpec((d, tn), lambda i, j: (0, j)),
                  pl.BlockSpec((d, tn), lambda i, j: (0, nj + j)),
                  pl.BlockSpec((d, tn), lambda i, j: (0, 2 * nj + j)),
                  pl.BlockSpec((N_BRANCH, bw, tn), lambda i, j: (0, 0, j)),
                  pl.BlockSpec((N_BRANCH, tn), lambda i, j: (0, j))],
        out_specs=pl.BlockSpec((tm, tn), lambda i, j: (i, j)),
        compiler_params=_params(("parallel", "arbitrary"), 56),
        name="gated_merge",
    )(h, y_fox, y_lru, y_mem, w_gate, w_gate, w_gate, w_branch, b_gate)


def _ffn_up_kernel(a_ref, wa_ref, wv_ref, wca_ref, wcv_ref, bca_ref, bcv_ref, o_ref,
                   halo_a_ref, halo_v_ref):
    @pl.when(pl.program_id(1) == 0)
    def _():
        halo_a_ref[...] = jnp.zeros_like(halo_a_ref)
        halo_v_ref[...] = jnp.zeros_like(halo_v_ref)

    a = a_ref[...]
    tm = a.shape[0]

    def conv(up, halo_ref, wc_ref, bc_ref):
        ext = jnp.concatenate([halo_ref[...], up], axis=0)
        halo_ref[...] = up[tm - V7X_SUBLANES:]
        return (wc_ref[2:3, :] * up + wc_ref[1:2, :] * _shift_rows(ext, 1)
                + wc_ref[0:1, :] * _shift_rows(ext, 2) + bc_ref[...])

    act = conv(jnp.dot(a, wa_ref[...], preferred_element_type=_F32), halo_a_ref, wca_ref,
               bca_ref)
    val = conv(jnp.dot(a, wv_ref[...], preferred_element_type=_F32), halo_v_ref, wcv_ref,
               bcv_ref)
    o_ref[...] = (jax.nn.gelu(act) * val).astype(o_ref.dtype)


def _ffn_up(h2, w_up, w_conv, b_conv, tm, tn):
    s, d = h2.shape
    f = FFN_HIDDEN
    nj = f // tn
    return pl.pallas_call(
        _ffn_up_kernel,
        out_shape=jax.ShapeDtypeStruct((s, f), _BF16),
        grid=(nj, s // tm),
        in_specs=[pl.BlockSpec((tm, d), lambda j, i: (i, 0)),
                  pl.BlockSpec((d, tn), lambda j, i: (0, j)),
                  pl.BlockSpec((d, tn), lambda j, i: (0, nj + j)),
                  pl.BlockSpec((FFN_CONV, tn), lambda j, i: (0, j)),
                  pl.BlockSpec((FFN_CONV, tn), lambda j, i: (0, nj + j)),
                  pl.BlockSpec((1, tn), lambda j, i: (0, j)),
                  pl.BlockSpec((1, tn), lambda j, i: (0, nj + j))],
        out_specs=pl.BlockSpec((tm, tn), lambda j, i: (i, j)),
        scratch_shapes=[pltpu.VMEM((V7X_SUBLANES, tn), _F32),
                        pltpu.VMEM((V7X_SUBLANES, tn), _F32)],
        compiler_params=_params(("parallel", "arbitrary"), 56),
        name="ffn_up_conv_geglu",
    )(h2, w_up, w_up, w_conv, w_conv, b_conv.reshape(1, -1), b_conv.reshape(1, -1))


def _layer(x, mem, g_mix, w_in, b_f, g_q_fox, g_k_fox, w_lru_conv, b_lru_conv, w_rg_a, b_rg_a,
           w_rg_x, b_rg_x, lru_lambda, g_mem, w_mem_kv, g_q_mem, g_k_mem, b_gate, w_branch,
           w_out, g_ffn, w_ffn_up, w_ffn_conv, b_ffn_conv, w_ffn_down):
    s = x.shape[0]
    c_k = 2 * FOX_WIDTH
    c_v = c_k + FOX_WIDTH
    c_f = c_v + FOX_HEADS
    c_l = c_f + 2 * LRU_WIDTH
    c_m = c_l + MEM_WIDTH

    h = _rmsnorm(x, g_mix, 512)

    gain_qk = jnp.concatenate([jnp.tile(g_q_fox, FOX_HEADS) * (LOG2_E * FOX_HEAD_DIM ** -0.5),
                               jnp.tile(g_k_fox, FOX_HEADS)]).reshape(1, -1)
    qk = _matmul(h, w_in[:, :c_k].astype(_BF16), tm=1024, tn=512, out_dtype=_BF16,
                 epilogue="gnorm", extra=gain_qk, group=FOX_HEAD_DIM, name="proj_qk")
    v = _matmul(h, w_in[:, c_k:c_v].astype(_BF16), tm=1024, tn=512, out_dtype=_BF16,
                name="proj_v")
    pad = ((0, 0), (0, V7X_LANES - FOX_HEADS))
    kb = _forget_bias(h, jnp.pad(w_in[:, c_v:c_f], pad).astype(_BF16),
                      jnp.pad(b_f.reshape(1, -1), pad), 256)
    y_fox = _fox_attention(qk, kb, v, 512)

    lxlg = _matmul(h, w_in[:, c_f:c_l].astype(_BF16), tm=1024, tn=512, out_dtype=_F32,
                   name="proj_lru")
    w_gate_lru = jnp.concatenate([w_rg_a, w_rg_x], axis=-1).astype(_BF16)
    b_gate_lru = jnp.concatenate([b_rg_a, b_rg_x], axis=-1).reshape(LRU_BLOCKS, 1, -1)
    y_lru = _lru_branch(lxlg, w_lru_conv, b_lru_conv, w_gate_lru, b_gate_lru, lru_lambda, 512)

    gain_mq = (jnp.tile(g_q_mem, MEM_HEADS) * (MEM_HEAD_DIM ** -0.5)).reshape(1, -1)
    mq = _matmul(h, w_in[:, c_l:c_m].astype(_BF16), tm=1024, tn=512, out_dtype=_BF16,
                 epilogue="gnorm", extra=gain_mq, group=MEM_HEAD_DIM, name="proj_mq")
    hm = _rmsnorm(mem, g_mem, mem.shape[0])
    mk = _matmul(hm, w_mem_kv[:, :MEM_WIDTH].astype(_BF16), tm=mem.shape[0], tn=512,
                 out_dtype=_BF16, epilogue="gnorm",
                 extra=jnp.tile(g_k_mem, MEM_HEADS).reshape(1, -1), group=MEM_HEAD_DIM,
                 name="proj_mk")
    mv = _matmul(hm, w_mem_kv[:, MEM_WIDTH:].astype(_BF16), tm=mem.shape[0], tn=512,
                 out_dtype=_BF16, name="proj_mv")
    y_mem = _mem_attention(mq, mk, mv, 1024)

    merged = _gated_merge(h, y_fox, y_lru, y_mem, w_in[:, c_m:].astype(_BF16),
                          w_branch.astype(_BF16), b_gate, 1024, 256)
    x2 = _matmul(merged, w_out.astype(_BF16), tm=1024, tn=512, out_dtype=_F32,
                 epilogue="residual", extra=x, name="proj_out")

    h2 = _rmsnorm(x2, g_ffn, 512)
    g = _ffn_up(h2, w_ffn_up.astype(_BF16), w_ffn_conv, b_ffn_conv, 1024, 512)
    return _matmul(g, w_ffn_down.astype(_BF16), tm=512, tn=512, out_dtype=_F32,
                   epilogue="residual", extra=x2, name="ffn_down")


def kernel(x, mem, g_mix, w_in, b_f, g_q_fox, g_k_fox, w_lru_conv, b_lru_conv, w_rg_a, b_rg_a,
           w_rg_x, b_rg_x, lru_lambda, g_mem, w_mem_kv, g_q_mem, g_k_mem, b_gate, w_branch,
           w_out, g_ffn, w_ffn_up, w_ffn_conv, b_ffn_conv, w_ffn_down):
    depth = g_mix.shape[0]
    outs = []
    for b in range(x.shape[0]):
        xb = x[b]
        for l in range(depth):
            xb = _layer(xb, mem[b], g_mix[l], w_in[l], b_f[l], g_q_fox[l], g_k_fox[l],
                        w_lru_conv[l], b_lru_conv[l], w_rg_a[l], b_rg_a[l], w_rg_x[l],
                        b_rg_x[l], lru_lambda[l], g_mem[l], w_mem_kv[l], g_q_mem[l],
                        g_k_mem[l], b_gate[l], w_branch[l], w_out[l], g_ffn[l], w_ffn_up[l],
                        w_ffn_conv[l], b_ffn_conv[l], w_ffn_down[l])
        outs.append(xb)
    return outs[0][None] if len(outs) == 1 else jnp.stack(outs)
```

```python
import functools

import jax
import jax.numpy as jnp
import numpy as np
from jax import lax
from jax.experimental import pallas as pl
from jax.experimental.pallas import tpu as pltpu

D_MODEL = 2048
FOX_HEADS = 8
FOX_HEAD_DIM = 128
FOX_WIDTH = FOX_HEADS * FOX_HEAD_DIM
LRU_WIDTH = 1024
LRU_BLOCKS = 8
LRU_BLOCK_DIM = LRU_WIDTH // LRU_BLOCKS
LRU_CONV = 4
LRU_C = 8.0
MEM_HEADS = 4
MEM_HEAD_DIM = 256
MEM_WIDTH = MEM_HEADS * MEM_HEAD_DIM
N_BRANCH = 3
FFN_HIDDEN = 5632
FFN_CONV = 3
EPS = 1e-6

V7X_SUBLANES = 8
V7X_LANES = 128
MASK_VALUE = -1e30
LOG2_E = 1.4426950408889634
FOX_TILE = 512
FOX_SKIP_BITS = 64.0

_BF16 = jnp.bfloat16
_F32 = jnp.float32


def _params(semantics, vmem_mib):
    return pltpu.CompilerParams(dimension_semantics=semantics,
                                vmem_limit_bytes=vmem_mib * 1024 * 1024)


def _rmsnorm_kernel(x_ref, g_ref, o_ref):
    x = x_ref[...]
    ms = jnp.mean(x * x, axis=-1, keepdims=True)
    o_ref[...] = (x * lax.rsqrt(ms + EPS) * g_ref[...]).astype(o_ref.dtype)


def _rmsnorm(x, g, tm):
    m, d = x.shape
    return pl.pallas_call(
        _rmsnorm_kernel,
        out_shape=jax.ShapeDtypeStruct((m, d), _BF16),
        grid=(m // tm,),
        in_specs=[pl.BlockSpec((tm, d), lambda i: (i, 0)),
                  pl.BlockSpec((1, d), lambda i: (0, 0))],
        out_specs=pl.BlockSpec((tm, d), lambda i: (i, 0)),
        compiler_params=_params(("parallel",), 32),
        name="rmsnorm",
    )(x, g.reshape(1, d))


def _dot_nt(a, w_t):
    return lax.dot_general(a, w_t, (((1,), (1,)), ((), ())), preferred_element_type=_F32)


def _mm_kernel(*refs, epilogue, group, cast_w, w_t):
    a_ref, w_ref = refs[0], refs[1]
    if cast_w:
        o_ref, wb_ref = refs[-2], refs[-1]

        @pl.when(pl.program_id(1) == 0)
        def _():
            wb_ref[...] = w_ref[...].astype(_BF16)

        w = wb_ref[...]
    else:
        o_ref = refs[-1]
        w = w_ref[...]
    if w_t:
        acc = _dot_nt(a_ref[...], w)
    else:
        acc = jnp.dot(a_ref[...], w, preferred_element_type=_F32)
    if epilogue == "gnorm":
        g_ref = refs[2]
        for gi in range(acc.shape[1] // group):
            sl = slice(gi * group, (gi + 1) * group)
            blk = acc[:, sl]
            ms = jnp.mean(blk * blk, axis=-1, keepdims=True)
            o_ref[:, sl] = (blk * lax.rsqrt(ms + EPS) * g_ref[:, sl]).astype(o_ref.dtype)
    elif epilogue == "residual":
        o_ref[...] = (refs[2][...] + acc).astype(o_ref.dtype)
    else:
        o_ref[...] = acc.astype(o_ref.dtype)


def _matmul(a, w, *, n, tm, tn, out_dtype, w_t=False, w_off=0, rows_outer=False,
            epilogue="plain", extra=None, group=None, name):
    m, k = a.shape
    cast_w = w.dtype != _BF16
    assert not (cast_w and rows_outer) and n % tn == 0 and m % tm == 0
    if rows_outer:
        grid = (m // tm, n // tn)
        ij = lambda i, j: (i, j)
    else:
        grid = (n // tn, m // tm)
        ij = lambda j, i: (i, j)
    if w_t:
        assert w_off % V7X_SUBLANES == 0
        w_block = (tn, k)
        w_spec = pl.BlockSpec((pl.Element(tn), pl.Element(k)),
                              lambda *g: (pl.multiple_of(w_off + ij(*g)[1] * tn, V7X_SUBLANES), 0))
    else:
        assert w_off % tn == 0
        w_block = (k, tn)
        w_spec = pl.BlockSpec(w_block, lambda *g: (0, w_off // tn + ij(*g)[1]))
    in_specs = [pl.BlockSpec((tm, k), lambda *g: (ij(*g)[0], 0)), w_spec]
    args = [a, w]
    block_bytes = (tm * k * 2 + k * tn * w.dtype.itemsize
                   + tm * tn * jnp.dtype(out_dtype).itemsize)
    if epilogue == "gnorm":
        in_specs.append(pl.BlockSpec((1, tn), lambda *g: (0, ij(*g)[1])))
        args.append(extra)
    elif epilogue == "residual":
        in_specs.append(pl.BlockSpec((tm, tn), lambda *g: ij(*g)))
        args.append(extra)
        block_bytes += tm * tn * extra.dtype.itemsize
    scratch = [pltpu.VMEM(w_block, _BF16)] if cast_w else []
    vmem_mib = -(-(2 * block_bytes + cast_w * k * tn * 2 + 2 * tm * tn * 4) // 2 ** 20) + 2
    return pl.pallas_call(
        functools.partial(_mm_kernel, epilogue=epilogue, group=group, cast_w=cast_w, w_t=w_t),
        out_shape=jax.ShapeDtypeStruct((m, n), out_dtype),
        grid=grid,
        in_specs=in_specs,
        out_specs=pl.BlockSpec((tm, tn), lambda *g: ij(*g)),
        scratch_shapes=scratch,
        compiler_params=_params(("parallel", "arbitrary"), vmem_mib),
        name=name,
    )(*args)


def _split3(x):
    hi = x.astype(_BF16)
    r1 = x - hi.astype(_F32)
    mid = r1.astype(_BF16)
    lo = (r1 - mid.astype(_F32)).astype(_BF16)
    return hi, mid, lo


def _forget_bias_kernel(h_ref, w_ref, b_ref, sel_ref, o_ref, edge_ref, carry_ref):
    @pl.when(pl.program_id(0) == 0)
    def _():
        carry_ref[...] = jnp.zeros_like(carry_ref)

    tc = h_ref.shape[0]
    w = jnp.concatenate([w_ref[...], jnp.zeros((V7X_LANES - FOX_HEADS, w_ref.shape[1]), _F32)],
                        axis=0).astype(_BF16)
    z = _dot_nt(h_ref[...], w) + b_ref[...]
    neg_log_f = (jnp.log1p(jnp.exp(-jnp.abs(z))) - jnp.minimum(z, 0.0)) * LOG2_E
    row = lax.broadcasted_iota(jnp.int32, (tc, tc), 0)
    col = lax.broadcasted_iota(jnp.int32, (tc, tc), 1)
    tri = jnp.where(col <= row, 1.0, 0.0).astype(_BF16)
    c = carry_ref[0:1, :]
    for part in _split3(neg_log_f):
        c = c + jnp.dot(tri, part, preferred_element_type=_F32)
    carry_ref[...] = jnp.broadcast_to(c[tc - 1:tc, :], carry_ref.shape)
    edge_ref[...] = jnp.concatenate(
        [c[0:1, :], c[tc - 1:tc, :], jnp.zeros((V7X_SUBLANES - 2, c.shape[1]), _F32)], axis=0)
    parts = jnp.concatenate(_split3(c), axis=1)
    o_ref[...] = jnp.dot(parts, sel_ref[...], preferred_element_type=_F32).astype(o_ref.dtype)


def _forget_bias(h, w_t, w_row0, b_pad, tc):
    s, d = h.shape
    lanes = V7X_LANES
    nt = s // tc
    sel = np.zeros((3 * lanes, FOX_HEADS * lanes), np.float32)
    for part in range(3):
        for head in range(FOX_HEADS):
            sel[part * lanes + head, head * lanes + part] = 1.0
    sel = jnp.asarray(sel, _BF16)
    kb, edges = pl.pallas_call(
        _forget_bias_kernel,
        out_shape=(jax.ShapeDtypeStruct((s, FOX_HEADS * lanes), _BF16),
                   jax.ShapeDtypeStruct((nt * V7X_SUBLANES, lanes), _F32)),
        grid=(nt,),
        in_specs=[pl.BlockSpec((tc, d), lambda i: (i, 0)),
                  pl.BlockSpec((pl.Element(FOX_HEADS), pl.Element(d)), lambda i: (w_row0, 0)),
                  pl.BlockSpec((1, lanes), lambda i: (0, 0)),
                  pl.BlockSpec((3 * lanes, FOX_HEADS * lanes), lambda i: (0, 0))],
        out_specs=(pl.BlockSpec((tc, FOX_HEADS * lanes), lambda i: (i, 0)),
                   pl.BlockSpec((V7X_SUBLANES, lanes), lambda i: (i, 0))),
        scratch_shapes=[pltpu.VMEM((V7X_SUBLANES, lanes), _F32)],
        compiler_params=_params(("arbitrary",), 32),
        name="forget_bias",
    )(h, w_t, b_pad, sel)
    edges = edges.reshape(nt, V7X_SUBLANES, lanes)
    return kb, edges[:, 0, :FOX_HEADS], edges[:, 1, :FOX_HEADS]


def _fox_kernel(bstart_ref, bend_ref, thr_ref, q_ref, k_ref, kb_ref, v_ref, o_ref, acc_ref, *,
                tq):
    head = pl.program_id(0)
    qi = pl.program_id(1)
    b_tile = bstart_ref[qi, head]
    thr = thr_ref[0]
    j0 = lax.fori_loop(
        0, qi, lambda j, n: n + jnp.where(b_tile - bend_ref[j, head] >= thr, 1, 0), 0)
    lane = lax.broadcasted_iota(jnp.int32, (tq, V7X_LANES), 1)
    ones3 = jnp.where(lane < 3, 1.0, 0.0).astype(_BF16)
    q_aug = jnp.concatenate([q_ref[...], ones3], axis=1)
    acc_ref[...] = jnp.zeros_like(acc_ref)

    def logits(j):
        ks = pl.multiple_of(j * tq, tq)
        k_aug = jnp.concatenate([k_ref[pl.ds(ks, tq), :], kb_ref[pl.ds(ks, tq), :]], axis=1)
        return lax.dot_general(k_aug, q_aug, (((1,), (1,)), ((), ())),
                               preferred_element_type=_F32)

    def accumulate(j, s, m_prev, l_prev):
        m_new = jnp.maximum(m_prev, jnp.max(s, axis=0, keepdims=True))
        alpha = jnp.exp2(m_prev - m_new)
        p = jnp.exp2(s - m_new)
        l_new = alpha * l_prev + jnp.sum(p, axis=0, keepdims=True)
        ks = pl.multiple_of(j * tq, tq)
        pv = lax.dot_general(v_ref[pl.ds(ks, tq), :], p.astype(_BF16),
                             (((0,), (0,)), ((), ())), preferred_element_type=_F32)
        acc_ref[...] = alpha * acc_ref[...] + pv
        return m_new, l_new

    def body(j, carry):
        m_prev, l_prev, s = carry
        s_next = logits(j + 1)
        m_new, l_new = accumulate(j, s, m_prev, l_prev)
        return m_new, l_new, s_next

    m0 = jnp.full((1, tq), MASK_VALUE, _F32)
    l0 = jnp.zeros((1, tq), _F32)
    m, l, s = lax.fori_loop(j0, qi, body, (m0, l0, logits(j0)))
    row = lax.broadcasted_iota(jnp.int32, s.shape, 0)
    col = lax.broadcasted_iota(jnp.int32, s.shape, 1)
    m, l = accumulate(qi, jnp.where(row <= col, s, MASK_VALUE), m, l)
    o_ref[...] = (acc_ref[...] / l).T.astype(o_ref.dtype)


def _fox_attention(qk, kb, b_start, b_end, qk_bound, v, tq):
    s = v.shape[0]
    hd = FOX_HEAD_DIM
    thr = (2.0 * qk_bound + FOX_SKIP_BITS).reshape(1).astype(_F32)
    smem = pl.BlockSpec(memory_space=pltpu.SMEM)
    return pl.pallas_call(
        functools.partial(_fox_kernel, tq=tq),
        out_shape=jax.ShapeDtypeStruct((s, FOX_WIDTH), _BF16),
        grid=(FOX_HEADS, s // tq),
        in_specs=[smem, smem, smem,
                  pl.BlockSpec((tq, hd), lambda h, i: (i, h)),
                  pl.BlockSpec((s, hd), lambda h, i: (0, FOX_HEADS + h)),
                  pl.BlockSpec((s, V7X_LANES), lambda h, i: (0, h)),
                  pl.BlockSpec((s, hd), lambda h, i: (0, h))],
        out_specs=pl.BlockSpec((tq, hd), lambda h, i: (i, h)),
        scratch_shapes=[pltpu.VMEM((hd, tq), _F32)],
        compiler_params=_params(("parallel", "arbitrary"), 32),
        name="fox_attention",
    )(b_start, b_end, thr, qk, qk, kb, v)


def _shift_rows(ext, d):
    return pltpu.roll(ext, d, axis=0)[V7X_SUBLANES:]


def _lru_kernel(lx_ref, lg_ref, wc_ref, bc_ref, wg_ref, bg_ref, lam_ref, o_ref,
                halo_ref, carry_ref):
    @pl.when(pl.program_id(1) == 0)
    def _():
        halo_ref[...] = jnp.zeros_like(halo_ref)
        carry_ref[...] = jnp.zeros_like(carry_ref)

    lx = lx_ref[...]
    ts = lx.shape[0]
    ext = jnp.concatenate([halo_ref[...], lx], axis=0)
    xr = (wc_ref[3:4, :] * lx + wc_ref[2:3, :] * _shift_rows(ext, 1)
          + wc_ref[1:2, :] * _shift_rows(ext, 2) + wc_ref[0:1, :] * _shift_rows(ext, 3)
          + bc_ref[...])
    halo_ref[...] = lx[ts - V7X_SUBLANES:]

    gates = jax.nn.sigmoid(jnp.dot(xr.astype(_BF16), wg_ref[...],
                                   preferred_element_type=_F32) + bg_ref[...])
    r = gates[:, :LRU_BLOCK_DIM]
    i_gate = gates[:, LRU_BLOCK_DIM:]
    lam = lam_ref[...]
    log_sig_lam = jnp.minimum(lam, 0.0) - jnp.log1p(jnp.exp(-jnp.abs(lam)))
    log_a = LRU_C * r * log_sig_lam
    a = jnp.exp(log_a)
    t = jnp.tanh(log_a)
    h = jnp.sqrt(-2.0 * t / (1.0 - t)) * (i_gate * xr)

    row = lax.broadcasted_iota(jnp.int32, (ts, 1), 0)
    d = 1
    while d < ts:
        valid = row >= d
        h = h + a * jnp.where(valid, pltpu.roll(h, d, axis=0), 0.0)
        a = a * jnp.where(valid, pltpu.roll(a, d, axis=0), 1.0)
        d *= 2
    h = h + a * carry_ref[0:1, :]
    carry_ref[...] = jnp.broadcast_to(h[ts - 1:ts, :], carry_ref.shape)
    o_ref[...] = (h * jax.nn.gelu(lg_ref[...])).astype(o_ref.dtype)


def _lru_branch(lxlg, w_conv, b_conv, w_gate, b_gate, lam, ts):
    s = lxlg.shape[0]
    bd = LRU_BLOCK_DIM
    return pl.pallas_call(
        _lru_kernel,
        out_shape=jax.ShapeDtypeStruct((s, LRU_WIDTH), _BF16),
        grid=(LRU_BLOCKS, s // ts),
        in_specs=[pl.BlockSpec((ts, bd), lambda n, i: (i, n)),
                  pl.BlockSpec((ts, bd), lambda n, i: (i, LRU_BLOCKS + n)),
                  pl.BlockSpec((LRU_CONV, bd), lambda n, i: (0, n)),
                  pl.BlockSpec((1, bd), lambda n, i: (0, n)),
                  pl.BlockSpec((None, bd, 2 * bd), lambda n, i: (n, 0, 0)),
                  pl.BlockSpec((None, 1, 2 * bd), lambda n, i: (n, 0, 0)),
                  pl.BlockSpec((1, bd), lambda n, i: (0, n))],
        out_specs=pl.BlockSpec((ts, bd), lambda n, i: (i, n)),
        scratch_shapes=[pltpu.VMEM((V7X_SUBLANES, bd), _F32),
                        pltpu.VMEM((V7X_SUBLANES, bd), _F32)],
        compiler_params=_params(("parallel", "arbitrary"), 32),
        name="conv_rglru",
    )(lxlg, lxlg, w_conv, b_conv.reshape(1, -1), w_gate, b_gate, lam.reshape(1, -1))


def _mem_attn_kernel(q_ref, k_ref, v_ref, o_ref):
    s = lax.dot_general(q_ref[...], k_ref[...], (((1,), (1,)), ((), ())),
                        preferred_element_type=_F32)
    m = jnp.max(s, axis=-1, keepdims=True)
    p = jnp.exp(s - m)
    l = jnp.sum(p, axis=-1, keepdims=True)
    acc = jnp.dot(p.astype(_BF16), v_ref[...], preferred_element_type=_F32)
    o_ref[...] = (acc / l).astype(o_ref.dtype)


def _mem_attention(q, k, v, ts):
    s = q.shape[0]
    m = k.shape[0]
    hd = MEM_HEAD_DIM
    return pl.pallas_call(
        _mem_attn_kernel,
        out_shape=jax.ShapeDtypeStruct((s, MEM_WIDTH), _BF16),
        grid=(s // ts, MEM_HEADS),
        in_specs=[pl.BlockSpec((ts, hd), lambda i, h: (i, h)),
                  pl.BlockSpec((m, hd), lambda i, h: (0, h)),
                  pl.BlockSpec((m, hd), lambda i, h: (0, h))],
        out_specs=pl.BlockSpec((ts, hd), lambda i, h: (i, h)),
        compiler_params=_params(("parallel", "arbitrary"), 32),
        name="mem_attention",
    )(q, k, v)


def _merge_kernel(h_ref, yf_ref, yl_ref, ym_ref, wg0_ref, wg1_ref, wg2_ref, wb_ref, bg_ref,
                  o_ref):
    h = h_ref[...]
    merged = None
    for n, (y_ref, wg_ref) in enumerate(((yf_ref, wg0_ref), (yl_ref, wg1_ref),
                                         (ym_ref, wg2_ref))):
        gate = jax.nn.sigmoid(_dot_nt(h, wg_ref[...]) + bg_ref[n:n + 1, :])
        term = gate * jnp.dot(y_ref[...], wb_ref[n], preferred_element_type=_F32)
        merged = term if merged is None else merged + term
    o_ref[...] = merged.astype(o_ref.dtype)


def _gated_merge(h, y_fox, y_lru, y_mem, w_gate_t, w_branch, b_gate, tm, tn):
    s, d = h.shape
    nj = d // tn
    bw = y_fox.shape[1]
    y_spec = pl.BlockSpec((tm, bw), lambda i, j: (i, 0))
    return pl.pallas_call(
        _merge_kernel,
        out_shape=jax.ShapeDtypeStruct((s, d), _BF16),
        grid=(s // tm, nj),
        in_specs=[pl.BlockSpec((tm, d), lambda i, j: (i, 0)), y_spec, y_spec, y_spec,
                  pl.BlockSpec((tn, d), lambda i, j: (j, 0)),
                  pl.BlockSpec((tn, d), lambda i, j: (nj + j, 0)),
                  pl.BlockSpec((tn, d), lambda i, j: (2 * nj + j, 0)),
                  pl.BlockSpec((N_BRANCH, bw, tn), lambda i, j: (0, 0, j)),
                  pl.BlockSpec((N_BRANCH, tn), lambda i, j: (0, j))],
        out_specs=pl.BlockSpec((tm, tn), lambda i, j: (i, j)),
        compiler_params=_params(("parallel", "arbitrary"), 56),
        name="gated_merge",
    )(h, y_fox, y_lru, y_mem, w_gate_t, w_gate_t, w_gate_t, w_branch, b_gate)


def _ffn_up_kernel(a_ref, wa_ref, wv_ref, wca_ref, wcv_ref, bca_ref, bcv_ref, o_ref,
                   halo_a_ref, halo_v_ref, wa_bf_ref, wv_bf_ref):
    @pl.when(pl.program_id(1) == 0)
    def _():
        halo_a_ref[...] = jnp.zeros_like(halo_a_ref)
        halo_v_ref[...] = jnp.zeros_like(halo_v_ref)
        wa_bf_ref[...] = wa_ref[...].astype(_BF16)
        wv_bf_ref[...] = wv_ref[...].astype(_BF16)

    a = a_ref[...]
    tm = a.shape[0]

    def conv(up, halo_ref, wc_ref, bc_ref):
        ext = jnp.concatenate([halo_ref[...], up], axis=0)
        halo_ref[...] = up[tm - V7X_SUBLANES:]
        return (wc_ref[2:3, :] * up + wc_ref[1:2, :] * _shift_rows(ext, 1)
                + wc_ref[0:1, :] * _shift_rows(ext, 2) + bc_ref[...])

    act = conv(jnp.dot(a, wa_bf_ref[...], preferred_element_type=_F32), halo_a_ref, wca_ref,
               bca_ref)
    val = conv(jnp.dot(a, wv_bf_ref[...], preferred_element_type=_F32), halo_v_ref, wcv_ref,
               bcv_ref)
    o_ref[...] = (jax.nn.gelu(act) * val).astype(o_ref.dtype)


def _ffn_up(h2, w_up, w_conv, b_conv, tm, tn):
    s, d = h2.shape
    f = FFN_HIDDEN
    nj = f // tn
    return pl.pallas_call(
        _ffn_up_kernel,
        out_shape=jax.ShapeDtypeStruct((s, f), _BF16),
        grid=(nj, s // tm),
        in_specs=[pl.BlockSpec((tm, d), lambda j, i: (i, 0)),
                  pl.BlockSpec((d, tn), lambda j, i: (0, j)),
                  pl.BlockSpec((d, tn), lambda j, i: (0, nj + j)),
                  pl.BlockSpec((FFN_CONV, tn), lambda j, i: (0, j)),
                  pl.BlockSpec((FFN_CONV, tn), lambda j, i: (0, nj + j)),
                  pl.BlockSpec((1, tn), lambda j, i: (0, j)),
                  pl.BlockSpec((1, tn), lambda j, i: (0, nj + j))],
        out_specs=pl.BlockSpec((tm, tn), lambda j, i: (i, j)),
        scratch_shapes=[pltpu.VMEM((V7X_SUBLANES, tn), _F32),
                        pltpu.VMEM((V7X_SUBLANES, tn), _F32),
                        pltpu.VMEM((d, tn), _BF16), pltpu.VMEM((d, tn), _BF16)],
        compiler_params=_params(("parallel", "arbitrary"), 56),
        name="ffn_up_conv_geglu",
    )(h2, w_up, w_up, w_conv, w_conv, b_conv.reshape(1, -1), b_conv.reshape(1, -1))


def _layer(x, mem, g_mix, w_in, b_f, g_q_fox, g_k_fox, w_lru_conv, b_lru_conv, w_rg_a, b_rg_a,
           w_rg_x, b_rg_x, lru_lambda, g_mem, w_mem_kv, g_q_mem, g_k_mem, b_gate, w_branch,
           w_out, g_ffn, w_ffn_up, w_ffn_conv, b_ffn_conv, w_ffn_down):
    s = x.shape[0]
    c_k = 2 * FOX_WIDTH
    c_v = c_k + FOX_WIDTH
    c_f = c_v + FOX_HEADS
    c_l = c_f + 2 * LRU_WIDTH
    c_m = c_l + MEM_WIDTH

    h = _rmsnorm(x, g_mix, 512)
    w_in_t = w_in.T

    g_q_scaled = g_q_fox * (LOG2_E * FOX_HEAD_DIM ** -0.5)
    gain_qk = jnp.concatenate([jnp.tile(g_q_scaled, FOX_HEADS),
                               jnp.tile(g_k_fox, FOX_HEADS)]).reshape(1, -1)
    qk = _matmul(h, w_in_t, w_t=True, n=c_k, tm=1024, tn=1024, out_dtype=_BF16,
                 epilogue="gnorm", extra=gain_qk, group=FOX_HEAD_DIM, name="proj_qk")
    v = _matmul(h, w_in_t, w_t=True, n=FOX_WIDTH, w_off=c_k, tm=1024, tn=1024,
                out_dtype=_BF16, name="proj_v")
    b_pad = jnp.pad(b_f.reshape(1, -1), ((0, 0), (0, V7X_LANES - FOX_HEADS)))
    kb, b_start, b_end = _forget_bias(h, w_in_t, c_v, b_pad, FOX_TILE)
    qk_bound = (1.02 * FOX_HEAD_DIM) * jnp.max(jnp.abs(g_q_scaled)) * jnp.max(jnp.abs(g_k_fox))
    y_fox = _fox_attention(qk, kb, b_start, b_end, qk_bound, v, FOX_TILE)

    lxlg = _matmul(h, w_in_t, w_t=True, n=2 * LRU_WIDTH, w_off=c_f, tm=1024, tn=1024,
                   out_dtype=_F32, name="proj_lru")
    w_gate_lru = jnp.concatenate([w_rg_a, w_rg_x], axis=-1).astype(_BF16)
    b_gate_lru = jnp.concatenate([b_rg_a, b_rg_x], axis=-1).reshape(LRU_BLOCKS, 1, -1)
    y_lru = _lru_branch(lxlg, w_lru_conv, b_lru_conv, w_gate_lru, b_gate_lru, lru_lambda, 512)

    gain_mq = (jnp.tile(g_q_mem, MEM_HEADS) * (MEM_HEAD_DIM ** -0.5)).reshape(1, -1)
    mq = _matmul(h, w_in_t, w_t=True, n=MEM_WIDTH, w_off=c_l, tm=1024, tn=1024,
                 out_dtype=_BF16, epilogue="gnorm", extra=gain_mq, group=MEM_HEAD_DIM,
                 name="proj_mq")
    hm = _rmsnorm(mem, g_mem, mem.shape[0])
    mk = _matmul(hm, w_mem_kv, n=MEM_WIDTH, tm=mem.shape[0], tn=512, out_dtype=_BF16,
                 epilogue="gnorm", extra=jnp.tile(g_k_mem, MEM_HEADS).reshape(1, -1),
                 group=MEM_HEAD_DIM, name="proj_mk")
    mv = _matmul(hm, w_mem_kv, n=MEM_WIDTH, w_off=MEM_WIDTH, tm=mem.shape[0], tn=512,
                 out_dtype=_BF16, name="proj_mv")
    y_mem = _mem_attention(mq, mk, mv, 1024)

    merged = _gated_merge(h, y_fox, y_lru, y_mem, w_in_t[c_m:].astype(_BF16),
                          w_branch.astype(_BF16), b_gate, 1024, 256)
    x2 = _matmul(merged, w_out, n=D_MODEL, tm=1024, tn=512, out_dtype=_F32,
                 epilogue="residual", extra=x, name="proj_out")

    h2 = _rmsnorm(x2, g_ffn, 512)
    g = _ffn_up(h2, w_ffn_up, w_ffn_conv, b_ffn_conv, 1024, 512)
    return _matmul(g, w_ffn_down.astype(_BF16), n=D_MODEL, tm=1024, tn=512, rows_outer=True,
                   out_dtype=_F32, epilogue="residual", extra=x2, name="ffn_down")


def kernel(x, mem, g_mix, w_in, b_f, g_q_fox, g_k_fox, w_lru_conv, b_lru_conv, w_rg_a, b_rg_a,
           w_rg_x, b_rg_x, lru_lambda, g_mem, w_mem_kv, g_q_mem, g_k_mem, b_gate, w_branch,
           w_out, g_ffn, w_ffn_up, w_ffn_conv, b_ffn_conv, w_ffn_down):
    depth = g_mix.shape[0]
    outs = []
    for b in range(x.shape[0]):
        xb = x[b]
        for l in range(depth):
            xb = _layer(xb, mem[b], g_mix[l], w_in[l], b_f[l], g_q_fox[l], g_k_fox[l],
                        w_lru_conv[l], b_lru_conv[l], w_rg_a[l], b_rg_a[l], w_rg_x[l],
                        b_rg_x[l], lru_lambda[l], g_mem[l], w_mem_kv[l], g_q_mem[l],
                        g_k_mem[l], b_gate[l], w_branch[l], w_out[l], g_ffn[l], w_ffn_up[l],
                        w_ffn_conv[l], b_ffn_conv[l], w_ffn_down[l])
        outs.append(xb)
    return outs[0][None] if len(outs) == 1 else jnp.stack(outs)
```

```python
import functools

import jax
import jax.numpy as jnp
import numpy as np
from jax import lax
from jax.experimental import pallas as pl
from jax.experimental.pallas import tpu as pltpu

D_MODEL = 2048
FOX_HEADS = 8
FOX_HEAD_DIM = 128
FOX_WIDTH = FOX_HEADS * FOX_HEAD_DIM
LRU_WIDTH = 1024
LRU_BLOCKS = 8
LRU_BLOCK_DIM = LRU_WIDTH // LRU_BLOCKS
LRU_CONV = 4
LRU_C = 8.0
MEM_HEADS = 4
MEM_HEAD_DIM = 256
MEM_WIDTH = MEM_HEADS * MEM_HEAD_DIM
N_BRANCH = 3
FFN_HIDDEN = 5632
FFN_CONV = 3
EPS = 1e-6

V7X_SUBLANES = 8
V7X_LANES = 128
MASK_VALUE = -1e30
LOG2_E = 1.4426950408889634
FOX_TILE = 512
FOX_SKIP_BITS = 64.0
LRU_SCAN_UNROLL = 8

_BF16 = jnp.bfloat16
_F32 = jnp.float32


def _params(semantics, vmem_mib):
    return pltpu.CompilerParams(dimension_semantics=semantics,
                                vmem_limit_bytes=vmem_mib * 1024 * 1024)


def _rmsnorm_kernel(x_ref, g_ref, o_ref):
    x = x_ref[...]
    ms = jnp.mean(x * x, axis=-1, keepdims=True)
    o_ref[...] = (x * lax.rsqrt(ms + EPS) * g_ref[...]).astype(o_ref.dtype)


def _rmsnorm(x, g, tm):
    m, d = x.shape
    return pl.pallas_call(
        _rmsnorm_kernel,
        out_shape=jax.ShapeDtypeStruct((m, d), _BF16),
        grid=(m // tm,),
        in_specs=[pl.BlockSpec((tm, d), lambda i: (i, 0)),
                  pl.BlockSpec((1, d), lambda i: (0, 0))],
        out_specs=pl.BlockSpec((tm, d), lambda i: (i, 0)),
        compiler_params=_params(("parallel",), 32),
        name="rmsnorm",
    )(x, g.reshape(1, d))


def _dot_nt(a, w_t):
    return lax.dot_general(a, w_t, (((1,), (1,)), ((), ())), preferred_element_type=_F32)


def _mm_kernel(*refs, epilogue, group, cast_w, w_t):
    a_ref, w_ref = refs[0], refs[1]
    if cast_w:
        o_ref, wb_ref = refs[-2], refs[-1]

        @pl.when(pl.program_id(1) == 0)
        def _():
            wb_ref[...] = w_ref[...].astype(_BF16)

        w = wb_ref[...]
    else:
        o_ref = refs[-1]
        w = w_ref[...]
    if w_t:
        acc = _dot_nt(a_ref[...], w)
    else:
        acc = jnp.dot(a_ref[...], w, preferred_element_type=_F32)
    if epilogue == "gnorm":
        g_ref = refs[2]
        for gi in range(acc.shape[1] // group):
            sl = slice(gi * group, (gi + 1) * group)
            blk = acc[:, sl]
            ms = jnp.mean(blk * blk, axis=-1, keepdims=True)
            o_ref[:, sl] = (blk * lax.rsqrt(ms + EPS) * g_ref[:, sl]).astype(o_ref.dtype)
    elif epilogue == "residual":
        o_ref[...] = (refs[2][...] + acc).astype(o_ref.dtype)
    else:
        o_ref[...] = acc.astype(o_ref.dtype)


def _matmul(a, w, *, n, tm, tn, out_dtype, w_t=False, w_off=0, rows_outer=False,
            epilogue="plain", extra=None, group=None, name):
    m, k = a.shape
    cast_w = w.dtype != _BF16
    assert not (cast_w and rows_outer) and n % tn == 0 and m % tm == 0
    if rows_outer:
        grid = (m // tm, n // tn)
        ij = lambda i, j: (i, j)
    else:
        grid = (n // tn, m // tm)
        ij = lambda j, i: (i, j)
    if w_t:
        assert w_off % V7X_SUBLANES == 0
        w_block = (tn, k)
        w_spec = pl.BlockSpec((pl.Element(tn), pl.Element(k)),
                              lambda *g: (pl.multiple_of(w_off + ij(*g)[1] * tn, V7X_SUBLANES), 0))
    else:
        assert w_off % tn == 0
        w_block = (k, tn)
        w_spec = pl.BlockSpec(w_block, lambda *g: (0, w_off // tn + ij(*g)[1]))
    in_specs = [pl.BlockSpec((tm, k), lambda *g: (ij(*g)[0], 0)), w_spec]
    args = [a, w]
    block_bytes = (tm * k * 2 + k * tn * w.dtype.itemsize
                   + tm * tn * jnp.dtype(out_dtype).itemsize)
    if epilogue == "gnorm":
        in_specs.append(pl.BlockSpec((1, tn), lambda *g: (0, ij(*g)[1])))
        args.append(extra)
    elif epilogue == "residual":
        in_specs.append(pl.BlockSpec((tm, tn), lambda *g: ij(*g)))
        args.append(extra)
        block_bytes += tm * tn * extra.dtype.itemsize
    scratch = [pltpu.VMEM(w_block, _BF16)] if cast_w else []
    vmem_mib = -(-(2 * block_bytes + cast_w * k * tn * 2 + 2 * tm * tn * 4) // 2 ** 20) + 2
    return pl.pallas_call(
        functools.partial(_mm_kernel, epilogue=epilogue, group=group, cast_w=cast_w, w_t=w_t),
        out_shape=jax.ShapeDtypeStruct((m, n), out_dtype),
        grid=grid,
        in_specs=in_specs,
        out_specs=pl.BlockSpec((tm, tn), lambda *g: ij(*g)),
        scratch_shapes=scratch,
        compiler_params=_params(("parallel", "arbitrary"), vmem_mib),
        name=name,
    )(*args)


def _split3(x):
    hi = x.astype(_BF16)
    r1 = x - hi.astype(_F32)
    mid = r1.astype(_BF16)
    lo = (r1 - mid.astype(_F32)).astype(_BF16)
    return hi, mid, lo


def _forget_bias_kernel(h_ref, w_ref, b_ref, sel_ref, o_ref, edge_ref, carry_ref):
    @pl.when(pl.program_id(0) == 0)
    def _():
        carry_ref[...] = jnp.zeros_like(carry_ref)

    tc = h_ref.shape[0]
    w = jnp.concatenate([w_ref[...], jnp.zeros((V7X_LANES - FOX_HEADS, w_ref.shape[1]), _F32)],
                        axis=0).astype(_BF16)
    z = _dot_nt(h_ref[...], w) + b_ref[...]
    neg_log_f = (jnp.log1p(jnp.exp(-jnp.abs(z))) - jnp.minimum(z, 0.0)) * LOG2_E
    row = lax.broadcasted_iota(jnp.int32, (tc, tc), 0)
    col = lax.broadcasted_iota(jnp.int32, (tc, tc), 1)
    tri = jnp.where(col <= row, 1.0, 0.0).astype(_BF16)
    c = carry_ref[0:1, :]
    for part in _split3(neg_log_f):
        c = c + jnp.dot(tri, part, preferred_element_type=_F32)
    carry_ref[...] = jnp.broadcast_to(c[tc - 1:tc, :], carry_ref.shape)
    edge_ref[...] = jnp.concatenate(
        [c[0:1, :], c[tc - 1:tc, :], jnp.zeros((V7X_SUBLANES - 2, c.shape[1]), _F32)], axis=0)
    parts = jnp.concatenate(_split3(c), axis=1)
    o_ref[...] = jnp.dot(parts, sel_ref[...], preferred_element_type=_F32).astype(o_ref.dtype)


def _forget_bias(h, w_t, w_row0, b_pad, tc):
    s, d = h.shape
    lanes = V7X_LANES
    nt = s // tc
    sel = np.zeros((3 * lanes, FOX_HEADS * lanes), np.float32)
    for part in range(3):
        for head in range(FOX_HEADS):
            sel[part * lanes + head, head * lanes + part] = 1.0
    sel = jnp.asarray(sel, _BF16)
    kb, edges = pl.pallas_call(
        _forget_bias_kernel,
        out_shape=(jax.ShapeDtypeStruct((s, FOX_HEADS * lanes), _BF16),
                   jax.ShapeDtypeStruct((nt * V7X_SUBLANES, lanes), _F32)),
        grid=(nt,),
        in_specs=[pl.BlockSpec((tc, d), lambda i: (i, 0)),
                  pl.BlockSpec((pl.Element(FOX_HEADS), pl.Element(d)), lambda i: (w_row0, 0)),
                  pl.BlockSpec((1, lanes), lambda i: (0, 0)),
                  pl.BlockSpec((3 * lanes, FOX_HEADS * lanes), lambda i: (0, 0))],
        out_specs=(pl.BlockSpec((tc, FOX_HEADS * lanes), lambda i: (i, 0)),
                   pl.BlockSpec((V7X_SUBLANES, lanes), lambda i: (i, 0))),
        scratch_shapes=[pltpu.VMEM((V7X_SUBLANES, lanes), _F32)],
        compiler_params=_params(("arbitrary",), 32),
        name="forget_bias",
    )(h, w_t, b_pad, sel)
    edges = edges.reshape(nt, V7X_SUBLANES, lanes)
    return kb, edges[:, 0, :FOX_HEADS], edges[:, 1, :FOX_HEADS]


def _fox_kernel(bstart_ref, bend_ref, thr_ref, q_ref, k_ref, kb_ref, v_ref, o_ref, acc_ref, *,
                tq):
    head = pl.program_id(0)
    qi = pl.program_id(1)
    b_tile = bstart_ref[qi, head]
    thr = thr_ref[0]
    j0 = lax.fori_loop(
        0, qi, lambda j, n: n + jnp.where(b_tile - bend_ref[j, head] >= thr, 1, 0), 0)
    lane = lax.broadcasted_iota(jnp.int32, (tq, V7X_LANES), 1)
    ones3 = jnp.where(lane < 3, 1.0, 0.0).astype(_BF16)
    q_aug = jnp.concatenate([q_ref[...], ones3], axis=1)
    acc_ref[...] = jnp.zeros_like(acc_ref)

    def logits(j):
        ks = pl.multiple_of(j * tq, tq)
        k_aug = jnp.concatenate([k_ref[pl.ds(ks, tq), :], kb_ref[pl.ds(ks, tq), :]], axis=1)
        return lax.dot_general(k_aug, q_aug, (((1,), (1,)), ((), ())),
                               preferred_element_type=_F32)

    def accumulate(j, s, m_prev, l_prev):
        m_new = jnp.maximum(m_prev, jnp.max(s, axis=0, keepdims=True))
        alpha = jnp.exp2(m_prev - m_new)
        p = jnp.exp2(s - m_new)
        l_new = alpha * l_prev + jnp.sum(p, axis=0, keepdims=True)
        ks = pl.multiple_of(j * tq, tq)
        pv = lax.dot_general(v_ref[pl.ds(ks, tq), :], p.astype(_BF16),
                             (((0,), (0,)), ((), ())), preferred_element_type=_F32)
        acc_ref[...] = alpha * acc_ref[...] + pv
        return m_new, l_new

    def body(j, carry):
        m_prev, l_prev, s = carry
        s_next = logits(j + 1)
        m_new, l_new = accumulate(j, s, m_prev, l_prev)
        return m_new, l_new, s_next

    m0 = jnp.full((1, tq), MASK_VALUE, _F32)
    l0 = jnp.zeros((1, tq), _F32)
    m, l, s = lax.fori_loop(j0, qi, body, (m0, l0, logits(j0)))
    row = lax.broadcasted_iota(jnp.int32, s.shape, 0)
    col = lax.broadcasted_iota(jnp.int32, s.shape, 1)
    m, l = accumulate(qi, jnp.where(row <= col, s, MASK_VALUE), m, l)
    o_ref[...] = (acc_ref[...] / l).T.astype(o_ref.dtype)


def _fox_attention(qk, kb, b_start, b_end, qk_bound, v, tq):
    s = v.shape[0]
    hd = FOX_HEAD_DIM
    thr = (2.0 * qk_bound + FOX_SKIP_BITS).reshape(1).astype(_F32)
    smem = pl.BlockSpec(memory_space=pltpu.SMEM)
    return pl.pallas_call(
        functools.partial(_fox_kernel, tq=tq),
        out_shape=jax.ShapeDtypeStruct((s, FOX_WIDTH), _BF16),
        grid=(FOX_HEADS, s // tq),
        in_specs=[smem, smem, smem,
                  pl.BlockSpec((tq, hd), lambda h, i: (i, h)),
                  pl.BlockSpec((s, hd), lambda h, i: (0, FOX_HEADS + h)),
                  pl.BlockSpec((s, V7X_LANES), lambda h, i: (0, h)),
                  pl.BlockSpec((s, hd), lambda h, i: (0, h))],
        out_specs=pl.BlockSpec((tq, hd), lambda h, i: (i, h)),
        scratch_shapes=[pltpu.VMEM((hd, tq), _F32)],
        compiler_params=_params(("parallel", "arbitrary"), 32),
        name="fox_attention",
    )(b_start, b_end, thr, qk, qk, kb, v)


def _shift_rows(ext, d):
    return pltpu.roll(ext, d, axis=0)[V7X_SUBLANES:]


def _lru_kernel(lx_ref, lg_ref, wc_ref, bc_ref, wg_ref, bg_ref, lam_ref, o_ref,
                halo_ref, carry_ref, a_ref, h_ref):
    @pl.when(pl.program_id(0) == 0)
    def _():
        halo_ref[...] = jnp.zeros_like(halo_ref)
        carry_ref[...] = jnp.zeros_like(carry_ref)

    lx = lx_ref[...]
    ts = lx.shape[0]
    ext = jnp.concatenate([halo_ref[...], lx], axis=0)
    xr = (wc_ref[3:4, :] * lx + wc_ref[2:3, :] * _shift_rows(ext, 1)
          + wc_ref[1:2, :] * _shift_rows(ext, 2) + wc_ref[0:1, :] * _shift_rows(ext, 3)
          + bc_ref[...])
    halo_ref[...] = lx[ts - V7X_SUBLANES:]

    lam = lam_ref[...]
    log_sig_lam = jnp.minimum(lam, 0.0) - jnp.log1p(jnp.exp(-jnp.abs(lam)))
    bd = LRU_BLOCK_DIM
    for nb in range(LRU_BLOCKS):
        sl = slice(nb * bd, (nb + 1) * bd)
        x_nb = xr[:, sl]
        gates = jax.nn.sigmoid(jnp.dot(x_nb.astype(_BF16), wg_ref[nb],
                                       preferred_element_type=_F32) + bg_ref[nb])
        log_a = LRU_C * gates[:, :bd] * log_sig_lam[:, sl]
        a_ref[:, sl] = jnp.exp(log_a)
        t = jnp.tanh(log_a)
        h_ref[:, sl] = jnp.sqrt(-2.0 * t / (1.0 - t)) * (gates[:, bd:] * x_nb)

    sub = lax.broadcasted_iota(jnp.int32, (V7X_SUBLANES, 1), 0)

    def group(g, carry):
        rows = pl.ds(pl.multiple_of(g * V7X_SUBLANES, V7X_SUBLANES), V7X_SUBLANES)
        a = a_ref[rows, :]
        h = h_ref[rows, :]
        for d in (1, 2, 4):
            valid = sub >= d
            h = h + a * jnp.where(valid, pltpu.roll(h, d, axis=0), 0.0)
            a = a * jnp.where(valid, pltpu.roll(a, d, axis=0), 1.0)
        h = h + a * carry
        h_ref[rows, :] = h
        return jnp.broadcast_to(h[V7X_SUBLANES - 1:, :], h.shape)

    carry_ref[...] = lax.fori_loop(0, ts // V7X_SUBLANES, group, carry_ref[...],
                                   unroll=LRU_SCAN_UNROLL)
    o_ref[...] = (h_ref[...] * jax.nn.gelu(lg_ref[...])).astype(o_ref.dtype)


def _lru_branch(lxlg, w_conv, b_conv, w_gate, b_gate, lam, ts):
    s = lxlg.shape[0]
    w = LRU_WIDTH
    bd = LRU_BLOCK_DIM
    return pl.pallas_call(
        _lru_kernel,
        out_shape=jax.ShapeDtypeStruct((s, w), _BF16),
        grid=(s // ts,),
        in_specs=[pl.BlockSpec((ts, w), lambda i: (i, 0)),
                  pl.BlockSpec((ts, w), lambda i: (i, 1)),
                  pl.BlockSpec((LRU_CONV, w), lambda i: (0, 0)),
                  pl.BlockSpec((1, w), lambda i: (0, 0)),
                  pl.BlockSpec((LRU_BLOCKS, bd, 2 * bd), lambda i: (0, 0, 0)),
                  pl.BlockSpec((LRU_BLOCKS, 1, 2 * bd), lambda i: (0, 0, 0)),
                  pl.BlockSpec((1, w), lambda i: (0, 0))],
        out_specs=pl.BlockSpec((ts, w), lambda i: (i, 0)),
        scratch_shapes=[pltpu.VMEM((V7X_SUBLANES, w), _F32), pltpu.VMEM((V7X_SUBLANES, w), _F32),
                        pltpu.VMEM((ts, w), _F32), pltpu.VMEM((ts, w), _F32)],
        compiler_params=_params(("arbitrary",), 40),
        name="conv_rglru",
    )(lxlg, lxlg, w_conv, b_conv.reshape(1, -1), w_gate, b_gate, lam.reshape(1, -1))


def _mem_attn_kernel(q_ref, k_ref, v_ref, o_ref):
    s = lax.dot_general(q_ref[...], k_ref[...], (((1,), (1,)), ((), ())),
                        preferred_element_type=_F32)
    m = jnp.max(s, axis=-1, keepdims=True)
    p = jnp.exp(s - m)
    l = jnp.sum(p, axis=-1, keepdims=True)
    acc = jnp.dot(p.astype(_BF16), v_ref[...], preferred_element_type=_F32)
    o_ref[...] = (acc / l).astype(o_ref.dtype)


def _mem_attention(q, k, v, ts):
    s = q.shape[0]
    m = k.shape[0]
    hd = MEM_HEAD_DIM
    return pl.pallas_call(
        _mem_attn_kernel,
        out_shape=jax.ShapeDtypeStruct((s, MEM_WIDTH), _BF16),
        grid=(s // ts, MEM_HEADS),
        in_specs=[pl.BlockSpec((ts, hd), lambda i, h: (i, h)),
                  pl.BlockSpec((m, hd), lambda i, h: (0, h)),
                  pl.BlockSpec((m, hd), lambda i, h: (0, h))],
        out_specs=pl.BlockSpec((ts, hd), lambda i, h: (i, h)),
        compiler_params=_params(("parallel", "arbitrary"), 32),
        name="mem_attention",
    )(q, k, v)


def _merge_kernel(h_ref, yf_ref, yl_ref, ym_ref, wg0_ref, wg1_ref, wg2_ref, wb_ref, bg_ref,
                  o_ref):
    h = h_ref[...]
    merged = None
    for n, (y_ref, wg_ref) in enumerate(((yf_ref, wg0_ref), (yl_ref, wg1_ref),
                                         (ym_ref, wg2_ref))):
        gate = jax.nn.sigmoid(_dot_nt(h, wg_ref[...]) + bg_ref[n:n + 1, :])
        term = gate * jnp.dot(y_ref[...], wb_ref[n], preferred_element_type=_F32)
        merged = term if merged is None else merged + term
    o_ref[...] = merged.astype(o_ref.dtype)


def _gated_merge(h, y_fox, y_lru, y_mem, w_gate_t, w_branch, b_gate, tm, tn):
    s, d = h.shape
    nj = d // tn
    bw = y_fox.shape[1]
    y_spec = pl.BlockSpec((tm, bw), lambda i, j: (i, 0))
    return pl.pallas_call(
        _merge_kernel,
        out_shape=jax.ShapeDtypeStruct((s, d), _BF16),
        grid=(s // tm, nj),
        in_specs=[pl.BlockSpec((tm, d), lambda i, j: (i, 0)), y_spec, y_spec, y_spec,
                  pl.BlockSpec((tn, d), lambda i, j: (j, 0)),
                  pl.BlockSpec((tn, d), lambda i, j: (nj + j, 0)),
                  pl.BlockSpec((tn, d), lambda i, j: (2 * nj + j, 0)),
                  pl.BlockSpec((N_BRANCH, bw, tn), lambda i, j: (0, 0, j)),
                  pl.BlockSpec((N_BRANCH, tn), lambda i, j: (0, j))],
        out_specs=pl.BlockSpec((tm, tn), lambda i, j: (i, j)),
        compiler_params=_params(("parallel", "arbitrary"), 56),
        name="gated_merge",
    )(h, y_fox, y_lru, y_mem, w_gate_t, w_gate_t, w_gate_t, w_branch, b_gate)


def _out_norm_kernel(a_ref, w_ref, x_ref, g_ref, x2_ref, h2_ref):
    x2 = x_ref[...] + jnp.dot(a_ref[...], w_ref[...], preferred_element_type=_F32)
    x2_ref[...] = x2
    ms = jnp.mean(x2 * x2, axis=-1, keepdims=True)
    h2_ref[...] = (x2 * lax.rsqrt(ms + EPS) * g_ref[...]).astype(h2_ref.dtype)


def _out_proj_norm(a, w, x, g, tm):
    m, k = a.shape
    d = w.shape[1]
    row = lambda i: (i, 0)
    return pl.pallas_call(
        _out_norm_kernel,
        out_shape=(jax.ShapeDtypeStruct((m, d), _F32), jax.ShapeDtypeStruct((m, d), _BF16)),
        grid=(m // tm,),
        in_specs=[pl.BlockSpec((tm, k), row), pl.BlockSpec((k, d), lambda i: (0, 0)),
                  pl.BlockSpec((tm, d), row), pl.BlockSpec((1, d), lambda i: (0, 0))],
        out_specs=(pl.BlockSpec((tm, d), row), pl.BlockSpec((tm, d), row)),
        compiler_params=_params(("parallel",), 48),
        name="proj_out_norm",
    )(a, w, x, g.reshape(1, d))


def _cast_kernel(w_ref, o_ref):
    o_ref[...] = w_ref[...].astype(o_ref.dtype)


def _cast_rows(w, row0, rows, tr):
    k = w.shape[1]
    assert row0 % V7X_SUBLANES == 0 and rows % tr == 0
    return pl.pallas_call(
        _cast_kernel,
        out_shape=jax.ShapeDtypeStruct((rows, k), _BF16),
        grid=(rows // tr,),
        in_specs=[pl.BlockSpec((pl.Element(tr), pl.Element(k)),
                               lambda i: (pl.multiple_of(row0 + i * tr, V7X_SUBLANES), 0))],
        out_specs=pl.BlockSpec((tr, k), lambda i: (i, 0)),
        compiler_params=_params(("parallel",), 32),
        name="cast_rows",
    )(w)


def _ffn_up_kernel(a_ref, wa_ref, wv_ref, wca_ref, wcv_ref, bca_ref, bcv_ref, o_ref,
                   halo_a_ref, halo_v_ref, wa_bf_ref, wv_bf_ref):
    @pl.when(pl.program_id(1) == 0)
    def _():
        halo_a_ref[...] = jnp.zeros_like(halo_a_ref)
        halo_v_ref[...] = jnp.zeros_like(halo_v_ref)
        wa_bf_ref[...] = wa_ref[...].astype(_BF16)
        wv_bf_ref[...] = wv_ref[...].astype(_BF16)

    a = a_ref[...]
    tm = a.shape[0]

    def conv(up, halo_ref, wc_ref, bc_ref):
        ext = jnp.concatenate([halo_ref[...], up], axis=0)
        halo_ref[...] = up[tm - V7X_SUBLANES:]
        return (wc_ref[2:3, :] * up + wc_ref[1:2, :] * _shift_rows(ext, 1)
                + wc_ref[0:1, :] * _shift_rows(ext, 2) + bc_ref[...])

    act = conv(jnp.dot(a, wa_bf_ref[...], preferred_element_type=_F32), halo_a_ref, wca_ref,
               bca_ref)
    val = conv(jnp.dot(a, wv_bf_ref[...], preferred_element_type=_F32), halo_v_ref, wcv_ref,
               bcv_ref)
    o_ref[...] = (jax.nn.gelu(act) * val).astype(o_ref.dtype)


def _ffn_up(h2, w_up, w_conv, b_conv, tm, tn):
    s, d = h2.shape
    f = FFN_HIDDEN
    nj = f // tn
    return pl.pallas_call(
        _ffn_up_kernel,
        out_shape=jax.ShapeDtypeStruct((s, f), _BF16),
        grid=(nj, s // tm),
        in_specs=[pl.BlockSpec((tm, d), lambda j, i: (i, 0)),
                  pl.BlockSpec((d, tn), lambda j, i: (0, j)),
                  pl.BlockSpec((d, tn), lambda j, i: (0, nj + j)),
                  pl.BlockSpec((FFN_CONV, tn), lambda j, i: (0, j)),
                  pl.BlockSpec((FFN_CONV, tn), lambda j, i: (0, nj + j)),
                  pl.BlockSpec((1, tn), lambda j, i: (0, j)),
                  pl.BlockSpec((1, tn), lambda j, i: (0, nj + j))],
        out_specs=pl.BlockSpec((tm, tn), lambda j, i: (i, j)),
        scratch_shapes=[pltpu.VMEM((V7X_SUBLANES, tn), _F32),
                        pltpu.VMEM((V7X_SUBLANES, tn), _F32),
                        pltpu.VMEM((d, tn), _BF16), pltpu.VMEM((d, tn), _BF16)],
        compiler_params=_params(("parallel", "arbitrary"), 56),
        name="ffn_up_conv_geglu",
    )(h2, w_up, w_up, w_conv, w_conv, b_conv.reshape(1, -1), b_conv.reshape(1, -1))


def _layer(x, mem, g_mix, w_in, b_f, g_q_fox, g_k_fox, w_lru_conv, b_lru_conv, w_rg_a, b_rg_a,
           w_rg_x, b_rg_x, lru_lambda, g_mem, w_mem_kv, g_q_mem, g_k_mem, b_gate, w_branch,
           w_out, g_ffn, w_ffn_up, w_ffn_conv, b_ffn_conv, w_ffn_down):
    s = x.shape[0]
    c_k = 2 * FOX_WIDTH
    c_v = c_k + FOX_WIDTH
    c_f = c_v + FOX_HEADS
    c_l = c_f + 2 * LRU_WIDTH
    c_m = c_l + MEM_WIDTH

    h = _rmsnorm(x, g_mix, 512)
    w_in_t = w_in.T

    g_q_scaled = g_q_fox * (LOG2_E * FOX_HEAD_DIM ** -0.5)
    gain_qk = jnp.concatenate([jnp.tile(g_q_scaled, FOX_HEADS),
                               jnp.tile(g_k_fox, FOX_HEADS)]).reshape(1, -1)
    qk = _matmul(h, w_in_t, w_t=True, n=c_k, tm=1024, tn=1024, out_dtype=_BF16,
                 epilogue="gnorm", extra=gain_qk, group=FOX_HEAD_DIM, name="proj_qk")
    v = _matmul(h, w_in_t, w_t=True, n=FOX_WIDTH, w_off=c_k, tm=1024, tn=1024,
                out_dtype=_BF16, name="proj_v")
    b_pad = jnp.pad(b_f.reshape(1, -1), ((0, 0), (0, V7X_LANES - FOX_HEADS)))
    kb, b_start, b_end = _forget_bias(h, w_in_t, c_v, b_pad, FOX_TILE)
    qk_bound = (1.02 * FOX_HEAD_DIM) * jnp.max(jnp.abs(g_q_scaled)) * jnp.max(jnp.abs(g_k_fox))
    y_fox = _fox_attention(qk, kb, b_start, b_end, qk_bound, v, FOX_TILE)

    lxlg = _matmul(h, w_in_t, w_t=True, n=2 * LRU_WIDTH, w_off=c_f, tm=1024, tn=1024,
                   out_dtype=_F32, name="proj_lru")
    w_gate_lru = jnp.concatenate([w_rg_a, w_rg_x], axis=-1).astype(_BF16)
    b_gate_lru = jnp.concatenate([b_rg_a, b_rg_x], axis=-1).reshape(LRU_BLOCKS, 1, -1)
    y_lru = _lru_branch(lxlg, w_lru_conv, b_lru_conv, w_gate_lru, b_gate_lru, lru_lambda, 512)

    gain_mq = (jnp.tile(g_q_mem, MEM_HEADS) * (MEM_HEAD_DIM ** -0.5)).reshape(1, -1)
    mq = _matmul(h, w_in_t, w_t=True, n=MEM_WIDTH, w_off=c_l, tm=1024, tn=1024,
                 out_dtype=_BF16, epilogue="gnorm", extra=gain_mq, group=MEM_HEAD_DIM,
                 name="proj_mq")
    hm = _rmsnorm(mem, g_mem, mem.shape[0])
    mk = _matmul(hm, w_mem_kv, n=MEM_WIDTH, tm=mem.shape[0], tn=512, out_dtype=_BF16,
                 epilogue="gnorm", extra=jnp.tile(g_k_mem, MEM_HEADS).reshape(1, -1),
                 group=MEM_HEAD_DIM, name="proj_mk")
    mv = _matmul(hm, w_mem_kv, n=MEM_WIDTH, w_off=MEM_WIDTH, tm=mem.shape[0], tn=512,
                 out_dtype=_BF16, name="proj_mv")
    y_mem = _mem_attention(mq, mk, mv, 1024)

    w_gate_t = _cast_rows(w_in_t, c_m, N_BRANCH * D_MODEL, 512)
    merged = _gated_merge(h, y_fox, y_lru, y_mem, w_gate_t, w_branch.astype(_BF16), b_gate,
                          1024, 256)
    x2, h2 = _out_proj_norm(merged, w_out.astype(_BF16), x, g_ffn, 512)

    g = _ffn_up(h2, w_ffn_up, w_ffn_conv, b_ffn_conv, 1024, 512)
    return _matmul(g, w_ffn_down.astype(_BF16), n=D_MODEL, tm=1024, tn=512, rows_outer=True,
                   out_dtype=_F32, epilogue="residual", extra=x2, name="ffn_down")


def kernel(x, mem, g_mix, w_in, b_f, g_q_fox, g_k_fox, w_lru_conv, b_lru_conv, w_rg_a, b_rg_a,
           w_rg_x, b_rg_x, lru_lambda, g_mem, w_mem_kv, g_q_mem, g_k_mem, b_gate, w_branch,
           w_out, g_ffn, w_ffn_up, w_ffn_conv, b_ffn_conv, w_ffn_down):
    depth = g_mix.shape[0]
    outs = []
    for b in range(x.shape[0]):
        xb = x[b]
        for l in range(depth):
            xb = _layer(xb, mem[b], g_mix[l], w_in[l], b_f[l], g_q_fox[l], g_k_fox[l],
                        w_lru_conv[l], b_lru_conv[l], w_rg_a[l], b_rg_a[l], w_rg_x[l],
                        b_rg_x[l], lru_lambda[l], g_mem[l], w_mem_kv[l], g_q_mem[l],
                        g_k_mem[l], b_gate[l], w_branch[l], w_out[l], g_ffn[l], w_ffn_up[l],
                        w_ffn_conv[l], b_ffn_conv[l], w_ffn_down[l])
        outs.append(xb)
    return outs[0][None] if len(outs) == 1 else jnp.stack(outs)
```

```python
import functools

import jax
import jax.numpy as jnp
import numpy as np
from jax import lax
from jax.experimental import pallas as pl
from jax.experimental.pallas import tpu as pltpu

D_MODEL = 2048
FOX_HEADS = 8
FOX_HEAD_DIM = 128
FOX_WIDTH = FOX_HEADS * FOX_HEAD_DIM
LRU_WIDTH = 1024
LRU_BLOCKS = 8
LRU_BLOCK_DIM = LRU_WIDTH // LRU_BLOCKS
LRU_CONV = 4
LRU_C = 8.0
MEM_HEADS = 4
MEM_HEAD_DIM = 256
MEM_WIDTH = MEM_HEADS * MEM_HEAD_DIM
N_BRANCH = 3
FFN_HIDDEN = 5632
FFN_CONV = 3
EPS = 1e-6

V7X_SUBLANES = 8
V7X_LANES = 128
MASK_VALUE = -1e30
LOG2_E = 1.4426950408889634
FOX_TILE = 512
FOX_SKIP_BITS = 64.0
FOX_BIAS_LANES = 16
FOX_FAST_MAX_LOGIT = 48.0
LRU_SCAN_UNROLL = 8

_BF16 = jnp.bfloat16
_F32 = jnp.float32


def _params(semantics, vmem_mib):
    return pltpu.CompilerParams(dimension_semantics=semantics,
                                vmem_limit_bytes=vmem_mib * 1024 * 1024)


def _rmsnorm_kernel(x_ref, g_ref, o_ref):
    x = x_ref[...]
    ms = jnp.mean(x * x, axis=-1, keepdims=True)
    o_ref[...] = (x * lax.rsqrt(ms + EPS) * g_ref[...]).astype(o_ref.dtype)


def _rmsnorm(x, g, tm):
    m, d = x.shape
    return pl.pallas_call(
        _rmsnorm_kernel,
        out_shape=jax.ShapeDtypeStruct((m, d), _BF16),
        grid=(m // tm,),
        in_specs=[pl.BlockSpec((tm, d), lambda i: (i, 0)),
                  pl.BlockSpec((1, d), lambda i: (0, 0))],
        out_specs=pl.BlockSpec((tm, d), lambda i: (i, 0)),
        compiler_params=_params(("parallel",), 32),
        name="rmsnorm",
    )(x, g.reshape(1, d))


def _dot_nt(a, w_t):
    return lax.dot_general(a, w_t, (((1,), (1,)), ((), ())), preferred_element_type=_F32)


def _mm_kernel(*refs, epilogue, group, cast_w, w_t):
    a_ref, w_ref = refs[0], refs[1]
    if cast_w:
        o_ref, wb_ref = refs[-2], refs[-1]

        @pl.when(pl.program_id(1) == 0)
        def _():
            wb_ref[...] = w_ref[...].astype(_BF16)

        w = wb_ref[...]
    else:
        o_ref = refs[-1]
        w = w_ref[...]
    if w_t:
        acc = _dot_nt(a_ref[...], w)
    else:
        acc = jnp.dot(a_ref[...], w, preferred_element_type=_F32)
    if epilogue == "gnorm":
        g_ref = refs[2]
        for gi in range(acc.shape[1] // group):
            sl = slice(gi * group, (gi + 1) * group)
            blk = acc[:, sl]
            ms = jnp.mean(blk * blk, axis=-1, keepdims=True)
            o_ref[:, sl] = (blk * lax.rsqrt(ms + EPS) * g_ref[:, sl]).astype(o_ref.dtype)
    elif epilogue == "residual":
        o_ref[...] = (refs[2][...] + acc).astype(o_ref.dtype)
    else:
        o_ref[...] = acc.astype(o_ref.dtype)


def _matmul(a, w, *, n, tm, tn, out_dtype, w_t=False, w_off=0, rows_outer=False,
            epilogue="plain", extra=None, group=None, name):
    m, k = a.shape
    cast_w = w.dtype != _BF16
    assert not (cast_w and rows_outer) and n % tn == 0 and m % tm == 0
    if rows_outer:
        grid = (m // tm, n // tn)
        ij = lambda i, j: (i, j)
    else:
        grid = (n // tn, m // tm)
        ij = lambda j, i: (i, j)
    if w_t:
        assert w_off % V7X_SUBLANES == 0
        w_block = (tn, k)
        w_spec = pl.BlockSpec((pl.Element(tn), pl.Element(k)),
                              lambda *g: (pl.multiple_of(w_off + ij(*g)[1] * tn, V7X_SUBLANES), 0))
    else:
        assert w_off % tn == 0
        w_block = (k, tn)
        w_spec = pl.BlockSpec(w_block, lambda *g: (0, w_off // tn + ij(*g)[1]))
    in_specs = [pl.BlockSpec((tm, k), lambda *g: (ij(*g)[0], 0)), w_spec]
    args = [a, w]
    block_bytes = (tm * k * 2 + k * tn * w.dtype.itemsize
                   + tm * tn * jnp.dtype(out_dtype).itemsize)
    if epilogue == "gnorm":
        in_specs.append(pl.BlockSpec((1, tn), lambda *g: (0, ij(*g)[1])))
        args.append(extra)
    elif epilogue == "residual":
        in_specs.append(pl.BlockSpec((tm, tn), lambda *g: ij(*g)))
        args.append(extra)
        block_bytes += tm * tn * extra.dtype.itemsize
    scratch = [pltpu.VMEM(w_block, _BF16)] if cast_w else []
    vmem_mib = -(-(2 * block_bytes + cast_w * k * tn * 2 + 2 * tm * tn * 4) // 2 ** 20) + 2
    return pl.pallas_call(
        functools.partial(_mm_kernel, epilogue=epilogue, group=group, cast_w=cast_w, w_t=w_t),
        out_shape=jax.ShapeDtypeStruct((m, n), out_dtype),
        grid=grid,
        in_specs=in_specs,
        out_specs=pl.BlockSpec((tm, tn), lambda *g: ij(*g)),
        scratch_shapes=scratch,
        compiler_params=_params(("parallel", "arbitrary"), vmem_mib),
        name=name,
    )(*args)


def _split3(x):
    hi = x.astype(_BF16)
    r1 = x - hi.astype(_F32)
    mid = r1.astype(_BF16)
    lo = (r1 - mid.astype(_F32)).astype(_BF16)
    return hi, mid, lo


def _forget_bias_kernel(shift_ref, h_ref, w_ref, b_ref, sel_ref, ones_ref, kb_ref, qb_ref,
                        edge_ref, carry_ref):
    @pl.when(pl.program_id(0) == 0)
    def _():
        carry_ref[...] = jnp.zeros_like(carry_ref)

    tc = h_ref.shape[0]
    w = jnp.concatenate([w_ref[...], jnp.zeros((V7X_LANES - FOX_HEADS, w_ref.shape[1]), _F32)],
                        axis=0).astype(_BF16)
    z = _dot_nt(h_ref[...], w) + b_ref[...]
    neg_log_f = (jnp.log1p(jnp.exp(-jnp.abs(z))) - jnp.minimum(z, 0.0)) * LOG2_E
    row = lax.broadcasted_iota(jnp.int32, (tc, tc), 0)
    col = lax.broadcasted_iota(jnp.int32, (tc, tc), 1)
    tri = jnp.where(col <= row, 1.0, 0.0).astype(_BF16)
    c = carry_ref[0:1, :]
    for part in _split3(neg_log_f):
        c = c + jnp.dot(tri, part, preferred_element_type=_F32)
    carry_ref[...] = jnp.broadcast_to(c[tc - 1:tc, :], carry_ref.shape)
    edge_ref[...] = jnp.concatenate(
        [c[0:1, :], c[tc - 1:tc, :], jnp.zeros((V7X_SUBLANES - 2, c.shape[1]), _F32)], axis=0)
    parts = jnp.concatenate(_split3(c) + _split3(-(c + shift_ref[0])), axis=1)
    routed = jnp.dot(parts, sel_ref[...], preferred_element_type=_F32) + ones_ref[...]
    half = kb_ref.shape[1]
    kb_ref[...] = routed[:, :half].astype(kb_ref.dtype)
    qb_ref[...] = routed[:, half:].astype(qb_ref.dtype)


def _forget_bias(h, w_t, w_row0, b_pad, shift, tc):
    s, d = h.shape
    lanes = V7X_LANES
    width = lanes
    nt = s // tc
    sel = np.zeros((6 * lanes, 2 * width), np.float32)
    ones = np.zeros((1, 2 * width), np.float32)
    for head in range(FOX_HEADS):
        for part in range(3):
            sel[part * lanes + head, FOX_BIAS_LANES * head + part] = 1.0
            sel[(3 + part) * lanes + head, width + FOX_BIAS_LANES * head + 3 + part] = 1.0
            ones[0, FOX_BIAS_LANES * head + 3 + part] = 1.0
            ones[0, width + FOX_BIAS_LANES * head + part] = 1.0
    kb, qb, edges = pl.pallas_call(
        _forget_bias_kernel,
        out_shape=(jax.ShapeDtypeStruct((s, width), _BF16),
                   jax.ShapeDtypeStruct((s, width), _BF16),
                   jax.ShapeDtypeStruct((nt * V7X_SUBLANES, lanes), _F32)),
        grid=(nt,),
        in_specs=[pl.BlockSpec(memory_space=pltpu.SMEM),
                  pl.BlockSpec((tc, d), lambda i: (i, 0)),
                  pl.BlockSpec((pl.Element(FOX_HEADS), pl.Element(d)), lambda i: (w_row0, 0)),
                  pl.BlockSpec((1, lanes), lambda i: (0, 0)),
                  pl.BlockSpec((6 * lanes, 2 * width), lambda i: (0, 0)),
                  pl.BlockSpec((1, 2 * width), lambda i: (0, 0))],
        out_specs=(pl.BlockSpec((tc, width), lambda i: (i, 0)),
                   pl.BlockSpec((tc, width), lambda i: (i, 0)),
                   pl.BlockSpec((V7X_SUBLANES, lanes), lambda i: (i, 0))),
        scratch_shapes=[pltpu.VMEM((V7X_SUBLANES, lanes), _F32)],
        compiler_params=_params(("arbitrary",), 32),
        name="forget_bias",
    )(shift.reshape(1).astype(_F32), h, w_t, b_pad, jnp.asarray(sel, _BF16), jnp.asarray(ones))
    edges = edges.reshape(nt, V7X_SUBLANES, lanes)
    return kb, qb, edges[:, 0, :FOX_HEADS], edges[:, 1, :FOX_HEADS]


def _unpack_bias(packed, sel_ref):
    return jnp.dot(packed, sel_ref[...], preferred_element_type=_F32).astype(_BF16)


def _fox_kernel(bstart_ref, bend_ref, thr_ref, q_ref, k_ref, kbp_ref, sel_ref, v_ref, o_ref,
                acc_ref, kb_ref, *, tq):
    head = pl.program_id(0)
    qi = pl.program_id(1)
    b_tile = bstart_ref[qi, head]
    thr = thr_ref[0]
    j0 = lax.fori_loop(
        0, qi, lambda j, n: n + jnp.where(b_tile - bend_ref[j, head] >= thr, 1, 0), 0)

    @pl.when(qi == 0)
    def _():
        kb_ref[...] = _unpack_bias(kbp_ref[...], sel_ref)

    lane = lax.broadcasted_iota(jnp.int32, (tq, V7X_LANES), 1)
    ones3 = jnp.where(lane < 3, 1.0, 0.0).astype(_BF16)
    q_aug = jnp.concatenate([q_ref[...], ones3], axis=1)
    acc_ref[...] = jnp.zeros_like(acc_ref)

    def logits(j):
        ks = pl.multiple_of(j * tq, tq)
        k_aug = jnp.concatenate([k_ref[pl.ds(ks, tq), :], kb_ref[pl.ds(ks, tq), :]], axis=1)
        return lax.dot_general(k_aug, q_aug, (((1,), (1,)), ((), ())),
                               preferred_element_type=_F32)

    def accumulate(j, s, m_prev, l_prev):
        m_new = jnp.maximum(m_prev, jnp.max(s, axis=0, keepdims=True))
        alpha = jnp.exp2(m_prev - m_new)
        p = jnp.exp2(s - m_new)
        l_new = alpha * l_prev + jnp.sum(p, axis=0, keepdims=True)
        ks = pl.multiple_of(j * tq, tq)
        pv = lax.dot_general(v_ref[pl.ds(ks, tq), :], p.astype(_BF16),
                             (((0,), (0,)), ((), ())), preferred_element_type=_F32)
        acc_ref[...] = alpha * acc_ref[...] + pv
        return m_new, l_new

    def body(j, carry):
        m_prev, l_prev, s = carry
        s_next = logits(j + 1)
        m_new, l_new = accumulate(j, s, m_prev, l_prev)
        return m_new, l_new, s_next

    m0 = jnp.full((1, tq), MASK_VALUE, _F32)
    l0 = jnp.zeros((1, tq), _F32)
    m, l, s = lax.fori_loop(j0, qi, body, (m0, l0, logits(j0)))
    row = lax.broadcasted_iota(jnp.int32, s.shape, 0)
    col = lax.broadcasted_iota(jnp.int32, s.shape, 1)
    m, l = accumulate(qi, jnp.where(row <= col, s, MASK_VALUE), m, l)
    o_ref[...] = (acc_ref[...] / l).T.astype(o_ref.dtype)


def _fox_fast_kernel(bstart_ref, bend_ref, thr_ref, q_ref, qbp_ref, k_ref, kbp_ref, sel_ref, v_ref,
                     o_ref, acc_ref, kb_ref, l_ref, s0_ref, s1_ref, *, tq):
    head = pl.program_id(0)
    qi = pl.program_id(1)
    b_tile = bstart_ref[qi, head]
    thr = thr_ref[0]
    j0 = lax.fori_loop(
        0, qi, lambda j, n: n + jnp.where(b_tile - bend_ref[j, head] >= thr, 1, 0), 0)

    @pl.when(qi == 0)
    def _():
        kb_ref[...] = _unpack_bias(kbp_ref[...], sel_ref)

    q_aug = jnp.concatenate([q_ref[...], _unpack_bias(qbp_ref[...], sel_ref)], axis=1)
    acc_ref[...] = jnp.zeros_like(acc_ref)

    def logits(j):
        ks = pl.multiple_of(j * tq, tq)
        k_aug = jnp.concatenate([k_ref[pl.ds(ks, tq), :], kb_ref[pl.ds(ks, tq), :]], axis=1)
        return _dot_nt(k_aug, q_aug)

    def accumulate(j, s):
        p = jnp.exp2(s)
        ks = pl.multiple_of(j * tq, tq)
        acc_ref[...] += lax.dot_general(v_ref[pl.ds(ks, tq), :], p.astype(_BF16),
                                        (((0,), (0,)), ((), ())), preferred_element_type=_F32)
        l_ref[...] += jnp.sum(p, axis=0, keepdims=True)

    l_ref[...] = jnp.zeros_like(l_ref)
    s0_ref[...] = logits(j0)
    n_before = qi - j0

    def pair(t, carry):
        j = j0 + 2 * t
        s1_ref[...] = logits(j + 1)
        accumulate(j, s0_ref[...])
        s0_ref[...] = logits(j + 2)
        accumulate(j + 1, s1_ref[...])
        return carry

    lax.fori_loop(0, n_before // 2, pair, 0)

    @pl.when(n_before % 2 == 1)
    def _():
        s1_ref[...] = logits(qi)
        accumulate(qi - 1, s0_ref[...])
        s0_ref[...] = s1_ref[...]

    row = lax.broadcasted_iota(jnp.int32, s0_ref.shape, 0)
    col = lax.broadcasted_iota(jnp.int32, s0_ref.shape, 1)
    accumulate(qi, jnp.where(row <= col, s0_ref[...], MASK_VALUE))
    o_ref[...] = (acc_ref[...] / l_ref[...]).T.astype(o_ref.dtype)


def _fox_attention(qk, kb, qb, b_start, b_end, qk_bound, v, tq):
    s = v.shape[0]
    hd = FOX_HEAD_DIM
    lanes = V7X_LANES
    thr = (2.0 * qk_bound + FOX_SKIP_BITS).reshape(1).astype(_F32)
    sel = np.zeros((FOX_HEADS, lanes, lanes), np.float32)
    for head in range(FOX_HEADS):
        for c in range(FOX_BIAS_LANES):
            sel[head, FOX_BIAS_LANES * head + c, c] = 1.0
    sel = jnp.asarray(sel, _BF16)
    smem = pl.BlockSpec(memory_space=pltpu.SMEM)
    q_spec = pl.BlockSpec((tq, hd), lambda h, i: (i, h))
    k_spec = pl.BlockSpec((s, hd), lambda h, i: (0, FOX_HEADS + h))
    v_spec = pl.BlockSpec((s, hd), lambda h, i: (0, h))
    kb_spec = pl.BlockSpec((s, lanes), lambda h, i: (0, 0))
    qb_spec = pl.BlockSpec((tq, lanes), lambda h, i: (i, 0))
    sel_spec = pl.BlockSpec((None, lanes, lanes), lambda h, i: (h, 0, 0))

    def call(kernel, in_specs, extra_scratch, *args):
        return pl.pallas_call(
            functools.partial(kernel, tq=tq),
            out_shape=jax.ShapeDtypeStruct((s, FOX_WIDTH), _BF16),
            grid=(FOX_HEADS, s // tq),
            in_specs=[smem, smem, smem] + in_specs,
            out_specs=q_spec,
            scratch_shapes=[pltpu.VMEM((hd, tq), _F32),
                            pltpu.VMEM((s, lanes), _BF16)] + extra_scratch,
            compiler_params=_params(("parallel", "arbitrary"), 32),
            name=kernel.__name__.strip("_"),
        )(b_start, b_end, thr, *args)

    fast_scratch = [pltpu.VMEM((1, tq), _F32), pltpu.VMEM((tq, tq), _F32),
                    pltpu.VMEM((tq, tq), _F32)]
    return lax.cond(
        qk_bound <= FOX_FAST_MAX_LOGIT,
        lambda: call(_fox_fast_kernel, [q_spec, qb_spec, k_spec, kb_spec, sel_spec, v_spec],
                     fast_scratch, qk, qb, qk, kb, sel, v),
        lambda: call(_fox_kernel, [q_spec, k_spec, kb_spec, sel_spec, v_spec], [],
                     qk, qk, kb, sel, v))


def _shift_rows(ext, d):
    return pltpu.roll(ext, d, axis=0)[V7X_SUBLANES:]


def _lru_kernel(lx_ref, lg_ref, wc_ref, bc_ref, wg_ref, bg_ref, lam_ref, o_ref,
                halo_ref, carry_ref, a_ref, h_ref):
    @pl.when(pl.program_id(0) == 0)
    def _():
        halo_ref[...] = jnp.zeros_like(halo_ref)
        carry_ref[...] = jnp.zeros_like(carry_ref)

    lx = lx_ref[...]
    ts = lx.shape[0]
    ext = jnp.concatenate([halo_ref[...], lx], axis=0)
    xr = (wc_ref[3:4, :] * lx + wc_ref[2:3, :] * _shift_rows(ext, 1)
          + wc_ref[1:2, :] * _shift_rows(ext, 2) + wc_ref[0:1, :] * _shift_rows(ext, 3)
          + bc_ref[...])
    halo_ref[...] = lx[ts - V7X_SUBLANES:]

    lam = lam_ref[...]
    log_sig_lam = jnp.minimum(lam, 0.0) - jnp.log1p(jnp.exp(-jnp.abs(lam)))
    bd = LRU_BLOCK_DIM
    for nb in range(LRU_BLOCKS):
        sl = slice(nb * bd, (nb + 1) * bd)
        x_nb = xr[:, sl]
        gates = jax.nn.sigmoid(jnp.dot(x_nb.astype(_BF16), wg_ref[nb],
                                       preferred_element_type=_F32) + bg_ref[nb])
        log_a = LRU_C * gates[:, :bd] * log_sig_lam[:, sl]
        a_ref[:, sl] = jnp.exp(log_a)
        t = jnp.tanh(log_a)
        h_ref[:, sl] = jnp.sqrt(-2.0 * t / (1.0 - t)) * (gates[:, bd:] * x_nb)

    sub = lax.broadcasted_iota(jnp.int32, (V7X_SUBLANES, 1), 0)

    def group(g, carry):
        rows = pl.ds(pl.multiple_of(g * V7X_SUBLANES, V7X_SUBLANES), V7X_SUBLANES)
        a = a_ref[rows, :]
        h = h_ref[rows, :]
        for d in (1, 2, 4):
            valid = sub >= d
            h = h + a * jnp.where(valid, pltpu.roll(h, d, axis=0), 0.0)
            a = a * jnp.where(valid, pltpu.roll(a, d, axis=0), 1.0)
        h = h + a * carry
        h_ref[rows, :] = h
        return jnp.broadcast_to(h[V7X_SUBLANES - 1:, :], h.shape)

    carry_ref[...] = lax.fori_loop(0, ts // V7X_SUBLANES, group, carry_ref[...],
                                   unroll=LRU_SCAN_UNROLL)
    o_ref[...] = (h_ref[...] * jax.nn.gelu(lg_ref[...])).astype(o_ref.dtype)


def _lru_branch(lxlg, w_conv, b_conv, w_gate, b_gate, lam, ts):
    s = lxlg.shape[0]
    w = LRU_WIDTH
    bd = LRU_BLOCK_DIM
    return pl.pallas_call(
        _lru_kernel,
        out_shape=jax.ShapeDtypeStruct((s, w), _BF16),
        grid=(s // ts,),
        in_specs=[pl.BlockSpec((ts, w), lambda i: (i, 0)),
                  pl.BlockSpec((ts, w), lambda i: (i, 1)),
                  pl.BlockSpec((LRU_CONV, w), lambda i: (0, 0)),
                  pl.BlockSpec((1, w), lambda i: (0, 0)),
                  pl.BlockSpec((LRU_BLOCKS, bd, 2 * bd), lambda i: (0, 0, 0)),
                  pl.BlockSpec((LRU_BLOCKS, 1, 2 * bd), lambda i: (0, 0, 0)),
                  pl.BlockSpec((1, w), lambda i: (0, 0))],
        out_specs=pl.BlockSpec((ts, w), lambda i: (i, 0)),
        scratch_shapes=[pltpu.VMEM((V7X_SUBLANES, w), _F32), pltpu.VMEM((V7X_SUBLANES, w), _F32),
                        pltpu.VMEM((ts, w), _F32), pltpu.VMEM((ts, w), _F32)],
        compiler_params=_params(("arbitrary",), 40),
        name="conv_rglru",
    )(lxlg, lxlg, w_conv, b_conv.reshape(1, -1), w_gate, b_gate, lam.reshape(1, -1))


def _mem_attn_kernel(q_ref, k_ref, v_ref, o_ref):
    s = lax.dot_general(q_ref[...], k_ref[...], (((1,), (1,)), ((), ())),
                        preferred_element_type=_F32)
    m = jnp.max(s, axis=-1, keepdims=True)
    p = jnp.exp(s - m)
    l = jnp.sum(p, axis=-1, keepdims=True)
    acc = jnp.dot(p.astype(_BF16), v_ref[...], preferred_element_type=_F32)
    o_ref[...] = (acc / l).astype(o_ref.dtype)


def _mem_attention(q, k, v, ts):
    s = q.shape[0]
    m = k.shape[0]
    hd = MEM_HEAD_DIM
    return pl.pallas_call(
        _mem_attn_kernel,
        out_shape=jax.ShapeDtypeStruct((s, MEM_WIDTH), _BF16),
        grid=(s // ts, MEM_HEADS),
        in_specs=[pl.BlockSpec((ts, hd), lambda i, h: (i, h)),
                  pl.BlockSpec((m, hd), lambda i, h: (0, h)),
                  pl.BlockSpec((m, hd), lambda i, h: (0, h))],
        out_specs=pl.BlockSpec((ts, hd), lambda i, h: (i, h)),
        compiler_params=_params(("parallel", "arbitrary"), 32),
        name="mem_attention",
    )(q, k, v)


def _merge_kernel(h_ref, yf_ref, yl_ref, ym_ref, wg0_ref, wg1_ref, wg2_ref, wb_ref, bg_ref,
                  o_ref):
    h = h_ref[...]
    merged = None
    for n, (y_ref, wg_ref) in enumerate(((yf_ref, wg0_ref), (yl_ref, wg1_ref),
                                         (ym_ref, wg2_ref))):
        gate = jax.nn.sigmoid(_dot_nt(h, wg_ref[...]) + bg_ref[n:n + 1, :])
        term = gate * jnp.dot(y_ref[...], wb_ref[n], preferred_element_type=_F32)
        merged = term if merged is None else merged + term
    o_ref[...] = merged.astype(o_ref.dtype)


def _gated_merge(h, y_fox, y_lru, y_mem, w_gate_t, w_branch, b_gate, tm, tn):
    s, d = h.shape
    nj = d // tn
    bw = y_fox.shape[1]
    y_spec = pl.BlockSpec((tm, bw), lambda i, j: (i, 0))
    return pl.pallas_call(
        _merge_kernel,
        out_shape=jax.ShapeDtypeStruct((s, d), _BF16),
        grid=(s // tm, nj),
        in_specs=[pl.BlockSpec((tm, d), lambda i, j: (i, 0)), y_spec, y_spec, y_spec,
                  pl.BlockSpec((tn, d), lambda i, j: (j, 0)),
                  pl.BlockSpec((tn, d), lambda i, j: (nj + j, 0)),
                  pl.BlockSpec((tn, d), lambda i, j: (2 * nj + j, 0)),
                  pl.BlockSpec((N_BRANCH, bw, tn), lambda i, j: (0, 0, j)),
                  pl.BlockSpec((N_BRANCH, tn), lambda i, j: (0, j))],
        out_specs=pl.BlockSpec((tm, tn), lambda i, j: (i, j)),
        compiler_params=_params(("parallel", "arbitrary"), 56),
        name="gated_merge",
    )(h, y_fox, y_lru, y_mem, w_gate_t, w_gate_t, w_gate_t, w_branch, b_gate)


def _out_norm_kernel(a_ref, w_ref, x_ref, g_ref, x2_ref, h2_ref):
    x2 = x_ref[...] + jnp.dot(a_ref[...], w_ref[...], preferred_element_type=_F32)
    x2_ref[...] = x2
    ms = jnp.mean(x2 * x2, axis=-1, keepdims=True)
    h2_ref[...] = (x2 * lax.rsqrt(ms + EPS) * g_ref[...]).astype(h2_ref.dtype)


def _out_proj_norm(a, w, x, g, tm):
    m, k = a.shape
    d = w.shape[1]
    row = lambda i: (i, 0)
    return pl.pallas_call(
        _out_norm_kernel,
        out_shape=(jax.ShapeDtypeStruct((m, d), _F32), jax.ShapeDtypeStruct((m, d), _BF16)),
        grid=(m // tm,),
        in_specs=[pl.BlockSpec((tm, k), row), pl.BlockSpec((k, d), lambda i: (0, 0)),
                  pl.BlockSpec((tm, d), row), pl.BlockSpec((1, d), lambda i: (0, 0))],
        out_specs=(pl.BlockSpec((tm, d), row), pl.BlockSpec((tm, d), row)),
        compiler_params=_params(("parallel",), 48),
        name="proj_out_norm",
    )(a, w, x, g.reshape(1, d))


def _cast_kernel(w_ref, o_ref):
    o_ref[...] = w_ref[...].astype(o_ref.dtype)


def _cast_rows(w, row0, rows, tr):
    k = w.shape[1]
    assert row0 % V7X_SUBLANES == 0 and rows % tr == 0
    return pl.pallas_call(
        _cast_kernel,
        out_shape=jax.ShapeDtypeStruct((rows, k), _BF16),
        grid=(rows // tr,),
        in_specs=[pl.BlockSpec((pl.Element(tr), pl.Element(k)),
                               lambda i: (pl.multiple_of(row0 + i * tr, V7X_SUBLANES), 0))],
        out_specs=pl.BlockSpec((tr, k), lambda i: (i, 0)),
        compiler_params=_params(("parallel",), 32),
        name="cast_rows",
    )(w)


def _ffn_up_kernel(a_ref, wa_ref, wv_ref, wca_ref, wcv_ref, bca_ref, bcv_ref, o_ref,
                   halo_a_ref, halo_v_ref, wa_bf_ref, wv_bf_ref):
    @pl.when(pl.program_id(1) == 0)
    def _():
        halo_a_ref[...] = jnp.zeros_like(halo_a_ref)
        halo_v_ref[...] = jnp.zeros_like(halo_v_ref)
        wa_bf_ref[...] = wa_ref[...].astype(_BF16)
        wv_bf_ref[...] = wv_ref[...].astype(_BF16)

    a = a_ref[...]
    tm = a.shape[0]

    def conv(up, halo_ref, wc_ref, bc_ref):
        ext = jnp.concatenate([halo_ref[...], up], axis=0)
        halo_ref[...] = up[tm - V7X_SUBLANES:]
        return (wc_ref[2:3, :] * up + wc_ref[1:2, :] * _shift_rows(ext, 1)
                + wc_ref[0:1, :] * _shift_rows(ext, 2) + bc_ref[...])

    act = conv(jnp.dot(a, wa_bf_ref[...], preferred_element_type=_F32), halo_a_ref, wca_ref,
               bca_ref)
    val = conv(jnp.dot(a, wv_bf_ref[...], preferred_element_type=_F32), halo_v_ref, wcv_ref,
               bcv_ref)
    o_ref[...] = (jax.nn.gelu(act) * val).astype(o_ref.dtype)


def _ffn_up(h2, w_up, w_conv, b_conv, tm, tn):
    s, d = h2.shape
    f = FFN_HIDDEN
    nj = f // tn
    return pl.pallas_call(
        _ffn_up_kernel,
        out_shape=jax.ShapeDtypeStruct((s, f), _BF16),
        grid=(nj, s // tm),
        in_specs=[pl.BlockSpec((tm, d), lambda j, i: (i, 0)),
                  pl.BlockSpec((d, tn), lambda j, i: (0, j)),
                  pl.BlockSpec((d, tn), lambda j, i: (0, nj + j)),
                  pl.BlockSpec((FFN_CONV, tn), lambda j, i: (0, j)),
                  pl.BlockSpec((FFN_CONV, tn), lambda j, i: (0, nj + j)),
                  pl.BlockSpec((1, tn), lambda j, i: (0, j)),
                  pl.BlockSpec((1, tn), lambda j, i: (0, nj + j))],
        out_specs=pl.BlockSpec((tm, tn), lambda j, i: (i, j)),
        scratch_shapes=[pltpu.VMEM((V7X_SUBLANES, tn), _F32),
                        pltpu.VMEM((V7X_SUBLANES, tn), _F32),
                        pltpu.VMEM((d, tn), _BF16), pltpu.VMEM((d, tn), _BF16)],
        compiler_params=_params(("parallel", "arbitrary"), 56),
        name="ffn_up_conv_geglu",
    )(h2, w_up, w_up, w_conv, w_conv, b_conv.reshape(1, -1), b_conv.reshape(1, -1))


def _layer(x, mem, g_mix, w_in, b_f, g_q_fox, g_k_fox, w_lru_conv, b_lru_conv, w_rg_a, b_rg_a,
           w_rg_x, b_rg_x, lru_lambda, g_mem, w_mem_kv, g_q_mem, g_k_mem, b_gate, w_branch,
           w_out, g_ffn, w_ffn_up, w_ffn_conv, b_ffn_conv, w_ffn_down):
    s = x.shape[0]
    c_k = 2 * FOX_WIDTH
    c_v = c_k + FOX_WIDTH
    c_f = c_v + FOX_HEADS
    c_l = c_f + 2 * LRU_WIDTH
    c_m = c_l + MEM_WIDTH

    h = _rmsnorm(x, g_mix, 512)
    w_in_t = w_in.T

    g_q_scaled = g_q_fox * (LOG2_E * FOX_HEAD_DIM ** -0.5)
    gain_qk = jnp.concatenate([jnp.tile(g_q_scaled, FOX_HEADS),
                               jnp.tile(g_k_fox, FOX_HEADS)]).reshape(1, -1)
    qk = _matmul(h, w_in_t, w_t=True, n=c_k, tm=1024, tn=1024, out_dtype=_BF16,
                 epilogue="gnorm", extra=gain_qk, group=FOX_HEAD_DIM, name="proj_qk")
    v = _matmul(h, w_in_t, w_t=True, n=FOX_WIDTH, w_off=c_k, tm=1024, tn=1024,
                out_dtype=_BF16, name="proj_v")
    b_pad = jnp.pad(b_f.reshape(1, -1), ((0, 0), (0, V7X_LANES - FOX_HEADS)))
    qk_bound = (1.02 * FOX_HEAD_DIM) * jnp.max(jnp.abs(g_q_scaled)) * jnp.max(jnp.abs(g_k_fox))
    kb, qb, b_start, b_end = _forget_bias(h, w_in_t, c_v, b_pad, qk_bound, FOX_TILE)
    y_fox = _fox_attention(qk, kb, qb, b_start, b_end, qk_bound, v, FOX_TILE)

    lxlg = _matmul(h, w_in_t, w_t=True, n=2 * LRU_WIDTH, w_off=c_f, tm=1024, tn=1024,
                   out_dtype=_F32, name="proj_lru")
    w_gate_lru = jnp.concatenate([w_rg_a, w_rg_x], axis=-1).astype(_BF16)
    b_gate_lru = jnp.concatenate([b_rg_a, b_rg_x], axis=-1).reshape(LRU_BLOCKS, 1, -1)
    y_lru = _lru_branch(lxlg, w_lru_conv, b_lru_conv, w_gate_lru, b_gate_lru, lru_lambda, 512)

    gain_mq = (jnp.tile(g_q_mem, MEM_HEADS) * (MEM_HEAD_DIM ** -0.5)).reshape(1, -1)
    mq = _matmul(h, w_in_t, w_t=True, n=MEM_WIDTH, w_off=c_l, tm=1024, tn=1024,
                 out_dtype=_BF16, epilogue="gnorm", extra=gain_mq, group=MEM_HEAD_DIM,
                 name="proj_mq")
    hm = _rmsnorm(mem, g_mem, mem.shape[0])
    mk = _matmul(hm, w_mem_kv, n=MEM_WIDTH, tm=mem.shape[0], tn=512, out_dtype=_BF16,
                 epilogue="gnorm", extra=jnp.tile(g_k_mem, MEM_HEADS).reshape(1, -1),
                 group=MEM_HEAD_DIM, name="proj_mk")
    mv = _matmul(hm, w_mem_kv, n=MEM_WIDTH, w_off=MEM_WIDTH, tm=mem.shape[0], tn=512,
                 out_dtype=_BF16, name="proj_mv")
    y_mem = _mem_attention(mq, mk, mv, 1024)

    w_gate_t = _cast_rows(w_in_t, c_m, N_BRANCH * D_MODEL, 512)
    merged = _gated_merge(h, y_fox, y_lru, y_mem, w_gate_t, w_branch.astype(_BF16), b_gate,
                          1024, 256)
    x2, h2 = _out_proj_norm(merged, w_out.astype(_BF16), x, g_ffn, 512)

    g = _ffn_up(h2, w_ffn_up, w_ffn_conv, b_ffn_conv, 1024, 512)
    return _matmul(g, w_ffn_down.astype(_BF16), n=D_MODEL, tm=1024, tn=512, rows_outer=True,
                   out_dtype=_F32, epilogue="residual", extra=x2, name="ffn_down")


def kernel(x, mem, g_mix, w_in, b_f, g_q_fox, g_k_fox, w_lru_conv, b_lru_conv, w_rg_a, b_rg_a,
           w_rg_x, b_rg_x, lru_lambda, g_mem, w_mem_kv, g_q_mem, g_k_mem, b_gate, w_branch,
           w_out, g_ffn, w_ffn_up, w_ffn_conv, b_ffn_conv, w_ffn_down):
    depth = g_mix.shape[0]
    outs = []
    for b in range(x.shape[0]):
        xb = x[b]
        for l in range(depth):
            xb = _layer(xb, mem[b], g_mix[l], w_in[l], b_f[l], g_q_fox[l], g_k_fox[l],
                        w_lru_conv[l], b_lru_conv[l], w_rg_a[l], b_rg_a[l], w_rg_x[l],
                        b_rg_x[l], lru_lambda[l], g_mem[l], w_mem_kv[l], g_q_mem[l],
                        g_k_mem[l], b_gate[l], w_branch[l], w_out[l], g_ffn[l], w_ffn_up[l],
                        w_ffn_conv[l], b_ffn_conv[l], w_ffn_down[l])
        outs.append(xb)
    return outs[0][None] if len(outs) == 1 else jnp.stack(outs)
```

```python
import functools

import jax
import jax.numpy as jnp
import numpy as np
from jax import lax
from jax.experimental import pallas as pl
from jax.experimental.pallas import tpu as pltpu

D_MODEL = 2048
FOX_HEADS = 8
FOX_HEAD_DIM = 128
FOX_WIDTH = FOX_HEADS * FOX_HEAD_DIM
LRU_WIDTH = 1024
LRU_BLOCKS = 8
LRU_BLOCK_DIM = LRU_WIDTH // LRU_BLOCKS
LRU_CONV = 4
LRU_C = 8.0
MEM_HEADS = 4
MEM_HEAD_DIM = 256
MEM_WIDTH = MEM_HEADS * MEM_HEAD_DIM
N_BRANCH = 3
FFN_HIDDEN = 5632
FFN_CONV = 3
EPS = 1e-6

V7X_SUBLANES = 8
V7X_LANES = 128
MASK_VALUE = -1e30
LOG2_E = 1.4426950408889634
FOX_TILE = 512
FOX_SKIP_BITS = 64.0
FOX_BIAS_LANES = 16
FOX_FAST_MAX_LOGIT = 48.0
LRU_SCAN_UNROLL = 8

_BF16 = jnp.bfloat16
_F32 = jnp.float32


def _params(semantics, vmem_mib):
    return pltpu.CompilerParams(dimension_semantics=semantics,
                                vmem_limit_bytes=vmem_mib * 1024 * 1024)


def _rmsnorm_kernel(x_ref, g_ref, o_ref):
    x = x_ref[...]
    ms = jnp.mean(x * x, axis=-1, keepdims=True)
    o_ref[...] = (x * lax.rsqrt(ms + EPS) * g_ref[...]).astype(o_ref.dtype)


def _rmsnorm(x, g, tm):
    m, d = x.shape
    return pl.pallas_call(
        _rmsnorm_kernel,
        out_shape=jax.ShapeDtypeStruct((m, d), _BF16),
        grid=(m // tm,),
        in_specs=[pl.BlockSpec((tm, d), lambda i: (i, 0)),
                  pl.BlockSpec((1, d), lambda i: (0, 0))],
        out_specs=pl.BlockSpec((tm, d), lambda i: (i, 0)),
        compiler_params=_params(("parallel",), 32),
        name="rmsnorm",
    )(x, g.reshape(1, d))


def _dot_nt(a, w_t):
    return lax.dot_general(a, w_t, (((1,), (1,)), ((), ())), preferred_element_type=_F32)


def _mm_kernel(*refs, epilogue, group, cast_w, w_t):
    a_ref, w_ref = refs[0], refs[1]
    if cast_w:
        o_ref, wb_ref = refs[-2], refs[-1]

        @pl.when(pl.program_id(1) == 0)
        def _():
            wb_ref[...] = w_ref[...].astype(_BF16)

        w = wb_ref[...]
    else:
        o_ref = refs[-1]
        w = w_ref[...]
    if w_t:
        acc = _dot_nt(a_ref[...], w)
    else:
        acc = jnp.dot(a_ref[...], w, preferred_element_type=_F32)
    if epilogue == "gnorm":
        g_ref = refs[2]
        for gi in range(acc.shape[1] // group):
            sl = slice(gi * group, (gi + 1) * group)
            blk = acc[:, sl]
            ms = jnp.mean(blk * blk, axis=-1, keepdims=True)
            o_ref[:, sl] = (blk * lax.rsqrt(ms + EPS) * g_ref[:, sl]).astype(o_ref.dtype)
    elif epilogue == "residual":
        o_ref[...] = (refs[2][...] + acc).astype(o_ref.dtype)
    else:
        o_ref[...] = acc.astype(o_ref.dtype)


def _matmul(a, w, *, n, tm, tn, out_dtype, w_t=False, w_off=0, rows_outer=False,
            epilogue="plain", extra=None, group=None, name):
    m, k = a.shape
    cast_w = w.dtype != _BF16
    assert not (cast_w and rows_outer) and n % tn == 0 and m % tm == 0
    if rows_outer:
        grid = (m // tm, n // tn)
        ij = lambda i, j: (i, j)
    else:
        grid = (n // tn, m // tm)
        ij = lambda j, i: (i, j)
    if w_t:
        assert w_off % V7X_SUBLANES == 0
        w_block = (tn, k)
        w_spec = pl.BlockSpec((pl.Element(tn), pl.Element(k)),
                              lambda *g: (pl.multiple_of(w_off + ij(*g)[1] * tn, V7X_SUBLANES), 0))
    else:
        assert w_off % tn == 0
        w_block = (k, tn)
        w_spec = pl.BlockSpec(w_block, lambda *g: (0, w_off // tn + ij(*g)[1]))
    in_specs = [pl.BlockSpec((tm, k), lambda *g: (ij(*g)[0], 0)), w_spec]
    args = [a, w]
    block_bytes = (tm * k * 2 + k * tn * w.dtype.itemsize
                   + tm * tn * jnp.dtype(out_dtype).itemsize)
    if epilogue == "gnorm":
        in_specs.append(pl.BlockSpec((1, tn), lambda *g: (0, ij(*g)[1])))
        args.append(extra)
    elif epilogue == "residual":
        in_specs.append(pl.BlockSpec((tm, tn), lambda *g: ij(*g)))
        args.append(extra)
        block_bytes += tm * tn * extra.dtype.itemsize
    scratch = [pltpu.VMEM(w_block, _BF16)] if cast_w else []
    vmem_mib = -(-(2 * block_bytes + cast_w * k * tn * 2 + 2 * tm * tn * 4) // 2 ** 20) + 2
    return pl.pallas_call(
        functools.partial(_mm_kernel, epilogue=epilogue, group=group, cast_w=cast_w, w_t=w_t),
        out_shape=jax.ShapeDtypeStruct((m, n), out_dtype),
        grid=grid,
        in_specs=in_specs,
        out_specs=pl.BlockSpec((tm, tn), lambda *g: ij(*g)),
        scratch_shapes=scratch,
        compiler_params=_params(("parallel", "arbitrary"), vmem_mib),
        name=name,
    )(*args)


def _split3(x):
    hi = x.astype(_BF16)
    r1 = x - hi.astype(_F32)
    mid = r1.astype(_BF16)
    lo = (r1 - mid.astype(_F32)).astype(_BF16)
    return hi, mid, lo


def _forget_bias_kernel(shift_ref, x_ref, g_ref, w_ref, b_ref, sel_ref, ones_ref, h_ref, kb_ref,
                        qb_ref, edge_ref, carry_ref):
    @pl.when(pl.program_id(0) == 0)
    def _():
        carry_ref[...] = jnp.zeros_like(carry_ref)

    tc = x_ref.shape[0]
    x = x_ref[...]
    ms = jnp.mean(x * x, axis=-1, keepdims=True)
    h = (x * lax.rsqrt(ms + EPS) * g_ref[...]).astype(_BF16)
    h_ref[...] = h
    w = jnp.concatenate([w_ref[...], jnp.zeros((V7X_LANES - FOX_HEADS, w_ref.shape[1]), _F32)],
                        axis=0).astype(_BF16)
    z = _dot_nt(h, w) + b_ref[...]
    neg_log_f = (jnp.log1p(jnp.exp(-jnp.abs(z))) - jnp.minimum(z, 0.0)) * LOG2_E
    row = lax.broadcasted_iota(jnp.int32, (tc, tc), 0)
    col = lax.broadcasted_iota(jnp.int32, (tc, tc), 1)
    tri = jnp.where(col <= row, 1.0, 0.0).astype(_BF16)
    c = carry_ref[0:1, :]
    for part in _split3(neg_log_f):
        c = c + jnp.dot(tri, part, preferred_element_type=_F32)
    carry_ref[...] = jnp.broadcast_to(c[tc - 1:tc, :], carry_ref.shape)
    edge_ref[...] = jnp.concatenate(
        [c[0:1, :], c[tc - 1:tc, :], jnp.zeros((V7X_SUBLANES - 2, c.shape[1]), _F32)], axis=0)
    parts = jnp.concatenate(_split3(c) + _split3(-(c + shift_ref[0])), axis=1)
    routed = jnp.dot(parts, sel_ref[...], preferred_element_type=_F32) + ones_ref[...]
    half = kb_ref.shape[1]
    kb_ref[...] = routed[:, :half].astype(kb_ref.dtype)
    qb_ref[...] = routed[:, half:].astype(qb_ref.dtype)


def _norm_forget_bias(x, g, w_t, w_row0, b_pad, shift, tc):
    s, d = x.shape
    lanes = V7X_LANES
    width = lanes
    nt = s // tc
    sel = np.zeros((6 * lanes, 2 * width), np.float32)
    ones = np.zeros((1, 2 * width), np.float32)
    for head in range(FOX_HEADS):
        for part in range(3):
            sel[part * lanes + head, FOX_BIAS_LANES * head + part] = 1.0
            sel[(3 + part) * lanes + head, width + FOX_BIAS_LANES * head + 3 + part] = 1.0
            ones[0, FOX_BIAS_LANES * head + 3 + part] = 1.0
            ones[0, width + FOX_BIAS_LANES * head + part] = 1.0
    h, kb, qb, edges = pl.pallas_call(
        _forget_bias_kernel,
        out_shape=(jax.ShapeDtypeStruct((s, d), _BF16),
                   jax.ShapeDtypeStruct((s, width), _BF16),
                   jax.ShapeDtypeStruct((s, width), _BF16),
                   jax.ShapeDtypeStruct((nt * V7X_SUBLANES, lanes), _F32)),
        grid=(nt,),
        in_specs=[pl.BlockSpec(memory_space=pltpu.SMEM),
                  pl.BlockSpec((tc, d), lambda i: (i, 0)),
                  pl.BlockSpec((1, d), lambda i: (0, 0)),
                  pl.BlockSpec((pl.Element(FOX_HEADS), pl.Element(d)), lambda i: (w_row0, 0)),
                  pl.BlockSpec((1, lanes), lambda i: (0, 0)),
                  pl.BlockSpec((6 * lanes, 2 * width), lambda i: (0, 0)),
                  pl.BlockSpec((1, 2 * width), lambda i: (0, 0))],
        out_specs=(pl.BlockSpec((tc, d), lambda i: (i, 0)),
                   pl.BlockSpec((tc, width), lambda i: (i, 0)),
                   pl.BlockSpec((tc, width), lambda i: (i, 0)),
                   pl.BlockSpec((V7X_SUBLANES, lanes), lambda i: (i, 0))),
        scratch_shapes=[pltpu.VMEM((V7X_SUBLANES, lanes), _F32)],
        compiler_params=_params(("arbitrary",), 40),
        name="norm_forget_bias",
    )(shift.reshape(1).astype(_F32), x, g.reshape(1, d), w_t, b_pad, jnp.asarray(sel, _BF16),
      jnp.asarray(ones))
    edges = edges.reshape(nt, V7X_SUBLANES, lanes)
    return h, kb, qb, edges[:, 0, :FOX_HEADS], edges[:, 1, :FOX_HEADS]


def _unpack_bias(packed, sel_ref):
    return jnp.dot(packed, sel_ref[...], preferred_element_type=_F32).astype(_BF16)


def _fox_kernel(bstart_ref, bend_ref, thr_ref, q_ref, k_ref, kbp_ref, sel_ref, v_ref, o_ref,
                acc_ref, kb_ref, *, tq):
    head = pl.program_id(0)
    qi = pl.program_id(1)
    b_tile = bstart_ref[qi, head]
    thr = thr_ref[0]
    j0 = lax.fori_loop(
        0, qi, lambda j, n: n + jnp.where(b_tile - bend_ref[j, head] >= thr, 1, 0), 0)

    @pl.when(qi == 0)
    def _():
        kb_ref[...] = _unpack_bias(kbp_ref[...], sel_ref)

    lane = lax.broadcasted_iota(jnp.int32, (tq, V7X_LANES), 1)
    ones3 = jnp.where(lane < 3, 1.0, 0.0).astype(_BF16)
    q_aug = jnp.concatenate([q_ref[...], ones3], axis=1)
    acc_ref[...] = jnp.zeros_like(acc_ref)

    def logits(j):
        ks = pl.multiple_of(j * tq, tq)
        k_aug = jnp.concatenate([k_ref[pl.ds(ks, tq), :], kb_ref[pl.ds(ks, tq), :]], axis=1)
        return lax.dot_general(k_aug, q_aug, (((1,), (1,)), ((), ())),
                               preferred_element_type=_F32)

    def accumulate(j, s, m_prev, l_prev):
        m_new = jnp.maximum(m_prev, jnp.max(s, axis=0, keepdims=True))
        alpha = jnp.exp2(m_prev - m_new)
        p = jnp.exp2(s - m_new)
        l_new = alpha * l_prev + jnp.sum(p, axis=0, keepdims=True)
        ks = pl.multiple_of(j * tq, tq)
        pv = lax.dot_general(v_ref[pl.ds(ks, tq), :], p.astype(_BF16),
                             (((0,), (0,)), ((), ())), preferred_element_type=_F32)
        acc_ref[...] = alpha * acc_ref[...] + pv
        return m_new, l_new

    def body(j, carry):
        m_prev, l_prev, s = carry
        s_next = logits(j + 1)
        m_new, l_new = accumulate(j, s, m_prev, l_prev)
        return m_new, l_new, s_next

    m0 = jnp.full((1, tq), MASK_VALUE, _F32)
    l0 = jnp.zeros((1, tq), _F32)
    m, l, s = lax.fori_loop(j0, qi, body, (m0, l0, logits(j0)))
    row = lax.broadcasted_iota(jnp.int32, s.shape, 0)
    col = lax.broadcasted_iota(jnp.int32, s.shape, 1)
    m, l = accumulate(qi, jnp.where(row <= col, s, MASK_VALUE), m, l)
    o_ref[...] = (acc_ref[...] / l).T.astype(o_ref.dtype)


def _fox_fast_kernel(bstart_ref, bend_ref, thr_ref, q_ref, qbp_ref, k_ref, kbp_ref, sel_ref, v_ref,
                     o_ref, acc_ref, kb_ref, l_ref, s0_ref, s1_ref, *, tq):
    head = pl.program_id(0)
    qi = pl.program_id(1)
    b_tile = bstart_ref[qi, head]
    thr = thr_ref[0]
    j0 = lax.fori_loop(
        0, qi, lambda j, n: n + jnp.where(b_tile - bend_ref[j, head] >= thr, 1, 0), 0)

    @pl.when(qi == 0)
    def _():
        kb_ref[...] = _unpack_bias(kbp_ref[...], sel_ref)

    q_aug = jnp.concatenate([q_ref[...], _unpack_bias(qbp_ref[...], sel_ref)], axis=1)
    acc_ref[...] = jnp.zeros_like(acc_ref)

    def logits(j):
        ks = pl.multiple_of(j * tq, tq)
        k_aug = jnp.concatenate([k_ref[pl.ds(ks, tq), :], kb_ref[pl.ds(ks, tq), :]], axis=1)
        return _dot_nt(k_aug, q_aug)

    def accumulate(j, s):
        p = jnp.exp2(s)
        ks = pl.multiple_of(j * tq, tq)
        acc_ref[...] += lax.dot_general(v_ref[pl.ds(ks, tq), :], p.astype(_BF16),
                                        (((0,), (0,)), ((), ())), preferred_element_type=_F32)
        l_ref[...] += jnp.sum(p, axis=0, keepdims=True)

    l_ref[...] = jnp.zeros_like(l_ref)
    row = lax.broadcasted_iota(jnp.int32, s1_ref.shape, 0)
    col = lax.broadcasted_iota(jnp.int32, s1_ref.shape, 1)
    s_diag = jnp.where(row <= col, logits(qi), MASK_VALUE)
    s0_ref[...] = logits(j0)
    accumulate(qi, s_diag)
    n_before = qi - j0

    def pair(t, carry):
        j = j0 + 2 * t
        s1_ref[...] = logits(j + 1)
        accumulate(j, s0_ref[...])
        s0_ref[...] = logits(j + 2)
        accumulate(j + 1, s1_ref[...])
        return carry

    lax.fori_loop(0, n_before // 2, pair, 0)

    @pl.when(n_before % 2 == 1)
    def _():
        accumulate(qi - 1, s0_ref[...])

    o_ref[...] = (acc_ref[...] / l_ref[...]).T.astype(o_ref.dtype)


def _fox_attention(qk, kb, qb, b_start, b_end, qk_bound, v, tq):
    s = v.shape[0]
    hd = FOX_HEAD_DIM
    lanes = V7X_LANES
    thr = (2.0 * qk_bound + FOX_SKIP_BITS).reshape(1).astype(_F32)
    sel = np.zeros((FOX_HEADS, lanes, lanes), np.float32)
    for head in range(FOX_HEADS):
        for c in range(FOX_BIAS_LANES):
            sel[head, FOX_BIAS_LANES * head + c, c] = 1.0
    sel = jnp.asarray(sel, _BF16)
    smem = pl.BlockSpec(memory_space=pltpu.SMEM)
    q_spec = pl.BlockSpec((tq, hd), lambda h, i: (i, h))
    k_spec = pl.BlockSpec((s, hd), lambda h, i: (0, FOX_HEADS + h))
    v_spec = pl.BlockSpec((s, hd), lambda h, i: (0, h))
    kb_spec = pl.BlockSpec((s, lanes), lambda h, i: (0, 0))
    qb_spec = pl.BlockSpec((tq, lanes), lambda h, i: (i, 0))
    sel_spec = pl.BlockSpec((None, lanes, lanes), lambda h, i: (h, 0, 0))

    def call(kernel, in_specs, extra_scratch, *args):
        return pl.pallas_call(
            functools.partial(kernel, tq=tq),
            out_shape=jax.ShapeDtypeStruct((s, FOX_WIDTH), _BF16),
            grid=(FOX_HEADS, s // tq),
            in_specs=[smem, smem, smem] + in_specs,
            out_specs=q_spec,
            scratch_shapes=[pltpu.VMEM((hd, tq), _F32),
                            pltpu.VMEM((s, lanes), _BF16)] + extra_scratch,
            compiler_params=_params(("parallel", "arbitrary"), 32),
            name=kernel.__name__.strip("_"),
        )(b_start, b_end, thr, *args)

    fast_scratch = [pltpu.VMEM((1, tq), _F32), pltpu.VMEM((tq, tq), _F32),
                    pltpu.VMEM((tq, tq), _F32)]
    return lax.cond(
        qk_bound <= FOX_FAST_MAX_LOGIT,
        lambda: call(_fox_fast_kernel, [q_spec, qb_spec, k_spec, kb_spec, sel_spec, v_spec],
                     fast_scratch, qk, qb, qk, kb, sel, v),
        lambda: call(_fox_kernel, [q_spec, k_spec, kb_spec, sel_spec, v_spec], [],
                     qk, qk, kb, sel, v))


def _shift_rows(ext, d):
    return pltpu.roll(ext, d, axis=0)[V7X_SUBLANES:]


def _lru_kernel(lx_ref, lg_ref, wc_ref, bc_ref, wg_ref, bg_ref, lam_ref, o_ref,
                halo_ref, carry_ref, a_ref, h_ref):
    @pl.when(pl.program_id(0) == 0)
    def _():
        halo_ref[...] = jnp.zeros_like(halo_ref)
        carry_ref[...] = jnp.zeros_like(carry_ref)

    lx = lx_ref[...]
    ts = lx.shape[0]
    ext = jnp.concatenate([halo_ref[...], lx], axis=0)
    xr = (wc_ref[3:4, :] * lx + wc_ref[2:3, :] * _shift_rows(ext, 1)
          + wc_ref[1:2, :] * _shift_rows(ext, 2) + wc_ref[0:1, :] * _shift_rows(ext, 3)
          + bc_ref[...])
    halo_ref[...] = lx[ts - V7X_SUBLANES:]

    lam = lam_ref[...]
    log_sig_lam = jnp.minimum(lam, 0.0) - jnp.log1p(jnp.exp(-jnp.abs(lam)))
    bd = LRU_BLOCK_DIM
    for nb in range(LRU_BLOCKS):
        sl = slice(nb * bd, (nb + 1) * bd)
        x_nb = xr[:, sl]
        gates = jax.nn.sigmoid(jnp.dot(x_nb.astype(_BF16), wg_ref[nb],
                                       preferred_element_type=_F32) + bg_ref[nb])
        log_a = LRU_C * gates[:, :bd] * log_sig_lam[:, sl]
        a_ref[:, sl] = jnp.exp(log_a)
        t = jnp.tanh(log_a)
        h_ref[:, sl] = jnp.sqrt(-2.0 * t / (1.0 - t)) * (gates[:, bd:] * x_nb)

    sub = lax.broadcasted_iota(jnp.int32, (V7X_SUBLANES, 1), 0)

    def group(g, carry):
        rows = pl.ds(pl.multiple_of(g * V7X_SUBLANES, V7X_SUBLANES), V7X_SUBLANES)
        a = a_ref[rows, :]
        h = h_ref[rows, :]
        for d in (1, 2, 4):
            valid = sub >= d
            h = h + a * jnp.where(valid, pltpu.roll(h, d, axis=0), 0.0)
            a = a * jnp.where(valid, pltpu.roll(a, d, axis=0), 1.0)
        h = h + a * carry
        h_ref[rows, :] = h
        return jnp.broadcast_to(h[V7X_SUBLANES - 1:, :], h.shape)

    carry_ref[...] = lax.fori_loop(0, ts // V7X_SUBLANES, group, carry_ref[...],
                                   unroll=LRU_SCAN_UNROLL)
    o_ref[...] = (h_ref[...] * jax.nn.gelu(lg_ref[...])).astype(o_ref.dtype)


def _lru_branch(lxlg, w_conv, b_conv, w_gate, b_gate, lam, ts):
    s = lxlg.shape[0]
    w = LRU_WIDTH
    bd = LRU_BLOCK_DIM
    return pl.pallas_call(
        _lru_kernel,
        out_shape=jax.ShapeDtypeStruct((s, w), _BF16),
        grid=(s // ts,),
        in_specs=[pl.BlockSpec((ts, w), lambda i: (i, 0)),
                  pl.BlockSpec((ts, w), lambda i: (i, 1)),
                  pl.BlockSpec((LRU_CONV, w), lambda i: (0, 0)),
                  pl.BlockSpec((1, w), lambda i: (0, 0)),
                  pl.BlockSpec((LRU_BLOCKS, bd, 2 * bd), lambda i: (0, 0, 0)),
                  pl.BlockSpec((LRU_BLOCKS, 1, 2 * bd), lambda i: (0, 0, 0)),
                  pl.BlockSpec((1, w), lambda i: (0, 0))],
        out_specs=pl.BlockSpec((ts, w), lambda i: (i, 0)),
        scratch_shapes=[pltpu.VMEM((V7X_SUBLANES, w), _F32), pltpu.VMEM((V7X_SUBLANES, w), _F32),
                        pltpu.VMEM((ts, w), _F32), pltpu.VMEM((ts, w), _F32)],
        compiler_params=_params(("arbitrary",), 40),
        name="conv_rglru",
    )(lxlg, lxlg, w_conv, b_conv.reshape(1, -1), w_gate, b_gate, lam.reshape(1, -1))


def _mem_attn_kernel(q_ref, k_ref, v_ref, o_ref):
    s = lax.dot_general(q_ref[...], k_ref[...], (((1,), (1,)), ((), ())),
                        preferred_element_type=_F32)
    m = jnp.max(s, axis=-1, keepdims=True)
    p = jnp.exp(s - m)
    l = jnp.sum(p, axis=-1, keepdims=True)
    acc = jnp.dot(p.astype(_BF16), v_ref[...], preferred_element_type=_F32)
    o_ref[...] = (acc / l).astype(o_ref.dtype)


def _mem_attention(q, k, v, ts):
    s = q.shape[0]
    m = k.shape[0]
    hd = MEM_HEAD_DIM
    return pl.pallas_call(
        _mem_attn_kernel,
        out_shape=jax.ShapeDtypeStruct((s, MEM_WIDTH), _BF16),
        grid=(s // ts, MEM_HEADS),
        in_specs=[pl.BlockSpec((ts, hd), lambda i, h: (i, h)),
                  pl.BlockSpec((m, hd), lambda i, h: (0, h)),
                  pl.BlockSpec((m, hd), lambda i, h: (0, h))],
        out_specs=pl.BlockSpec((ts, hd), lambda i, h: (i, h)),
        compiler_params=_params(("parallel", "arbitrary"), 32),
        name="mem_attention",
    )(q, k, v)


def _merge_kernel(h_ref, yf_ref, yl_ref, ym_ref, wg0_ref, wg1_ref, wg2_ref, wb_ref, bg_ref,
                  o_ref):
    h = h_ref[...]
    merged = None
    for n, (y_ref, wg_ref) in enumerate(((yf_ref, wg0_ref), (yl_ref, wg1_ref),
                                         (ym_ref, wg2_ref))):
        gate = jax.nn.sigmoid(_dot_nt(h, wg_ref[...]) + bg_ref[n:n + 1, :])
        term = gate * jnp.dot(y_ref[...], wb_ref[n], preferred_element_type=_F32)
        merged = term if merged is None else merged + term
    o_ref[...] = merged.astype(o_ref.dtype)


def _gated_merge(h, y_fox, y_lru, y_mem, w_gate_t, w_branch, b_gate, tm, tn):
    s, d = h.shape
    nj = d // tn
    bw = y_fox.shape[1]
    y_spec = pl.BlockSpec((tm, bw), lambda i, j: (i, 0))
    return pl.pallas_call(
        _merge_kernel,
        out_shape=jax.ShapeDtypeStruct((s, d), _BF16),
        grid=(s // tm, nj),
        in_specs=[pl.BlockSpec((tm, d), lambda i, j: (i, 0)), y_spec, y_spec, y_spec,
                  pl.BlockSpec((tn, d), lambda i, j: (j, 0)),
                  pl.BlockSpec((tn, d), lambda i, j: (nj + j, 0)),
                  pl.BlockSpec((tn, d), lambda i, j: (2 * nj + j, 0)),
                  pl.BlockSpec((N_BRANCH, bw, tn), lambda i, j: (0, 0, j)),
                  pl.BlockSpec((N_BRANCH, tn), lambda i, j: (0, j))],
        out_specs=pl.BlockSpec((tm, tn), lambda i, j: (i, j)),
        compiler_params=_params(("parallel", "arbitrary"), 56),
        name="gated_merge",
    )(h, y_fox, y_lru, y_mem, w_gate_t, w_gate_t, w_gate_t, w_branch, b_gate)


def _out_norm_kernel(a_ref, w_ref, x_ref, g_ref, x2_ref, h2_ref):
    x2 = x_ref[...] + jnp.dot(a_ref[...], w_ref[...], preferred_element_type=_F32)
    x2_ref[...] = x2
    ms = jnp.mean(x2 * x2, axis=-1, keepdims=True)
    h2_ref[...] = (x2 * lax.rsqrt(ms + EPS) * g_ref[...]).astype(h2_ref.dtype)


def _out_proj_norm(a, w, x, g, tm):
    m, k = a.shape
    d = w.shape[1]
    row = lambda i: (i, 0)
    return pl.pallas_call(
        _out_norm_kernel,
        out_shape=(jax.ShapeDtypeStruct((m, d), _F32), jax.ShapeDtypeStruct((m, d), _BF16)),
        grid=(m // tm,),
        in_specs=[pl.BlockSpec((tm, k), row), pl.BlockSpec((k, d), lambda i: (0, 0)),
                  pl.BlockSpec((tm, d), row), pl.BlockSpec((1, d), lambda i: (0, 0))],
        out_specs=(pl.BlockSpec((tm, d), row), pl.BlockSpec((tm, d), row)),
        compiler_params=_params(("parallel",), 48),
        name="proj_out_norm",
    )(a, w, x, g.reshape(1, d))


def _cast_kernel(w_ref, o_ref):
    o_ref[...] = w_ref[...].astype(o_ref.dtype)


def _cast_rows(w, row0, rows, tr):
    k = w.shape[1]
    assert row0 % V7X_SUBLANES == 0 and rows % tr == 0
    return pl.pallas_call(
        _cast_kernel,
        out_shape=jax.ShapeDtypeStruct((rows, k), _BF16),
        grid=(rows // tr,),
        in_specs=[pl.BlockSpec((pl.Element(tr), pl.Element(k)),
                               lambda i: (pl.multiple_of(row0 + i * tr, V7X_SUBLANES), 0))],
        out_specs=pl.BlockSpec((tr, k), lambda i: (i, 0)),
        compiler_params=_params(("parallel",), 32),
        name="cast_rows",
    )(w)


def _ffn_up_kernel(a_ref, wa_ref, wv_ref, wca_ref, wcv_ref, bca_ref, bcv_ref, o_ref,
                   halo_a_ref, halo_v_ref, wa_bf_ref, wv_bf_ref):
    @pl.when(pl.program_id(1) == 0)
    def _():
        halo_a_ref[...] = jnp.zeros_like(halo_a_ref)
        halo_v_ref[...] = jnp.zeros_like(halo_v_ref)
        wa_bf_ref[...] = wa_ref[...].astype(_BF16)
        wv_bf_ref[...] = wv_ref[...].astype(_BF16)

    a = a_ref[...]
    tm = a.shape[0]

    def conv(up, halo_ref, wc_ref, bc_ref):
        ext = jnp.concatenate([halo_ref[...], up], axis=0)
        halo_ref[...] = up[tm - V7X_SUBLANES:]
        return (wc_ref[2:3, :] * up + wc_ref[1:2, :] * _shift_rows(ext, 1)
                + wc_ref[0:1, :] * _shift_rows(ext, 2) + bc_ref[...])

    act = conv(jnp.dot(a, wa_bf_ref[...], preferred_element_type=_F32), halo_a_ref, wca_ref,
               bca_ref)
    val = conv(jnp.dot(a, wv_bf_ref[...], preferred_element_type=_F32), halo_v_ref, wcv_ref,
               bcv_ref)
    o_ref[...] = (jax.nn.gelu(act) * val).astype(o_ref.dtype)


def _ffn_up(h2, w_up, w_conv, b_conv, tm, tn):
    s, d = h2.shape
    f = FFN_HIDDEN
    nj = f // tn
    return pl.pallas_call(
        _ffn_up_kernel,
        out_shape=jax.ShapeDtypeStruct((s, f), _BF16),
        grid=(nj, s // tm),
        in_specs=[pl.BlockSpec((tm, d), lambda j, i: (i, 0)),
                  pl.BlockSpec((d, tn), lambda j, i: (0, j)),
                  pl.BlockSpec((d, tn), lambda j, i: (0, nj + j)),
                  pl.BlockSpec((FFN_CONV, tn), lambda j, i: (0, j)),
                  pl.BlockSpec((FFN_CONV, tn), lambda j, i: (0, nj + j)),
                  pl.BlockSpec((1, tn), lambda j, i: (0, j)),
                  pl.BlockSpec((1, tn), lambda j, i: (0, nj + j))],
        out_specs=pl.BlockSpec((tm, tn), lambda j, i: (i, j)),
        scratch_shapes=[pltpu.VMEM((V7X_SUBLANES, tn), _F32),
                        pltpu.VMEM((V7X_SUBLANES, tn), _F32),
                        pltpu.VMEM((d, tn), _BF16), pltpu.VMEM((d, tn), _BF16)],
        compiler_params=_params(("parallel", "arbitrary"), 56),
        name="ffn_up_conv_geglu",
    )(h2, w_up, w_up, w_conv, w_conv, b_conv.reshape(1, -1), b_conv.reshape(1, -1))


def _layer(x, mem, g_mix, w_in, b_f, g_q_fox, g_k_fox, w_lru_conv, b_lru_conv, w_rg_a, b_rg_a,
           w_rg_x, b_rg_x, lru_lambda, g_mem, w_mem_kv, g_q_mem, g_k_mem, b_gate, w_branch,
           w_out, g_ffn, w_ffn_up, w_ffn_conv, b_ffn_conv, w_ffn_down):
    s = x.shape[0]
    c_k = 2 * FOX_WIDTH
    c_v = c_k + FOX_WIDTH
    c_f = c_v + FOX_HEADS
    c_l = c_f + 2 * LRU_WIDTH
    c_m = c_l + MEM_WIDTH

    w_in_t = w_in.T

    g_q_scaled = g_q_fox * (LOG2_E * FOX_HEAD_DIM ** -0.5)
    qk_bound = (1.02 * FOX_HEAD_DIM) * jnp.max(jnp.abs(g_q_scaled)) * jnp.max(jnp.abs(g_k_fox))
    b_pad = jnp.pad(b_f.reshape(1, -1), ((0, 0), (0, V7X_LANES - FOX_HEADS)))
    h, kb, qb, b_start, b_end = _norm_forget_bias(x, g_mix, w_in_t, c_v, b_pad, qk_bound,
                                                  FOX_TILE)
    gain_qk = jnp.concatenate([jnp.tile(g_q_scaled, FOX_HEADS),
                               jnp.tile(g_k_fox, FOX_HEADS)]).reshape(1, -1)
    qk = _matmul(h, w_in_t, w_t=True, n=c_k, tm=1024, tn=1024, out_dtype=_BF16,
                 epilogue="gnorm", extra=gain_qk, group=FOX_HEAD_DIM, name="proj_qk")
    v = _matmul(h, w_in_t, w_t=True, n=FOX_WIDTH, w_off=c_k, tm=1024, tn=1024,
                out_dtype=_BF16, name="proj_v")
    y_fox = _fox_attention(qk, kb, qb, b_start, b_end, qk_bound, v, FOX_TILE)

    lxlg = _matmul(h, w_in_t, w_t=True, n=2 * LRU_WIDTH, w_off=c_f, tm=1024, tn=1024,
                   out_dtype=_F32, name="proj_lru")
    w_gate_lru = jnp.concatenate([w_rg_a, w_rg_x], axis=-1).astype(_BF16)
    b_gate_lru = jnp.concatenate([b_rg_a, b_rg_x], axis=-1).reshape(LRU_BLOCKS, 1, -1)
    y_lru = _lru_branch(lxlg, w_lru_conv, b_lru_conv, w_gate_lru, b_gate_lru, lru_lambda, 512)

    gain_mq = (jnp.tile(g_q_mem, MEM_HEADS) * (MEM_HEAD_DIM ** -0.5)).reshape(1, -1)
    mq = _matmul(h, w_in_t, w_t=True, n=MEM_WIDTH, w_off=c_l, tm=1024, tn=1024,
                 out_dtype=_BF16, epilogue="gnorm", extra=gain_mq, group=MEM_HEAD_DIM,
                 name="proj_mq")
    hm = _rmsnorm(mem, g_mem, mem.shape[0])
    mk = _matmul(hm, w_mem_kv, n=MEM_WIDTH, tm=mem.shape[0], tn=512, out_dtype=_BF16,
                 epilogue="gnorm", extra=jnp.tile(g_k_mem, MEM_HEADS).reshape(1, -1),
                 group=MEM_HEAD_DIM, name="proj_mk")
    mv = _matmul(hm, w_mem_kv, n=MEM_WIDTH, w_off=MEM_WIDTH, tm=mem.shape[0], tn=512,
                 out_dtype=_BF16, name="proj_mv")
    y_mem = _mem_attention(mq, mk, mv, 1024)

    w_gate_t = _cast_rows(w_in_t, c_m, N_BRANCH * D_MODEL, 512)
    merged = _gated_merge(h, y_fox, y_lru, y_mem, w_gate_t, w_branch.astype(_BF16), b_gate,
                          1024, 256)
    x2, h2 = _out_proj_norm(merged, w_out.astype(_BF16), x, g_ffn, 512)

    g = _ffn_up(h2, w_ffn_up, w_ffn_conv, b_ffn_conv, 1024, 512)
    return _matmul(g, w_ffn_down.astype(_BF16), n=D_MODEL, tm=1024, tn=512, rows_outer=True,
                   out_dtype=_F32, epilogue="residual", extra=x2, name="ffn_down")


def kernel(x, mem, g_mix, w_in, b_f, g_q_fox, g_k_fox, w_lru_conv, b_lru_conv, w_rg_a, b_rg_a,
           w_rg_x, b_rg_x, lru_lambda, g_mem, w_mem_kv, g_q_mem, g_k_mem, b_gate, w_branch,
           w_out, g_ffn, w_ffn_up, w_ffn_conv, b_ffn_conv, w_ffn_down):
    depth = g_mix.shape[0]
    outs = []
    for b in range(x.shape[0]):
        xb = x[b]
        for l in range(depth):
            xb = _layer(xb, mem[b], g_mix[l], w_in[l], b_f[l], g_q_fox[l], g_k_fox[l],
                        w_lru_conv[l], b_lru_conv[l], w_rg_a[l], b_rg_a[l], w_rg_x[l],
                        b_rg_x[l], lru_lambda[l], g_mem[l], w_mem_kv[l], g_q_mem[l],
                        g_k_mem[l], b_gate[l], w_branch[l], w_out[l], g_ffn[l], w_ffn_up[l],
                        w_ffn_conv[l], b_ffn_conv[l], w_ffn_down[l])
        outs.append(xb)
    return outs[0][None] if len(outs) == 1 else jnp.stack(outs)
```

```python
import functools

import jax
import jax.numpy as jnp
import numpy as np
from jax import lax
from jax.experimental import pallas as pl
from jax.experimental.pallas import tpu as pltpu

D_MODEL = 2048
FOX_HEADS = 8
FOX_HEAD_DIM = 128
FOX_WIDTH = FOX_HEADS * FOX_HEAD_DIM
LRU_WIDTH = 1024
LRU_BLOCKS = 8
LRU_BLOCK_DIM = LRU_WIDTH // LRU_BLOCKS
LRU_CONV = 4
LRU_C = 8.0
MEM_HEADS = 4
MEM_HEAD_DIM = 256
MEM_WIDTH = MEM_HEADS * MEM_HEAD_DIM
N_BRANCH = 3
FFN_HIDDEN = 5632
FFN_CONV = 3
EPS = 1e-6

V7X_SUBLANES = 8
V7X_LANES = 128
MASK_VALUE = -1e30
LOG2_E = 1.4426950408889634
ROW_TILE = 1024
COL_TILE = 1024
FFN_COL_TILE = 512
MERGE_COL_TILE = 512
SEQ_TILE = 512
FOX_TILE = 512
FOX_SKIP_BITS = 64.0
FOX_BIAS_LANES = 16
FOX_FAST_MAX_LOGIT = 48.0
LRU_SCAN_UNROLL = 8

_BF16 = jnp.bfloat16
_F32 = jnp.float32


def _params(semantics, vmem_mib):
    return pltpu.CompilerParams(dimension_semantics=semantics,
                                vmem_limit_bytes=vmem_mib * 1024 * 1024)


def _rmsnorm_kernel(x_ref, g_ref, o_ref):
    x = x_ref[...]
    ms = jnp.mean(x * x, axis=-1, keepdims=True)
    o_ref[...] = (x * lax.rsqrt(ms + EPS) * g_ref[...]).astype(o_ref.dtype)


def _rmsnorm(x, g, tm):
    m, d = x.shape
    return pl.pallas_call(
        _rmsnorm_kernel,
        out_shape=jax.ShapeDtypeStruct((m, d), _BF16),
        grid=(m // tm,),
        in_specs=[pl.BlockSpec((tm, d), lambda i: (i, 0)),
                  pl.BlockSpec((1, d), lambda i: (0, 0))],
        out_specs=pl.BlockSpec((tm, d), lambda i: (i, 0)),
        compiler_params=_params(("parallel",), 32),
        name="rmsnorm",
    )(x, g.reshape(1, d))


def _dot_nt(a, w_t):
    return lax.dot_general(a, w_t, (((1,), (1,)), ((), ())), preferred_element_type=_F32)


def _mm_kernel(*refs, epilogue, group, gnorm_blocks, cast_w, w_t):
    a_ref, w_ref = refs[0], refs[1]
    if cast_w:
        o_ref, wb_ref = refs[-2], refs[-1]

        @pl.when(pl.program_id(1) == 0)
        def _():
            wb_ref[...] = w_ref[...].astype(_BF16)

        w = wb_ref[...]
    else:
        o_ref = refs[-1]
        w = w_ref[...]
    if w_t:
        acc = _dot_nt(a_ref[...], w)
    else:
        acc = jnp.dot(a_ref[...], w, preferred_element_type=_F32)
    if epilogue == "gnorm":
        g_ref = refs[2]

        @pl.when(pl.program_id(0) < gnorm_blocks)
        def _():
            for gi in range(acc.shape[1] // group):
                sl = slice(gi * group, (gi + 1) * group)
                blk = acc[:, sl]
                ms = jnp.mean(blk * blk, axis=-1, keepdims=True)
                o_ref[:, sl] = (blk * lax.rsqrt(ms + EPS) * g_ref[:, sl]).astype(o_ref.dtype)

        @pl.when(pl.program_id(0) >= gnorm_blocks)
        def _():
            o_ref[...] = acc.astype(o_ref.dtype)
    elif epilogue == "residual":
        o_ref[...] = (refs[2][...] + acc).astype(o_ref.dtype)
    else:
        o_ref[...] = acc.astype(o_ref.dtype)


def _matmul(a, w, *, n, tm, tn, out_dtype, w_t=False, w_off=0, rows_outer=False,
            epilogue="plain", extra=None, group=None, gnorm_cols=None, name):
    m, k = a.shape
    cast_w = w.dtype != _BF16
    assert not (cast_w and rows_outer) and n % tn == 0 and m % tm == 0
    gnorm_cols = n if gnorm_cols is None else gnorm_cols
    assert gnorm_cols % tn == 0 and not (epilogue == "gnorm" and rows_outer)
    if rows_outer:
        grid = (m // tm, n // tn)
        ij = lambda i, j: (i, j)
    else:
        grid = (n // tn, m // tm)
        ij = lambda j, i: (i, j)
    if w_t:
        assert w_off % V7X_SUBLANES == 0
        w_block = (tn, k)
        w_spec = pl.BlockSpec((pl.Element(tn), pl.Element(k)),
                              lambda *g: (pl.multiple_of(w_off + ij(*g)[1] * tn, V7X_SUBLANES), 0))
    else:
        assert w_off % tn == 0
        w_block = (k, tn)
        w_spec = pl.BlockSpec(w_block, lambda *g: (0, w_off // tn + ij(*g)[1]))
    in_specs = [pl.BlockSpec((tm, k), lambda *g: (ij(*g)[0], 0)), w_spec]
    args = [a, w]
    block_bytes = (tm * k * 2 + k * tn * w.dtype.itemsize
                   + tm * tn * jnp.dtype(out_dtype).itemsize)
    if epilogue == "gnorm":
        in_specs.append(pl.BlockSpec((1, tn), lambda *g: (0, ij(*g)[1])))
        args.append(extra)
    elif epilogue == "residual":
        in_specs.append(pl.BlockSpec((tm, tn), lambda *g: ij(*g)))
        args.append(extra)
        block_bytes += tm * tn * extra.dtype.itemsize
    scratch = [pltpu.VMEM(w_block, _BF16)] if cast_w else []
    vmem_mib = -(-(2 * block_bytes + cast_w * k * tn * 2 + 2 * tm * tn * 4) // 2 ** 20) + 2
    return pl.pallas_call(
        functools.partial(_mm_kernel, epilogue=epilogue, group=group,
                          gnorm_blocks=gnorm_cols // tn, cast_w=cast_w, w_t=w_t),
        out_shape=jax.ShapeDtypeStruct((m, n), out_dtype),
        grid=grid,
        in_specs=in_specs,
        out_specs=pl.BlockSpec((tm, tn), lambda *g: ij(*g)),
        scratch_shapes=scratch,
        compiler_params=_params(("parallel", "arbitrary"), vmem_mib),
        name=name,
    )(*args)


def _split3(x):
    hi = x.astype(_BF16)
    r1 = x - hi.astype(_F32)
    mid = r1.astype(_BF16)
    lo = (r1 - mid.astype(_F32)).astype(_BF16)
    return hi, mid, lo


def _forget_bias_kernel(shift_ref, x_ref, g_ref, w_ref, b_ref, sel_ref, ones_ref, h_ref, kb_ref,
                        qb_ref, edge_ref, carry_ref):
    @pl.when(pl.program_id(0) == 0)
    def _():
        carry_ref[...] = jnp.zeros_like(carry_ref)

    tc = x_ref.shape[0]
    x = x_ref[...]
    ms = jnp.mean(x * x, axis=-1, keepdims=True)
    h = (x * lax.rsqrt(ms + EPS) * g_ref[...]).astype(_BF16)
    h_ref[...] = h
    w = jnp.concatenate([w_ref[...], jnp.zeros((V7X_LANES - FOX_HEADS, w_ref.shape[1]), _F32)],
                        axis=0).astype(_BF16)
    z = _dot_nt(h, w) + b_ref[...]
    neg_log_f = (jnp.log1p(jnp.exp(-jnp.abs(z))) - jnp.minimum(z, 0.0)) * LOG2_E
    row = lax.broadcasted_iota(jnp.int32, (tc, tc), 0)
    col = lax.broadcasted_iota(jnp.int32, (tc, tc), 1)
    tri = jnp.where(col <= row, 1.0, 0.0).astype(_BF16)
    c = carry_ref[0:1, :]
    for part in _split3(neg_log_f):
        c = c + jnp.dot(tri, part, preferred_element_type=_F32)
    carry_ref[...] = jnp.broadcast_to(c[tc - 1:tc, :], carry_ref.shape)
    edge_ref[...] = jnp.concatenate(
        [c[0:1, :], c[tc - 1:tc, :], jnp.zeros((V7X_SUBLANES - 2, c.shape[1]), _F32)], axis=0)
    parts = jnp.concatenate(_split3(c) + _split3(-(c + shift_ref[0])), axis=1)
    routed = jnp.dot(parts, sel_ref[...], preferred_element_type=_F32) + ones_ref[...]
    half = kb_ref.shape[1]
    kb_ref[...] = routed[:, :half].astype(kb_ref.dtype)
    qb_ref[...] = routed[:, half:].astype(qb_ref.dtype)


def _norm_forget_bias(x, g, w_t, w_row0, b_pad, shift, tc):
    s, d = x.shape
    lanes = V7X_LANES
    width = lanes
    nt = s // tc
    sel = np.zeros((6 * lanes, 2 * width), np.float32)
    ones = np.zeros((1, 2 * width), np.float32)
    for head in range(FOX_HEADS):
        for part in range(3):
            sel[part * lanes + head, FOX_BIAS_LANES * head + part] = 1.0
            sel[(3 + part) * lanes + head, width + FOX_BIAS_LANES * head + 3 + part] = 1.0
            ones[0, FOX_BIAS_LANES * head + 3 + part] = 1.0
            ones[0, width + FOX_BIAS_LANES * head + part] = 1.0
    h, kb, qb, edges = pl.pallas_call(
        _forget_bias_kernel,
        out_shape=(jax.ShapeDtypeStruct((s, d), _BF16),
                   jax.ShapeDtypeStruct((s, width), _BF16),
                   jax.ShapeDtypeStruct((s, width), _BF16),
                   jax.ShapeDtypeStruct((nt * V7X_SUBLANES, lanes), _F32)),
        grid=(nt,),
        in_specs=[pl.BlockSpec(memory_space=pltpu.SMEM),
                  pl.BlockSpec((tc, d), lambda i: (i, 0)),
                  pl.BlockSpec((1, d), lambda i: (0, 0)),
                  pl.BlockSpec((pl.Element(FOX_HEADS), pl.Element(d)), lambda i: (w_row0, 0)),
                  pl.BlockSpec((1, lanes), lambda i: (0, 0)),
                  pl.BlockSpec((6 * lanes, 2 * width), lambda i: (0, 0)),
                  pl.BlockSpec((1, 2 * width), lambda i: (0, 0))],
        out_specs=(pl.BlockSpec((tc, d), lambda i: (i, 0)),
                   pl.BlockSpec((tc, width), lambda i: (i, 0)),
                   pl.BlockSpec((tc, width), lambda i: (i, 0)),
                   pl.BlockSpec((V7X_SUBLANES, lanes), lambda i: (i, 0))),
        scratch_shapes=[pltpu.VMEM((V7X_SUBLANES, lanes), _F32)],
        compiler_params=_params(("arbitrary",), 40),
        name="norm_forget_bias",
    )(shift.reshape(1).astype(_F32), x, g.reshape(1, d), w_t, b_pad, jnp.asarray(sel, _BF16),
      jnp.asarray(ones))
    edges = edges.reshape(nt, V7X_SUBLANES, lanes)
    return h, kb, qb, edges[:, 0, :FOX_HEADS], edges[:, 1, :FOX_HEADS]


def _unpack_bias(packed, sel_ref):
    return jnp.dot(packed, sel_ref[...], preferred_element_type=_F32).astype(_BF16)


def _fox_kernel(bstart_ref, bend_ref, thr_ref, q_ref, k_ref, kbp_ref, sel_ref, v_ref, o_ref,
                acc_ref, kb_ref, *, tq):
    head = pl.program_id(0)
    qi = pl.program_id(1)
    b_tile = bstart_ref[qi, head]
    thr = thr_ref[0]
    j0 = lax.fori_loop(
        0, qi, lambda j, n: n + jnp.where(b_tile - bend_ref[j, head] >= thr, 1, 0), 0)

    @pl.when(qi == 0)
    def _():
        kb_ref[...] = _unpack_bias(kbp_ref[...], sel_ref)

    lane = lax.broadcasted_iota(jnp.int32, (tq, V7X_LANES), 1)
    ones3 = jnp.where(lane < 3, 1.0, 0.0).astype(_BF16)
    q_aug = jnp.concatenate([q_ref[...], ones3], axis=1)
    acc_ref[...] = jnp.zeros_like(acc_ref)

    def logits(j):
        ks = pl.multiple_of(j * tq, tq)
        k_aug = jnp.concatenate([k_ref[pl.ds(ks, tq), :], kb_ref[pl.ds(ks, tq), :]], axis=1)
        return lax.dot_general(k_aug, q_aug, (((1,), (1,)), ((), ())),
                               preferred_element_type=_F32)

    def accumulate(j, s, m_prev, l_prev):
        m_new = jnp.maximum(m_prev, jnp.max(s, axis=0, keepdims=True))
        alpha = jnp.exp2(m_prev - m_new)
        p = jnp.exp2(s - m_new)
        l_new = alpha * l_prev + jnp.sum(p, axis=0, keepdims=True)
        ks = pl.multiple_of(j * tq, tq)
        pv = lax.dot_general(v_ref[pl.ds(ks, tq), :], p.astype(_BF16),
                             (((0,), (0,)), ((), ())), preferred_element_type=_F32)
        acc_ref[...] = alpha * acc_ref[...] + pv
        return m_new, l_new

    def body(j, carry):
        m_prev, l_prev, s = carry
        s_next = logits(j + 1)
        m_new, l_new = accumulate(j, s, m_prev, l_prev)
        return m_new, l_new, s_next

    m0 = jnp.full((1, tq), MASK_VALUE, _F32)
    l0 = jnp.zeros((1, tq), _F32)
    m, l, s = lax.fori_loop(j0, qi, body, (m0, l0, logits(j0)))
    row = lax.broadcasted_iota(jnp.int32, s.shape, 0)
    col = lax.broadcasted_iota(jnp.int32, s.shape, 1)
    m, l = accumulate(qi, jnp.where(row <= col, s, MASK_VALUE), m, l)
    o_ref[...] = (acc_ref[...] / l).T.astype(o_ref.dtype)


def _fox_fast_kernel(bstart_ref, bend_ref, thr_ref, q_ref, qbp_ref, k_ref, kbp_ref, sel_ref, v_ref,
                     o_ref, acc_ref, kb_ref, l_ref, s0_ref, s1_ref, *, tq):
    head = pl.program_id(0)
    qi = pl.program_id(1)
    b_tile = bstart_ref[qi, head]
    thr = thr_ref[0]
    j0 = lax.fori_loop(
        0, qi, lambda j, n: n + jnp.where(b_tile - bend_ref[j, head] >= thr, 1, 0), 0)

    @pl.when(qi == 0)
    def _():
        kb_ref[...] = _unpack_bias(kbp_ref[...], sel_ref)

    q_aug = jnp.concatenate([q_ref[...], _unpack_bias(qbp_ref[...], sel_ref)], axis=1)
    acc_ref[...] = jnp.zeros_like(acc_ref)

    def logits(j):
        ks = pl.multiple_of(j * tq, tq)
        k_aug = jnp.concatenate([k_ref[pl.ds(ks, tq), :], kb_ref[pl.ds(ks, tq), :]], axis=1)
        return _dot_nt(k_aug, q_aug)

    def accumulate(j, s):
        p = jnp.exp2(s)
        ks = pl.multiple_of(j * tq, tq)
        acc_ref[...] += lax.dot_general(v_ref[pl.ds(ks, tq), :], p.astype(_BF16),
                                        (((0,), (0,)), ((), ())), preferred_element_type=_F32)
        l_ref[...] += jnp.sum(p, axis=0, keepdims=True)

    l_ref[...] = jnp.zeros_like(l_ref)
    row = lax.broadcasted_iota(jnp.int32, s1_ref.shape, 0)
    col = lax.broadcasted_iota(jnp.int32, s1_ref.shape, 1)
    s_diag = jnp.where(row <= col, logits(qi), MASK_VALUE)
    s0_ref[...] = logits(j0)
    accumulate(qi, s_diag)
    n_before = qi - j0

    def pair(t, carry):
        j = j0 + 2 * t
        s1_ref[...] = logits(j + 1)
        accumulate(j, s0_ref[...])
        s0_ref[...] = logits(j + 2)
        accumulate(j + 1, s1_ref[...])
        return carry

    lax.fori_loop(0, n_before // 2, pair, 0)

    @pl.when(n_before % 2 == 1)
    def _():
        accumulate(qi - 1, s0_ref[...])

    o_ref[...] = (acc_ref[...] / l_ref[...]).T.astype(o_ref.dtype)


def _fox_attention(qkv, kb, qb, b_start, b_end, qk_bound, tq):
    s = qkv.shape[0]
    hd = FOX_HEAD_DIM
    lanes = V7X_LANES
    thr = (2.0 * qk_bound + FOX_SKIP_BITS).reshape(1).astype(_F32)
    sel = np.zeros((FOX_HEADS, lanes, lanes), np.float32)
    for head in range(FOX_HEADS):
        for c in range(FOX_BIAS_LANES):
            sel[head, FOX_BIAS_LANES * head + c, c] = 1.0
    sel = jnp.asarray(sel, _BF16)
    smem = pl.BlockSpec(memory_space=pltpu.SMEM)
    q_spec = pl.BlockSpec((tq, hd), lambda h, i: (i, h))
    k_spec = pl.BlockSpec((s, hd), lambda h, i: (0, FOX_HEADS + h))
    v_spec = pl.BlockSpec((s, hd), lambda h, i: (0, 2 * FOX_HEADS + h))
    kb_spec = pl.BlockSpec((s, lanes), lambda h, i: (0, 0))
    qb_spec = pl.BlockSpec((tq, lanes), lambda h, i: (i, 0))
    sel_spec = pl.BlockSpec((None, lanes, lanes), lambda h, i: (h, 0, 0))

    def call(kernel, in_specs, extra_scratch, *args):
        return pl.pallas_call(
            functools.partial(kernel, tq=tq),
            out_shape=jax.ShapeDtypeStruct((s, FOX_WIDTH), _BF16),
            grid=(FOX_HEADS, s // tq),
            in_specs=[smem, smem, smem] + in_specs,
            out_specs=q_spec,
            scratch_shapes=[pltpu.VMEM((hd, tq), _F32),
                            pltpu.VMEM((s, lanes), _BF16)] + extra_scratch,
            compiler_params=_params(("parallel", "arbitrary"), 32),
            name=kernel.__name__.strip("_"),
        )(b_start, b_end, thr, *args)

    fast_scratch = [pltpu.VMEM((1, tq), _F32), pltpu.VMEM((tq, tq), _F32),
                    pltpu.VMEM((tq, tq), _F32)]
    return lax.cond(
        qk_bound <= FOX_FAST_MAX_LOGIT,
        lambda: call(_fox_fast_kernel, [q_spec, qb_spec, k_spec, kb_spec, sel_spec, v_spec],
                     fast_scratch, qkv, qb, qkv, kb, sel, qkv),
        lambda: call(_fox_kernel, [q_spec, k_spec, kb_spec, sel_spec, v_spec], [],
                     qkv, qkv, kb, sel, qkv))


def _shift_rows(ext, d):
    return pltpu.roll(ext, d, axis=0)[V7X_SUBLANES:]


def _lru_kernel(lx_ref, lg_ref, wc_ref, bc_ref, wg_ref, bg_ref, lam_ref, o_ref,
                halo_ref, carry_ref, a_ref, h_ref):
    @pl.when(pl.program_id(0) == 0)
    def _():
        halo_ref[...] = jnp.zeros_like(halo_ref)
        carry_ref[...] = jnp.zeros_like(carry_ref)

    lx = lx_ref[...]
    ts = lx.shape[0]
    ext = jnp.concatenate([halo_ref[...], lx], axis=0)
    xr = (wc_ref[3:4, :] * lx + wc_ref[2:3, :] * _shift_rows(ext, 1)
          + wc_ref[1:2, :] * _shift_rows(ext, 2) + wc_ref[0:1, :] * _shift_rows(ext, 3)
          + bc_ref[...])
    halo_ref[...] = lx[ts - V7X_SUBLANES:]

    lam = lam_ref[...]
    log_sig_lam = jnp.minimum(lam, 0.0) - jnp.log1p(jnp.exp(-jnp.abs(lam)))
    bd = LRU_BLOCK_DIM
    for nb in range(LRU_BLOCKS):
        sl = slice(nb * bd, (nb + 1) * bd)
        x_nb = xr[:, sl]
        gates = jax.nn.sigmoid(jnp.dot(x_nb.astype(_BF16), wg_ref[nb],
                                       preferred_element_type=_F32) + bg_ref[nb])
        log_a = LRU_C * gates[:, :bd] * log_sig_lam[:, sl]
        a_ref[:, sl] = jnp.exp(log_a)
        t = jnp.tanh(log_a)
        h_ref[:, sl] = jnp.sqrt(-2.0 * t / (1.0 - t)) * (gates[:, bd:] * x_nb)

    sub = lax.broadcasted_iota(jnp.int32, (V7X_SUBLANES, 1), 0)

    def group(g, carry):
        rows = pl.ds(pl.multiple_of(g * V7X_SUBLANES, V7X_SUBLANES), V7X_SUBLANES)
        a = a_ref[rows, :]
        h = h_ref[rows, :]
        for d in (1, 2, 4):
            valid = sub >= d
            h = h + a * jnp.where(valid, pltpu.roll(h, d, axis=0), 0.0)
            a = a * jnp.where(valid, pltpu.roll(a, d, axis=0), 1.0)
        h = h + a * carry
        h_ref[rows, :] = h
        return jnp.broadcast_to(h[V7X_SUBLANES - 1:, :], h.shape)

    carry_ref[...] = lax.fori_loop(0, ts // V7X_SUBLANES, group, carry_ref[...],
                                   unroll=LRU_SCAN_UNROLL)
    o_ref[...] = (h_ref[...] * jax.nn.gelu(lg_ref[...])).astype(o_ref.dtype)


def _lru_branch(lxlg, w_conv, b_conv, w_gate, b_gate, lam, ts):
    s = lxlg.shape[0]
    w = LRU_WIDTH
    bd = LRU_BLOCK_DIM
    return pl.pallas_call(
        _lru_kernel,
        out_shape=jax.ShapeDtypeStruct((s, w), _BF16),
        grid=(s // ts,),
        in_specs=[pl.BlockSpec((ts, w), lambda i: (i, 0)),
                  pl.BlockSpec((ts, w), lambda i: (i, 1)),
                  pl.BlockSpec((LRU_CONV, w), lambda i: (0, 0)),
                  pl.BlockSpec((1, w), lambda i: (0, 0)),
                  pl.BlockSpec((LRU_BLOCKS, bd, 2 * bd), lambda i: (0, 0, 0)),
                  pl.BlockSpec((LRU_BLOCKS, 1, 2 * bd), lambda i: (0, 0, 0)),
                  pl.BlockSpec((1, w), lambda i: (0, 0))],
        out_specs=pl.BlockSpec((ts, w), lambda i: (i, 0)),
        scratch_shapes=[pltpu.VMEM((V7X_SUBLANES, w), _F32), pltpu.VMEM((V7X_SUBLANES, w), _F32),
                        pltpu.VMEM((ts, w), _F32), pltpu.VMEM((ts, w), _F32)],
        compiler_params=_params(("arbitrary",), 40),
        name="conv_rglru",
    )(lxlg, lxlg, w_conv, b_conv.reshape(1, -1), w_gate, b_gate, lam.reshape(1, -1))


def _mem_attn_kernel(q_ref, k_ref, v_ref, o_ref):
    s = lax.dot_general(q_ref[...], k_ref[...], (((1,), (1,)), ((), ())),
                        preferred_element_type=_F32)
    m = jnp.max(s, axis=-1, keepdims=True)
    p = jnp.exp(s - m)
    l = jnp.sum(p, axis=-1, keepdims=True)
    acc = jnp.dot(p.astype(_BF16), v_ref[...], preferred_element_type=_F32)
    o_ref[...] = (acc / l).astype(o_ref.dtype)


def _mem_attention(q, k, v, ts):
    s = q.shape[0]
    m = k.shape[0]
    hd = MEM_HEAD_DIM
    return pl.pallas_call(
        _mem_attn_kernel,
        out_shape=jax.ShapeDtypeStruct((s, MEM_WIDTH), _BF16),
        grid=(s // ts, MEM_HEADS),
        in_specs=[pl.BlockSpec((ts, hd), lambda i, h: (i, h)),
                  pl.BlockSpec((m, hd), lambda i, h: (0, h)),
                  pl.BlockSpec((m, hd), lambda i, h: (0, h))],
        out_specs=pl.BlockSpec((ts, hd), lambda i, h: (i, h)),
        compiler_params=_params(("parallel", "arbitrary"), 32),
        name="mem_attention",
    )(q, k, v)


def _merge_kernel(h_ref, yf_ref, yl_ref, ym_ref, wg0_ref, wg1_ref, wg2_ref, wb_ref, bg_ref,
                  o_ref):
    h = h_ref[...]
    merged = None
    for n, (y_ref, wg_ref) in enumerate(((yf_ref, wg0_ref), (yl_ref, wg1_ref),
                                         (ym_ref, wg2_ref))):
        gate = jax.nn.sigmoid(_dot_nt(h, wg_ref[...]) + bg_ref[n:n + 1, :])
        term = gate * jnp.dot(y_ref[...], wb_ref[n], preferred_element_type=_F32)
        merged = term if merged is None else merged + term
    o_ref[...] = merged.astype(o_ref.dtype)


def _gated_merge(h, y_fox, y_lru, y_mem, w_gate_t, w_branch, b_gate, tm, tn):
    s, d = h.shape
    nj = d // tn
    bw = y_fox.shape[1]
    y_spec = pl.BlockSpec((tm, bw), lambda i, j: (i, 0))
    return pl.pallas_call(
        _merge_kernel,
        out_shape=jax.ShapeDtypeStruct((s, d), _BF16),
        grid=(s // tm, nj),
        in_specs=[pl.BlockSpec((tm, d), lambda i, j: (i, 0)), y_spec, y_spec, y_spec,
                  pl.BlockSpec((tn, d), lambda i, j: (j, 0)),
                  pl.BlockSpec((tn, d), lambda i, j: (nj + j, 0)),
                  pl.BlockSpec((tn, d), lambda i, j: (2 * nj + j, 0)),
                  pl.BlockSpec((N_BRANCH, bw, tn), lambda i, j: (0, 0, j)),
                  pl.BlockSpec((N_BRANCH, tn), lambda i, j: (0, j))],
        out_specs=pl.BlockSpec((tm, tn), lambda i, j: (i, j)),
        compiler_params=_params(("parallel", "arbitrary"), 56),
        name="gated_merge",
    )(h, y_fox, y_lru, y_mem, w_gate_t, w_gate_t, w_gate_t, w_branch, b_gate)


def _out_norm_kernel(a_ref, w_ref, x_ref, g_ref, x2_ref, h2_ref):
    x2 = x_ref[...] + jnp.dot(a_ref[...], w_ref[...], preferred_element_type=_F32)
    x2_ref[...] = x2
    ms = jnp.mean(x2 * x2, axis=-1, keepdims=True)
    h2_ref[...] = (x2 * lax.rsqrt(ms + EPS) * g_ref[...]).astype(h2_ref.dtype)


def _out_proj_norm(a, w, x, g, tm):
    m, k = a.shape
    d = w.shape[1]
    row = lambda i: (i, 0)
    return pl.pallas_call(
        _out_norm_kernel,
        out_shape=(jax.ShapeDtypeStruct((m, d), _F32), jax.ShapeDtypeStruct((m, d), _BF16)),
        grid=(m // tm,),
        in_specs=[pl.BlockSpec((tm, k), row), pl.BlockSpec((k, d), lambda i: (0, 0)),
                  pl.BlockSpec((tm, d), row), pl.BlockSpec((1, d), lambda i: (0, 0))],
        out_specs=(pl.BlockSpec((tm, d), row), pl.BlockSpec((tm, d), row)),
        compiler_params=_params(("parallel",), 48),
        name="proj_out_norm",
    )(a, w, x, g.reshape(1, d))


def _cast_kernel(w_ref, o_ref):
    o_ref[...] = w_ref[...].astype(o_ref.dtype)


def _cast_rows(w, row0, rows, tr):
    k = w.shape[1]
    assert row0 % V7X_SUBLANES == 0 and rows % tr == 0
    return pl.pallas_call(
        _cast_kernel,
        out_shape=jax.ShapeDtypeStruct((rows, k), _BF16),
        grid=(rows // tr,),
        in_specs=[pl.BlockSpec((pl.Element(tr), pl.Element(k)),
                               lambda i: (pl.multiple_of(row0 + i * tr, V7X_SUBLANES), 0))],
        out_specs=pl.BlockSpec((tr, k), lambda i: (i, 0)),
        compiler_params=_params(("parallel",), 32),
        name="cast_rows",
    )(w)


def _ffn_up_kernel(a_ref, wa_ref, wv_ref, wca_ref, wcv_ref, bca_ref, bcv_ref, o_ref,
                   halo_a_ref, halo_v_ref, wa_bf_ref, wv_bf_ref):
    @pl.when(pl.program_id(1) == 0)
    def _():
        halo_a_ref[...] = jnp.zeros_like(halo_a_ref)
        halo_v_ref[...] = jnp.zeros_like(halo_v_ref)
        wa_bf_ref[...] = wa_ref[...].astype(_BF16)
        wv_bf_ref[...] = wv_ref[...].astype(_BF16)

    a = a_ref[...]
    tm = a.shape[0]

    def conv(up, halo_ref, wc_ref, bc_ref):
        ext = jnp.concatenate([halo_ref[...], up], axis=0)
        halo_ref[...] = up[tm - V7X_SUBLANES:]
        return (wc_ref[2:3, :] * up + wc_ref[1:2, :] * _shift_rows(ext, 1)
                + wc_ref[0:1, :] * _shift_rows(ext, 2) + bc_ref[...])

    act = conv(jnp.dot(a, wa_bf_ref[...], preferred_element_type=_F32), halo_a_ref, wca_ref,
               bca_ref)
    val = conv(jnp.dot(a, wv_bf_ref[...], preferred_element_type=_F32), halo_v_ref, wcv_ref,
               bcv_ref)
    o_ref[...] = (jax.nn.gelu(act) * val).astype(o_ref.dtype)


def _ffn_up(h2, w_up, w_conv, b_conv, tm, tn):
    s, d = h2.shape
    f = FFN_HIDDEN
    nj = f // tn
    return pl.pallas_call(
        _ffn_up_kernel,
        out_shape=jax.ShapeDtypeStruct((s, f), _BF16),
        grid=(nj, s // tm),
        in_specs=[pl.BlockSpec((tm, d), lambda j, i: (i, 0)),
                  pl.BlockSpec((d, tn), lambda j, i: (0, j)),
                  pl.BlockSpec((d, tn), lambda j, i: (0, nj + j)),
                  pl.BlockSpec((FFN_CONV, tn), lambda j, i: (0, j)),
                  pl.BlockSpec((FFN_CONV, tn), lambda j, i: (0, nj + j)),
                  pl.BlockSpec((1, tn), lambda j, i: (0, j)),
                  pl.BlockSpec((1, tn), lambda j, i: (0, nj + j))],
        out_specs=pl.BlockSpec((tm, tn), lambda j, i: (i, j)),
        scratch_shapes=[pltpu.VMEM((V7X_SUBLANES, tn), _F32),
                        pltpu.VMEM((V7X_SUBLANES, tn), _F32),
                        pltpu.VMEM((d, tn), _BF16), pltpu.VMEM((d, tn), _BF16)],
        compiler_params=_params(("parallel", "arbitrary"), 56),
        name="ffn_up_conv_geglu",
    )(h2, w_up, w_up, w_conv, w_conv, b_conv.reshape(1, -1), b_conv.reshape(1, -1))


def _layer(x, mem, g_mix, w_in, b_f, g_q_fox, g_k_fox, w_lru_conv, b_lru_conv, w_rg_a, b_rg_a,
           w_rg_x, b_rg_x, lru_lambda, g_mem, w_mem_kv, g_q_mem, g_k_mem, b_gate, w_branch,
           w_out, g_ffn, w_ffn_up, w_ffn_conv, b_ffn_conv, w_ffn_down):
    c_k = 2 * FOX_WIDTH
    c_v = c_k + FOX_WIDTH
    c_f = c_v + FOX_HEADS
    c_l = c_f + 2 * LRU_WIDTH
    c_m = c_l + MEM_WIDTH

    w_in_t = w_in.T

    g_q_scaled = g_q_fox * (LOG2_E * FOX_HEAD_DIM ** -0.5)
    qk_bound = (1.02 * FOX_HEAD_DIM) * jnp.max(jnp.abs(g_q_scaled)) * jnp.max(jnp.abs(g_k_fox))
    b_pad = jnp.pad(b_f.reshape(1, -1), ((0, 0), (0, V7X_LANES - FOX_HEADS)))
    h, kb, qb, b_start, b_end = _norm_forget_bias(x, g_mix, w_in_t, c_v, b_pad, qk_bound,
                                                  FOX_TILE)
    gain_qkv = jnp.concatenate([jnp.tile(g_q_scaled, FOX_HEADS), jnp.tile(g_k_fox, FOX_HEADS),
                                jnp.ones((FOX_WIDTH,), _F32)]).reshape(1, -1)
    qkv = _matmul(h, w_in_t, w_t=True, n=c_v, tm=ROW_TILE, tn=COL_TILE, out_dtype=_BF16,
                  epilogue="gnorm", extra=gain_qkv, group=FOX_HEAD_DIM, gnorm_cols=c_k,
                  name="proj_qkv")
    y_fox = _fox_attention(qkv, kb, qb, b_start, b_end, qk_bound, FOX_TILE)

    lxlg = _matmul(h, w_in_t, w_t=True, n=2 * LRU_WIDTH, w_off=c_f, tm=ROW_TILE, tn=COL_TILE,
                   out_dtype=_F32, name="proj_lru")
    w_gate_lru = jnp.concatenate([w_rg_a, w_rg_x], axis=-1).astype(_BF16)
    b_gate_lru = jnp.concatenate([b_rg_a, b_rg_x], axis=-1).reshape(LRU_BLOCKS, 1, -1)
    y_lru = _lru_branch(lxlg, w_lru_conv, b_lru_conv, w_gate_lru, b_gate_lru, lru_lambda,
                        SEQ_TILE)

    gain_mq = (jnp.tile(g_q_mem, MEM_HEADS) * (MEM_HEAD_DIM ** -0.5)).reshape(1, -1)
    mq = _matmul(h, w_in_t, w_t=True, n=MEM_WIDTH, w_off=c_l, tm=ROW_TILE, tn=COL_TILE,
                 out_dtype=_BF16, epilogue="gnorm", extra=gain_mq, group=MEM_HEAD_DIM,
                 name="proj_mq")
    hm = _rmsnorm(mem, g_mem, mem.shape[0])
    mk = _matmul(hm, w_mem_kv, n=MEM_WIDTH, tm=mem.shape[0], tn=COL_TILE, out_dtype=_BF16,
                 epilogue="gnorm", extra=jnp.tile(g_k_mem, MEM_HEADS).reshape(1, -1),
                 group=MEM_HEAD_DIM, name="proj_mk")
    mv = _matmul(hm, w_mem_kv, n=MEM_WIDTH, w_off=MEM_WIDTH, tm=mem.shape[0], tn=COL_TILE,
                 out_dtype=_BF16, name="proj_mv")
    y_mem = _mem_attention(mq, mk, mv, ROW_TILE)

    w_gate_t = _cast_rows(w_in_t, c_m, N_BRANCH * D_MODEL, SEQ_TILE)
    merged = _gated_merge(h, y_fox, y_lru, y_mem, w_gate_t, w_branch.astype(_BF16), b_gate,
                          ROW_TILE, MERGE_COL_TILE)
    x2, h2 = _out_proj_norm(merged, w_out.astype(_BF16), x, g_ffn, SEQ_TILE)

    g = _ffn_up(h2, w_ffn_up, w_ffn_conv, b_ffn_conv, ROW_TILE, FFN_COL_TILE)
    return _matmul(g, w_ffn_down.astype(_BF16), n=D_MODEL, tm=ROW_TILE, tn=FFN_COL_TILE,
                   rows_outer=True,
                   out_dtype=_F32, epilogue="residual", extra=x2, name="ffn_down")


def kernel(x, mem, g_mix, w_in, b_f, g_q_fox, g_k_fox, w_lru_conv, b_lru_conv, w_rg_a, b_rg_a,
           w_rg_x, b_rg_x, lru_lambda, g_mem, w_mem_kv, g_q_mem, g_k_mem, b_gate, w_branch,
           w_out, g_ffn, w_ffn_up, w_ffn_conv, b_ffn_conv, w_ffn_down):
    depth = g_mix.shape[0]
    outs = []
    for b in range(x.shape[0]):
        xb = x[b]
        for l in range(depth):
            xb = _layer(xb, mem[b], g_mix[l], w_in[l], b_f[l], g_q_fox[l], g_k_fox[l],
                        w_lru_conv[l], b_lru_conv[l], w_rg_a[l], b_rg_a[l], w_rg_x[l],
                        b_rg_x[l], lru_lambda[l], g_mem[l], w_mem_kv[l], g_q_mem[l],
                        g_k_mem[l], b_gate[l], w_branch[l], w_out[l], g_ffn[l], w_ffn_up[l],
                        w_ffn_conv[l], b_ffn_conv[l], w_ffn_down[l])
        outs.append(xb)
    return outs[0][None] if len(outs) == 1 else jnp.stack(outs)
```

```python
import functools

import jax
import jax.numpy as jnp
import numpy as np
from jax import lax
from jax.experimental import pallas as pl
from jax.experimental.pallas import tpu as pltpu

D_MODEL = 2048
FOX_HEADS = 8
FOX_HEAD_DIM = 128
FOX_WIDTH = FOX_HEADS * FOX_HEAD_DIM
LRU_WIDTH = 1024
LRU_BLOCKS = 8
LRU_BLOCK_DIM = LRU_WIDTH // LRU_BLOCKS
LRU_CONV = 4
LRU_C = 8.0
MEM_HEADS = 4
MEM_HEAD_DIM = 256
MEM_WIDTH = MEM_HEADS * MEM_HEAD_DIM
N_BRANCH = 3
FFN_HIDDEN = 5632
FFN_CONV = 3
EPS = 1e-6

V7X_SUBLANES = 8
V7X_LANES = 128
MASK_VALUE = -1e30
LOG2_E = 1.4426950408889634
ROW_TILE = 1024
COL_TILE = 1024
FFN_COL_TILE = 512
MERGE_COL_TILE = 512
SEQ_TILE = 512
FOX_TILE = 512
FOX_SKIP_BITS = 64.0
FOX_BIAS_LANES = 16
FOX_FAST_MAX_LOGIT = 48.0
LRU_SCAN_UNROLL = 8

_BF16 = jnp.bfloat16
_F32 = jnp.float32


def _params(semantics, vmem_mib):
    return pltpu.CompilerParams(dimension_semantics=semantics,
                                vmem_limit_bytes=vmem_mib * 1024 * 1024)


def _dot_nt(a, w_t):
    return lax.dot_general(a, w_t, (((1,), (1,)), ((), ())), preferred_element_type=_F32)


def _mm_kernel(*refs, epilogue, group, cast_w, w_t):
    a_ref, w_ref = refs[0], refs[1]
    if cast_w:
        o_ref, wb_ref = refs[-2], refs[-1]

        @pl.when(pl.program_id(1) == 0)
        def _():
            wb_ref[...] = w_ref[...].astype(_BF16)

        w = wb_ref[...]
    else:
        o_ref = refs[-1]
        w = w_ref[...]
    if w_t:
        acc = _dot_nt(a_ref[...], w)
    else:
        acc = jnp.dot(a_ref[...], w, preferred_element_type=_F32)
    if epilogue == "gnorm":
        g_ref = refs[2]
        for gi in range(acc.shape[1] // group):
            sl = slice(gi * group, (gi + 1) * group)
            blk = acc[:, sl]
            ms = jnp.mean(blk * blk, axis=-1, keepdims=True)
            o_ref[:, sl] = (blk * lax.rsqrt(ms + EPS) * g_ref[:, sl]).astype(o_ref.dtype)
    elif epilogue == "residual":
        o_ref[...] = (refs[2][...] + acc).astype(o_ref.dtype)
    else:
        o_ref[...] = acc.astype(o_ref.dtype)


def _matmul(a, w, *, n, tm, tn, out_dtype, w_t=False, w_off=0, rows_outer=False,
            epilogue="plain", extra=None, group=None, name):
    m, k = a.shape
    cast_w = w.dtype != _BF16
    assert not (cast_w and rows_outer) and n % tn == 0 and m % tm == 0
    if rows_outer:
        grid = (m // tm, n // tn)
        ij = lambda i, j: (i, j)
    else:
        grid = (n // tn, m // tm)
        ij = lambda j, i: (i, j)
    if w_t:
        assert w_off % V7X_SUBLANES == 0
        w_block = (tn, k)
        w_spec = pl.BlockSpec((pl.Element(tn), pl.Element(k)),
                              lambda *g: (pl.multiple_of(w_off + ij(*g)[1] * tn, V7X_SUBLANES), 0))
    else:
        assert w_off % tn == 0
        w_block = (k, tn)
        w_spec = pl.BlockSpec(w_block, lambda *g: (0, w_off // tn + ij(*g)[1]))
    in_specs = [pl.BlockSpec((tm, k), lambda *g: (ij(*g)[0], 0)), w_spec]
    args = [a, w]
    block_bytes = (tm * k * 2 + k * tn * w.dtype.itemsize
                   + tm * tn * jnp.dtype(out_dtype).itemsize)
    if epilogue == "gnorm":
        in_specs.append(pl.BlockSpec((1, tn), lambda *g: (0, ij(*g)[1])))
        args.append(extra)
    elif epilogue == "residual":
        in_specs.append(pl.BlockSpec((tm, tn), lambda *g: ij(*g)))
        args.append(extra)
        block_bytes += tm * tn * extra.dtype.itemsize
    scratch = [pltpu.VMEM(w_block, _BF16)] if cast_w else []
    vmem_mib = -(-(2 * block_bytes + cast_w * k * tn * 2 + 2 * tm * tn * 4) // 2 ** 20) + 2
    return pl.pallas_call(
        functools.partial(_mm_kernel, epilogue=epilogue, group=group, cast_w=cast_w, w_t=w_t),
        out_shape=jax.ShapeDtypeStruct((m, n), out_dtype),
        grid=grid,
        in_specs=in_specs,
        out_specs=pl.BlockSpec((tm, tn), lambda *g: ij(*g)),
        scratch_shapes=scratch,
        compiler_params=_params(("parallel", "arbitrary"), vmem_mib),
        name=name,
    )(*args)


def _split3(x):
    hi = x.astype(_BF16)
    r1 = x - hi.astype(_F32)
    mid = r1.astype(_BF16)
    lo = (r1 - mid.astype(_F32)).astype(_BF16)
    return hi, mid, lo


def _forget_bias_kernel(shift_ref, x_ref, g_ref, w_ref, b_ref, sel_ref, ones_ref, h_ref, kb_ref,
                        qb_ref, edge_ref, carry_ref):
    @pl.when(pl.program_id(0) == 0)
    def _():
        carry_ref[...] = jnp.zeros_like(carry_ref)

    tc = x_ref.shape[0]
    x = x_ref[...]
    ms = jnp.mean(x * x, axis=-1, keepdims=True)
    h = (x * lax.rsqrt(ms + EPS) * g_ref[...]).astype(_BF16)
    h_ref[...] = h
    w = jnp.concatenate([w_ref[...], jnp.zeros((V7X_LANES - FOX_HEADS, w_ref.shape[1]), _F32)],
                        axis=0).astype(_BF16)
    z = _dot_nt(h, w) + b_ref[...]
    neg_log_f = (jnp.log1p(jnp.exp(-jnp.abs(z))) - jnp.minimum(z, 0.0)) * LOG2_E
    row = lax.broadcasted_iota(jnp.int32, (tc, tc), 0)
    col = lax.broadcasted_iota(jnp.int32, (tc, tc), 1)
    tri = jnp.where(col <= row, 1.0, 0.0).astype(_BF16)
    c = carry_ref[0:1, :]
    for part in _split3(neg_log_f):
        c = c + jnp.dot(tri, part, preferred_element_type=_F32)
    carry_ref[...] = jnp.broadcast_to(c[tc - 1:tc, :], carry_ref.shape)
    edge_ref[...] = jnp.concatenate(
        [c[0:1, :], c[tc - 1:tc, :], jnp.zeros((V7X_SUBLANES - 2, c.shape[1]), _F32)], axis=0)
    parts = jnp.concatenate(_split3(c) + _split3(-(c + shift_ref[0])), axis=1)
    routed = jnp.dot(parts, sel_ref[...], preferred_element_type=_F32) + ones_ref[...]
    half = kb_ref.shape[1]
    kb_ref[...] = routed[:, :half].astype(kb_ref.dtype)
    qb_ref[...] = routed[:, half:].astype(qb_ref.dtype)


def _norm_forget_bias(x, g, w_t, w_row0, b_pad, shift, tc):
    s, d = x.shape
    lanes = V7X_LANES
    width = lanes
    nt = s // tc
    sel = np.zeros((6 * lanes, 2 * width), np.float32)
    ones = np.zeros((1, 2 * width), np.float32)
    for head in range(FOX_HEADS):
        for part in range(3):
            sel[part * lanes + head, FOX_BIAS_LANES * head + part] = 1.0
            sel[(3 + part) * lanes + head, width + FOX_BIAS_LANES * head + 3 + part] = 1.0
            ones[0, FOX_BIAS_LANES * head + 3 + part] = 1.0
            ones[0, width + FOX_BIAS_LANES * head + part] = 1.0
    h, kb, qb, edges = pl.pallas_call(
        _forget_bias_kernel,
        out_shape=(jax.ShapeDtypeStruct((s, d), _BF16),
                   jax.ShapeDtypeStruct((s, width), _BF16),
                   jax.ShapeDtypeStruct((s, width), _BF16),
                   jax.ShapeDtypeStruct((nt * V7X_SUBLANES, lanes), _F32)),
        grid=(nt,),
        in_specs=[pl.BlockSpec(memory_space=pltpu.SMEM),
                  pl.BlockSpec((tc, d), lambda i: (i, 0)),
                  pl.BlockSpec((1, d), lambda i: (0, 0)),
                  pl.BlockSpec((pl.Element(FOX_HEADS), pl.Element(d)), lambda i: (w_row0, 0)),
                  pl.BlockSpec((1, lanes), lambda i: (0, 0)),
                  pl.BlockSpec((6 * lanes, 2 * width), lambda i: (0, 0)),
                  pl.BlockSpec((1, 2 * width), lambda i: (0, 0))],
        out_specs=(pl.BlockSpec((tc, d), lambda i: (i, 0)),
                   pl.BlockSpec((tc, width), lambda i: (i, 0)),
                   pl.BlockSpec((tc, width), lambda i: (i, 0)),
                   pl.BlockSpec((V7X_SUBLANES, lanes), lambda i: (i, 0))),
        scratch_shapes=[pltpu.VMEM((V7X_SUBLANES, lanes), _F32)],
        compiler_params=_params(("arbitrary",), 40),
        name="norm_forget_bias",
    )(shift.reshape(1).astype(_F32), x, g.reshape(1, d), w_t, b_pad, jnp.asarray(sel, _BF16),
      jnp.asarray(ones))
    edges = edges.reshape(nt, V7X_SUBLANES, lanes)
    return h, kb, qb, edges[:, 0, :FOX_HEADS], edges[:, 1, :FOX_HEADS]


def _unpack_bias(packed, sel_ref):
    return jnp.dot(packed, sel_ref[...], preferred_element_type=_F32).astype(_BF16)


def _fox_kernel(bstart_ref, bend_ref, thr_ref, q_ref, k_ref, kbp_ref, sel_ref, v_ref, o_ref,
                acc_ref, kb_ref, *, tq):
    head = pl.program_id(0)
    qi = pl.program_id(1)
    b_tile = bstart_ref[qi, head]
    thr = thr_ref[0]
    j0 = lax.fori_loop(
        0, qi, lambda j, n: n + jnp.where(b_tile - bend_ref[j, head] >= thr, 1, 0), 0)

    @pl.when(qi == 0)
    def _():
        kb_ref[...] = _unpack_bias(kbp_ref[...], sel_ref)

    lane = lax.broadcasted_iota(jnp.int32, (tq, V7X_LANES), 1)
    ones3 = jnp.where(lane < 3, 1.0, 0.0).astype(_BF16)
    q_aug = jnp.concatenate([q_ref[...], ones3], axis=1)
    acc_ref[...] = jnp.zeros_like(acc_ref)

    def logits(j):
        ks = pl.multiple_of(j * tq, tq)
        k_aug = jnp.concatenate([k_ref[pl.ds(ks, tq), :], kb_ref[pl.ds(ks, tq), :]], axis=1)
        return lax.dot_general(k_aug, q_aug, (((1,), (1,)), ((), ())),
                               preferred_element_type=_F32)

    def accumulate(j, s, m_prev, l_prev):
        m_new = jnp.maximum(m_prev, jnp.max(s, axis=0, keepdims=True))
        alpha = jnp.exp2(m_prev - m_new)
        p = jnp.exp2(s - m_new)
        l_new = alpha * l_prev + jnp.sum(p, axis=0, keepdims=True)
        ks = pl.multiple_of(j * tq, tq)
        pv = lax.dot_general(v_ref[pl.ds(ks, tq), :], p.astype(_BF16),
                             (((0,), (0,)), ((), ())), preferred_element_type=_F32)
        acc_ref[...] = alpha * acc_ref[...] + pv
        return m_new, l_new

    def body(j, carry):
        m_prev, l_prev, s = carry
        s_next = logits(j + 1)
        m_new, l_new = accumulate(j, s, m_prev, l_prev)
        return m_new, l_new, s_next

    m0 = jnp.full((1, tq), MASK_VALUE, _F32)
    l0 = jnp.zeros((1, tq), _F32)
    m, l, s = lax.fori_loop(j0, qi, body, (m0, l0, logits(j0)))
    row = lax.broadcasted_iota(jnp.int32, s.shape, 0)
    col = lax.broadcasted_iota(jnp.int32, s.shape, 1)
    m, l = accumulate(qi, jnp.where(row <= col, s, MASK_VALUE), m, l)
    o_ref[...] = (acc_ref[...] / l).T.astype(o_ref.dtype)


def _fox_fast_kernel(bstart_ref, bend_ref, thr_ref, q_ref, qbp_ref, k_ref, kbp_ref, sel_ref, v_ref,
                     o_ref, acc_ref, kb_ref, l_ref, s0_ref, s1_ref, *, tq):
    head = pl.program_id(0)
    qi = pl.program_id(1)
    b_tile = bstart_ref[qi, head]
    thr = thr_ref[0]
    j0 = lax.fori_loop(
        0, qi, lambda j, n: n + jnp.where(b_tile - bend_ref[j, head] >= thr, 1, 0), 0)

    @pl.when(qi == 0)
    def _():
        kb_ref[...] = _unpack_bias(kbp_ref[...], sel_ref)

    q_aug = jnp.concatenate([q_ref[...], _unpack_bias(qbp_ref[...], sel_ref)], axis=1)
    acc_ref[...] = jnp.zeros_like(acc_ref)

    def logits(j):
        ks = pl.multiple_of(j * tq, tq)
        k_aug = jnp.concatenate([k_ref[pl.ds(ks, tq), :], kb_ref[pl.ds(ks, tq), :]], axis=1)
        return _dot_nt(k_aug, q_aug)

    def accumulate(j, s):
        p = jnp.exp2(s)
        ks = pl.multiple_of(j * tq, tq)
        acc_ref[...] += lax.dot_general(v_ref[pl.ds(ks, tq), :], p.astype(_BF16),
                                        (((0,), (0,)), ((), ())), preferred_element_type=_F32)
        l_ref[...] += jnp.sum(p, axis=0, keepdims=True)

    l_ref[...] = jnp.zeros_like(l_ref)
    row = lax.broadcasted_iota(jnp.int32, s1_ref.shape, 0)
    col = lax.broadcasted_iota(jnp.int32, s1_ref.shape, 1)
    s_diag = jnp.where(row <= col, logits(qi), MASK_VALUE)
    s0_ref[...] = logits(j0)
    accumulate(qi, s_diag)
    n_before = qi - j0

    def pair(t, carry):
        j = j0 + 2 * t
        s1_ref[...] = logits(j + 1)
        accumulate(j, s0_ref[...])
        s0_ref[...] = logits(j + 2)
        accumulate(j + 1, s1_ref[...])
        return carry

    lax.fori_loop(0, n_before // 2, pair, 0)

    @pl.when(n_before % 2 == 1)
    def _():
        accumulate(qi - 1, s0_ref[...])

    o_ref[...] = (acc_ref[...] / l_ref[...]).T.astype(o_ref.dtype)


def _fox_attention(qk, kb, qb, b_start, b_end, qk_bound, v, tq):
    s = v.shape[0]
    hd = FOX_HEAD_DIM
    lanes = V7X_LANES
    thr = (2.0 * qk_bound + FOX_SKIP_BITS).reshape(1).astype(_F32)
    sel = np.zeros((FOX_HEADS, lanes, lanes), np.float32)
    for head in range(FOX_HEADS):
        for c in range(FOX_BIAS_LANES):
            sel[head, FOX_BIAS_LANES * head + c, c] = 1.0
    sel = jnp.asarray(sel, _BF16)
    smem = pl.BlockSpec(memory_space=pltpu.SMEM)
    q_spec = pl.BlockSpec((tq, hd), lambda h, i: (i, h))
    k_spec = pl.BlockSpec((s, hd), lambda h, i: (0, FOX_HEADS + h))
    v_spec = pl.BlockSpec((s, hd), lambda h, i: (0, h))
    kb_spec = pl.BlockSpec((s, lanes), lambda h, i: (0, 0))
    qb_spec = pl.BlockSpec((tq, lanes), lambda h, i: (i, 0))
    sel_spec = pl.BlockSpec((None, lanes, lanes), lambda h, i: (h, 0, 0))

    def call(kernel, in_specs, extra_scratch, *args):
        return pl.pallas_call(
            functools.partial(kernel, tq=tq),
            out_shape=jax.ShapeDtypeStruct((s, FOX_WIDTH), _BF16),
            grid=(FOX_HEADS, s // tq),
            in_specs=[smem, smem, smem] + in_specs,
            out_specs=q_spec,
            scratch_shapes=[pltpu.VMEM((hd, tq), _F32),
                            pltpu.VMEM((s, lanes), _BF16)] + extra_scratch,
            compiler_params=_params(("parallel", "arbitrary"), 32),
            name=kernel.__name__.strip("_"),
        )(b_start, b_end, thr, *args)

    fast_scratch = [pltpu.VMEM((1, tq), _F32), pltpu.VMEM((tq, tq), _F32),
                    pltpu.VMEM((tq, tq), _F32)]
    return lax.cond(
        qk_bound <= FOX_FAST_MAX_LOGIT,
        lambda: call(_fox_fast_kernel, [q_spec, qb_spec, k_spec, kb_spec, sel_spec, v_spec],
                     fast_scratch, qk, qb, qk, kb, sel, v),
        lambda: call(_fox_kernel, [q_spec, k_spec, kb_spec, sel_spec, v_spec], [],
                     qk, qk, kb, sel, v))


def _shift_rows(ext, d):
    return pltpu.roll(ext, d, axis=0)[V7X_SUBLANES:]


def _lru_kernel(lx_ref, lg_ref, wc_ref, bc_ref, wg_ref, bg_ref, lam_ref, o_ref,
                halo_ref, carry_ref, a_ref, h_ref):
    @pl.when(pl.program_id(0) == 0)
    def _():
        halo_ref[...] = jnp.zeros_like(halo_ref)
        carry_ref[...] = jnp.zeros_like(carry_ref)

    lx = lx_ref[...]
    ts = lx.shape[0]
    ext = jnp.concatenate([halo_ref[...], lx], axis=0)
    xr = (wc_ref[3:4, :] * lx + wc_ref[2:3, :] * _shift_rows(ext, 1)
          + wc_ref[1:2, :] * _shift_rows(ext, 2) + wc_ref[0:1, :] * _shift_rows(ext, 3)
          + bc_ref[...])
    halo_ref[...] = lx[ts - V7X_SUBLANES:]

    lam = lam_ref[...]
    log_sig_lam = jnp.minimum(lam, 0.0) - jnp.log1p(jnp.exp(-jnp.abs(lam)))
    bd = LRU_BLOCK_DIM
    for nb in range(LRU_BLOCKS):
        sl = slice(nb * bd, (nb + 1) * bd)
        x_nb = xr[:, sl]
        gates = jax.nn.sigmoid(jnp.dot(x_nb.astype(_BF16), wg_ref[nb],
                                       preferred_element_type=_F32) + bg_ref[nb])
        log_a = LRU_C * gates[:, :bd] * log_sig_lam[:, sl]
        a_ref[:, sl] = jnp.exp(log_a)
        t = jnp.tanh(log_a)
        h_ref[:, sl] = jnp.sqrt(-2.0 * t / (1.0 - t)) * (gates[:, bd:] * x_nb)

    sub = lax.broadcasted_iota(jnp.int32, (V7X_SUBLANES, 1), 0)

    def group(g, carry):
        rows = pl.ds(pl.multiple_of(g * V7X_SUBLANES, V7X_SUBLANES), V7X_SUBLANES)
        a = a_ref[rows, :]
        h = h_ref[rows, :]
        for d in (1, 2, 4):
            valid = sub >= d
            h = h + a * jnp.where(valid, pltpu.roll(h, d, axis=0), 0.0)
            a = a * jnp.where(valid, pltpu.roll(a, d, axis=0), 1.0)
        h = h + a * carry
        h_ref[rows, :] = h
        return jnp.broadcast_to(h[V7X_SUBLANES - 1:, :], h.shape)

    carry_ref[...] = lax.fori_loop(0, ts // V7X_SUBLANES, group, carry_ref[...],
                                   unroll=LRU_SCAN_UNROLL)
    o_ref[...] = (h_ref[...] * jax.nn.gelu(lg_ref[...])).astype(o_ref.dtype)


def _lru_branch(lxlg, w_conv, b_conv, w_gate, b_gate, lam, ts):
    s = lxlg.shape[0]
    w = LRU_WIDTH
    bd = LRU_BLOCK_DIM
    return pl.pallas_call(
        _lru_kernel,
        out_shape=jax.ShapeDtypeStruct((s, w), _BF16),
        grid=(s // ts,),
        in_specs=[pl.BlockSpec((ts, w), lambda i: (i, 0)),
                  pl.BlockSpec((ts, w), lambda i: (i, 1)),
                  pl.BlockSpec((LRU_CONV, w), lambda i: (0, 0)),
                  pl.BlockSpec((1, w), lambda i: (0, 0)),
                  pl.BlockSpec((LRU_BLOCKS, bd, 2 * bd), lambda i: (0, 0, 0)),
                  pl.BlockSpec((LRU_BLOCKS, 1, 2 * bd), lambda i: (0, 0, 0)),
                  pl.BlockSpec((1, w), lambda i: (0, 0))],
        out_specs=pl.BlockSpec((ts, w), lambda i: (i, 0)),
        scratch_shapes=[pltpu.VMEM((V7X_SUBLANES, w), _F32), pltpu.VMEM((V7X_SUBLANES, w), _F32),
                        pltpu.VMEM((ts, w), _F32), pltpu.VMEM((ts, w), _F32)],
        compiler_params=_params(("arbitrary",), 40),
        name="conv_rglru",
    )(lxlg, lxlg, w_conv, b_conv.reshape(1, -1), w_gate, b_gate, lam.reshape(1, -1))


def _mem_kv_kernel(mem_ref, g_ref, w_ref, gk_ref, k_ref, v_ref):
    x = mem_ref[...]
    ms = jnp.mean(x * x, axis=-1, keepdims=True)
    hm = (x * lax.rsqrt(ms + EPS) * g_ref[...]).astype(_BF16)
    acc = jnp.dot(hm, w_ref[...].astype(_BF16), preferred_element_type=_F32)

    @pl.when(pl.program_id(0) == 0)
    def _():
        for gi in range(MEM_HEADS):
            sl = slice(gi * MEM_HEAD_DIM, (gi + 1) * MEM_HEAD_DIM)
            blk = acc[:, sl]
            ms_k = jnp.mean(blk * blk, axis=-1, keepdims=True)
            k_ref[:, sl] = (blk * lax.rsqrt(ms_k + EPS) * gk_ref[:, sl]).astype(k_ref.dtype)

    @pl.when(pl.program_id(0) == 1)
    def _():
        v_ref[...] = acc.astype(v_ref.dtype)


def _mem_kv(mem, g_mem, w_kv, g_k):
    m, d = mem.shape
    w = MEM_WIDTH
    const = lambda j: (0, 0)
    return pl.pallas_call(
        _mem_kv_kernel,
        out_shape=(jax.ShapeDtypeStruct((m, w), _BF16), jax.ShapeDtypeStruct((m, w), _BF16)),
        grid=(2,),
        in_specs=[pl.BlockSpec((m, d), const), pl.BlockSpec((1, d), const),
                  pl.BlockSpec((d, w), lambda j: (0, j)), pl.BlockSpec((1, w), const)],
        out_specs=(pl.BlockSpec((m, w), const), pl.BlockSpec((m, w), const)),
        compiler_params=_params(("arbitrary",), 40),
        name="mem_kv",
    )(mem, g_mem.reshape(1, d), w_kv, jnp.tile(g_k, MEM_HEADS).reshape(1, w))


def _mem_attn_kernel(q_ref, k_ref, v_ref, o_ref):
    s = lax.dot_general(q_ref[...], k_ref[...], (((1,), (1,)), ((), ())),
                        preferred_element_type=_F32)
    m = jnp.max(s, axis=-1, keepdims=True)
    p = jnp.exp(s - m)
    l = jnp.sum(p, axis=-1, keepdims=True)
    acc = jnp.dot(p.astype(_BF16), v_ref[...], preferred_element_type=_F32)
    o_ref[...] = (acc / l).astype(o_ref.dtype)


def _mem_attention(q, k, v, ts):
    s = q.shape[0]
    m = k.shape[0]
    hd = MEM_HEAD_DIM
    return pl.pallas_call(
        _mem_attn_kernel,
        out_shape=jax.ShapeDtypeStruct((s, MEM_WIDTH), _BF16),
        grid=(s // ts, MEM_HEADS),
        in_specs=[pl.BlockSpec((ts, hd), lambda i, h: (i, h)),
                  pl.BlockSpec((m, hd), lambda i, h: (0, h)),
                  pl.BlockSpec((m, hd), lambda i, h: (0, h))],
        out_specs=pl.BlockSpec((ts, hd), lambda i, h: (i, h)),
        compiler_params=_params(("parallel", "arbitrary"), 32),
        name="mem_attention",
    )(q, k, v)


def _merge_kernel(h_ref, yf_ref, yl_ref, ym_ref, wg0_ref, wg1_ref, wg2_ref, wb_ref, bg_ref,
                  o_ref):
    h = h_ref[...]
    merged = None
    for n, (y_ref, wg_ref) in enumerate(((yf_ref, wg0_ref), (yl_ref, wg1_ref),
                                         (ym_ref, wg2_ref))):
        gate = jax.nn.sigmoid(_dot_nt(h, wg_ref[...]) + bg_ref[n:n + 1, :])
        term = gate * jnp.dot(y_ref[...], wb_ref[n], preferred_element_type=_F32)
        merged = term if merged is None else merged + term
    o_ref[...] = merged.astype(o_ref.dtype)


def _gated_merge(h, y_fox, y_lru, y_mem, w_gate_t, w_branch, b_gate, tm, tn):
    s, d = h.shape
    nj = d // tn
    bw = y_fox.shape[1]
    y_spec = pl.BlockSpec((tm, bw), lambda i, j: (i, 0))
    return pl.pallas_call(
        _merge_kernel,
        out_shape=jax.ShapeDtypeStruct((s, d), _BF16),
        grid=(s // tm, nj),
        in_specs=[pl.BlockSpec((tm, d), lambda i, j: (i, 0)), y_spec, y_spec, y_spec,
                  pl.BlockSpec((tn, d), lambda i, j: (j, 0)),
                  pl.BlockSpec((tn, d), lambda i, j: (nj + j, 0)),
                  pl.BlockSpec((tn, d), lambda i, j: (2 * nj + j, 0)),
                  pl.BlockSpec((N_BRANCH, bw, tn), lambda i, j: (0, 0, j)),
                  pl.BlockSpec((N_BRANCH, tn), lambda i, j: (0, j))],
        out_specs=pl.BlockSpec((tm, tn), lambda i, j: (i, j)),
        compiler_params=_params(("parallel", "arbitrary"), 56),
        name="gated_merge",
    )(h, y_fox, y_lru, y_mem, w_gate_t, w_gate_t, w_gate_t, w_branch, b_gate)


def _out_norm_kernel(a_ref, w_ref, x_ref, g_ref, x2_ref, h2_ref):
    x2 = x_ref[...] + jnp.dot(a_ref[...], w_ref[...], preferred_element_type=_F32)
    x2_ref[...] = x2
    ms = jnp.mean(x2 * x2, axis=-1, keepdims=True)
    h2_ref[...] = (x2 * lax.rsqrt(ms + EPS) * g_ref[...]).astype(h2_ref.dtype)


def _out_proj_norm(a, w, x, g, tm):
    m, k = a.shape
    d = w.shape[1]
    row = lambda i: (i, 0)
    return pl.pallas_call(
        _out_norm_kernel,
        out_shape=(jax.ShapeDtypeStruct((m, d), _F32), jax.ShapeDtypeStruct((m, d), _BF16)),
        grid=(m // tm,),
        in_specs=[pl.BlockSpec((tm, k), row), pl.BlockSpec((k, d), lambda i: (0, 0)),
                  pl.BlockSpec((tm, d), row), pl.BlockSpec((1, d), lambda i: (0, 0))],
        out_specs=(pl.BlockSpec((tm, d), row), pl.BlockSpec((tm, d), row)),
        compiler_params=_params(("parallel",), 48),
        name="proj_out_norm",
    )(a, w, x, g.reshape(1, d))


def _cast_kernel(w_ref, o_ref):
    o_ref[...] = w_ref[...].astype(o_ref.dtype)


def _cast_rows(w, row0, rows, tr):
    k = w.shape[1]
    assert row0 % V7X_SUBLANES == 0 and rows % tr == 0
    return pl.pallas_call(
        _cast_kernel,
        out_shape=jax.ShapeDtypeStruct((rows, k), _BF16),
        grid=(rows // tr,),
        in_specs=[pl.BlockSpec((pl.Element(tr), pl.Element(k)),
                               lambda i: (pl.multiple_of(row0 + i * tr, V7X_SUBLANES), 0))],
        out_specs=pl.BlockSpec((tr, k), lambda i: (i, 0)),
        compiler_params=_params(("parallel",), 32),
        name="cast_rows",
    )(w)


def _ffn_up_kernel(a_ref, wa_ref, wv_ref, wca_ref, wcv_ref, bca_ref, bcv_ref, o_ref,
                   halo_a_ref, halo_v_ref, wa_bf_ref, wv_bf_ref):
    @pl.when(pl.program_id(1) == 0)
    def _():
        halo_a_ref[...] = jnp.zeros_like(halo_a_ref)
        halo_v_ref[...] = jnp.zeros_like(halo_v_ref)
        wa_bf_ref[...] = wa_ref[...].astype(_BF16)
        wv_bf_ref[...] = wv_ref[...].astype(_BF16)

    a = a_ref[...]
    tm = a.shape[0]

    def conv(up, halo_ref, wc_ref, bc_ref):
        ext = jnp.concatenate([halo_ref[...], up], axis=0)
        halo_ref[...] = up[tm - V7X_SUBLANES:]
        return (wc_ref[2:3, :] * up + wc_ref[1:2, :] * _shift_rows(ext, 1)
                + wc_ref[0:1, :] * _shift_rows(ext, 2) + bc_ref[...])

    act = conv(jnp.dot(a, wa_bf_ref[...], preferred_element_type=_F32), halo_a_ref, wca_ref,
               bca_ref)
    val = conv(jnp.dot(a, wv_bf_ref[...], preferred_element_type=_F32), halo_v_ref, wcv_ref,
               bcv_ref)
    o_ref[...] = (jax.nn.gelu(act) * val).astype(o_ref.dtype)


def _ffn_up(h2, w_up, w_conv, b_conv, tm, tn):
    s, d = h2.shape
    f = FFN_HIDDEN
    nj = f // tn
    return pl.pallas_call(
        _ffn_up_kernel,
        out_shape=jax.ShapeDtypeStruct((s, f), _BF16),
        grid=(nj, s // tm),
        in_specs=[pl.BlockSpec((tm, d), lambda j, i: (i, 0)),
                  pl.BlockSpec((d, tn), lambda j, i: (0, j)),
                  pl.BlockSpec((d, tn), lambda j, i: (0, nj + j)),
                  pl.BlockSpec((FFN_CONV, tn), lambda j, i: (0, j)),
                  pl.BlockSpec((FFN_CONV, tn), lambda j, i: (0, nj + j)),
                  pl.BlockSpec((1, tn), lambda j, i: (0, j)),
                  pl.BlockSpec((1, tn), lambda j, i: (0, nj + j))],
        out_specs=pl.BlockSpec((tm, tn), lambda j, i: (i, j)),
        scratch_shapes=[pltpu.VMEM((V7X_SUBLANES, tn), _F32),
                        pltpu.VMEM((V7X_SUBLANES, tn), _F32),
                        pltpu.VMEM((d, tn), _BF16), pltpu.VMEM((d, tn), _BF16)],
        compiler_params=_params(("parallel", "arbitrary"), 56),
        name="ffn_up_conv_geglu",
    )(h2, w_up, w_up, w_conv, w_conv, b_conv.reshape(1, -1), b_conv.reshape(1, -1))


def _layer(x, mem, g_mix, w_in, b_f, g_q_fox, g_k_fox, w_lru_conv, b_lru_conv, w_rg_a, b_rg_a,
           w_rg_x, b_rg_x, lru_lambda, g_mem, w_mem_kv, g_q_mem, g_k_mem, b_gate, w_branch,
           w_out, g_ffn, w_ffn_up, w_ffn_conv, b_ffn_conv, w_ffn_down):
    c_k = 2 * FOX_WIDTH
    c_v = c_k + FOX_WIDTH
    c_f = c_v + FOX_HEADS
    c_l = c_f + 2 * LRU_WIDTH
    c_m = c_l + MEM_WIDTH

    w_in_t = w_in.T

    g_q_scaled = g_q_fox * (LOG2_E * FOX_HEAD_DIM ** -0.5)
    qk_bound = (1.02 * FOX_HEAD_DIM) * jnp.max(jnp.abs(g_q_scaled)) * jnp.max(jnp.abs(g_k_fox))
    b_pad = jnp.pad(b_f.reshape(1, -1), ((0, 0), (0, V7X_LANES - FOX_HEADS)))
    h, kb, qb, b_start, b_end = _norm_forget_bias(x, g_mix, w_in_t, c_v, b_pad, qk_bound,
                                                  FOX_TILE)
    gain_qk = jnp.concatenate([jnp.tile(g_q_scaled, FOX_HEADS),
                               jnp.tile(g_k_fox, FOX_HEADS)]).reshape(1, -1)
    qk = _matmul(h, w_in_t, w_t=True, n=c_k, tm=ROW_TILE, tn=COL_TILE, out_dtype=_BF16,
                 epilogue="gnorm", extra=gain_qk, group=FOX_HEAD_DIM, name="proj_qk")
    v = _matmul(h, w_in_t, w_t=True, n=FOX_WIDTH, w_off=c_k, tm=ROW_TILE, tn=COL_TILE,
                out_dtype=_BF16, name="proj_v")
    y_fox = _fox_attention(qk, kb, qb, b_start, b_end, qk_bound, v, FOX_TILE)

    lxlg = _matmul(h, w_in_t, w_t=True, n=2 * LRU_WIDTH, w_off=c_f, tm=ROW_TILE, tn=COL_TILE,
                   out_dtype=_F32, name="proj_lru")
    w_gate_lru = jnp.concatenate([w_rg_a, w_rg_x], axis=-1).astype(_BF16)
    b_gate_lru = jnp.concatenate([b_rg_a, b_rg_x], axis=-1).reshape(LRU_BLOCKS, 1, -1)
    y_lru = _lru_branch(lxlg, w_lru_conv, b_lru_conv, w_gate_lru, b_gate_lru, lru_lambda,
                        SEQ_TILE)

    gain_mq = (jnp.tile(g_q_mem, MEM_HEADS) * (MEM_HEAD_DIM ** -0.5)).reshape(1, -1)
    mq = _matmul(h, w_in_t, w_t=True, n=MEM_WIDTH, w_off=c_l, tm=ROW_TILE, tn=COL_TILE,
                 out_dtype=_BF16, epilogue="gnorm", extra=gain_mq, group=MEM_HEAD_DIM,
                 name="proj_mq")
    mk, mv = _mem_kv(mem, g_mem, w_mem_kv, g_k_mem)
    y_mem = _mem_attention(mq, mk, mv, ROW_TILE)

    w_gate_t = _cast_rows(w_in_t, c_m, N_BRANCH * D_MODEL, SEQ_TILE)
    merged = _gated_merge(h, y_fox, y_lru, y_mem, w_gate_t, w_branch.astype(_BF16), b_gate,
                          ROW_TILE, MERGE_COL_TILE)
    x2, h2 = _out_proj_norm(merged, w_out.astype(_BF16), x, g_ffn, SEQ_TILE)

    g = _ffn_up(h2, w_ffn_up, w_ffn_conv, b_ffn_conv, ROW_TILE, FFN_COL_TILE)
    return _matmul(g, w_ffn_down.astype(_BF16), n=D_MODEL, tm=ROW_TILE, tn=FFN_COL_TILE,
                   rows_outer=True,
                   out_dtype=_F32, epilogue="residual", extra=x2, name="ffn_down")


def kernel(x, mem, g_mix, w_in, b_f, g_q_fox, g_k_fox, w_lru_conv, b_lru_conv, w_rg_a, b_rg_a,
           w_rg_x, b_rg_x, lru_lambda, g_mem, w_mem_kv, g_q_mem, g_k_mem, b_gate, w_branch,
           w_out, g_ffn, w_ffn_up, w_ffn_conv, b_ffn_conv, w_ffn_down):
    depth = g_mix.shape[0]
    outs = []
    for b in range(x.shape[0]):
        xb = x[b]
        for l in range(depth):
            xb = _layer(xb, mem[b], g_mix[l], w_in[l], b_f[l], g_q_fox[l], g_k_fox[l],
                        w_lru_conv[l], b_lru_conv[l], w_rg_a[l], b_rg_a[l], w_rg_x[l],
                        b_rg_x[l], lru_lambda[l], g_mem[l], w_mem_kv[l], g_q_mem[l],
                        g_k_mem[l], b_gate[l], w_branch[l], w_out[l], g_ffn[l], w_ffn_up[l],
                        w_ffn_conv[l], b_ffn_conv[l], w_ffn_down[l])
        outs.append(xb)
    return outs[0][None] if len(outs) == 1 else jnp.stack(outs)
```

```python
import functools

import jax
import jax.numpy as jnp
import numpy as np
from jax import lax
from jax.experimental import pallas as pl
from jax.experimental.pallas import tpu as pltpu

D_MODEL = 2048
FOX_HEADS = 8
FOX_HEAD_DIM = 128
FOX_WIDTH = FOX_HEADS * FOX_HEAD_DIM
LRU_WIDTH = 1024
LRU_BLOCKS = 8
LRU_BLOCK_DIM = LRU_WIDTH // LRU_BLOCKS
LRU_CONV = 4
LRU_C = 8.0
MEM_HEADS = 4
MEM_HEAD_DIM = 256
MEM_WIDTH = MEM_HEADS * MEM_HEAD_DIM
N_BRANCH = 3
FFN_HIDDEN = 5632
FFN_CONV = 3
EPS = 1e-6

V7X_SUBLANES = 8
V7X_LANES = 128
MASK_VALUE = -1e30
LOG2_E = 1.4426950408889634
ROW_TILE = 1024
COL_TILE = 1024
FFN_COL_TILE = 512
MERGE_COL_TILE = 512
SEQ_TILE = 512
FOX_TILE = 512
FOX_SKIP_BITS = 64.0
FOX_BIAS_LANES = 16
FOX_FAST_MAX_LOGIT = 48.0
LRU_SCAN_UNROLL = 8

_BF16 = jnp.bfloat16
_F32 = jnp.float32


def _params(semantics, vmem_mib):
    return pltpu.CompilerParams(dimension_semantics=semantics,
                                vmem_limit_bytes=vmem_mib * 1024 * 1024)


def _dot_nt(a, w_t):
    return lax.dot_general(a, w_t, (((1,), (1,)), ((), ())), preferred_element_type=_F32)


def _mm_kernel(*refs, epilogue, group, cast_w, w_t):
    a_ref, w_ref = refs[0], refs[1]
    if cast_w:
        o_ref, wb_ref = refs[-2], refs[-1]

        @pl.when(pl.program_id(1) == 0)
        def _():
            wb_ref[...] = w_ref[...].astype(_BF16)

        w = wb_ref[...]
    else:
        o_ref = refs[-1]
        w = w_ref[...]
    if w_t:
        acc = _dot_nt(a_ref[...], w)
    else:
        acc = jnp.dot(a_ref[...], w, preferred_element_type=_F32)
    if epilogue == "gnorm":
        g_ref = refs[2]
        for gi in range(acc.shape[1] // group):
            sl = slice(gi * group, (gi + 1) * group)
            blk = acc[:, sl]
            ms = jnp.mean(blk * blk, axis=-1, keepdims=True)
            o_ref[:, sl] = (blk * lax.rsqrt(ms + EPS) * g_ref[:, sl]).astype(o_ref.dtype)
    elif epilogue == "residual":
        o_ref[...] = (refs[2][...] + acc).astype(o_ref.dtype)
    else:
        o_ref[...] = acc.astype(o_ref.dtype)


def _matmul(a, w, *, n, tm, tn, out_dtype, w_t=False, w_off=0, rows_outer=False,
            epilogue="plain", extra=None, group=None, name):
    m, k = a.shape
    cast_w = w.dtype != _BF16
    assert not (cast_w and rows_outer) and n % tn == 0 and m % tm == 0
    if rows_outer:
        grid = (m // tm, n // tn)
        ij = lambda i, j: (i, j)
    else:
        grid = (n // tn, m // tm)
        ij = lambda j, i: (i, j)
    if w_t:
        assert w_off % V7X_SUBLANES == 0
        w_block = (tn, k)
        w_spec = pl.BlockSpec((pl.Element(tn), pl.Element(k)),
                              lambda *g: (pl.multiple_of(w_off + ij(*g)[1] * tn, V7X_SUBLANES), 0))
    else:
        assert w_off % tn == 0
        w_block = (k, tn)
        w_spec = pl.BlockSpec(w_block, lambda *g: (0, w_off // tn + ij(*g)[1]))
    in_specs = [pl.BlockSpec((tm, k), lambda *g: (ij(*g)[0], 0)), w_spec]
    args = [a, w]
    block_bytes = (tm * k * 2 + k * tn * w.dtype.itemsize
                   + tm * tn * jnp.dtype(out_dtype).itemsize)
    if epilogue == "gnorm":
        in_specs.append(pl.BlockSpec((1, tn), lambda *g: (0, ij(*g)[1])))
        args.append(extra)
    elif epilogue == "residual":
        in_specs.append(pl.BlockSpec((tm, tn), lambda *g: ij(*g)))
        args.append(extra)
        block_bytes += tm * tn * extra.dtype.itemsize
    scratch = [pltpu.VMEM(w_block, _BF16)] if cast_w else []
    vmem_mib = -(-(2 * block_bytes + cast_w * k * tn * 2 + 2 * tm * tn * 4) // 2 ** 20) + 2
    return pl.pallas_call(
        functools.partial(_mm_kernel, epilogue=epilogue, group=group, cast_w=cast_w, w_t=w_t),
        out_shape=jax.ShapeDtypeStruct((m, n), out_dtype),
        grid=grid,
        in_specs=in_specs,
        out_specs=pl.BlockSpec((tm, tn), lambda *g: ij(*g)),
        scratch_shapes=scratch,
        compiler_params=_params(("parallel", "arbitrary"), vmem_mib),
        name=name,
    )(*args)


def _matmul_residual_pipelined(a, w, res, tm, tn, *, name):
    m, k = a.shape
    n = w.shape[1]

    def tile(a_ref, w_ref, r_ref, o_ref):
        o_ref[...] = r_ref[...] + jnp.dot(a_ref[...], w_ref[...], preferred_element_type=_F32)

    def body(a_hbm, w_hbm, r_hbm, o_hbm):
        pltpu.emit_pipeline(
            tile,
            grid=(m // tm, n // tn),
            in_specs=[pl.BlockSpec((tm, k), lambda i, j: (i, 0)),
                      pl.BlockSpec((k, tn), lambda i, j: (0, j), pipeline_mode=pl.Buffered(3)),
                      pl.BlockSpec((tm, tn), lambda i, j: (i, j))],
            out_specs=[pl.BlockSpec((tm, tn), lambda i, j: (i, j))],
        )(a_hbm, w_hbm, r_hbm, o_hbm)

    hbm = pl.BlockSpec(memory_space=pl.ANY)
    block_bytes = 2 * tm * k * 2 + 3 * k * tn * 2 + 4 * tm * tn * 4
    return pl.pallas_call(
        body,
        out_shape=jax.ShapeDtypeStruct((m, n), _F32),
        in_specs=[hbm, hbm, hbm],
        out_specs=hbm,
        compiler_params=pltpu.CompilerParams(
            vmem_limit_bytes=block_bytes + 2 * tm * tn * 4 + 2 * 2 ** 20),
        name=name,
    )(a, w, res)


def _split3(x):
    hi = x.astype(_BF16)
    r1 = x - hi.astype(_F32)
    mid = r1.astype(_BF16)
    lo = (r1 - mid.astype(_F32)).astype(_BF16)
    return hi, mid, lo


def _forget_bias_kernel(shift_ref, x_ref, g_ref, w_ref, b_ref, sel_ref, ones_ref, h_ref, kb_ref,
                        qb_ref, edge_ref, carry_ref):
    @pl.when(pl.program_id(0) == 0)
    def _():
        carry_ref[...] = jnp.zeros_like(carry_ref)

    tc = x_ref.shape[0]
    x = x_ref[...]
    ms = jnp.mean(x * x, axis=-1, keepdims=True)
    h = (x * lax.rsqrt(ms + EPS) * g_ref[...]).astype(_BF16)
    h_ref[...] = h
    w = jnp.concatenate([w_ref[...], jnp.zeros((V7X_LANES - FOX_HEADS, w_ref.shape[1]), _F32)],
                        axis=0).astype(_BF16)
    z = _dot_nt(h, w) + b_ref[...]
    neg_log_f = (jnp.log1p(jnp.exp(-jnp.abs(z))) - jnp.minimum(z, 0.0)) * LOG2_E
    row = lax.broadcasted_iota(jnp.int32, (tc, tc), 0)
    col = lax.broadcasted_iota(jnp.int32, (tc, tc), 1)
    tri = jnp.where(col <= row, 1.0, 0.0).astype(_BF16)
    c = carry_ref[0:1, :]
    for part in _split3(neg_log_f):
        c = c + jnp.dot(tri, part, preferred_element_type=_F32)
    carry_ref[...] = jnp.broadcast_to(c[tc - 1:tc, :], carry_ref.shape)
    edge_ref[...] = jnp.concatenate(
        [c[0:1, :], c[tc - 1:tc, :], jnp.zeros((V7X_SUBLANES - 2, c.shape[1]), _F32)], axis=0)
    parts = jnp.concatenate(_split3(c) + _split3(-(c + shift_ref[0])), axis=1)
    routed = jnp.dot(parts, sel_ref[...], preferred_element_type=_F32) + ones_ref[...]
    half = kb_ref.shape[1]
    kb_ref[...] = routed[:, :half].astype(kb_ref.dtype)
    qb_ref[...] = routed[:, half:].astype(qb_ref.dtype)


def _norm_forget_bias(x, g, w_t, w_row0, b_pad, shift, tc):
    s, d = x.shape
    lanes = V7X_LANES
    width = lanes
    nt = s // tc
    sel = np.zeros((6 * lanes, 2 * width), np.float32)
    ones = np.zeros((1, 2 * width), np.float32)
    for head in range(FOX_HEADS):
        for part in range(3):
            sel[part * lanes + head, FOX_BIAS_LANES * head + part] = 1.0
            sel[(3 + part) * lanes + head, width + FOX_BIAS_LANES * head + 3 + part] = 1.0
            ones[0, FOX_BIAS_LANES * head + 3 + part] = 1.0
            ones[0, width + FOX_BIAS_LANES * head + part] = 1.0
    h, kb, qb, edges = pl.pallas_call(
        _forget_bias_kernel,
        out_shape=(jax.ShapeDtypeStruct((s, d), _BF16),
                   jax.ShapeDtypeStruct((s, width), _BF16),
                   jax.ShapeDtypeStruct((s, width), _BF16),
                   jax.ShapeDtypeStruct((nt * V7X_SUBLANES, lanes), _F32)),
        grid=(nt,),
        in_specs=[pl.BlockSpec(memory_space=pltpu.SMEM),
                  pl.BlockSpec((tc, d), lambda i: (i, 0)),
                  pl.BlockSpec((1, d), lambda i: (0, 0)),
                  pl.BlockSpec((pl.Element(FOX_HEADS), pl.Element(d)), lambda i: (w_row0, 0)),
                  pl.BlockSpec((1, lanes), lambda i: (0, 0)),
                  pl.BlockSpec((6 * lanes, 2 * width), lambda i: (0, 0)),
                  pl.BlockSpec((1, 2 * width), lambda i: (0, 0))],
        out_specs=(pl.BlockSpec((tc, d), lambda i: (i, 0)),
                   pl.BlockSpec((tc, width), lambda i: (i, 0)),
                   pl.BlockSpec((tc, width), lambda i: (i, 0)),
                   pl.BlockSpec((V7X_SUBLANES, lanes), lambda i: (i, 0))),
        scratch_shapes=[pltpu.VMEM((V7X_SUBLANES, lanes), _F32)],
        compiler_params=_params(("arbitrary",), 40),
        name="norm_forget_bias",
    )(shift.reshape(1).astype(_F32), x, g.reshape(1, d), w_t, b_pad, jnp.asarray(sel, _BF16),
      jnp.asarray(ones))
    edges = edges.reshape(nt, V7X_SUBLANES, lanes)
    return h, kb, qb, edges[:, 0, :FOX_HEADS], edges[:, 1, :FOX_HEADS]


def _unpack_bias(packed, sel_ref):
    return jnp.dot(packed, sel_ref[...], preferred_element_type=_F32).astype(_BF16)


def _fox_kernel(bstart_ref, bend_ref, thr_ref, q_ref, k_ref, kbp_ref, sel_ref, v_ref, o_ref,
                acc_ref, kb_ref, *, tq):
    head = pl.program_id(0)
    qi = pl.program_id(1)
    b_tile = bstart_ref[qi, head]
    thr = thr_ref[0]
    j0 = lax.fori_loop(
        0, qi, lambda j, n: n + jnp.where(b_tile - bend_ref[j, head] >= thr, 1, 0), 0)

    @pl.when(qi == 0)
    def _():
        kb_ref[...] = _unpack_bias(kbp_ref[...], sel_ref)

    lane = lax.broadcasted_iota(jnp.int32, (tq, V7X_LANES), 1)
    ones3 = jnp.where(lane < 3, 1.0, 0.0).astype(_BF16)
    q_aug = jnp.concatenate([q_ref[...], ones3], axis=1)
    acc_ref[...] = jnp.zeros_like(acc_ref)

    def logits(j):
        ks = pl.multiple_of(j * tq, tq)
        k_aug = jnp.concatenate([k_ref[pl.ds(ks, tq), :], kb_ref[pl.ds(ks, tq), :]], axis=1)
        return lax.dot_general(k_aug, q_aug, (((1,), (1,)), ((), ())),
                               preferred_element_type=_F32)

    def accumulate(j, s, m_prev, l_prev):
        m_new = jnp.maximum(m_prev, jnp.max(s, axis=0, keepdims=True))
        alpha = jnp.exp2(m_prev - m_new)
        p = jnp.exp2(s - m_new)
        l_new = alpha * l_prev + jnp.sum(p, axis=0, keepdims=True)
        ks = pl.multiple_of(j * tq, tq)
        pv = lax.dot_general(v_ref[pl.ds(ks, tq), :], p.astype(_BF16),
                             (((0,), (0,)), ((), ())), preferred_element_type=_F32)
        acc_ref[...] = alpha * acc_ref[...] + pv
        return m_new, l_new

    def body(j, carry):
        m_prev, l_prev, s = carry
        s_next = logits(j + 1)
        m_new, l_new = accumulate(j, s, m_prev, l_prev)
        return m_new, l_new, s_next

    m0 = jnp.full((1, tq), MASK_VALUE, _F32)
    l0 = jnp.zeros((1, tq), _F32)
    m, l, s = lax.fori_loop(j0, qi, body, (m0, l0, logits(j0)))
    row = lax.broadcasted_iota(jnp.int32, s.shape, 0)
    col = lax.broadcasted_iota(jnp.int32, s.shape, 1)
    m, l = accumulate(qi, jnp.where(row <= col, s, MASK_VALUE), m, l)
    o_ref[...] = (acc_ref[...] / l).T.astype(o_ref.dtype)


def _fox_fast_kernel(bstart_ref, bend_ref, thr_ref, q_ref, qbp_ref, k_ref, kbp_ref, sel_ref, v_ref,
                     o_ref, acc_ref, kb_ref, l_ref, s0_ref, s1_ref, *, tq):
    head = pl.program_id(0)
    qi = pl.program_id(1)
    b_tile = bstart_ref[qi, head]
    thr = thr_ref[0]
    j0 = lax.fori_loop(
        0, qi, lambda j, n: n + jnp.where(b_tile - bend_ref[j, head] >= thr, 1, 0), 0)

    @pl.when(qi == 0)
    def _():
        kb_ref[...] = _unpack_bias(kbp_ref[...], sel_ref)

    q_aug = jnp.concatenate([q_ref[...], _unpack_bias(qbp_ref[...], sel_ref)], axis=1)
    acc_ref[...] = jnp.zeros_like(acc_ref)

    def logits(j):
        ks = pl.multiple_of(j * tq, tq)
        k_aug = jnp.concatenate([k_ref[pl.ds(ks, tq), :], kb_ref[pl.ds(ks, tq), :]], axis=1)
        return _dot_nt(k_aug, q_aug)

    def accumulate(j, s):
        p = jnp.exp2(s)
        ks = pl.multiple_of(j * tq, tq)
        acc_ref[...] += lax.dot_general(v_ref[pl.ds(ks, tq), :], p.astype(_BF16),
                                        (((0,), (0,)), ((), ())), preferred_element_type=_F32)
        l_ref[...] += jnp.sum(p, axis=0, keepdims=True)

    l_ref[...] = jnp.zeros_like(l_ref)
    row = lax.broadcasted_iota(jnp.int32, s1_ref.shape, 0)
    col = lax.broadcasted_iota(jnp.int32, s1_ref.shape, 1)
    s_diag = jnp.where(row <= col, logits(qi), MASK_VALUE)
    s0_ref[...] = logits(j0)
    accumulate(qi, s_diag)
    n_before = qi - j0

    def pair(t, carry):
        j = j0 + 2 * t
        s1_ref[...] = logits(j + 1)
        accumulate(j, s0_ref[...])
        s0_ref[...] = logits(j + 2)
        accumulate(j + 1, s1_ref[...])
        return carry

    lax.fori_loop(0, n_before // 2, pair, 0)

    @pl.when(n_before % 2 == 1)
    def _():
        accumulate(qi - 1, s0_ref[...])

    o_ref[...] = (acc_ref[...] / l_ref[...]).T.astype(o_ref.dtype)


def _fox_attention(qk, kb, qb, b_start, b_end, qk_bound, v, tq):
    s = v.shape[0]
    hd = FOX_HEAD_DIM
    lanes = V7X_LANES
    thr = (2.0 * qk_bound + FOX_SKIP_BITS).reshape(1).astype(_F32)
    sel = np.zeros((FOX_HEADS, lanes, lanes), np.float32)
    for head in range(FOX_HEADS):
        for c in range(FOX_BIAS_LANES):
            sel[head, FOX_BIAS_LANES * head + c, c] = 1.0
    sel = jnp.asarray(sel, _BF16)
    smem = pl.BlockSpec(memory_space=pltpu.SMEM)
    q_spec = pl.BlockSpec((tq, hd), lambda h, i: (i, h))
    k_spec = pl.BlockSpec((s, hd), lambda h, i: (0, FOX_HEADS + h))
    v_spec = pl.BlockSpec((s, hd), lambda h, i: (0, h))
    kb_spec = pl.BlockSpec((s, lanes), lambda h, i: (0, 0))
    qb_spec = pl.BlockSpec((tq, lanes), lambda h, i: (i, 0))
    sel_spec = pl.BlockSpec((None, lanes, lanes), lambda h, i: (h, 0, 0))

    def call(kernel, in_specs, extra_scratch, *args):
        return pl.pallas_call(
            functools.partial(kernel, tq=tq),
            out_shape=jax.ShapeDtypeStruct((s, FOX_WIDTH), _BF16),
            grid=(FOX_HEADS, s // tq),
            in_specs=[smem, smem, smem] + in_specs,
            out_specs=q_spec,
            scratch_shapes=[pltpu.VMEM((hd, tq), _F32),
                            pltpu.VMEM((s, lanes), _BF16)] + extra_scratch,
            compiler_params=_params(("parallel", "arbitrary"), 32),
            name=kernel.__name__.strip("_"),
        )(b_start, b_end, thr, *args)

    fast_scratch = [pltpu.VMEM((1, tq), _F32), pltpu.VMEM((tq, tq), _F32),
                    pltpu.VMEM((tq, tq), _F32)]
    return lax.cond(
        qk_bound <= FOX_FAST_MAX_LOGIT,
        lambda: call(_fox_fast_kernel, [q_spec, qb_spec, k_spec, kb_spec, sel_spec, v_spec],
                     fast_scratch, qk, qb, qk, kb, sel, v),
        lambda: call(_fox_kernel, [q_spec, k_spec, kb_spec, sel_spec, v_spec], [],
                     qk, qk, kb, sel, v))


def _shift_rows(ext, d):
    return pltpu.roll(ext, d, axis=0)[V7X_SUBLANES:]


def _lru_kernel(lx_ref, lg_ref, wc_ref, bc_ref, wg_ref, bg_ref, lam_ref, o_ref,
                halo_ref, carry_ref, a_ref, h_ref):
    @pl.when(pl.program_id(0) == 0)
    def _():
        halo_ref[...] = jnp.zeros_like(halo_ref)
        carry_ref[...] = jnp.zeros_like(carry_ref)

    lx = lx_ref[...]
    ts = lx.shape[0]
    ext = jnp.concatenate([halo_ref[...], lx], axis=0)
    xr = (wc_ref[3:4, :] * lx + wc_ref[2:3, :] * _shift_rows(ext, 1)
          + wc_ref[1:2, :] * _shift_rows(ext, 2) + wc_ref[0:1, :] * _shift_rows(ext, 3)
          + bc_ref[...])
    halo_ref[...] = lx[ts - V7X_SUBLANES:]

    lam = lam_ref[...]
    log_sig_lam = jnp.minimum(lam, 0.0) - jnp.log1p(jnp.exp(-jnp.abs(lam)))
    bd = LRU_BLOCK_DIM
    for nb in range(LRU_BLOCKS):
        sl = slice(nb * bd, (nb + 1) * bd)
        x_nb = xr[:, sl]
        gates = jax.nn.sigmoid(jnp.dot(x_nb.astype(_BF16), wg_ref[nb],
                                       preferred_element_type=_F32) + bg_ref[nb])
        log_a = LRU_C * gates[:, :bd] * log_sig_lam[:, sl]
        a_ref[:, sl] = jnp.exp(log_a)
        t = jnp.tanh(log_a)
        h_ref[:, sl] = jnp.sqrt(-2.0 * t / (1.0 - t)) * (gates[:, bd:] * x_nb)

    sub = lax.broadcasted_iota(jnp.int32, (V7X_SUBLANES, 1), 0)

    def group(g, carry):
        rows = pl.ds(pl.multiple_of(g * V7X_SUBLANES, V7X_SUBLANES), V7X_SUBLANES)
        a = a_ref[rows, :]
        h = h_ref[rows, :]
        for d in (1, 2, 4):
            valid = sub >= d
            h = h + a * jnp.where(valid, pltpu.roll(h, d, axis=0), 0.0)
            a = a * jnp.where(valid, pltpu.roll(a, d, axis=0), 1.0)
        h = h + a * carry
        h_ref[rows, :] = h
        return jnp.broadcast_to(h[V7X_SUBLANES - 1:, :], h.shape)

    carry_ref[...] = lax.fori_loop(0, ts // V7X_SUBLANES, group, carry_ref[...],
                                   unroll=LRU_SCAN_UNROLL)
    o_ref[...] = (h_ref[...] * jax.nn.gelu(lg_ref[...])).astype(o_ref.dtype)


def _lru_branch(lxlg, w_conv, b_conv, w_gate, b_gate, lam, ts):
    s = lxlg.shape[0]
    w = LRU_WIDTH
    bd = LRU_BLOCK_DIM
    return pl.pallas_call(
        _lru_kernel,
        out_shape=jax.ShapeDtypeStruct((s, w), _BF16),
        grid=(s // ts,),
        in_specs=[pl.BlockSpec((ts, w), lambda i: (i, 0)),
                  pl.BlockSpec((ts, w), lambda i: (i, 1)),
                  pl.BlockSpec((LRU_CONV, w), lambda i: (0, 0)),
                  pl.BlockSpec((1, w), lambda i: (0, 0)),
                  pl.BlockSpec((LRU_BLOCKS, bd, 2 * bd), lambda i: (0, 0, 0)),
                  pl.BlockSpec((LRU_BLOCKS, 1, 2 * bd), lambda i: (0, 0, 0)),
                  pl.BlockSpec((1, w), lambda i: (0, 0))],
        out_specs=pl.BlockSpec((ts, w), lambda i: (i, 0)),
        scratch_shapes=[pltpu.VMEM((V7X_SUBLANES, w), _F32), pltpu.VMEM((V7X_SUBLANES, w), _F32),
                        pltpu.VMEM((ts, w), _F32), pltpu.VMEM((ts, w), _F32)],
        compiler_params=_params(("arbitrary",), 40),
        name="conv_rglru",
    )(lxlg, lxlg, w_conv, b_conv.reshape(1, -1), w_gate, b_gate, lam.reshape(1, -1))


def _mem_kv_kernel(mem_ref, g_ref, w_ref, gk_ref, k_ref, v_ref):
    x = mem_ref[...]
    ms = jnp.mean(x * x, axis=-1, keepdims=True)
    hm = (x * lax.rsqrt(ms + EPS) * g_ref[...]).astype(_BF16)
    acc = jnp.dot(hm, w_ref[...].astype(_BF16), preferred_element_type=_F32)

    @pl.when(pl.program_id(0) == 0)
    def _():
        for gi in range(MEM_HEADS):
            sl = slice(gi * MEM_HEAD_DIM, (gi + 1) * MEM_HEAD_DIM)
            blk = acc[:, sl]
            ms_k = jnp.mean(blk * blk, axis=-1, keepdims=True)
            k_ref[:, sl] = (blk * lax.rsqrt(ms_k + EPS) * gk_ref[:, sl]).astype(k_ref.dtype)

    @pl.when(pl.program_id(0) == 1)
    def _():
        v_ref[...] = acc.astype(v_ref.dtype)


def _mem_kv(mem, g_mem, w_kv, g_k):
    m, d = mem.shape
    w = MEM_WIDTH
    const = lambda j: (0, 0)
    return pl.pallas_call(
        _mem_kv_kernel,
        out_shape=(jax.ShapeDtypeStruct((m, w), _BF16), jax.ShapeDtypeStruct((m, w), _BF16)),
        grid=(2,),
        in_specs=[pl.BlockSpec((m, d), const), pl.BlockSpec((1, d), const),
                  pl.BlockSpec((d, w), lambda j: (0, j)), pl.BlockSpec((1, w), const)],
        out_specs=(pl.BlockSpec((m, w), const), pl.BlockSpec((m, w), const)),
        compiler_params=_params(("arbitrary",), 40),
        name="mem_kv",
    )(mem, g_mem.reshape(1, d), w_kv, jnp.tile(g_k, MEM_HEADS).reshape(1, w))


def _mem_attn_kernel(q_ref, k_ref, v_ref, o_ref):
    s = lax.dot_general(q_ref[...], k_ref[...], (((1,), (1,)), ((), ())),
                        preferred_element_type=_F32)
    m = jnp.max(s, axis=-1, keepdims=True)
    p = jnp.exp(s - m)
    l = jnp.sum(p, axis=-1, keepdims=True)
    acc = jnp.dot(p.astype(_BF16), v_ref[...], preferred_element_type=_F32)
    o_ref[...] = (acc / l).astype(o_ref.dtype)


def _mem_attention(q, k, v, ts):
    s = q.shape[0]
    m = k.shape[0]
    hd = MEM_HEAD_DIM
    return pl.pallas_call(
        _mem_attn_kernel,
        out_shape=jax.ShapeDtypeStruct((s, MEM_WIDTH), _BF16),
        grid=(s // ts, MEM_HEADS),
        in_specs=[pl.BlockSpec((ts, hd), lambda i, h: (i, h)),
                  pl.BlockSpec((m, hd), lambda i, h: (0, h)),
                  pl.BlockSpec((m, hd), lambda i, h: (0, h))],
        out_specs=pl.BlockSpec((ts, hd), lambda i, h: (i, h)),
        compiler_params=_params(("parallel", "arbitrary"), 32),
        name="mem_attention",
    )(q, k, v)


def _merge_kernel(h_ref, yf_ref, yl_ref, ym_ref, wg0_ref, wg1_ref, wg2_ref, wb_ref, bg_ref,
                  o_ref):
    h = h_ref[...]
    merged = None
    for n, (y_ref, wg_ref) in enumerate(((yf_ref, wg0_ref), (yl_ref, wg1_ref),
                                         (ym_ref, wg2_ref))):
        gate = jax.nn.sigmoid(_dot_nt(h, wg_ref[...]) + bg_ref[n:n + 1, :])
        term = gate * jnp.dot(y_ref[...], wb_ref[n], preferred_element_type=_F32)
        merged = term if merged is None else merged + term
    o_ref[...] = merged.astype(o_ref.dtype)


def _gated_merge(h, y_fox, y_lru, y_mem, w_gate_t, w_branch, b_gate, tm, tn):
    s, d = h.shape
    nj = d // tn
    bw = y_fox.shape[1]
    y_spec = pl.BlockSpec((tm, bw), lambda i, j: (i, 0))
    return pl.pallas_call(
        _merge_kernel,
        out_shape=jax.ShapeDtypeStruct((s, d), _BF16),
        grid=(s // tm, nj),
        in_specs=[pl.BlockSpec((tm, d), lambda i, j: (i, 0)), y_spec, y_spec, y_spec,
                  pl.BlockSpec((tn, d), lambda i, j: (j, 0)),
                  pl.BlockSpec((tn, d), lambda i, j: (nj + j, 0)),
                  pl.BlockSpec((tn, d), lambda i, j: (2 * nj + j, 0)),
                  pl.BlockSpec((N_BRANCH, bw, tn), lambda i, j: (0, 0, j)),
                  pl.BlockSpec((N_BRANCH, tn), lambda i, j: (0, j))],
        out_specs=pl.BlockSpec((tm, tn), lambda i, j: (i, j)),
        compiler_params=_params(("parallel", "arbitrary"), 56),
        name="gated_merge",
    )(h, y_fox, y_lru, y_mem, w_gate_t, w_gate_t, w_gate_t, w_branch, b_gate)


def _out_norm_kernel(a_ref, w_ref, x_ref, g_ref, x2_ref, h2_ref):
    x2 = x_ref[...] + jnp.dot(a_ref[...], w_ref[...], preferred_element_type=_F32)
    x2_ref[...] = x2
    ms = jnp.mean(x2 * x2, axis=-1, keepdims=True)
    h2_ref[...] = (x2 * lax.rsqrt(ms + EPS) * g_ref[...]).astype(h2_ref.dtype)


def _out_proj_norm(a, w, x, g, tm):
    m, k = a.shape
    d = w.shape[1]
    row = lambda i: (i, 0)
    return pl.pallas_call(
        _out_norm_kernel,
        out_shape=(jax.ShapeDtypeStruct((m, d), _F32), jax.ShapeDtypeStruct((m, d), _BF16)),
        grid=(m // tm,),
        in_specs=[pl.BlockSpec((tm, k), row), pl.BlockSpec((k, d), lambda i: (0, 0)),
                  pl.BlockSpec((tm, d), row), pl.BlockSpec((1, d), lambda i: (0, 0))],
        out_specs=(pl.BlockSpec((tm, d), row), pl.BlockSpec((tm, d), row)),
        compiler_params=_params(("parallel",), 48),
        name="proj_out_norm",
    )(a, w, x, g.reshape(1, d))


def _cast_kernel(w_ref, o_ref):
    o_ref[...] = w_ref[...].astype(o_ref.dtype)


def _cast_rows(w, row0, rows, tr):
    k = w.shape[1]
    assert row0 % V7X_SUBLANES == 0 and rows % tr == 0
    return pl.pallas_call(
        _cast_kernel,
        out_shape=jax.ShapeDtypeStruct((rows, k), _BF16),
        grid=(rows // tr,),
        in_specs=[pl.BlockSpec((pl.Element(tr), pl.Element(k)),
                               lambda i: (pl.multiple_of(row0 + i * tr, V7X_SUBLANES), 0))],
        out_specs=pl.BlockSpec((tr, k), lambda i: (i, 0)),
        compiler_params=_params(("parallel",), 32),
        name="cast_rows",
    )(w)


def _ffn_up_kernel(a_ref, wa_ref, wv_ref, wca_ref, wcv_ref, bca_ref, bcv_ref, o_ref,
                   halo_a_ref, halo_v_ref, wa_bf_ref, wv_bf_ref):
    @pl.when(pl.program_id(1) == 0)
    def _():
        halo_a_ref[...] = jnp.zeros_like(halo_a_ref)
        halo_v_ref[...] = jnp.zeros_like(halo_v_ref)
        wa_bf_ref[...] = wa_ref[...].astype(_BF16)
        wv_bf_ref[...] = wv_ref[...].astype(_BF16)

    a = a_ref[...]
    tm = a.shape[0]

    def conv(up, halo_ref, wc_ref, bc_ref):
        ext = jnp.concatenate([halo_ref[...], up], axis=0)
        halo_ref[...] = up[tm - V7X_SUBLANES:]
        return (wc_ref[2:3, :] * up + wc_ref[1:2, :] * _shift_rows(ext, 1)
                + wc_ref[0:1, :] * _shift_rows(ext, 2) + bc_ref[...])

    act = conv(jnp.dot(a, wa_bf_ref[...], preferred_element_type=_F32), halo_a_ref, wca_ref,
               bca_ref)
    val = conv(jnp.dot(a, wv_bf_ref[...], preferred_element_type=_F32), halo_v_ref, wcv_ref,
               bcv_ref)
    o_ref[...] = (jax.nn.gelu(act) * val).astype(o_ref.dtype)


def _ffn_up(h2, w_up, w_conv, b_conv, tm, tn):
    s, d = h2.shape
    f = FFN_HIDDEN
    nj = f // tn
    return pl.pallas_call(
        _ffn_up_kernel,
        out_shape=jax.ShapeDtypeStruct((s, f), _BF16),
        grid=(nj, s // tm),
        in_specs=[pl.BlockSpec((tm, d), lambda j, i: (i, 0)),
                  pl.BlockSpec((d, tn), lambda j, i: (0, j)),
                  pl.BlockSpec((d, tn), lambda j, i: (0, nj + j)),
                  pl.BlockSpec((FFN_CONV, tn), lambda j, i: (0, j)),
                  pl.BlockSpec((FFN_CONV, tn), lambda j, i: (0, nj + j)),
                  pl.BlockSpec((1, tn), lambda j, i: (0, j)),
                  pl.BlockSpec((1, tn), lambda j, i: (0, nj + j))],
        out_specs=pl.BlockSpec((tm, tn), lambda j, i: (i, j)),
        scratch_shapes=[pltpu.VMEM((V7X_SUBLANES, tn), _F32),
                        pltpu.VMEM((V7X_SUBLANES, tn), _F32),
                        pltpu.VMEM((d, tn), _BF16), pltpu.VMEM((d, tn), _BF16)],
        compiler_params=_params(("parallel", "arbitrary"), 56),
        name="ffn_up_conv_geglu",
    )(h2, w_up, w_up, w_conv, w_conv, b_conv.reshape(1, -1), b_conv.reshape(1, -1))


def _layer(x, mem, g_mix, w_in, b_f, g_q_fox, g_k_fox, w_lru_conv, b_lru_conv, w_rg_a, b_rg_a,
           w_rg_x, b_rg_x, lru_lambda, g_mem, w_mem_kv, g_q_mem, g_k_mem, b_gate, w_branch,
           w_out, g_ffn, w_ffn_up, w_ffn_conv, b_ffn_conv, w_ffn_down):
    c_k = 2 * FOX_WIDTH
    c_v = c_k + FOX_WIDTH
    c_f = c_v + FOX_HEADS
    c_l = c_f + 2 * LRU_WIDTH
    c_m = c_l + MEM_WIDTH

    w_in_t = w_in.T

    g_q_scaled = g_q_fox * (LOG2_E * FOX_HEAD_DIM ** -0.5)
    qk_bound = (1.02 * FOX_HEAD_DIM) * jnp.max(jnp.abs(g_q_scaled)) * jnp.max(jnp.abs(g_k_fox))
    b_pad = jnp.pad(b_f.reshape(1, -1), ((0, 0), (0, V7X_LANES - FOX_HEADS)))
    h, kb, qb, b_start, b_end = _norm_forget_bias(x, g_mix, w_in_t, c_v, b_pad, qk_bound,
                                                  FOX_TILE)
    gain_qk = jnp.concatenate([jnp.tile(g_q_scaled, FOX_HEADS),
                               jnp.tile(g_k_fox, FOX_HEADS)]).reshape(1, -1)
    qk = _matmul(h, w_in_t, w_t=True, n=c_k, tm=ROW_TILE, tn=COL_TILE, out_dtype=_BF16,
                 epilogue="gnorm", extra=gain_qk, group=FOX_HEAD_DIM, name="proj_qk")
    v = _matmul(h, w_in_t, w_t=True, n=FOX_WIDTH, w_off=c_k, tm=ROW_TILE, tn=COL_TILE,
                out_dtype=_BF16, name="proj_v")
    y_fox = _fox_attention(qk, kb, qb, b_start, b_end, qk_bound, v, FOX_TILE)

    lxlg = _matmul(h, w_in_t, w_t=True, n=2 * LRU_WIDTH, w_off=c_f, tm=ROW_TILE, tn=COL_TILE,
                   out_dtype=_F32, name="proj_lru")
    w_gate_lru = jnp.concatenate([w_rg_a, w_rg_x], axis=-1).astype(_BF16)
    b_gate_lru = jnp.concatenate([b_rg_a, b_rg_x], axis=-1).reshape(LRU_BLOCKS, 1, -1)
    y_lru = _lru_branch(lxlg, w_lru_conv, b_lru_conv, w_gate_lru, b_gate_lru, lru_lambda,
                        SEQ_TILE)

    gain_mq = (jnp.tile(g_q_mem, MEM_HEADS) * (MEM_HEAD_DIM ** -0.5)).reshape(1, -1)
    mq = _matmul(h, w_in_t, w_t=True, n=MEM_WIDTH, w_off=c_l, tm=ROW_TILE, tn=COL_TILE,
                 out_dtype=_BF16, epilogue="gnorm", extra=gain_mq, group=MEM_HEAD_DIM,
                 name="proj_mq")
    mk, mv = _mem_kv(mem, g_mem, w_mem_kv, g_k_mem)
    y_mem = _mem_attention(mq, mk, mv, ROW_TILE)

    w_gate_t = _cast_rows(w_in_t, c_m, N_BRANCH * D_MODEL, SEQ_TILE)
    merged = _gated_merge(h, y_fox, y_lru, y_mem, w_gate_t, w_branch.astype(_BF16), b_gate,
                          ROW_TILE, MERGE_COL_TILE)
    x2, h2 = _out_proj_norm(merged, w_out.astype(_BF16), x, g_ffn, SEQ_TILE)

    g = _ffn_up(h2, w_ffn_up, w_ffn_conv, b_ffn_conv, ROW_TILE, FFN_COL_TILE)
    return _matmul_residual_pipelined(g, w_ffn_down.astype(_BF16), x2, ROW_TILE, FFN_COL_TILE,
                                      name="ffn_down")


def kernel(x, mem, g_mix, w_in, b_f, g_q_fox, g_k_fox, w_lru_conv, b_lru_conv, w_rg_a, b_rg_a,
           w_rg_x, b_rg_x, lru_lambda, g_mem, w_mem_kv, g_q_mem, g_k_mem, b_gate, w_branch,
           w_out, g_ffn, w_ffn_up, w_ffn_conv, b_ffn_conv, w_ffn_down):
    depth = g_mix.shape[0]
    outs = []
    for b in range(x.shape[0]):
        xb = x[b]
        for l in range(depth):
            xb = _layer(xb, mem[b], g_mix[l], w_in[l], b_f[l], g_q_fox[l], g_k_fox[l],
                        w_lru_conv[l], b_lru_conv[l], w_rg_a[l], b_rg_a[l], w_rg_x[l],
                        b_rg_x[l], lru_lambda[l], g_mem[l], w_mem_kv[l], g_q_mem[l],
                        g_k_mem[l], b_gate[l], w_branch[l], w_out[l], g_ffn[l], w_ffn_up[l],
                        w_ffn_conv[l], b_ffn_conv[l], w_ffn_down[l])
        outs.append(xb)
    return outs[0][None] if len(outs) == 1 else jnp.stack(outs)
```

```python
import functools

import jax
import jax.numpy as jnp
import numpy as np
from jax import lax
from jax.experimental import pallas as pl
from jax.experimental.pallas import tpu as pltpu

D_MODEL = 2048
FOX_HEADS = 8
FOX_HEAD_DIM = 128
FOX_WIDTH = FOX_HEADS * FOX_HEAD_DIM
LRU_WIDTH = 1024
LRU_BLOCKS = 8
LRU_BLOCK_DIM = LRU_WIDTH // LRU_BLOCKS
LRU_CONV = 4
LRU_C = 8.0
MEM_HEADS = 4
MEM_HEAD_DIM = 256
MEM_WIDTH = MEM_HEADS * MEM_HEAD_DIM
N_BRANCH = 3
FFN_HIDDEN = 5632
FFN_CONV = 3
EPS = 1e-6

V7X_SUBLANES = 8
V7X_LANES = 128
MASK_VALUE = -1e30
LOG2_E = 1.4426950408889634
ROW_TILE = 1024
COL_TILE = 1024
FFN_COL_TILE = 512
MERGE_COL_TILE = 256
FFN_DOWN_COL_TILE = 256
SEQ_TILE = 512
FOX_TILE = 512
FOX_SKIP_BITS = 64.0
FOX_BIAS_LANES = 16
FOX_FAST_MAX_LOGIT = 48.0
LRU_SCAN_UNROLL = 8

_BF16 = jnp.bfloat16
_F32 = jnp.float32


def _params(semantics, vmem_mib):
    return pltpu.CompilerParams(dimension_semantics=semantics,
                                vmem_limit_bytes=vmem_mib * 1024 * 1024)


def _dot_nt(a, w_t):
    return lax.dot_general(a, w_t, (((1,), (1,)), ((), ())), preferred_element_type=_F32)


def _mm_kernel(*refs, epilogue, group, cast_w, w_t):
    a_ref, w_ref = refs[0], refs[1]
    if cast_w:
        o_ref, wb_ref = refs[-2], refs[-1]

        @pl.when(pl.program_id(1) == 0)
        def _():
            wb_ref[...] = w_ref[...].astype(_BF16)

        w = wb_ref[...]
    else:
        o_ref = refs[-1]
        w = w_ref[...]
    if w_t:
        acc = _dot_nt(a_ref[...], w)
    else:
        acc = lax.dot_general(a_ref[...], w, (((1,), (0,)), ((), ())),
                              preferred_element_type=_F32)
    if epilogue == "gnorm":
        g_ref = refs[2]
        for gi in range(acc.shape[1] // group):
            sl = slice(gi * group, (gi + 1) * group)
            blk = acc[:, sl]
            ms = jnp.mean(blk * blk, axis=-1, keepdims=True)
            o_ref[:, sl] = (blk * lax.rsqrt(ms + EPS) * g_ref[:, sl]).astype(o_ref.dtype)
    elif epilogue == "residual":
        o_ref[...] = (refs[2][...] + acc).astype(o_ref.dtype)
    else:
        o_ref[...] = acc.astype(o_ref.dtype)


def _matmul(a, w, *, n, tm, tn, out_dtype, w_t=False, w_off=0, rows_outer=False,
            epilogue="plain", extra=None, group=None, name):
    m, k = a.shape
    cast_w = w.dtype != _BF16 and not rows_outer
    assert n % tn == 0 and m % tm == 0
    if rows_outer:
        grid = (m // tm, n // tn)
        ij = lambda i, j: (i, j)
    else:
        grid = (n // tn, m // tm)
        ij = lambda j, i: (i, j)
    if w_t:
        assert w_off % V7X_SUBLANES == 0
        w_block = (tn, k)
        w_spec = pl.BlockSpec((pl.Element(tn), pl.Element(k)),
                              lambda *g: (pl.multiple_of(w_off + ij(*g)[1] * tn, V7X_SUBLANES), 0))
    else:
        assert w_off % tn == 0
        w_block = (k, tn)
        w_spec = pl.BlockSpec(w_block, lambda *g: (0, w_off // tn + ij(*g)[1]))
    in_specs = [pl.BlockSpec((tm, k), lambda *g: (ij(*g)[0], 0)), w_spec]
    args = [a, w]
    block_bytes = (tm * k * 2 + k * tn * w.dtype.itemsize
                   + tm * tn * jnp.dtype(out_dtype).itemsize)
    if epilogue == "gnorm":
        in_specs.append(pl.BlockSpec((1, tn), lambda *g: (0, ij(*g)[1])))
        args.append(extra)
    elif epilogue == "residual":
        in_specs.append(pl.BlockSpec((tm, tn), lambda *g: ij(*g)))
        args.append(extra)
        block_bytes += tm * tn * extra.dtype.itemsize
    scratch = [pltpu.VMEM(w_block, _BF16)] if cast_w else []
    vmem_mib = -(-(2 * block_bytes + cast_w * k * tn * 2 + 2 * tm * tn * 4) // 2 ** 20) + 2
    return pl.pallas_call(
        functools.partial(_mm_kernel, epilogue=epilogue, group=group, cast_w=cast_w, w_t=w_t),
        out_shape=jax.ShapeDtypeStruct((m, n), out_dtype),
        grid=grid,
        in_specs=in_specs,
        out_specs=pl.BlockSpec((tm, tn), lambda *g: ij(*g)),
        scratch_shapes=scratch,
        compiler_params=_params(("parallel", "arbitrary"), vmem_mib),
        name=name,
    )(*args)


def _split3(x):
    hi = x.astype(_BF16)
    r1 = x - hi.astype(_F32)
    mid = r1.astype(_BF16)
    lo = (r1 - mid.astype(_F32)).astype(_BF16)
    return hi, mid, lo


def _forget_bias_kernel(shift_ref, x_ref, g_ref, w_ref, b_ref, sel_ref, ones_ref, h_ref, kb_ref,
                        qb_ref, edge_ref, carry_ref):
    @pl.when(pl.program_id(0) == 0)
    def _():
        carry_ref[...] = jnp.zeros_like(carry_ref)

    tc = x_ref.shape[0]
    x = x_ref[...]
    ms = jnp.mean(x * x, axis=-1, keepdims=True)
    h = (x * lax.rsqrt(ms + EPS) * g_ref[...]).astype(_BF16)
    h_ref[...] = h
    w = jnp.concatenate([w_ref[...], jnp.zeros((V7X_LANES - FOX_HEADS, w_ref.shape[1]), _F32)],
                        axis=0).astype(_BF16)
    z = _dot_nt(h, w) + b_ref[...]
    neg_log_f = (jnp.log1p(jnp.exp(-jnp.abs(z))) - jnp.minimum(z, 0.0)) * LOG2_E
    row = lax.broadcasted_iota(jnp.int32, (tc, tc), 0)
    col = lax.broadcasted_iota(jnp.int32, (tc, tc), 1)
    tri = jnp.where(col <= row, 1.0, 0.0).astype(_BF16)
    c = carry_ref[0:1, :]
    for part in _split3(neg_log_f):
        c = c + jnp.dot(tri, part, preferred_element_type=_F32)
    carry_ref[...] = jnp.broadcast_to(c[tc - 1:tc, :], carry_ref.shape)
    edge_ref[...] = jnp.concatenate(
        [c[0:1, :], c[tc - 1:tc, :], jnp.zeros((V7X_SUBLANES - 2, c.shape[1]), _F32)], axis=0)
    parts = jnp.concatenate(_split3(c) + _split3(-(c + shift_ref[0])), axis=1)
    routed = jnp.dot(parts, sel_ref[...], preferred_element_type=_F32) + ones_ref[...]
    half = kb_ref.shape[1]
    kb_ref[...] = routed[:, :half].astype(kb_ref.dtype)
    qb_ref[...] = routed[:, half:].astype(qb_ref.dtype)


def _norm_forget_bias(x, g, w_t, w_row0, b_pad, shift, tc):
    s, d = x.shape
    lanes = V7X_LANES
    width = lanes
    nt = s // tc
    sel = np.zeros((6 * lanes, 2 * width), np.float32)
    ones = np.zeros((1, 2 * width), np.float32)
    for head in range(FOX_HEADS):
        for part in range(3):
            sel[part * lanes + head, FOX_BIAS_LANES * head + part] = 1.0
            sel[(3 + part) * lanes + head, width + FOX_BIAS_LANES * head + 3 + part] = 1.0
            ones[0, FOX_BIAS_LANES * head + 3 + part] = 1.0
            ones[0, width + FOX_BIAS_LANES * head + part] = 1.0
    h, kb, qb, edges = pl.pallas_call(
        _forget_bias_kernel,
        out_shape=(jax.ShapeDtypeStruct((s, d), _BF16),
                   jax.ShapeDtypeStruct((s, width), _BF16),
                   jax.ShapeDtypeStruct((s, width), _BF16),
                   jax.ShapeDtypeStruct((nt * V7X_SUBLANES, lanes), _F32)),
        grid=(nt,),
        in_specs=[pl.BlockSpec(memory_space=pltpu.SMEM),
                  pl.BlockSpec((tc, d), lambda i: (i, 0)),
                  pl.BlockSpec((1, d), lambda i: (0, 0)),
                  pl.BlockSpec((pl.Element(FOX_HEADS), pl.Element(d)), lambda i: (w_row0, 0)),
                  pl.BlockSpec((1, lanes), lambda i: (0, 0)),
                  pl.BlockSpec((6 * lanes, 2 * width), lambda i: (0, 0)),
                  pl.BlockSpec((1, 2 * width), lambda i: (0, 0))],
        out_specs=(pl.BlockSpec((tc, d), lambda i: (i, 0)),
                   pl.BlockSpec((tc, width), lambda i: (i, 0)),
                   pl.BlockSpec((tc, width), lambda i: (i, 0)),
                   pl.BlockSpec((V7X_SUBLANES, lanes), lambda i: (i, 0))),
        scratch_shapes=[pltpu.VMEM((V7X_SUBLANES, lanes), _F32)],
        compiler_params=_params(("arbitrary",), 40),
        name="norm_forget_bias",
    )(shift.reshape(1).astype(_F32), x, g.reshape(1, d), w_t, b_pad, jnp.asarray(sel, _BF16),
      jnp.asarray(ones))
    edges = edges.reshape(nt, V7X_SUBLANES, lanes)
    return h, kb, qb, edges[:, 0, :FOX_HEADS], edges[:, 1, :FOX_HEADS]


def _unpack_bias(packed, sel_ref):
    return jnp.dot(packed, sel_ref[...], preferred_element_type=_F32).astype(_BF16)


def _fox_kernel(bstart_ref, bend_ref, thr_ref, q_ref, k_ref, kbp_ref, sel_ref, v_ref, o_ref,
                acc_ref, kb_ref, *, tq):
    head = pl.program_id(0)
    qi = pl.program_id(1)
    b_tile = bstart_ref[qi, head]
    thr = thr_ref[0]
    j0 = lax.fori_loop(
        0, qi, lambda j, n: n + jnp.where(b_tile - bend_ref[j, head] >= thr, 1, 0), 0)

    @pl.when(qi == 0)
    def _():
        kb_ref[...] = _unpack_bias(kbp_ref[...], sel_ref)

    lane = lax.broadcasted_iota(jnp.int32, (tq, V7X_LANES), 1)
    ones3 = jnp.where(lane < 3, 1.0, 0.0).astype(_BF16)
    q_aug = jnp.concatenate([q_ref[...], ones3], axis=1)
    acc_ref[...] = jnp.zeros_like(acc_ref)

    def logits(j):
        ks = pl.multiple_of(j * tq, tq)
        k_aug = jnp.concatenate([k_ref[pl.ds(ks, tq), :], kb_ref[pl.ds(ks, tq), :]], axis=1)
        return lax.dot_general(k_aug, q_aug, (((1,), (1,)), ((), ())),
                               preferred_element_type=_F32)

    def accumulate(j, s, m_prev, l_prev):
        m_new = jnp.maximum(m_prev, jnp.max(s, axis=0, keepdims=True))
        alpha = jnp.exp2(m_prev - m_new)
        p = jnp.exp2(s - m_new)
        l_new = alpha * l_prev + jnp.sum(p, axis=0, keepdims=True)
        ks = pl.multiple_of(j * tq, tq)
        pv = lax.dot_general(v_ref[pl.ds(ks, tq), :], p.astype(_BF16),
                             (((0,), (0,)), ((), ())), preferred_element_type=_F32)
        acc_ref[...] = alpha * acc_ref[...] + pv
        return m_new, l_new

    def body(j, carry):
        m_prev, l_prev, s = carry
        s_next = logits(j + 1)
        m_new, l_new = accumulate(j, s, m_prev, l_prev)
        return m_new, l_new, s_next

    m0 = jnp.full((1, tq), MASK_VALUE, _F32)
    l0 = jnp.zeros((1, tq), _F32)
    m, l, s = lax.fori_loop(j0, qi, body, (m0, l0, logits(j0)))
    row = lax.broadcasted_iota(jnp.int32, s.shape, 0)
    col = lax.broadcasted_iota(jnp.int32, s.shape, 1)
    m, l = accumulate(qi, jnp.where(row <= col, s, MASK_VALUE), m, l)
    o_ref[...] = (acc_ref[...] / l).T.astype(o_ref.dtype)


def _fox_fast_kernel(bstart_ref, bend_ref, thr_ref, q_ref, qbp_ref, k_ref, kbp_ref, sel_ref, v_ref,
                     o_ref, acc_ref, kb_ref, l_ref, s0_ref, s1_ref, *, tq):
    head = pl.program_id(0)
    qi = pl.program_id(1)
    b_tile = bstart_ref[qi, head]
    thr = thr_ref[0]
    j0 = lax.fori_loop(
        0, qi, lambda j, n: n + jnp.where(b_tile - bend_ref[j, head] >= thr, 1, 0), 0)

    @pl.when(qi == 0)
    def _():
        kb_ref[...] = _unpack_bias(kbp_ref[...], sel_ref)

    q_aug = jnp.concatenate([q_ref[...], _unpack_bias(qbp_ref[...], sel_ref)], axis=1)
    acc_ref[...] = jnp.zeros_like(acc_ref)

    def logits(j):
        ks = pl.multiple_of(j * tq, tq)
        k_aug = jnp.concatenate([k_ref[pl.ds(ks, tq), :], kb_ref[pl.ds(ks, tq), :]], axis=1)
        return _dot_nt(k_aug, q_aug)

    def accumulate(j, s):
        p = jnp.exp2(s)
        ks = pl.multiple_of(j * tq, tq)
        acc_ref[...] += lax.dot_general(v_ref[pl.ds(ks, tq), :], p.astype(_BF16),
                                        (((0,), (0,)), ((), ())), preferred_element_type=_F32)
        l_ref[...] += jnp.sum(p, axis=0, keepdims=True)

    l_ref[...] = jnp.zeros_like(l_ref)
    row = lax.broadcasted_iota(jnp.int32, s1_ref.shape, 0)
    col = lax.broadcasted_iota(jnp.int32, s1_ref.shape, 1)
    s_diag = jnp.where(row <= col, logits(qi), MASK_VALUE)
    s0_ref[...] = logits(j0)
    accumulate(qi, s_diag)
    n_before = qi - j0

    def pair(t, carry):
        j = j0 + 2 * t
        s1_ref[...] = logits(j + 1)
        accumulate(j, s0_ref[...])
        s0_ref[...] = logits(j + 2)
        accumulate(j + 1, s1_ref[...])
        return carry

    lax.fori_loop(0, n_before // 2, pair, 0)

    @pl.when(n_before % 2 == 1)
    def _():
        accumulate(qi - 1, s0_ref[...])

    o_ref[...] = (acc_ref[...] / l_ref[...]).T.astype(o_ref.dtype)


def _fox_attention(qk, kb, qb, b_start, b_end, qk_bound, v, tq):
    s = v.shape[0]
    hd = FOX_HEAD_DIM
    lanes = V7X_LANES
    thr = (2.0 * qk_bound + FOX_SKIP_BITS).reshape(1).astype(_F32)
    sel = np.zeros((FOX_HEADS, lanes, lanes), np.float32)
    for head in range(FOX_HEADS):
        for c in range(FOX_BIAS_LANES):
            sel[head, FOX_BIAS_LANES * head + c, c] = 1.0
    sel = jnp.asarray(sel, _BF16)
    smem = pl.BlockSpec(memory_space=pltpu.SMEM)
    q_spec = pl.BlockSpec((tq, hd), lambda h, i: (i, h))
    k_spec = pl.BlockSpec((s, hd), lambda h, i: (0, FOX_HEADS + h))
    v_spec = pl.BlockSpec((s, hd), lambda h, i: (0, h))
    kb_spec = pl.BlockSpec((s, lanes), lambda h, i: (0, 0))
    qb_spec = pl.BlockSpec((tq, lanes), lambda h, i: (i, 0))
    sel_spec = pl.BlockSpec((None, lanes, lanes), lambda h, i: (h, 0, 0))

    def call(kernel, in_specs, extra_scratch, *args):
        return pl.pallas_call(
            functools.partial(kernel, tq=tq),
            out_shape=jax.ShapeDtypeStruct((s, FOX_WIDTH), _BF16),
            grid=(FOX_HEADS, s // tq),
            in_specs=[smem, smem, smem] + in_specs,
            out_specs=q_spec,
            scratch_shapes=[pltpu.VMEM((hd, tq), _F32),
                            pltpu.VMEM((s, lanes), _BF16)] + extra_scratch,
            compiler_params=_params(("parallel", "arbitrary"), 32),
            name=kernel.__name__.strip("_"),
        )(b_start, b_end, thr, *args)

    fast_scratch = [pltpu.VMEM((1, tq), _F32), pltpu.VMEM((tq, tq), _F32),
                    pltpu.VMEM((tq, tq), _F32)]
    return lax.cond(
        qk_bound <= FOX_FAST_MAX_LOGIT,
        lambda: call(_fox_fast_kernel, [q_spec, qb_spec, k_spec, kb_spec, sel_spec, v_spec],
                     fast_scratch, qk, qb, qk, kb, sel, v),
        lambda: call(_fox_kernel, [q_spec, k_spec, kb_spec, sel_spec, v_spec], [],
                     qk, qk, kb, sel, v))


def _shift_rows(ext, d):
    return pltpu.roll(ext, d, axis=0)[V7X_SUBLANES:]


def _lru_kernel(lx_ref, lg_ref, wc_ref, bc_ref, wg_ref, bg_ref, lam_ref, o_ref,
                halo_ref, carry_ref, a_ref, h_ref):
    @pl.when(pl.program_id(0) == 0)
    def _():
        halo_ref[...] = jnp.zeros_like(halo_ref)
        carry_ref[...] = jnp.zeros_like(carry_ref)

    lx = lx_ref[...]
    ts = lx.shape[0]
    ext = jnp.concatenate([halo_ref[...], lx], axis=0)
    xr = (wc_ref[3:4, :] * lx + wc_ref[2:3, :] * _shift_rows(ext, 1)
          + wc_ref[1:2, :] * _shift_rows(ext, 2) + wc_ref[0:1, :] * _shift_rows(ext, 3)
          + bc_ref[...])
    halo_ref[...] = lx[ts - V7X_SUBLANES:]

    lam = lam_ref[...]
    log_sig_lam = jnp.minimum(lam, 0.0) - jnp.log1p(jnp.exp(-jnp.abs(lam)))
    bd = LRU_BLOCK_DIM
    for nb in range(LRU_BLOCKS):
        sl = slice(nb * bd, (nb + 1) * bd)
        x_nb = xr[:, sl]
        gates = jax.nn.sigmoid(jnp.dot(x_nb.astype(_BF16), wg_ref[nb],
                                       preferred_element_type=_F32) + bg_ref[nb])
        log_a = LRU_C * gates[:, :bd] * log_sig_lam[:, sl]
        a_ref[:, sl] = jnp.exp(log_a)
        t = jnp.tanh(log_a)
        h_ref[:, sl] = jnp.sqrt(-2.0 * t / (1.0 - t)) * (gates[:, bd:] * x_nb)

    sub = lax.broadcasted_iota(jnp.int32, (V7X_SUBLANES, 1), 0)

    def group(g, carry):
        rows = pl.ds(pl.multiple_of(g * V7X_SUBLANES, V7X_SUBLANES), V7X_SUBLANES)
        a = a_ref[rows, :]
        h = h_ref[rows, :]
        for d in (1, 2, 4):
            valid = sub >= d
            h = h + a * jnp.where(valid, pltpu.roll(h, d, axis=0), 0.0)
            a = a * jnp.where(valid, pltpu.roll(a, d, axis=0), 1.0)
        h = h + a * carry
        h_ref[rows, :] = h
        return jnp.broadcast_to(h[V7X_SUBLANES - 1:, :], h.shape)

    carry_ref[...] = lax.fori_loop(0, ts // V7X_SUBLANES, group, carry_ref[...],
                                   unroll=LRU_SCAN_UNROLL)
    o_ref[...] = (h_ref[...] * jax.nn.gelu(lg_ref[...])).astype(o_ref.dtype)


def _lru_branch(lxlg, w_conv, b_conv, w_gate, b_gate, lam, ts):
    s = lxlg.shape[0]
    w = LRU_WIDTH
    bd = LRU_BLOCK_DIM
    return pl.pallas_call(
        _lru_kernel,
        out_shape=jax.ShapeDtypeStruct((s, w), _BF16),
        grid=(s // ts,),
        in_specs=[pl.BlockSpec((ts, w), lambda i: (i, 0)),
                  pl.BlockSpec((ts, w), lambda i: (i, 1)),
                  pl.BlockSpec((LRU_CONV, w), lambda i: (0, 0)),
                  pl.BlockSpec((1, w), lambda i: (0, 0)),
                  pl.BlockSpec((LRU_BLOCKS, bd, 2 * bd), lambda i: (0, 0, 0)),
                  pl.BlockSpec((LRU_BLOCKS, 1, 2 * bd), lambda i: (0, 0, 0)),
                  pl.BlockSpec((1, w), lambda i: (0, 0))],
        out_specs=pl.BlockSpec((ts, w), lambda i: (i, 0)),
        scratch_shapes=[pltpu.VMEM((V7X_SUBLANES, w), _F32), pltpu.VMEM((V7X_SUBLANES, w), _F32),
                        pltpu.VMEM((ts, w), _F32), pltpu.VMEM((ts, w), _F32)],
        compiler_params=_params(("arbitrary",), 40),
        name="conv_rglru",
    )(lxlg, lxlg, w_conv, b_conv.reshape(1, -1), w_gate, b_gate, lam.reshape(1, -1))


def _mem_kv_kernel(mem_ref, g_ref, w_ref, gk_ref, k_ref, v_ref):
    x = mem_ref[...]
    ms = jnp.mean(x * x, axis=-1, keepdims=True)
    hm = (x * lax.rsqrt(ms + EPS) * g_ref[...]).astype(_BF16)
    acc = jnp.dot(hm, w_ref[...].astype(_BF16), preferred_element_type=_F32)

    @pl.when(pl.program_id(0) == 0)
    def _():
        for gi in range(MEM_HEADS):
            sl = slice(gi * MEM_HEAD_DIM, (gi + 1) * MEM_HEAD_DIM)
            blk = acc[:, sl]
            ms_k = jnp.mean(blk * blk, axis=-1, keepdims=True)
            k_ref[:, sl] = (blk * lax.rsqrt(ms_k + EPS) * gk_ref[:, sl]).astype(k_ref.dtype)

    @pl.when(pl.program_id(0) == 1)
    def _():
        v_ref[...] = acc.astype(v_ref.dtype)


def _mem_kv(mem, g_mem, w_kv, g_k):
    m, d = mem.shape
    w = MEM_WIDTH
    const = lambda j: (0, 0)
    return pl.pallas_call(
        _mem_kv_kernel,
        out_shape=(jax.ShapeDtypeStruct((m, w), _BF16), jax.ShapeDtypeStruct((m, w), _BF16)),
        grid=(2,),
        in_specs=[pl.BlockSpec((m, d), const), pl.BlockSpec((1, d), const),
                  pl.BlockSpec((d, w), lambda j: (0, j)), pl.BlockSpec((1, w), const)],
        out_specs=(pl.BlockSpec((m, w), const), pl.BlockSpec((m, w), const)),
        compiler_params=_params(("arbitrary",), 40),
        name="mem_kv",
    )(mem, g_mem.reshape(1, d), w_kv, jnp.tile(g_k, MEM_HEADS).reshape(1, w))


def _mem_attn_kernel(q_ref, k_ref, v_ref, o_ref):
    s = lax.dot_general(q_ref[...], k_ref[...], (((1,), (1,)), ((), ())),
                        preferred_element_type=_F32)
    m = jnp.max(s, axis=-1, keepdims=True)
    p = jnp.exp(s - m)
    l = jnp.sum(p, axis=-1, keepdims=True)
    acc = jnp.dot(p.astype(_BF16), v_ref[...], preferred_element_type=_F32)
    o_ref[...] = (acc / l).astype(o_ref.dtype)


def _mem_attention(q, k, v, ts):
    s = q.shape[0]
    m = k.shape[0]
    hd = MEM_HEAD_DIM
    return pl.pallas_call(
        _mem_attn_kernel,
        out_shape=jax.ShapeDtypeStruct((s, MEM_WIDTH), _BF16),
        grid=(s // ts, MEM_HEADS),
        in_specs=[pl.BlockSpec((ts, hd), lambda i, h: (i, h)),
                  pl.BlockSpec((m, hd), lambda i, h: (0, h)),
                  pl.BlockSpec((m, hd), lambda i, h: (0, h))],
        out_specs=pl.BlockSpec((ts, hd), lambda i, h: (i, h)),
        compiler_params=_params(("parallel", "arbitrary"), 32),
        name="mem_attention",
    )(q, k, v)


def _merge_kernel(h_ref, yf_ref, yl_ref, ym_ref, wg0_ref, wg1_ref, wg2_ref, wb_ref, bg_ref,
                  o_ref):
    h = h_ref[...]
    merged = None
    for n, (y_ref, wg_ref) in enumerate(((yf_ref, wg0_ref), (yl_ref, wg1_ref),
                                         (ym_ref, wg2_ref))):
        gate = jax.nn.sigmoid(_dot_nt(h, wg_ref[...]) + bg_ref[n:n + 1, :])
        term = gate * lax.dot_general(y_ref[...], wb_ref[n], (((1,), (0,)), ((), ())),
                                      preferred_element_type=_F32)
        merged = term if merged is None else merged + term
    o_ref[...] = merged.astype(o_ref.dtype)


def _gated_merge(h, y_fox, y_lru, y_mem, w_t, gate_row0, w_branch, b_gate, tm, tn):
    s, d = h.shape
    nj = d // tn
    bw = y_fox.shape[1]
    y_spec = pl.BlockSpec((tm, bw), lambda i, j: (i, 0))
    assert gate_row0 % V7X_SUBLANES == 0

    def gate_spec(n):
        return pl.BlockSpec(
            (pl.Element(tn), pl.Element(d)),
            lambda i, j: (pl.multiple_of(gate_row0 + n * d + j * tn, V7X_SUBLANES), 0))

    return pl.pallas_call(
        _merge_kernel,
        out_shape=jax.ShapeDtypeStruct((s, d), _BF16),
        grid=(s // tm, nj),
        in_specs=[pl.BlockSpec((tm, d), lambda i, j: (i, 0)), y_spec, y_spec, y_spec,
                  gate_spec(0), gate_spec(1), gate_spec(2),
                  pl.BlockSpec((N_BRANCH, bw, tn), lambda i, j: (0, 0, j)),
                  pl.BlockSpec((N_BRANCH, tn), lambda i, j: (0, j))],
        out_specs=pl.BlockSpec((tm, tn), lambda i, j: (i, j)),
        compiler_params=_params(("parallel", "arbitrary"), 56),
        name="gated_merge",
    )(h, y_fox, y_lru, y_mem, w_t, w_t, w_t, w_branch, b_gate)


def _out_norm_kernel(a_ref, w_ref, x_ref, g_ref, x2_ref, h2_ref):
    x2 = x_ref[...] + jnp.dot(a_ref[...], w_ref[...], preferred_element_type=_F32)
    x2_ref[...] = x2
    ms = jnp.mean(x2 * x2, axis=-1, keepdims=True)
    h2_ref[...] = (x2 * lax.rsqrt(ms + EPS) * g_ref[...]).astype(h2_ref.dtype)


def _out_proj_norm(a, w, x, g, tm):
    m, k = a.shape
    d = w.shape[1]
    row = lambda i: (i, 0)
    return pl.pallas_call(
        _out_norm_kernel,
        out_shape=(jax.ShapeDtypeStruct((m, d), _F32), jax.ShapeDtypeStruct((m, d), _BF16)),
        grid=(m // tm,),
        in_specs=[pl.BlockSpec((tm, k), row), pl.BlockSpec((k, d), lambda i: (0, 0)),
                  pl.BlockSpec((tm, d), row), pl.BlockSpec((1, d), lambda i: (0, 0))],
        out_specs=(pl.BlockSpec((tm, d), row), pl.BlockSpec((tm, d), row)),
        compiler_params=_params(("parallel",), 48),
        name="proj_out_norm",
    )(a, w, x, g.reshape(1, d))


def _ffn_up_kernel(a_ref, wa_ref, wv_ref, wca_ref, wcv_ref, bca_ref, bcv_ref, o_ref,
                   halo_a_ref, halo_v_ref, wa_bf_ref, wv_bf_ref):
    @pl.when(pl.program_id(1) == 0)
    def _():
        halo_a_ref[...] = jnp.zeros_like(halo_a_ref)
        halo_v_ref[...] = jnp.zeros_like(halo_v_ref)
        wa_bf_ref[...] = wa_ref[...].astype(_BF16)
        wv_bf_ref[...] = wv_ref[...].astype(_BF16)

    a = a_ref[...]
    tm = a.shape[0]

    def conv(up, halo_ref, wc_ref, bc_ref):
        ext = jnp.concatenate([halo_ref[...], up], axis=0)
        halo_ref[...] = up[tm - V7X_SUBLANES:]
        return (wc_ref[2:3, :] * up + wc_ref[1:2, :] * _shift_rows(ext, 1)
                + wc_ref[0:1, :] * _shift_rows(ext, 2) + bc_ref[...])

    act = conv(jnp.dot(a, wa_bf_ref[...], preferred_element_type=_F32), halo_a_ref, wca_ref,
               bca_ref)
    val = conv(jnp.dot(a, wv_bf_ref[...], preferred_element_type=_F32), halo_v_ref, wcv_ref,
               bcv_ref)
    o_ref[...] = (jax.nn.gelu(act) * val).astype(o_ref.dtype)


def _ffn_up(h2, w_up, w_conv, b_conv, tm, tn):
    s, d = h2.shape
    f = FFN_HIDDEN
    nj = f // tn
    return pl.pallas_call(
        _ffn_up_kernel,
        out_shape=jax.ShapeDtypeStruct((s, f), _BF16),
        grid=(nj, s // tm),
        in_specs=[pl.BlockSpec((tm, d), lambda j, i: (i, 0)),
                  pl.BlockSpec((d, tn), lambda j, i: (0, j)),
                  pl.BlockSpec((d, tn), lambda j, i: (0, nj + j)),
                  pl.BlockSpec((FFN_CONV, tn), lambda j, i: (0, j)),
                  pl.BlockSpec((FFN_CONV, tn), lambda j, i: (0, nj + j)),
                  pl.BlockSpec((1, tn), lambda j, i: (0, j)),
                  pl.BlockSpec((1, tn), lambda j, i: (0, nj + j))],
        out_specs=pl.BlockSpec((tm, tn), lambda j, i: (i, j)),
        scratch_shapes=[pltpu.VMEM((V7X_SUBLANES, tn), _F32),
                        pltpu.VMEM((V7X_SUBLANES, tn), _F32),
                        pltpu.VMEM((d, tn), _BF16), pltpu.VMEM((d, tn), _BF16)],
        compiler_params=_params(("parallel", "arbitrary"), 56),
        name="ffn_up_conv_geglu",
    )(h2, w_up, w_up, w_conv, w_conv, b_conv.reshape(1, -1), b_conv.reshape(1, -1))


def _layer(x, mem, g_mix, w_in, b_f, g_q_fox, g_k_fox, w_lru_conv, b_lru_conv, w_rg_a, b_rg_a,
           w_rg_x, b_rg_x, lru_lambda, g_mem, w_mem_kv, g_q_mem, g_k_mem, b_gate, w_branch,
           w_out, g_ffn, w_ffn_up, w_ffn_conv, b_ffn_conv, w_ffn_down):
    c_k = 2 * FOX_WIDTH
    c_v = c_k + FOX_WIDTH
    c_f = c_v + FOX_HEADS
    c_l = c_f + 2 * LRU_WIDTH
    c_m = c_l + MEM_WIDTH

    w_in_t = w_in.T

    g_q_scaled = g_q_fox * (LOG2_E * FOX_HEAD_DIM ** -0.5)
    qk_bound = (1.02 * FOX_HEAD_DIM) * jnp.max(jnp.abs(g_q_scaled)) * jnp.max(jnp.abs(g_k_fox))
    b_pad = jnp.pad(b_f.reshape(1, -1), ((0, 0), (0, V7X_LANES - FOX_HEADS)))
    h, kb, qb, b_start, b_end = _norm_forget_bias(x, g_mix, w_in_t, c_v, b_pad, qk_bound,
                                                  FOX_TILE)
    gain_qk = jnp.concatenate([jnp.tile(g_q_scaled, FOX_HEADS),
                               jnp.tile(g_k_fox, FOX_HEADS)]).reshape(1, -1)
    qk = _matmul(h, w_in_t, w_t=True, n=c_k, tm=ROW_TILE, tn=COL_TILE, out_dtype=_BF16,
                 epilogue="gnorm", extra=gain_qk, group=FOX_HEAD_DIM, name="proj_qk")
    v = _matmul(h, w_in_t, w_t=True, n=FOX_WIDTH, w_off=c_k, tm=ROW_TILE, tn=COL_TILE,
                out_dtype=_BF16, name="proj_v")
    y_fox = _fox_attention(qk, kb, qb, b_start, b_end, qk_bound, v, FOX_TILE)

    lxlg = _matmul(h, w_in_t, w_t=True, n=2 * LRU_WIDTH, w_off=c_f, tm=ROW_TILE, tn=COL_TILE,
                   out_dtype=_F32, name="proj_lru")
    w_gate_lru = jnp.concatenate([w_rg_a, w_rg_x], axis=-1).astype(_BF16)
    b_gate_lru = jnp.concatenate([b_rg_a, b_rg_x], axis=-1).reshape(LRU_BLOCKS, 1, -1)
    y_lru = _lru_branch(lxlg, w_lru_conv, b_lru_conv, w_gate_lru, b_gate_lru, lru_lambda,
                        SEQ_TILE)

    gain_mq = (jnp.tile(g_q_mem, MEM_HEADS) * (MEM_HEAD_DIM ** -0.5)).reshape(1, -1)
    mq = _matmul(h, w_in_t, w_t=True, n=MEM_WIDTH, w_off=c_l, tm=ROW_TILE, tn=COL_TILE,
                 out_dtype=_BF16, epilogue="gnorm", extra=gain_mq, group=MEM_HEAD_DIM,
                 name="proj_mq")
    mk, mv = _mem_kv(mem, g_mem, w_mem_kv, g_k_mem)
    y_mem = _mem_attention(mq, mk, mv, ROW_TILE)

    merged = _gated_merge(h, y_fox, y_lru, y_mem, w_in_t, c_m, w_branch, b_gate,
                          ROW_TILE, MERGE_COL_TILE)
    x2, h2 = _out_proj_norm(merged, w_out.astype(_BF16), x, g_ffn, SEQ_TILE)

    g = _ffn_up(h2, w_ffn_up, w_ffn_conv, b_ffn_conv, ROW_TILE, FFN_COL_TILE)
    return _matmul(g, w_ffn_down, n=D_MODEL, tm=ROW_TILE, tn=FFN_DOWN_COL_TILE,
                   rows_outer=True,
                   out_dtype=_F32, epilogue="residual", extra=x2, name="ffn_down")


def kernel(x, mem, g_mix, w_in, b_f, g_q_fox, g_k_fox, w_lru_conv, b_lru_conv, w_rg_a, b_rg_a,
           w_rg_x, b_rg_x, lru_lambda, g_mem, w_mem_kv, g_q_mem, g_k_mem, b_gate, w_branch,
           w_out, g_ffn, w_ffn_up, w_ffn_conv, b_ffn_conv, w_ffn_down):
    depth = g_mix.shape[0]
    outs = []
    for b in range(x.shape[0]):
        xb = x[b]
        for l in range(depth):
            xb = _layer(xb, mem[b], g_mix[l], w_in[l], b_f[l], g_q_fox[l], g_k_fox[l],
                        w_lru_conv[l], b_lru_conv[l], w_rg_a[l], b_rg_a[l], w_rg_x[l],
                        b_rg_x[l], lru_lambda[l], g_mem[l], w_mem_kv[l], g_q_mem[l],
                        g_k_mem[l], b_gate[l], w_branch[l], w_out[l], g_ffn[l], w_ffn_up[l],
                        w_ffn_conv[l], b_ffn_conv[l], w_ffn_down[l])
        outs.append(xb)
    return outs[0][None] if len(outs) == 1 else jnp.stack(outs)
```

```python
import functools

import jax
import jax.numpy as jnp
import numpy as np
from jax import lax
from jax.experimental import pallas as pl
from jax.experimental.pallas import tpu as pltpu

D_MODEL = 2048
FOX_HEADS = 8
FOX_HEAD_DIM = 128
FOX_WIDTH = FOX_HEADS * FOX_HEAD_DIM
LRU_WIDTH = 1024
LRU_BLOCKS = 8
LRU_BLOCK_DIM = LRU_WIDTH // LRU_BLOCKS
LRU_CONV = 4
LRU_C = 8.0
MEM_HEADS = 4
MEM_HEAD_DIM = 256
MEM_WIDTH = MEM_HEADS * MEM_HEAD_DIM
N_BRANCH = 3
FFN_HIDDEN = 5632
FFN_CONV = 3
EPS = 1e-6

V7X_SUBLANES = 8
V7X_LANES = 128
MASK_VALUE = -1e30
LOG2_E = 1.4426950408889634
ROW_TILE = 1024
COL_TILE = 1024
FFN_COL_TILE = 512
MERGE_COL_TILE = 256
SEQ_TILE = 512
FOX_TILE = 512
FOX_SKIP_BITS = 64.0
FOX_BIAS_LANES = 16
FOX_FAST_MAX_LOGIT = 48.0
LRU_SCAN_UNROLL = 8

_BF16 = jnp.bfloat16
_F32 = jnp.float32


def _params(semantics, vmem_mib):
    return pltpu.CompilerParams(dimension_semantics=semantics,
                                vmem_limit_bytes=vmem_mib * 1024 * 1024)


def _dot_nt(a, w_t):
    return lax.dot_general(a, w_t, (((1,), (1,)), ((), ())), preferred_element_type=_F32)


def _mm_kernel(*refs, epilogue, group, cast_w, w_t):
    a_ref, w_ref = refs[0], refs[1]
    if cast_w:
        o_ref, wb_ref = refs[-2], refs[-1]

        @pl.when(pl.program_id(1) == 0)
        def _():
            wb_ref[...] = w_ref[...].astype(_BF16)

        w = wb_ref[...]
    else:
        o_ref = refs[-1]
        w = w_ref[...]
    if w_t:
        acc = _dot_nt(a_ref[...], w)
    else:
        acc = lax.dot_general(a_ref[...], w, (((1,), (0,)), ((), ())),
                              preferred_element_type=_F32)
    if epilogue == "gnorm":
        g_ref = refs[2]
        for gi in range(acc.shape[1] // group):
            sl = slice(gi * group, (gi + 1) * group)
            blk = acc[:, sl]
            ms = jnp.mean(blk * blk, axis=-1, keepdims=True)
            o_ref[:, sl] = (blk * lax.rsqrt(ms + EPS) * g_ref[:, sl]).astype(o_ref.dtype)
    elif epilogue == "residual":
        o_ref[...] = (refs[2][...] + acc).astype(o_ref.dtype)
    else:
        o_ref[...] = acc.astype(o_ref.dtype)


def _matmul(a, w, *, n, tm, tn, out_dtype, w_t=False, w_off=0, rows_outer=False,
            epilogue="plain", extra=None, group=None, name):
    m, k = a.shape
    cast_w = w.dtype != _BF16 and not rows_outer
    assert n % tn == 0 and m % tm == 0
    if rows_outer:
        grid = (m // tm, n // tn)
        ij = lambda i, j: (i, j)
    else:
        grid = (n // tn, m // tm)
        ij = lambda j, i: (i, j)
    if w_t:
        assert w_off % V7X_SUBLANES == 0
        w_block = (tn, k)
        w_spec = pl.BlockSpec((pl.Element(tn), pl.Element(k)),
                              lambda *g: (pl.multiple_of(w_off + ij(*g)[1] * tn, V7X_SUBLANES), 0))
    else:
        assert w_off % tn == 0
        w_block = (k, tn)
        w_mode = {"pipeline_mode": pl.Buffered(1)} if n == tn else {}
        w_spec = pl.BlockSpec(w_block, lambda *g: (0, w_off // tn + ij(*g)[1]), **w_mode)
    in_specs = [pl.BlockSpec((tm, k), lambda *g: (ij(*g)[0], 0)), w_spec]
    args = [a, w]
    block_bytes = (tm * k * 2 + k * tn * w.dtype.itemsize
                   + tm * tn * jnp.dtype(out_dtype).itemsize)
    if epilogue == "gnorm":
        in_specs.append(pl.BlockSpec((1, tn), lambda *g: (0, ij(*g)[1])))
        args.append(extra)
    elif epilogue == "residual":
        in_specs.append(pl.BlockSpec((tm, tn), lambda *g: ij(*g)))
        args.append(extra)
        block_bytes += tm * tn * extra.dtype.itemsize
    scratch = [pltpu.VMEM(w_block, _BF16)] if cast_w else []
    w_single = (not w_t and n == tn) * k * tn * w.dtype.itemsize
    vmem_mib = -(-(2 * block_bytes - w_single + cast_w * k * tn * 2 + 2 * tm * tn * 4)
                 // 2 ** 20) + 2
    return pl.pallas_call(
        functools.partial(_mm_kernel, epilogue=epilogue, group=group, cast_w=cast_w, w_t=w_t),
        out_shape=jax.ShapeDtypeStruct((m, n), out_dtype),
        grid=grid,
        in_specs=in_specs,
        out_specs=pl.BlockSpec((tm, tn), lambda *g: ij(*g)),
        scratch_shapes=scratch,
        compiler_params=_params(("parallel", "arbitrary"), vmem_mib),
        name=name,
    )(*args)


def _split3(x):
    hi = x.astype(_BF16)
    r1 = x - hi.astype(_F32)
    mid = r1.astype(_BF16)
    lo = (r1 - mid.astype(_F32)).astype(_BF16)
    return hi, mid, lo


def _forget_bias_kernel(shift_ref, x_ref, g_ref, w_ref, b_ref, sel_ref, ones_ref, h_ref, kb_ref,
                        qb_ref, edge_ref, carry_ref):
    @pl.when(pl.program_id(0) == 0)
    def _():
        carry_ref[...] = jnp.zeros_like(carry_ref)

    tc = x_ref.shape[0]
    x = x_ref[...]
    ms = jnp.mean(x * x, axis=-1, keepdims=True)
    h = (x * lax.rsqrt(ms + EPS) * g_ref[...]).astype(_BF16)
    h_ref[...] = h
    w = jnp.concatenate([w_ref[...], jnp.zeros((V7X_LANES - FOX_HEADS, w_ref.shape[1]), _F32)],
                        axis=0).astype(_BF16)
    z = _dot_nt(h, w) + b_ref[...]
    neg_log_f = (jnp.log1p(jnp.exp(-jnp.abs(z))) - jnp.minimum(z, 0.0)) * LOG2_E
    row = lax.broadcasted_iota(jnp.int32, (tc, tc), 0)
    col = lax.broadcasted_iota(jnp.int32, (tc, tc), 1)
    tri = jnp.where(col <= row, 1.0, 0.0).astype(_BF16)
    c = carry_ref[0:1, :]
    for part in _split3(neg_log_f):
        c = c + jnp.dot(tri, part, preferred_element_type=_F32)
    carry_ref[...] = jnp.broadcast_to(c[tc - 1:tc, :], carry_ref.shape)
    edge_ref[...] = jnp.concatenate(
        [c[0:1, :], c[tc - 1:tc, :], jnp.zeros((V7X_SUBLANES - 2, c.shape[1]), _F32)], axis=0)
    parts = jnp.concatenate(_split3(c) + _split3(-(c + shift_ref[0])), axis=1)
    routed = jnp.dot(parts, sel_ref[...], preferred_element_type=_F32) + ones_ref[...]
    half = kb_ref.shape[1]
    kb_ref[...] = routed[:, :half].astype(kb_ref.dtype)
    qb_ref[...] = routed[:, half:].astype(qb_ref.dtype)


def _norm_forget_bias(x, g, w_t, w_row0, b_pad, shift, tc):
    s, d = x.shape
    lanes = V7X_LANES
    width = lanes
    nt = s // tc
    sel = np.zeros((6 * lanes, 2 * width), np.float32)
    ones = np.zeros((1, 2 * width), np.float32)
    for head in range(FOX_HEADS):
        for part in range(3):
            sel[part * lanes + head, FOX_BIAS_LANES * head + part] = 1.0
            sel[(3 + part) * lanes + head, width + FOX_BIAS_LANES * head + 3 + part] = 1.0
            ones[0, FOX_BIAS_LANES * head + 3 + part] = 1.0
            ones[0, width + FOX_BIAS_LANES * head + part] = 1.0
    h, kb, qb, edges = pl.pallas_call(
        _forget_bias_kernel,
        out_shape=(jax.ShapeDtypeStruct((s, d), _BF16),
                   jax.ShapeDtypeStruct((s, width), _BF16),
                   jax.ShapeDtypeStruct((s, width), _BF16),
                   jax.ShapeDtypeStruct((nt * V7X_SUBLANES, lanes), _F32)),
        grid=(nt,),
        in_specs=[pl.BlockSpec(memory_space=pltpu.SMEM),
                  pl.BlockSpec((tc, d), lambda i: (i, 0)),
                  pl.BlockSpec((1, d), lambda i: (0, 0)),
                  pl.BlockSpec((pl.Element(FOX_HEADS), pl.Element(d)), lambda i: (w_row0, 0)),
                  pl.BlockSpec((1, lanes), lambda i: (0, 0)),
                  pl.BlockSpec((6 * lanes, 2 * width), lambda i: (0, 0)),
                  pl.BlockSpec((1, 2 * width), lambda i: (0, 0))],
        out_specs=(pl.BlockSpec((tc, d), lambda i: (i, 0)),
                   pl.BlockSpec((tc, width), lambda i: (i, 0)),
                   pl.BlockSpec((tc, width), lambda i: (i, 0)),
                   pl.BlockSpec((V7X_SUBLANES, lanes), lambda i: (i, 0))),
        scratch_shapes=[pltpu.VMEM((V7X_SUBLANES, lanes), _F32)],
        compiler_params=_params(("arbitrary",), 40),
        name="norm_forget_bias",
    )(shift.reshape(1).astype(_F32), x, g.reshape(1, d), w_t, b_pad, jnp.asarray(sel, _BF16),
      jnp.asarray(ones))
    edges = edges.reshape(nt, V7X_SUBLANES, lanes)
    return h, kb, qb, edges[:, 0, :FOX_HEADS], edges[:, 1, :FOX_HEADS]


def _unpack_bias(packed, sel_ref):
    return jnp.dot(packed, sel_ref[...], preferred_element_type=_F32).astype(_BF16)


def _fox_kernel(bstart_ref, bend_ref, thr_ref, q_ref, k_ref, kbp_ref, sel_ref, v_ref, o_ref,
                acc_ref, kb_ref, *, tq):
    head = pl.program_id(0)
    qi = pl.program_id(1)
    b_tile = bstart_ref[qi, head]
    thr = thr_ref[0]
    j0 = lax.fori_loop(
        0, qi, lambda j, n: n + jnp.where(b_tile - bend_ref[j, head] >= thr, 1, 0), 0)

    @pl.when(qi == 0)
    def _():
        kb_ref[...] = _unpack_bias(kbp_ref[...], sel_ref)

    lane = lax.broadcasted_iota(jnp.int32, (tq, V7X_LANES), 1)
    ones3 = jnp.where(lane < 3, 1.0, 0.0).astype(_BF16)
    q_aug = jnp.concatenate([q_ref[...], ones3], axis=1)
    acc_ref[...] = jnp.zeros_like(acc_ref)

    def logits(j):
        ks = pl.multiple_of(j * tq, tq)
        k_aug = jnp.concatenate([k_ref[pl.ds(ks, tq), :], kb_ref[pl.ds(ks, tq), :]], axis=1)
        return lax.dot_general(k_aug, q_aug, (((1,), (1,)), ((), ())),
                               preferred_element_type=_F32)

    def accumulate(j, s, m_prev, l_prev):
        m_new = jnp.maximum(m_prev, jnp.max(s, axis=0, keepdims=True))
        alpha = jnp.exp2(m_prev - m_new)
        p = jnp.exp2(s - m_new)
        l_new = alpha * l_prev + jnp.sum(p, axis=0, keepdims=True)
        ks = pl.multiple_of(j * tq, tq)
        pv = lax.dot_general(v_ref[pl.ds(ks, tq), :], p.astype(_BF16),
                             (((0,), (0,)), ((), ())), preferred_element_type=_F32)
        acc_ref[...] = alpha * acc_ref[...] + pv
        return m_new, l_new

    def body(j, carry):
        m_prev, l_prev, s = carry
        s_next = logits(j + 1)
        m_new, l_new = accumulate(j, s, m_prev, l_prev)
        return m_new, l_new, s_next

    m0 = jnp.full((1, tq), MASK_VALUE, _F32)
    l0 = jnp.zeros((1, tq), _F32)
    m, l, s = lax.fori_loop(j0, qi, body, (m0, l0, logits(j0)))
    row = lax.broadcasted_iota(jnp.int32, s.shape, 0)
    col = lax.broadcasted_iota(jnp.int32, s.shape, 1)
    m, l = accumulate(qi, jnp.where(row <= col, s, MASK_VALUE), m, l)
    o_ref[...] = (acc_ref[...] / l).T.astype(o_ref.dtype)


def _fox_fast_kernel(bstart_ref, bend_ref, thr_ref, q_ref, qbp_ref, k_ref, kbp_ref, sel_ref, v_ref,
                     o_ref, acc_ref, kb_ref, l_ref, s0_ref, s1_ref, *, tq):
    head = pl.program_id(0)
    qi = pl.program_id(1)
    b_tile = bstart_ref[qi, head]
    thr = thr_ref[0]
    j0 = lax.fori_loop(
        0, qi, lambda j, n: n + jnp.where(b_tile - bend_ref[j, head] >= thr, 1, 0), 0)

    @pl.when(qi == 0)
    def _():
        kb_ref[...] = _unpack_bias(kbp_ref[...], sel_ref)

    q_aug = jnp.concatenate([q_ref[...], _unpack_bias(qbp_ref[...], sel_ref)], axis=1)
    acc_ref[...] = jnp.zeros_like(acc_ref)

    def logits(j):
        ks = pl.multiple_of(j * tq, tq)
        k_aug = jnp.concatenate([k_ref[pl.ds(ks, tq), :], kb_ref[pl.ds(ks, tq), :]], axis=1)
        return _dot_nt(k_aug, q_aug)

    def accumulate(j, s):
        p = jnp.exp2(s)
        ks = pl.multiple_of(j * tq, tq)
        acc_ref[...] += lax.dot_general(v_ref[pl.ds(ks, tq), :], p.astype(_BF16),
                                        (((0,), (0,)), ((), ())), preferred_element_type=_F32)
        l_ref[...] += jnp.sum(p, axis=0, keepdims=True)

    l_ref[...] = jnp.zeros_like(l_ref)
    row = lax.broadcasted_iota(jnp.int32, s1_ref.shape, 0)
    col = lax.broadcasted_iota(jnp.int32, s1_ref.shape, 1)
    s_diag = jnp.where(row <= col, logits(qi), MASK_VALUE)
    s0_ref[...] = logits(j0)
    accumulate(qi, s_diag)
    n_before = qi - j0

    def pair(t, carry):
        j = j0 + 2 * t
        s1_ref[...] = logits(j + 1)
        accumulate(j, s0_ref[...])
        s0_ref[...] = logits(j + 2)
        accumulate(j + 1, s1_ref[...])
        return carry

    lax.fori_loop(0, n_before // 2, pair, 0)

    @pl.when(n_before % 2 == 1)
    def _():
        accumulate(qi - 1, s0_ref[...])

    o_ref[...] = (acc_ref[...] / l_ref[...]).T.astype(o_ref.dtype)


def _fox_attention(qk, kb, qb, b_start, b_end, qk_bound, v, tq):
    s = v.shape[0]
    hd = FOX_HEAD_DIM
    lanes = V7X_LANES
    thr = (2.0 * qk_bound + FOX_SKIP_BITS).reshape(1).astype(_F32)
    sel = np.zeros((FOX_HEADS, lanes, lanes), np.float32)
    for head in range(FOX_HEADS):
        for c in range(FOX_BIAS_LANES):
            sel[head, FOX_BIAS_LANES * head + c, c] = 1.0
    sel = jnp.asarray(sel, _BF16)
    smem = pl.BlockSpec(memory_space=pltpu.SMEM)
    q_spec = pl.BlockSpec((tq, hd), lambda h, i: (i, h))
    k_spec = pl.BlockSpec((s, hd), lambda h, i: (0, FOX_HEADS + h))
    v_spec = pl.BlockSpec((s, hd), lambda h, i: (0, h))
    kb_spec = pl.BlockSpec((s, lanes), lambda h, i: (0, 0))
    qb_spec = pl.BlockSpec((tq, lanes), lambda h, i: (i, 0))
    sel_spec = pl.BlockSpec((None, lanes, lanes), lambda h, i: (h, 0, 0))

    def call(kernel, in_specs, extra_scratch, *args):
        return pl.pallas_call(
            functools.partial(kernel, tq=tq),
            out_shape=jax.ShapeDtypeStruct((s, FOX_WIDTH), _BF16),
            grid=(FOX_HEADS, s // tq),
            in_specs=[smem, smem, smem] + in_specs,
            out_specs=q_spec,
            scratch_shapes=[pltpu.VMEM((hd, tq), _F32),
                            pltpu.VMEM((s, lanes), _BF16)] + extra_scratch,
            compiler_params=_params(("parallel", "arbitrary"), 32),
            name=kernel.__name__.strip("_"),
        )(b_start, b_end, thr, *args)

    fast_scratch = [pltpu.VMEM((1, tq), _F32), pltpu.VMEM((tq, tq), _F32),
                    pltpu.VMEM((tq, tq), _F32)]
    return lax.cond(
        qk_bound <= FOX_FAST_MAX_LOGIT,
        lambda: call(_fox_fast_kernel, [q_spec, qb_spec, k_spec, kb_spec, sel_spec, v_spec],
                     fast_scratch, qk, qb, qk, kb, sel, v),
        lambda: call(_fox_kernel, [q_spec, k_spec, kb_spec, sel_spec, v_spec], [],
                     qk, qk, kb, sel, v))


def _shift_rows(ext, d):
    return pltpu.roll(ext, d, axis=0)[V7X_SUBLANES:]


def _lru_kernel(lx_ref, lg_ref, wc_ref, bc_ref, wg_ref, bg_ref, lam_ref, o_ref,
                halo_ref, carry_ref, a_ref, h_ref):
    @pl.when(pl.program_id(0) == 0)
    def _():
        halo_ref[...] = jnp.zeros_like(halo_ref)
        carry_ref[...] = jnp.zeros_like(carry_ref)

    lx = lx_ref[...]
    ts = lx.shape[0]
    ext = jnp.concatenate([halo_ref[...], lx], axis=0)
    xr = (wc_ref[3:4, :] * lx + wc_ref[2:3, :] * _shift_rows(ext, 1)
          + wc_ref[1:2, :] * _shift_rows(ext, 2) + wc_ref[0:1, :] * _shift_rows(ext, 3)
          + bc_ref[...])
    halo_ref[...] = lx[ts - V7X_SUBLANES:]

    lam = lam_ref[...]
    log_sig_lam = jnp.minimum(lam, 0.0) - jnp.log1p(jnp.exp(-jnp.abs(lam)))
    bd = LRU_BLOCK_DIM
    for nb in range(LRU_BLOCKS):
        sl = slice(nb * bd, (nb + 1) * bd)
        x_nb = xr[:, sl]
        gates = jax.nn.sigmoid(jnp.dot(x_nb.astype(_BF16), wg_ref[nb],
                                       preferred_element_type=_F32) + bg_ref[nb])
        log_a = LRU_C * gates[:, :bd] * log_sig_lam[:, sl]
        a_ref[:, sl] = jnp.exp(log_a)
        t = jnp.tanh(log_a)
        h_ref[:, sl] = jnp.sqrt(-2.0 * t / (1.0 - t)) * (gates[:, bd:] * x_nb)

    sub = lax.broadcasted_iota(jnp.int32, (V7X_SUBLANES, 1), 0)

    def group(g, carry):
        rows = pl.ds(pl.multiple_of(g * V7X_SUBLANES, V7X_SUBLANES), V7X_SUBLANES)
        a = a_ref[rows, :]
        h = h_ref[rows, :]
        for d in (1, 2, 4):
            valid = sub >= d
            h = h + a * jnp.where(valid, pltpu.roll(h, d, axis=0), 0.0)
            a = a * jnp.where(valid, pltpu.roll(a, d, axis=0), 1.0)
        h = h + a * carry
        h_ref[rows, :] = h
        return jnp.broadcast_to(h[V7X_SUBLANES - 1:, :], h.shape)

    carry_ref[...] = lax.fori_loop(0, ts // V7X_SUBLANES, group, carry_ref[...],
                                   unroll=LRU_SCAN_UNROLL)
    o_ref[...] = (h_ref[...] * jax.nn.gelu(lg_ref[...])).astype(o_ref.dtype)


def _lru_branch(lxlg, w_conv, b_conv, w_gate, b_gate, lam, ts):
    s = lxlg.shape[0]
    w = LRU_WIDTH
    bd = LRU_BLOCK_DIM
    return pl.pallas_call(
        _lru_kernel,
        out_shape=jax.ShapeDtypeStruct((s, w), _BF16),
        grid=(s // ts,),
        in_specs=[pl.BlockSpec((ts, w), lambda i: (i, 0)),
                  pl.BlockSpec((ts, w), lambda i: (i, 1)),
                  pl.BlockSpec((LRU_CONV, w), lambda i: (0, 0)),
                  pl.BlockSpec((1, w), lambda i: (0, 0)),
                  pl.BlockSpec((LRU_BLOCKS, bd, 2 * bd), lambda i: (0, 0, 0)),
                  pl.BlockSpec((LRU_BLOCKS, 1, 2 * bd), lambda i: (0, 0, 0)),
                  pl.BlockSpec((1, w), lambda i: (0, 0))],
        out_specs=pl.BlockSpec((ts, w), lambda i: (i, 0)),
        scratch_shapes=[pltpu.VMEM((V7X_SUBLANES, w), _F32), pltpu.VMEM((V7X_SUBLANES, w), _F32),
                        pltpu.VMEM((ts, w), _F32), pltpu.VMEM((ts, w), _F32)],
        compiler_params=_params(("arbitrary",), 40),
        name="conv_rglru",
    )(lxlg, lxlg, w_conv, b_conv.reshape(1, -1), w_gate, b_gate, lam.reshape(1, -1))


def _mem_kv_kernel(mem_ref, g_ref, w_ref, gk_ref, k_ref, v_ref):
    x = mem_ref[...]
    ms = jnp.mean(x * x, axis=-1, keepdims=True)
    hm = (x * lax.rsqrt(ms + EPS) * g_ref[...]).astype(_BF16)
    acc = jnp.dot(hm, w_ref[...].astype(_BF16), preferred_element_type=_F32)

    @pl.when(pl.program_id(0) == 0)
    def _():
        for gi in range(MEM_HEADS):
            sl = slice(gi * MEM_HEAD_DIM, (gi + 1) * MEM_HEAD_DIM)
            blk = acc[:, sl]
            ms_k = jnp.mean(blk * blk, axis=-1, keepdims=True)
            k_ref[:, sl] = (blk * lax.rsqrt(ms_k + EPS) * gk_ref[:, sl]).astype(k_ref.dtype)

    @pl.when(pl.program_id(0) == 1)
    def _():
        v_ref[...] = acc.astype(v_ref.dtype)


def _mem_kv(mem, g_mem, w_kv, g_k):
    m, d = mem.shape
    w = MEM_WIDTH
    const = lambda j: (0, 0)
    return pl.pallas_call(
        _mem_kv_kernel,
        out_shape=(jax.ShapeDtypeStruct((m, w), _BF16), jax.ShapeDtypeStruct((m, w), _BF16)),
        grid=(2,),
        in_specs=[pl.BlockSpec((m, d), const), pl.BlockSpec((1, d), const),
                  pl.BlockSpec((d, w), lambda j: (0, j)), pl.BlockSpec((1, w), const)],
        out_specs=(pl.BlockSpec((m, w), const), pl.BlockSpec((m, w), const)),
        compiler_params=_params(("arbitrary",), 40),
        name="mem_kv",
    )(mem, g_mem.reshape(1, d), w_kv, jnp.tile(g_k, MEM_HEADS).reshape(1, w))


def _mem_attn_kernel(q_ref, k_ref, v_ref, o_ref):
    s = lax.dot_general(q_ref[...], k_ref[...], (((1,), (1,)), ((), ())),
                        preferred_element_type=_F32)
    m = jnp.max(s, axis=-1, keepdims=True)
    p = jnp.exp(s - m)
    l = jnp.sum(p, axis=-1, keepdims=True)
    acc = jnp.dot(p.astype(_BF16), v_ref[...], preferred_element_type=_F32)
    o_ref[...] = (acc / l).astype(o_ref.dtype)


def _mem_attention(q, k, v, ts):
    s = q.shape[0]
    m = k.shape[0]
    hd = MEM_HEAD_DIM
    return pl.pallas_call(
        _mem_attn_kernel,
        out_shape=jax.ShapeDtypeStruct((s, MEM_WIDTH), _BF16),
        grid=(s // ts, MEM_HEADS),
        in_specs=[pl.BlockSpec((ts, hd), lambda i, h: (i, h)),
                  pl.BlockSpec((m, hd), lambda i, h: (0, h)),
                  pl.BlockSpec((m, hd), lambda i, h: (0, h))],
        out_specs=pl.BlockSpec((ts, hd), lambda i, h: (i, h)),
        compiler_params=_params(("parallel", "arbitrary"), 32),
        name="mem_attention",
    )(q, k, v)


def _merge_kernel(h_ref, yf_ref, yl_ref, ym_ref, wg0_ref, wg1_ref, wg2_ref, wb_ref, bg_ref,
                  o_ref):
    h = h_ref[...]
    merged = None
    for n, (y_ref, wg_ref) in enumerate(((yf_ref, wg0_ref), (yl_ref, wg1_ref),
                                         (ym_ref, wg2_ref))):
        gate = jax.nn.sigmoid(_dot_nt(h, wg_ref[...]) + bg_ref[n:n + 1, :])
        term = gate * lax.dot_general(y_ref[...], wb_ref[n], (((1,), (0,)), ((), ())),
                                      preferred_element_type=_F32)
        merged = term if merged is None else merged + term
    o_ref[...] = merged.astype(o_ref.dtype)


def _gated_merge(h, y_fox, y_lru, y_mem, w_t, gate_row0, w_branch, b_gate, tm, tn):
    s, d = h.shape
    nj = d // tn
    bw = y_fox.shape[1]
    y_spec = pl.BlockSpec((tm, bw), lambda i, j: (i, 0))
    assert gate_row0 % V7X_SUBLANES == 0

    def gate_spec(n):
        return pl.BlockSpec(
            (pl.Element(tn), pl.Element(d)),
            lambda i, j: (pl.multiple_of(gate_row0 + n * d + j * tn, V7X_SUBLANES), 0))

    return pl.pallas_call(
        _merge_kernel,
        out_shape=jax.ShapeDtypeStruct((s, d), _BF16),
        grid=(s // tm, nj),
        in_specs=[pl.BlockSpec((tm, d), lambda i, j: (i, 0)), y_spec, y_spec, y_spec,
                  gate_spec(0), gate_spec(1), gate_spec(2),
                  pl.BlockSpec((N_BRANCH, bw, tn), lambda i, j: (0, 0, j)),
                  pl.BlockSpec((N_BRANCH, tn), lambda i, j: (0, j))],
        out_specs=pl.BlockSpec((tm, tn), lambda i, j: (i, j)),
        compiler_params=_params(("parallel", "arbitrary"), 56),
        name="gated_merge",
    )(h, y_fox, y_lru, y_mem, w_t, w_t, w_t, w_branch, b_gate)


def _out_norm_kernel(a_ref, w_ref, x_ref, g_ref, x2_ref, h2_ref):
    x2 = x_ref[...] + jnp.dot(a_ref[...], w_ref[...], preferred_element_type=_F32)
    x2_ref[...] = x2
    ms = jnp.mean(x2 * x2, axis=-1, keepdims=True)
    h2_ref[...] = (x2 * lax.rsqrt(ms + EPS) * g_ref[...]).astype(h2_ref.dtype)


def _out_proj_norm(a, w, x, g, tm):
    m, k = a.shape
    d = w.shape[1]
    row = lambda i: (i, 0)
    return pl.pallas_call(
        _out_norm_kernel,
        out_shape=(jax.ShapeDtypeStruct((m, d), _F32), jax.ShapeDtypeStruct((m, d), _BF16)),
        grid=(m // tm,),
        in_specs=[pl.BlockSpec((tm, k), row), pl.BlockSpec((k, d), lambda i: (0, 0)),
                  pl.BlockSpec((tm, d), row), pl.BlockSpec((1, d), lambda i: (0, 0))],
        out_specs=(pl.BlockSpec((tm, d), row), pl.BlockSpec((tm, d), row)),
        compiler_params=_params(("parallel",), 48),
        name="proj_out_norm",
    )(a, w, x, g.reshape(1, d))


def _ffn_up_kernel(a_ref, wa_ref, wv_ref, wca_ref, wcv_ref, bca_ref, bcv_ref, o_ref,
                   halo_a_ref, halo_v_ref, wa_bf_ref, wv_bf_ref):
    @pl.when(pl.program_id(1) == 0)
    def _():
        halo_a_ref[...] = jnp.zeros_like(halo_a_ref)
        halo_v_ref[...] = jnp.zeros_like(halo_v_ref)
        wa_bf_ref[...] = wa_ref[...].astype(_BF16)
        wv_bf_ref[...] = wv_ref[...].astype(_BF16)

    a = a_ref[...]
    tm = a.shape[0]

    def conv(up, halo_ref, wc_ref, bc_ref):
        ext = jnp.concatenate([halo_ref[...], up], axis=0)
        halo_ref[...] = up[tm - V7X_SUBLANES:]
        return (wc_ref[2:3, :] * up + wc_ref[1:2, :] * _shift_rows(ext, 1)
                + wc_ref[0:1, :] * _shift_rows(ext, 2) + bc_ref[...])

    act = conv(jnp.dot(a, wa_bf_ref[...], preferred_element_type=_F32), halo_a_ref, wca_ref,
               bca_ref)
    val = conv(jnp.dot(a, wv_bf_ref[...], preferred_element_type=_F32), halo_v_ref, wcv_ref,
               bcv_ref)
    o_ref[...] = (jax.nn.gelu(act) * val).astype(o_ref.dtype)


def _ffn_up(h2, w_up, w_conv, b_conv, tm, tn):
    s, d = h2.shape
    f = FFN_HIDDEN
    nj = f // tn
    return pl.pallas_call(
        _ffn_up_kernel,
        out_shape=jax.ShapeDtypeStruct((s, f), _BF16),
        grid=(nj, s // tm),
        in_specs=[pl.BlockSpec((tm, d), lambda j, i: (i, 0)),
                  pl.BlockSpec((d, tn), lambda j, i: (0, j)),
                  pl.BlockSpec((d, tn), lambda j, i: (0, nj + j)),
                  pl.BlockSpec((FFN_CONV, tn), lambda j, i: (0, j)),
                  pl.BlockSpec((FFN_CONV, tn), lambda j, i: (0, nj + j)),
                  pl.BlockSpec((1, tn), lambda j, i: (0, j)),
                  pl.BlockSpec((1, tn), lambda j, i: (0, nj + j))],
        out_specs=pl.BlockSpec((tm, tn), lambda j, i: (i, j)),
        scratch_shapes=[pltpu.VMEM((V7X_SUBLANES, tn), _F32),
                        pltpu.VMEM((V7X_SUBLANES, tn), _F32),
                        pltpu.VMEM((d, tn), _BF16), pltpu.VMEM((d, tn), _BF16)],
        compiler_params=_params(("parallel", "arbitrary"), 56),
        name="ffn_up_conv_geglu",
    )(h2, w_up, w_up, w_conv, w_conv, b_conv.reshape(1, -1), b_conv.reshape(1, -1))


def _layer(x, mem, g_mix, w_in, b_f, g_q_fox, g_k_fox, w_lru_conv, b_lru_conv, w_rg_a, b_rg_a,
           w_rg_x, b_rg_x, lru_lambda, g_mem, w_mem_kv, g_q_mem, g_k_mem, b_gate, w_branch,
           w_out, g_ffn, w_ffn_up, w_ffn_conv, b_ffn_conv, w_ffn_down):
    c_k = 2 * FOX_WIDTH
    c_v = c_k + FOX_WIDTH
    c_f = c_v + FOX_HEADS
    c_l = c_f + 2 * LRU_WIDTH
    c_m = c_l + MEM_WIDTH

    w_in_t = w_in.T

    g_q_scaled = g_q_fox * (LOG2_E * FOX_HEAD_DIM ** -0.5)
    qk_bound = (1.02 * FOX_HEAD_DIM) * jnp.max(jnp.abs(g_q_scaled)) * jnp.max(jnp.abs(g_k_fox))
    b_pad = jnp.pad(b_f.reshape(1, -1), ((0, 0), (0, V7X_LANES - FOX_HEADS)))
    h, kb, qb, b_start, b_end = _norm_forget_bias(x, g_mix, w_in_t, c_v, b_pad, qk_bound,
                                                  FOX_TILE)
    gain_qk = jnp.concatenate([jnp.tile(g_q_scaled, FOX_HEADS),
                               jnp.tile(g_k_fox, FOX_HEADS)]).reshape(1, -1)
    qk = _matmul(h, w_in_t, w_t=True, n=c_k, tm=ROW_TILE, tn=COL_TILE, out_dtype=_BF16,
                 epilogue="gnorm", extra=gain_qk, group=FOX_HEAD_DIM, name="proj_qk")
    v = _matmul(h, w_in_t, w_t=True, n=FOX_WIDTH, w_off=c_k, tm=ROW_TILE, tn=COL_TILE,
                out_dtype=_BF16, name="proj_v")
    y_fox = _fox_attention(qk, kb, qb, b_start, b_end, qk_bound, v, FOX_TILE)

    lxlg = _matmul(h, w_in_t, w_t=True, n=2 * LRU_WIDTH, w_off=c_f, tm=ROW_TILE, tn=COL_TILE,
                   out_dtype=_F32, name="proj_lru")
    w_gate_lru = jnp.concatenate([w_rg_a, w_rg_x], axis=-1).astype(_BF16)
    b_gate_lru = jnp.concatenate([b_rg_a, b_rg_x], axis=-1).reshape(LRU_BLOCKS, 1, -1)
    y_lru = _lru_branch(lxlg, w_lru_conv, b_lru_conv, w_gate_lru, b_gate_lru, lru_lambda,
                        SEQ_TILE)

    gain_mq = (jnp.tile(g_q_mem, MEM_HEADS) * (MEM_HEAD_DIM ** -0.5)).reshape(1, -1)
    mq = _matmul(h, w_in_t, w_t=True, n=MEM_WIDTH, w_off=c_l, tm=ROW_TILE, tn=COL_TILE,
                 out_dtype=_BF16, epilogue="gnorm", extra=gain_mq, group=MEM_HEAD_DIM,
                 name="proj_mq")
    mk, mv = _mem_kv(mem, g_mem, w_mem_kv, g_k_mem)
    y_mem = _mem_attention(mq, mk, mv, ROW_TILE)

    merged = _gated_merge(h, y_fox, y_lru, y_mem, w_in_t, c_m, w_branch, b_gate,
                          ROW_TILE, MERGE_COL_TILE)
    x2, h2 = _out_proj_norm(merged, w_out.astype(_BF16), x, g_ffn, SEQ_TILE)

    g = _ffn_up(h2, w_ffn_up, w_ffn_conv, b_ffn_conv, ROW_TILE, FFN_COL_TILE)
    return _matmul(g, w_ffn_down.astype(_BF16), n=D_MODEL, tm=SEQ_TILE, tn=D_MODEL,
                   rows_outer=True,
                   out_dtype=_F32, epilogue="residual", extra=x2, name="ffn_down")


def kernel(x, mem, g_mix, w_in, b_f, g_q_fox, g_k_fox, w_lru_conv, b_lru_conv, w_rg_a, b_rg_a,
           w_rg_x, b_rg_x, lru_lambda, g_mem, w_mem_kv, g_q_mem, g_k_mem, b_gate, w_branch,
           w_out, g_ffn, w_ffn_up, w_ffn_conv, b_ffn_conv, w_ffn_down):
    depth = g_mix.shape[0]
    outs = []
    for b in range(x.shape[0]):
        xb = x[b]
        for l in range(depth):
            xb = _layer(xb, mem[b], g_mix[l], w_in[l], b_f[l], g_q_fox[l], g_k_fox[l],
                        w_lru_conv[l], b_lru_conv[l], w_rg_a[l], b_rg_a[l], w_rg_x[l],
                        b_rg_x[l], lru_lambda[l], g_mem[l], w_mem_kv[l], g_q_mem[l],
                        g_k_mem[l], b_gate[l], w_branch[l], w_out[l], g_ffn[l], w_ffn_up[l],
                        w_ffn_conv[l], b_ffn_conv[l], w_ffn_down[l])
        outs.append(xb)
    return outs[0][None] if len(outs) == 1 else jnp.stack(outs)
```

```python
import functools

import jax
import jax.numpy as jnp
import numpy as np
from jax import lax
from jax.experimental import pallas as pl
from jax.experimental.pallas import tpu as pltpu

D_MODEL = 2048
FOX_HEADS = 8
FOX_HEAD_DIM = 128
FOX_WIDTH = FOX_HEADS * FOX_HEAD_DIM
LRU_WIDTH = 1024
LRU_BLOCKS = 8
LRU_BLOCK_DIM = LRU_WIDTH // LRU_BLOCKS
LRU_CONV = 4
LRU_C = 8.0
MEM_HEADS = 4
MEM_HEAD_DIM = 256
MEM_WIDTH = MEM_HEADS * MEM_HEAD_DIM
N_BRANCH = 3
FFN_HIDDEN = 5632
FFN_CONV = 3
EPS = 1e-6

V7X_SUBLANES = 8
V7X_LANES = 128
MASK_VALUE = -1e30
LOG2_E = 1.4426950408889634
ROW_TILE = 1024
COL_TILE = 1024
FFN_COL_TILE = 512
MERGE_COL_TILE = 256
SEQ_TILE = 512
FOX_TILE = 512
FOX_SKIP_BITS = 64.0
FOX_BIAS_LANES = 16
FOX_FAST_MAX_LOGIT = 48.0
LRU_SCAN_UNROLL = 8

_BF16 = jnp.bfloat16
_F32 = jnp.float32


def _params(semantics, vmem_mib):
    return pltpu.CompilerParams(dimension_semantics=semantics,
                                vmem_limit_bytes=vmem_mib * 1024 * 1024)


def _dot_nt(a, w_t):
    return lax.dot_general(a, w_t, (((1,), (1,)), ((), ())), preferred_element_type=_F32)


def _mm_kernel(*refs, epilogue, group, cast_w, w_t):
    a_ref, w_ref = refs[0], refs[1]
    if cast_w:
        o_ref, wb_ref = refs[-2], refs[-1]

        @pl.when(pl.program_id(1) == 0)
        def _():
            wb_ref[...] = w_ref[...].astype(_BF16)

        w = wb_ref[...]
    else:
        o_ref = refs[-1]
        w = w_ref[...]
    if w_t:
        acc = _dot_nt(a_ref[...], w)
    else:
        acc = lax.dot_general(a_ref[...], w, (((1,), (0,)), ((), ())),
                              preferred_element_type=_F32)
    if epilogue == "gnorm":
        g_ref = refs[2]
        for gi in range(acc.shape[1] // group):
            sl = slice(gi * group, (gi + 1) * group)
            blk = acc[:, sl]
            ms = jnp.mean(blk * blk, axis=-1, keepdims=True)
            o_ref[:, sl] = (blk * lax.rsqrt(ms + EPS) * g_ref[:, sl]).astype(o_ref.dtype)
    elif epilogue == "residual":
        o_ref[...] = (refs[2][...] + acc).astype(o_ref.dtype)
    else:
        o_ref[...] = acc.astype(o_ref.dtype)


def _matmul(a, w, *, n, tm, tn, out_dtype, w_t=False, w_off=0, rows_outer=False,
            epilogue="plain", extra=None, group=None, name):
    m, k = a.shape
    cast_w = w.dtype != _BF16 and not rows_outer
    assert n % tn == 0 and m % tm == 0
    if rows_outer:
        grid = (m // tm, n // tn)
        ij = lambda i, j: (i, j)
    else:
        grid = (n // tn, m // tm)
        ij = lambda j, i: (i, j)
    if w_t:
        assert w_off % V7X_SUBLANES == 0
        w_block = (tn, k)
        w_spec = pl.BlockSpec((pl.Element(tn), pl.Element(k)),
                              lambda *g: (pl.multiple_of(w_off + ij(*g)[1] * tn, V7X_SUBLANES), 0))
    else:
        assert w_off % tn == 0
        w_block = (k, tn)
        w_mode = {"pipeline_mode": pl.Buffered(1)} if n == tn else {}
        w_spec = pl.BlockSpec(w_block, lambda *g: (0, w_off // tn + ij(*g)[1]), **w_mode)
    in_specs = [pl.BlockSpec((tm, k), lambda *g: (ij(*g)[0], 0)), w_spec]
    args = [a, w]
    block_bytes = (tm * k * 2 + k * tn * w.dtype.itemsize
                   + tm * tn * jnp.dtype(out_dtype).itemsize)
    if epilogue == "gnorm":
        in_specs.append(pl.BlockSpec((1, tn), lambda *g: (0, ij(*g)[1])))
        args.append(extra)
    elif epilogue == "residual":
        in_specs.append(pl.BlockSpec((tm, tn), lambda *g: ij(*g)))
        args.append(extra)
        block_bytes += tm * tn * extra.dtype.itemsize
    scratch = [pltpu.VMEM(w_block, _BF16)] if cast_w else []
    w_single = (not w_t and n == tn) * k * tn * w.dtype.itemsize
    vmem_mib = -(-(2 * block_bytes - w_single + cast_w * k * tn * 2 + 2 * tm * tn * 4)
                 // 2 ** 20) + 2
    return pl.pallas_call(
        functools.partial(_mm_kernel, epilogue=epilogue, group=group, cast_w=cast_w, w_t=w_t),
        out_shape=jax.ShapeDtypeStruct((m, n), out_dtype),
        grid=grid,
        in_specs=in_specs,
        out_specs=pl.BlockSpec((tm, tn), lambda *g: ij(*g)),
        scratch_shapes=scratch,
        compiler_params=_params(("parallel", "arbitrary"), vmem_mib),
        name=name,
    )(*args)


def _split3(x):
    hi = x.astype(_BF16)
    r1 = x - hi.astype(_F32)
    mid = r1.astype(_BF16)
    lo = (r1 - mid.astype(_F32)).astype(_BF16)
    return hi, mid, lo


def _forget_bias_kernel(shift_ref, x_ref, g_ref, w_ref, b_ref, sel_ref, ones_ref, h_ref, kb_ref,
                        qb_ref, edge_ref, carry_ref):
    @pl.when(pl.program_id(0) == 0)
    def _():
        carry_ref[...] = jnp.zeros_like(carry_ref)

    tc = x_ref.shape[0]
    x = x_ref[...]
    ms = jnp.mean(x * x, axis=-1, keepdims=True)
    h = (x * lax.rsqrt(ms + EPS) * g_ref[...]).astype(_BF16)
    h_ref[...] = h
    w = jnp.concatenate([w_ref[...], jnp.zeros((V7X_LANES - FOX_HEADS, w_ref.shape[1]), _F32)],
                        axis=0).astype(_BF16)
    z = _dot_nt(h, w) + b_ref[...]
    neg_log_f = (jnp.log1p(jnp.exp(-jnp.abs(z))) - jnp.minimum(z, 0.0)) * LOG2_E
    row = lax.broadcasted_iota(jnp.int32, (tc, tc), 0)
    col = lax.broadcasted_iota(jnp.int32, (tc, tc), 1)
    tri = jnp.where(col <= row, 1.0, 0.0).astype(_BF16)
    c = carry_ref[0:1, :]
    for part in _split3(neg_log_f):
        c = c + jnp.dot(tri, part, preferred_element_type=_F32)
    carry_ref[...] = jnp.broadcast_to(c[tc - 1:tc, :], carry_ref.shape)
    edge_ref[...] = jnp.concatenate(
        [c[0:1, :], c[tc - 1:tc, :], jnp.zeros((V7X_SUBLANES - 2, c.shape[1]), _F32)], axis=0)
    parts = jnp.concatenate(_split3(c) + _split3(-(c + shift_ref[0])), axis=1)
    routed = jnp.dot(parts, sel_ref[...], preferred_element_type=_F32) + ones_ref[...]
    half = kb_ref.shape[1]
    kb_ref[...] = routed[:, :half].astype(kb_ref.dtype)
    qb_ref[...] = routed[:, half:].astype(qb_ref.dtype)


def _norm_forget_bias(x, g, w_t, w_row0, b_pad, shift, tc):
    s, d = x.shape
    lanes = V7X_LANES
    width = lanes
    nt = s // tc
    sel = np.zeros((6 * lanes, 2 * width), np.float32)
    ones = np.zeros((1, 2 * width), np.float32)
    for head in range(FOX_HEADS):
        for part in range(3):
            sel[part * lanes + head, FOX_BIAS_LANES * head + part] = 1.0
            sel[(3 + part) * lanes + head, width + FOX_BIAS_LANES * head + 3 + part] = 1.0
            ones[0, FOX_BIAS_LANES * head + 3 + part] = 1.0
            ones[0, width + FOX_BIAS_LANES * head + part] = 1.0
    h, kb, qb, edges = pl.pallas_call(
        _forget_bias_kernel,
        out_shape=(jax.ShapeDtypeStruct((s, d), _BF16),
                   jax.ShapeDtypeStruct((s, width), _BF16),
                   jax.ShapeDtypeStruct((s, width), _BF16),
                   jax.ShapeDtypeStruct((nt * V7X_SUBLANES, lanes), _F32)),
        grid=(nt,),
        in_specs=[pl.BlockSpec(memory_space=pltpu.SMEM),
                  pl.BlockSpec((tc, d), lambda i: (i, 0)),
                  pl.BlockSpec((1, d), lambda i: (0, 0)),
                  pl.BlockSpec((pl.Element(FOX_HEADS), pl.Element(d)), lambda i: (w_row0, 0)),
                  pl.BlockSpec((1, lanes), lambda i: (0, 0)),
                  pl.BlockSpec((6 * lanes, 2 * width), lambda i: (0, 0)),
                  pl.BlockSpec((1, 2 * width), lambda i: (0, 0))],
        out_specs=(pl.BlockSpec((tc, d), lambda i: (i, 0)),
                   pl.BlockSpec((tc, width), lambda i: (i, 0)),
                   pl.BlockSpec((tc, width), lambda i: (i, 0)),
                   pl.BlockSpec((V7X_SUBLANES, lanes), lambda i: (i, 0))),
        scratch_shapes=[pltpu.VMEM((V7X_SUBLANES, lanes), _F32)],
        compiler_params=_params(("arbitrary",), 40),
        name="norm_forget_bias",
    )(shift.reshape(1).astype(_F32), x, g.reshape(1, d), w_t, b_pad, jnp.asarray(sel, _BF16),
      jnp.asarray(ones))
    edges = edges.reshape(nt, V7X_SUBLANES, lanes)
    return h, kb, qb, edges[:, 0, :FOX_HEADS], edges[:, 1, :FOX_HEADS]


def _unpack_bias(packed, sel_ref):
    return jnp.dot(packed, sel_ref[...], preferred_element_type=_F32).astype(_BF16)


def _fox_kernel(bstart_ref, bend_ref, thr_ref, q_ref, k_ref, kbp_ref, sel_ref, v_ref, o_ref,
                acc_ref, kb_ref, *, tq):
    head = pl.program_id(0)
    qi = pl.program_id(1)
    b_tile = bstart_ref[qi, head]
    thr = thr_ref[0]
    j0 = lax.fori_loop(
        0, qi, lambda j, n: n + jnp.where(b_tile - bend_ref[j, head] >= thr, 1, 0), 0)

    @pl.when(qi == 0)
    def _():
        kb_ref[...] = _unpack_bias(kbp_ref[...], sel_ref)

    lane = lax.broadcasted_iota(jnp.int32, (tq, V7X_LANES), 1)
    ones3 = jnp.where(lane < 3, 1.0, 0.0).astype(_BF16)
    q_aug = jnp.concatenate([q_ref[...], ones3], axis=1)
    acc_ref[...] = jnp.zeros_like(acc_ref)

    def logits(j):
        ks = pl.multiple_of(j * tq, tq)
        k_aug = jnp.concatenate([k_ref[pl.ds(ks, tq), :], kb_ref[pl.ds(ks, tq), :]], axis=1)
        return lax.dot_general(k_aug, q_aug, (((1,), (1,)), ((), ())),
                               preferred_element_type=_F32)

    def accumulate(j, s, m_prev, l_prev):
        m_new = jnp.maximum(m_prev, jnp.max(s, axis=0, keepdims=True))
        alpha = jnp.exp2(m_prev - m_new)
        p = jnp.exp2(s - m_new)
        l_new = alpha * l_prev + jnp.sum(p, axis=0, keepdims=True)
        ks = pl.multiple_of(j * tq, tq)
        pv = lax.dot_general(v_ref[pl.ds(ks, tq), :], p.astype(_BF16),
                             (((0,), (0,)), ((), ())), preferred_element_type=_F32)
        acc_ref[...] = alpha * acc_ref[...] + pv
        return m_new, l_new

    def body(j, carry):
        m_prev, l_prev, s = carry
        s_next = logits(j + 1)
        m_new, l_new = accumulate(j, s, m_prev, l_prev)
        return m_new, l_new, s_next

    m0 = jnp.full((1, tq), MASK_VALUE, _F32)
    l0 = jnp.zeros((1, tq), _F32)
    m, l, s = lax.fori_loop(j0, qi, body, (m0, l0, logits(j0)))
    row = lax.broadcasted_iota(jnp.int32, s.shape, 0)
    col = lax.broadcasted_iota(jnp.int32, s.shape, 1)
    m, l = accumulate(qi, jnp.where(row <= col, s, MASK_VALUE), m, l)
    o_ref[...] = (acc_ref[...] / l).T.astype(o_ref.dtype)


def _fox_fast_kernel(bstart_ref, bend_ref, thr_ref, q_ref, qbp_ref, k_ref, kbp_ref, sel_ref, v_ref,
                     o_ref, acc_ref, kb_ref, l_ref, s0_ref, s1_ref, *, tq):
    head = pl.program_id(0)
    qi = pl.program_id(1)
    b_tile = bstart_ref[qi, head]
    thr = thr_ref[0]
    j0 = lax.fori_loop(
        0, qi, lambda j, n: n + jnp.where(b_tile - bend_ref[j, head] >= thr, 1, 0), 0)

    @pl.when(qi == 0)
    def _():
        kb_ref[...] = _unpack_bias(kbp_ref[...], sel_ref)

    q_aug = jnp.concatenate([q_ref[...], _unpack_bias(qbp_ref[...], sel_ref)], axis=1)
    acc_ref[...] = jnp.zeros_like(acc_ref)

    def logits(j):
        ks = pl.multiple_of(j * tq, tq)
        k_aug = jnp.concatenate([k_ref[pl.ds(ks, tq), :], kb_ref[pl.ds(ks, tq), :]], axis=1)
        return _dot_nt(k_aug, q_aug)

    def accumulate(j, s):
        p = jnp.exp2(s)
        ks = pl.multiple_of(j * tq, tq)
        acc_ref[...] += lax.dot_general(v_ref[pl.ds(ks, tq), :], p.astype(_BF16),
                                        (((0,), (0,)), ((), ())), preferred_element_type=_F32)
        l_ref[...] += jnp.sum(p, axis=0, keepdims=True)

    l_ref[...] = jnp.zeros_like(l_ref)
    row = lax.broadcasted_iota(jnp.int32, s1_ref.shape, 0)
    col = lax.broadcasted_iota(jnp.int32, s1_ref.shape, 1)
    s_diag = jnp.where(row <= col, logits(qi), MASK_VALUE)
    s0_ref[...] = logits(j0)
    accumulate(qi, s_diag)
    n_before = qi - j0

    def pair(t, carry):
        j = j0 + 2 * t
        s1_ref[...] = logits(j + 1)
        accumulate(j, s0_ref[...])
        s0_ref[...] = logits(j + 2)
        accumulate(j + 1, s1_ref[...])
        return carry

    lax.fori_loop(0, n_before // 2, pair, 0)

    @pl.when(n_before % 2 == 1)
    def _():
        accumulate(qi - 1, s0_ref[...])

    o_ref[...] = (acc_ref[...] / l_ref[...]).T.astype(o_ref.dtype)


def _fox_attention(qk, kb, qb, b_start, b_end, qk_bound, v, tq):
    s = v.shape[0]
    hd = FOX_HEAD_DIM
    lanes = V7X_LANES
    thr = (2.0 * qk_bound + FOX_SKIP_BITS).reshape(1).astype(_F32)
    sel = np.zeros((FOX_HEADS, lanes, lanes), np.float32)
    for head in range(FOX_HEADS):
        for c in range(FOX_BIAS_LANES):
            sel[head, FOX_BIAS_LANES * head + c, c] = 1.0
    sel = jnp.asarray(sel, _BF16)
    smem = pl.BlockSpec(memory_space=pltpu.SMEM)
    q_spec = pl.BlockSpec((tq, hd), lambda h, i: (i, h))
    k_spec = pl.BlockSpec((s, hd), lambda h, i: (0, FOX_HEADS + h))
    v_spec = pl.BlockSpec((s, hd), lambda h, i: (0, h))
    kb_spec = pl.BlockSpec((s, lanes), lambda h, i: (0, 0))
    qb_spec = pl.BlockSpec((tq, lanes), lambda h, i: (i, 0))
    sel_spec = pl.BlockSpec((None, lanes, lanes), lambda h, i: (h, 0, 0))

    def call(kernel, in_specs, extra_scratch, *args):
        return pl.pallas_call(
            functools.partial(kernel, tq=tq),
            out_shape=jax.ShapeDtypeStruct((s, FOX_WIDTH), _BF16),
            grid=(FOX_HEADS, s // tq),
            in_specs=[smem, smem, smem] + in_specs,
            out_specs=q_spec,
            scratch_shapes=[pltpu.VMEM((hd, tq), _F32),
                            pltpu.VMEM((s, lanes), _BF16)] + extra_scratch,
            compiler_params=_params(("parallel", "arbitrary"), 32),
            name=kernel.__name__.strip("_"),
        )(b_start, b_end, thr, *args)

    fast_scratch = [pltpu.VMEM((1, tq), _F32), pltpu.VMEM((tq, tq), _F32),
                    pltpu.VMEM((tq, tq), _F32)]
    return lax.cond(
        qk_bound <= FOX_FAST_MAX_LOGIT,
        lambda: call(_fox_fast_kernel, [q_spec, qb_spec, k_spec, kb_spec, sel_spec, v_spec],
                     fast_scratch, qk, qb, qk, kb, sel, v),
        lambda: call(_fox_kernel, [q_spec, k_spec, kb_spec, sel_spec, v_spec], [],
                     qk, qk, kb, sel, v))


def _shift_rows(ext, d):
    return pltpu.roll(ext, d, axis=0)[V7X_SUBLANES:]


def _lru_kernel(lx_ref, lg_ref, wc_ref, bc_ref, wg_ref, bg_ref, lam_ref, o_ref,
                halo_ref, carry_ref, a_ref, h_ref):
    @pl.when(pl.program_id(0) == 0)
    def _():
        halo_ref[...] = jnp.zeros_like(halo_ref)
        carry_ref[...] = jnp.zeros_like(carry_ref)

    lx = lx_ref[...]
    ts = lx.shape[0]
    ext = jnp.concatenate([halo_ref[...], lx], axis=0)
    xr = (wc_ref[3:4, :] * lx + wc_ref[2:3, :] * _shift_rows(ext, 1)
          + wc_ref[1:2, :] * _shift_rows(ext, 2) + wc_ref[0:1, :] * _shift_rows(ext, 3)
          + bc_ref[...])
    halo_ref[...] = lx[ts - V7X_SUBLANES:]

    lam = lam_ref[...]
    log_sig_lam = jnp.minimum(lam, 0.0) - jnp.log1p(jnp.exp(-jnp.abs(lam)))
    bd = LRU_BLOCK_DIM
    for nb in range(LRU_BLOCKS):
        sl = slice(nb * bd, (nb + 1) * bd)
        x_nb = xr[:, sl]
        gates = jax.nn.sigmoid(jnp.dot(x_nb.astype(_BF16), wg_ref[nb],
                                       preferred_element_type=_F32) + bg_ref[nb])
        log_a = LRU_C * gates[:, :bd] * log_sig_lam[:, sl]
        a_ref[:, sl] = jnp.exp(log_a)
        t = jnp.tanh(log_a)
        h_ref[:, sl] = jnp.sqrt(-2.0 * t / (1.0 - t)) * (gates[:, bd:] * x_nb)

    sub = lax.broadcasted_iota(jnp.int32, (V7X_SUBLANES, 1), 0)

    def group(g, carry):
        rows = pl.ds(pl.multiple_of(g * V7X_SUBLANES, V7X_SUBLANES), V7X_SUBLANES)
        a = a_ref[rows, :]
        h = h_ref[rows, :]
        for d in (1, 2, 4):
            valid = sub >= d
            h = h + a * jnp.where(valid, pltpu.roll(h, d, axis=0), 0.0)
            a = a * jnp.where(valid, pltpu.roll(a, d, axis=0), 1.0)
        h = h + a * carry
        h_ref[rows, :] = h
        return jnp.broadcast_to(h[V7X_SUBLANES - 1:, :], h.shape)

    carry_ref[...] = lax.fori_loop(0, ts // V7X_SUBLANES, group, carry_ref[...],
                                   unroll=LRU_SCAN_UNROLL)
    o_ref[...] = (h_ref[...] * jax.nn.gelu(lg_ref[...])).astype(o_ref.dtype)


def _lru_branch(lxlg, w_conv, b_conv, w_gate, b_gate, lam, ts):
    s = lxlg.shape[0]
    w = LRU_WIDTH
    bd = LRU_BLOCK_DIM
    return pl.pallas_call(
        _lru_kernel,
        out_shape=jax.ShapeDtypeStruct((s, w), _BF16),
        grid=(s // ts,),
        in_specs=[pl.BlockSpec((ts, w), lambda i: (i, 0)),
                  pl.BlockSpec((ts, w), lambda i: (i, 1)),
                  pl.BlockSpec((LRU_CONV, w), lambda i: (0, 0)),
                  pl.BlockSpec((1, w), lambda i: (0, 0)),
                  pl.BlockSpec((LRU_BLOCKS, bd, 2 * bd), lambda i: (0, 0, 0)),
                  pl.BlockSpec((LRU_BLOCKS, 1, 2 * bd), lambda i: (0, 0, 0)),
                  pl.BlockSpec((1, w), lambda i: (0, 0))],
        out_specs=pl.BlockSpec((ts, w), lambda i: (i, 0)),
        scratch_shapes=[pltpu.VMEM((V7X_SUBLANES, w), _F32), pltpu.VMEM((V7X_SUBLANES, w), _F32),
                        pltpu.VMEM((ts, w), _F32), pltpu.VMEM((ts, w), _F32)],
        compiler_params=_params(("arbitrary",), 40),
        name="conv_rglru",
    )(lxlg, lxlg, w_conv, b_conv.reshape(1, -1), w_gate, b_gate, lam.reshape(1, -1))


def _mem_kv_kernel(mem_ref, g_ref, w_ref, gk_ref, k_ref, v_ref):
    x = mem_ref[...]
    ms = jnp.mean(x * x, axis=-1, keepdims=True)
    hm = (x * lax.rsqrt(ms + EPS) * g_ref[...]).astype(_BF16)
    acc = jnp.dot(hm, w_ref[...].astype(_BF16), preferred_element_type=_F32)

    @pl.when(pl.program_id(0) == 0)
    def _():
        for gi in range(MEM_HEADS):
            sl = slice(gi * MEM_HEAD_DIM, (gi + 1) * MEM_HEAD_DIM)
            blk = acc[:, sl]
            ms_k = jnp.mean(blk * blk, axis=-1, keepdims=True)
            k_ref[:, sl] = (blk * lax.rsqrt(ms_k + EPS) * gk_ref[:, sl]).astype(k_ref.dtype)

    @pl.when(pl.program_id(0) == 1)
    def _():
        v_ref[...] = acc.astype(v_ref.dtype)


def _mem_kv(mem, g_mem, w_kv, g_k):
    m, d = mem.shape
    w = MEM_WIDTH
    const = lambda j: (0, 0)
    return pl.pallas_call(
        _mem_kv_kernel,
        out_shape=(jax.ShapeDtypeStruct((m, w), _BF16), jax.ShapeDtypeStruct((m, w), _BF16)),
        grid=(2,),
        in_specs=[pl.BlockSpec((m, d), const), pl.BlockSpec((1, d), const),
                  pl.BlockSpec((d, w), lambda j: (0, j)), pl.BlockSpec((1, w), const)],
        out_specs=(pl.BlockSpec((m, w), const), pl.BlockSpec((m, w), const)),
        compiler_params=_params(("arbitrary",), 40),
        name="mem_kv",
    )(mem, g_mem.reshape(1, d), w_kv, jnp.tile(g_k, MEM_HEADS).reshape(1, w))


def _mem_attn_kernel(q_ref, k_ref, v_ref, o_ref):
    s = lax.dot_general(q_ref[...], k_ref[...], (((1,), (1,)), ((), ())),
                        preferred_element_type=_F32)
    m = jnp.max(s, axis=-1, keepdims=True)
    p = jnp.exp(s - m)
    l = jnp.sum(p, axis=-1, keepdims=True)
    acc = jnp.dot(p.astype(_BF16), v_ref[...], preferred_element_type=_F32)
    o_ref[...] = (acc / l).astype(o_ref.dtype)


def _mem_attention(q, k, v, ts):
    s = q.shape[0]
    m = k.shape[0]
    hd = MEM_HEAD_DIM
    return pl.pallas_call(
        _mem_attn_kernel,
        out_shape=jax.ShapeDtypeStruct((s, MEM_WIDTH), _BF16),
        grid=(s // ts, MEM_HEADS),
        in_specs=[pl.BlockSpec((ts, hd), lambda i, h: (i, h)),
                  pl.BlockSpec((m, hd), lambda i, h: (0, h)),
                  pl.BlockSpec((m, hd), lambda i, h: (0, h))],
        out_specs=pl.BlockSpec((ts, hd), lambda i, h: (i, h)),
        compiler_params=_params(("parallel", "arbitrary"), 32),
        name="mem_attention",
    )(q, k, v)


def _merge_kernel(h_ref, yf_ref, yl_ref, ym_ref, wg0_ref, wg1_ref, wg2_ref, wb_ref, bg_ref,
                  o_ref):
    h = h_ref[...]
    merged = None
    for n, (y_ref, wg_ref) in enumerate(((yf_ref, wg0_ref), (yl_ref, wg1_ref),
                                         (ym_ref, wg2_ref))):
        gate = jax.nn.sigmoid(_dot_nt(h, wg_ref[...]) + bg_ref[n:n + 1, :])
        term = gate * lax.dot_general(y_ref[...], wb_ref[n], (((1,), (0,)), ((), ())),
                                      preferred_element_type=_F32)
        merged = term if merged is None else merged + term
    o_ref[...] = merged.astype(o_ref.dtype)


def _gated_merge(h, y_fox, y_lru, y_mem, w_t, gate_row0, w_branch, b_gate, tm, tn):
    s, d = h.shape
    nj = d // tn
    bw = y_fox.shape[1]
    y_spec = pl.BlockSpec((tm, bw), lambda i, j: (i, 0))
    assert gate_row0 % V7X_SUBLANES == 0

    def gate_spec(n):
        return pl.BlockSpec(
            (pl.Element(tn), pl.Element(d)),
            lambda i, j: (pl.multiple_of(gate_row0 + n * d + j * tn, V7X_SUBLANES), 0))

    return pl.pallas_call(
        _merge_kernel,
        out_shape=jax.ShapeDtypeStruct((s, d), _BF16),
        grid=(s // tm, nj),
        in_specs=[pl.BlockSpec((tm, d), lambda i, j: (i, 0)), y_spec, y_spec, y_spec,
                  gate_spec(0), gate_spec(1), gate_spec(2),
                  pl.BlockSpec((N_BRANCH, bw, tn), lambda i, j: (0, 0, j)),
                  pl.BlockSpec((N_BRANCH, tn), lambda i, j: (0, j))],
        out_specs=pl.BlockSpec((tm, tn), lambda i, j: (i, j)),
        compiler_params=_params(("parallel", "arbitrary"), 56),
        name="gated_merge",
    )(h, y_fox, y_lru, y_mem, w_t, w_t, w_t, w_branch, b_gate)


def _out_norm_kernel(a_ref, w_ref, x_ref, g_ref, x2_ref, h2_ref):
    x2 = x_ref[...] + jnp.dot(a_ref[...], w_ref[...], preferred_element_type=_F32)
    x2_ref[...] = x2
    ms = jnp.mean(x2 * x2, axis=-1, keepdims=True)
    h2_ref[...] = (x2 * lax.rsqrt(ms + EPS) * g_ref[...]).astype(h2_ref.dtype)


def _out_proj_norm(a, w, x, g, tm):
    m, k = a.shape
    d = w.shape[1]
    row = lambda i: (i, 0)
    return pl.pallas_call(
        _out_norm_kernel,
        out_shape=(jax.ShapeDtypeStruct((m, d), _F32), jax.ShapeDtypeStruct((m, d), _BF16)),
        grid=(m // tm,),
        in_specs=[pl.BlockSpec((tm, k), row), pl.BlockSpec((k, d), lambda i: (0, 0)),
                  pl.BlockSpec((tm, d), row), pl.BlockSpec((1, d), lambda i: (0, 0))],
        out_specs=(pl.BlockSpec((tm, d), row), pl.BlockSpec((tm, d), row)),
        compiler_params=_params(("parallel",), 48),
        name="proj_out_norm",
    )(a, w, x, g.reshape(1, d))


def _ffn_up_kernel(a_ref, wa_ref, wv_ref, wca_ref, wcv_ref, bca_ref, bcv_ref, wd_ref, o_ref,
                   wd_bf_ref, halo_a_ref, halo_v_ref, wa_bf_ref, wv_bf_ref):
    @pl.when(pl.program_id(1) == 0)
    def _():
        halo_a_ref[...] = jnp.zeros_like(halo_a_ref)
        halo_v_ref[...] = jnp.zeros_like(halo_v_ref)
        wa_bf_ref[...] = wa_ref[...].astype(_BF16)
        wv_bf_ref[...] = wv_ref[...].astype(_BF16)
        wd_bf_ref[...] = wd_ref[...].astype(_BF16)

    a = a_ref[...]
    tm = a.shape[0]

    def conv(up, halo_ref, wc_ref, bc_ref):
        ext = jnp.concatenate([halo_ref[...], up], axis=0)
        halo_ref[...] = up[tm - V7X_SUBLANES:]
        return (wc_ref[2:3, :] * up + wc_ref[1:2, :] * _shift_rows(ext, 1)
                + wc_ref[0:1, :] * _shift_rows(ext, 2) + bc_ref[...])

    act = conv(jnp.dot(a, wa_bf_ref[...], preferred_element_type=_F32), halo_a_ref, wca_ref,
               bca_ref)
    val = conv(jnp.dot(a, wv_bf_ref[...], preferred_element_type=_F32), halo_v_ref, wcv_ref,
               bcv_ref)
    o_ref[...] = (jax.nn.gelu(act) * val).astype(o_ref.dtype)


def _ffn_up(h2, w_up, w_conv, b_conv, w_down, tm, tn):
    s, d = h2.shape
    f = FFN_HIDDEN
    nj = f // tn
    dd = w_down.shape[1]
    return pl.pallas_call(
        _ffn_up_kernel,
        out_shape=(jax.ShapeDtypeStruct((s, f), _BF16), jax.ShapeDtypeStruct((f, dd), _BF16)),
        grid=(nj, s // tm),
        in_specs=[pl.BlockSpec((tm, d), lambda j, i: (i, 0)),
                  pl.BlockSpec((d, tn), lambda j, i: (0, j)),
                  pl.BlockSpec((d, tn), lambda j, i: (0, nj + j)),
                  pl.BlockSpec((FFN_CONV, tn), lambda j, i: (0, j)),
                  pl.BlockSpec((FFN_CONV, tn), lambda j, i: (0, nj + j)),
                  pl.BlockSpec((1, tn), lambda j, i: (0, j)),
                  pl.BlockSpec((1, tn), lambda j, i: (0, nj + j)),
                  pl.BlockSpec((tn, dd), lambda j, i: (j, 0))],
        out_specs=(pl.BlockSpec((tm, tn), lambda j, i: (i, j)),
                   pl.BlockSpec((tn, dd), lambda j, i: (j, 0))),
        scratch_shapes=[pltpu.VMEM((V7X_SUBLANES, tn), _F32),
                        pltpu.VMEM((V7X_SUBLANES, tn), _F32),
                        pltpu.VMEM((d, tn), _BF16), pltpu.VMEM((d, tn), _BF16)],
        compiler_params=_params(("parallel", "arbitrary"), 56),
        name="ffn_up_conv_geglu",
    )(h2, w_up, w_up, w_conv, w_conv, b_conv.reshape(1, -1), b_conv.reshape(1, -1), w_down)


def _layer(x, mem, g_mix, w_in, b_f, g_q_fox, g_k_fox, w_lru_conv, b_lru_conv, w_rg_a, b_rg_a,
           w_rg_x, b_rg_x, lru_lambda, g_mem, w_mem_kv, g_q_mem, g_k_mem, b_gate, w_branch,
           w_out, g_ffn, w_ffn_up, w_ffn_conv, b_ffn_conv, w_ffn_down):
    c_k = 2 * FOX_WIDTH
    c_v = c_k + FOX_WIDTH
    c_f = c_v + FOX_HEADS
    c_l = c_f + 2 * LRU_WIDTH
    c_m = c_l + MEM_WIDTH

    w_in_t = w_in.T

    g_q_scaled = g_q_fox * (LOG2_E * FOX_HEAD_DIM ** -0.5)
    qk_bound = (1.02 * FOX_HEAD_DIM) * jnp.max(jnp.abs(g_q_scaled)) * jnp.max(jnp.abs(g_k_fox))
    b_pad = jnp.pad(b_f.reshape(1, -1), ((0, 0), (0, V7X_LANES - FOX_HEADS)))
    h, kb, qb, b_start, b_end = _norm_forget_bias(x, g_mix, w_in_t, c_v, b_pad, qk_bound,
                                                  FOX_TILE)
    gain_qk = jnp.concatenate([jnp.tile(g_q_scaled, FOX_HEADS),
                               jnp.tile(g_k_fox, FOX_HEADS)]).reshape(1, -1)
    qk = _matmul(h, w_in_t, w_t=True, n=c_k, tm=ROW_TILE, tn=COL_TILE, out_dtype=_BF16,
                 epilogue="gnorm", extra=gain_qk, group=FOX_HEAD_DIM, name="proj_qk")
    v = _matmul(h, w_in_t, w_t=True, n=FOX_WIDTH, w_off=c_k, tm=ROW_TILE, tn=COL_TILE,
                out_dtype=_BF16, name="proj_v")
    y_fox = _fox_attention(qk, kb, qb, b_start, b_end, qk_bound, v, FOX_TILE)

    lxlg = _matmul(h, w_in_t, w_t=True, n=2 * LRU_WIDTH, w_off=c_f, tm=ROW_TILE, tn=COL_TILE,
                   out_dtype=_F32, name="proj_lru")
    w_gate_lru = jnp.concatenate([w_rg_a, w_rg_x], axis=-1).astype(_BF16)
    b_gate_lru = jnp.concatenate([b_rg_a, b_rg_x], axis=-1).reshape(LRU_BLOCKS, 1, -1)
    y_lru = _lru_branch(lxlg, w_lru_conv, b_lru_conv, w_gate_lru, b_gate_lru, lru_lambda,
                        SEQ_TILE)

    gain_mq = (jnp.tile(g_q_mem, MEM_HEADS) * (MEM_HEAD_DIM ** -0.5)).reshape(1, -1)
    mq = _matmul(h, w_in_t, w_t=True, n=MEM_WIDTH, w_off=c_l, tm=ROW_TILE, tn=COL_TILE,
                 out_dtype=_BF16, epilogue="gnorm", extra=gain_mq, group=MEM_HEAD_DIM,
                 name="proj_mq")
    mk, mv = _mem_kv(mem, g_mem, w_mem_kv, g_k_mem)
    y_mem = _mem_attention(mq, mk, mv, ROW_TILE)

    merged = _gated_merge(h, y_fox, y_lru, y_mem, w_in_t, c_m, w_branch, b_gate,
                          ROW_TILE, MERGE_COL_TILE)
    x2, h2 = _out_proj_norm(merged, w_out.astype(_BF16), x, g_ffn, SEQ_TILE)

    g, w_down_bf = _ffn_up(h2, w_ffn_up, w_ffn_conv, b_ffn_conv, w_ffn_down, ROW_TILE,
                           FFN_COL_TILE)
    return _matmul(g, w_down_bf, n=D_MODEL, tm=SEQ_TILE, tn=D_MODEL,
                   rows_outer=True,
                   out_dtype=_F32, epilogue="residual", extra=x2, name="ffn_down")


def kernel(x, mem, g_mix, w_in, b_f, g_q_fox, g_k_fox, w_lru_conv, b_lru_conv, w_rg_a, b_rg_a,
           w_rg_x, b_rg_x, lru_lambda, g_mem, w_mem_kv, g_q_mem, g_k_mem, b_gate, w_branch,
           w_out, g_ffn, w_ffn_up, w_ffn_conv, b_ffn_conv, w_ffn_down):
    depth = g_mix.shape[0]
    outs = []
    for b in range(x.shape[0]):
        xb = x[b]
        for l in range(depth):
            xb = _layer(xb, mem[b], g_mix[l], w_in[l], b_f[l], g_q_fox[l], g_k_fox[l],
                        w_lru_conv[l], b_lru_conv[l], w_rg_a[l], b_rg_a[l], w_rg_x[l],
                        b_rg_x[l], lru_lambda[l], g_mem[l], w_mem_kv[l], g_q_mem[l],
                        g_k_mem[l], b_gate[l], w_branch[l], w_out[l], g_ffn[l], w_ffn_up[l],
                        w_ffn_conv[l], b_ffn_conv[l], w_ffn_down[l])
        outs.append(xb)
    return outs[0][None] if len(outs) == 1 else jnp.stack(outs)
```

```python
import functools

import jax
import jax.numpy as jnp
import numpy as np
from jax import lax
from jax.experimental import pallas as pl
from jax.experimental.pallas import tpu as pltpu

D_MODEL = 2048
FOX_HEADS = 8
FOX_HEAD_DIM = 128
FOX_WIDTH = FOX_HEADS * FOX_HEAD_DIM
LRU_WIDTH = 1024
LRU_BLOCKS = 8
LRU_BLOCK_DIM = LRU_WIDTH // LRU_BLOCKS
LRU_CONV = 4
LRU_C = 8.0
MEM_HEADS = 4
MEM_HEAD_DIM = 256
MEM_WIDTH = MEM_HEADS * MEM_HEAD_DIM
N_BRANCH = 3
FFN_HIDDEN = 5632
FFN_CONV = 3
EPS = 1e-6

V7X_SUBLANES = 8
V7X_LANES = 128
MASK_VALUE = -1e30
LOG2_E = 1.4426950408889634
ROW_TILE = 1024
COL_TILE = 1024
FFN_COL_TILE = 512
MERGE_COL_TILE = 256
SEQ_TILE = 512
FOX_TILE = 512
FOX_SKIP_BITS = 64.0
FOX_BIAS_LANES = 16
FOX_FAST_MAX_LOGIT = 48.0
LRU_SCAN_UNROLL = 8

_BF16 = jnp.bfloat16
_F32 = jnp.float32


def _params(semantics, vmem_mib):
    return pltpu.CompilerParams(dimension_semantics=semantics,
                                vmem_limit_bytes=vmem_mib * 1024 * 1024)


def _dot_nt(a, w_t):
    return lax.dot_general(a, w_t, (((1,), (1,)), ((), ())), preferred_element_type=_F32)


def _mm_kernel(*refs, epilogue, group, cast_w, w_t):
    a_ref, w_ref = refs[0], refs[1]
    if cast_w:
        o_ref, wb_ref = refs[-2], refs[-1]

        @pl.when(pl.program_id(1) == 0)
        def _():
            wb_ref[...] = w_ref[...].astype(_BF16)

        w = wb_ref[...]
    else:
        o_ref = refs[-1]
        w = w_ref[...]
    if w_t:
        acc = _dot_nt(a_ref[...], w)
    else:
        acc = lax.dot_general(a_ref[...], w, (((1,), (0,)), ((), ())),
                              preferred_element_type=_F32)
    if epilogue == "gnorm":
        g_ref = refs[2]
        for gi in range(acc.shape[1] // group):
            sl = slice(gi * group, (gi + 1) * group)
            blk = acc[:, sl]
            ms = jnp.mean(blk * blk, axis=-1, keepdims=True)
            o_ref[:, sl] = (blk * lax.rsqrt(ms + EPS) * g_ref[:, sl]).astype(o_ref.dtype)
    elif epilogue == "residual":
        o_ref[...] = (refs[2][...] + acc).astype(o_ref.dtype)
    else:
        o_ref[...] = acc.astype(o_ref.dtype)


def _matmul(a, w, *, n, tm, tn, out_dtype, w_t=False, w_off=0, rows_outer=False,
            epilogue="plain", extra=None, group=None, name):
    m, k = a.shape
    cast_w = w.dtype != _BF16 and not rows_outer
    assert n % tn == 0 and m % tm == 0
    if rows_outer:
        grid = (m // tm, n // tn)
        ij = lambda i, j: (i, j)
    else:
        grid = (n // tn, m // tm)
        ij = lambda j, i: (i, j)
    if w_t:
        assert w_off % V7X_SUBLANES == 0
        w_block = (tn, k)
        w_spec = pl.BlockSpec((pl.Element(tn), pl.Element(k)),
                              lambda *g: (pl.multiple_of(w_off + ij(*g)[1] * tn, V7X_SUBLANES), 0))
    else:
        assert w_off % tn == 0
        w_block = (k, tn)
        w_mode = {"pipeline_mode": pl.Buffered(1)} if n == tn else {}
        w_spec = pl.BlockSpec(w_block, lambda *g: (0, w_off // tn + ij(*g)[1]), **w_mode)
    in_specs = [pl.BlockSpec((tm, k), lambda *g: (ij(*g)[0], 0)), w_spec]
    args = [a, w]
    block_bytes = (tm * k * 2 + k * tn * w.dtype.itemsize
                   + tm * tn * jnp.dtype(out_dtype).itemsize)
    if epilogue == "gnorm":
        in_specs.append(pl.BlockSpec((1, tn), lambda *g: (0, ij(*g)[1])))
        args.append(extra)
    elif epilogue == "residual":
        in_specs.append(pl.BlockSpec((tm, tn), lambda *g: ij(*g)))
        args.append(extra)
        block_bytes += tm * tn * extra.dtype.itemsize
    scratch = [pltpu.VMEM(w_block, _BF16)] if cast_w else []
    w_single = (not w_t and n == tn) * k * tn * w.dtype.itemsize
    vmem_mib = -(-(2 * block_bytes - w_single + cast_w * k * tn * 2 + 2 * tm * tn * 4)
                 // 2 ** 20) + 2
    return pl.pallas_call(
        functools.partial(_mm_kernel, epilogue=epilogue, group=group, cast_w=cast_w, w_t=w_t),
        out_shape=jax.ShapeDtypeStruct((m, n), out_dtype),
        grid=grid,
        in_specs=in_specs,
        out_specs=pl.BlockSpec((tm, tn), lambda *g: ij(*g)),
        scratch_shapes=scratch,
        compiler_params=_params(("parallel", "arbitrary"), vmem_mib),
        name=name,
    )(*args)


def _split3(x):
    hi = x.astype(_BF16)
    r1 = x - hi.astype(_F32)
    mid = r1.astype(_BF16)
    lo = (r1 - mid.astype(_F32)).astype(_BF16)
    return hi, mid, lo


def _forget_bias_kernel(shift_ref, x_ref, g_ref, w_ref, b_ref, sel_ref, ones_ref, h_ref, kb_ref,
                        qb_ref, edge_ref, carry_ref):
    @pl.when(pl.program_id(0) == 0)
    def _():
        carry_ref[...] = jnp.zeros_like(carry_ref)

    tc = x_ref.shape[0]
    x = x_ref[...]
    ms = jnp.mean(x * x, axis=-1, keepdims=True)
    h = (x * lax.rsqrt(ms + EPS) * g_ref[...]).astype(_BF16)
    h_ref[...] = h
    w = jnp.concatenate([w_ref[...], jnp.zeros((V7X_LANES - FOX_HEADS, w_ref.shape[1]), _F32)],
                        axis=0).astype(_BF16)
    z = _dot_nt(h, w) + b_ref[...]
    neg_log_f = (jnp.log1p(jnp.exp(-jnp.abs(z))) - jnp.minimum(z, 0.0)) * LOG2_E
    chunk = 2 * V7X_LANES
    row = lax.broadcasted_iota(jnp.int32, (chunk, chunk), 0)
    col = lax.broadcasted_iota(jnp.int32, (chunk, chunk), 1)
    tri = jnp.where(col <= row, 1.0, 0.0).astype(_BF16)
    total = carry_ref[0:1, :]
    pieces = []
    for r0 in range(0, tc, chunk):
        c_chunk = total
        for part in _split3(neg_log_f[r0:r0 + chunk, :]):
            c_chunk = c_chunk + jnp.dot(tri, part, preferred_element_type=_F32)
        pieces.append(c_chunk)
        total = c_chunk[chunk - 1:chunk, :]
    c = jnp.concatenate(pieces, axis=0)
    carry_ref[...] = jnp.broadcast_to(c[tc - 1:tc, :], carry_ref.shape)
    edge_ref[...] = jnp.concatenate(
        [c[0:1, :], c[tc - 1:tc, :], jnp.zeros((V7X_SUBLANES - 2, c.shape[1]), _F32)], axis=0)
    parts = jnp.concatenate(_split3(c) + _split3(-(c + shift_ref[0])), axis=1)
    routed = jnp.dot(parts, sel_ref[...], preferred_element_type=_F32) + ones_ref[...]
    half = kb_ref.shape[1]
    kb_ref[...] = routed[:, :half].astype(kb_ref.dtype)
    qb_ref[...] = routed[:, half:].astype(qb_ref.dtype)


def _norm_forget_bias(x, g, w_t, w_row0, b_pad, shift, tc):
    s, d = x.shape
    lanes = V7X_LANES
    width = lanes
    nt = s // tc
    sel = np.zeros((6 * lanes, 2 * width), np.float32)
    ones = np.zeros((1, 2 * width), np.float32)
    for head in range(FOX_HEADS):
        for part in range(3):
            sel[part * lanes + head, FOX_BIAS_LANES * head + part] = 1.0
            sel[(3 + part) * lanes + head, width + FOX_BIAS_LANES * head + 3 + part] = 1.0
            ones[0, FOX_BIAS_LANES * head + 3 + part] = 1.0
            ones[0, width + FOX_BIAS_LANES * head + part] = 1.0
    h, kb, qb, edges = pl.pallas_call(
        _forget_bias_kernel,
        out_shape=(jax.ShapeDtypeStruct((s, d), _BF16),
                   jax.ShapeDtypeStruct((s, width), _BF16),
                   jax.ShapeDtypeStruct((s, width), _BF16),
                   jax.ShapeDtypeStruct((nt * V7X_SUBLANES, lanes), _F32)),
        grid=(nt,),
        in_specs=[pl.BlockSpec(memory_space=pltpu.SMEM),
                  pl.BlockSpec((tc, d), lambda i: (i, 0)),
                  pl.BlockSpec((1, d), lambda i: (0, 0)),
                  pl.BlockSpec((pl.Element(FOX_HEADS), pl.Element(d)), lambda i: (w_row0, 0)),
                  pl.BlockSpec((1, lanes), lambda i: (0, 0)),
                  pl.BlockSpec((6 * lanes, 2 * width), lambda i: (0, 0)),
                  pl.BlockSpec((1, 2 * width), lambda i: (0, 0))],
        out_specs=(pl.BlockSpec((tc, d), lambda i: (i, 0)),
                   pl.BlockSpec((tc, width), lambda i: (i, 0)),
                   pl.BlockSpec((tc, width), lambda i: (i, 0)),
                   pl.BlockSpec((V7X_SUBLANES, lanes), lambda i: (i, 0))),
        scratch_shapes=[pltpu.VMEM((V7X_SUBLANES, lanes), _F32)],
        compiler_params=_params(("arbitrary",), 40),
        name="norm_forget_bias",
    )(shift.reshape(1).astype(_F32), x, g.reshape(1, d), w_t, b_pad, jnp.asarray(sel, _BF16),
      jnp.asarray(ones))
    edges = edges.reshape(nt, V7X_SUBLANES, lanes)
    return h, kb, qb, edges[:, 0, :FOX_HEADS], edges[:, 1, :FOX_HEADS]


def _unpack_bias(packed, sel_ref):
    return jnp.dot(packed, sel_ref[...], preferred_element_type=_F32).astype(_BF16)


def _fox_kernel(bstart_ref, bend_ref, thr_ref, q_ref, k_ref, kbp_ref, sel_ref, v_ref, o_ref,
                acc_ref, kb_ref, *, tq):
    head = pl.program_id(0)
    qi = pl.program_id(1)
    b_tile = bstart_ref[qi, head]
    thr = thr_ref[0]
    j0 = lax.fori_loop(
        0, qi, lambda j, n: n + jnp.where(b_tile - bend_ref[j, head] >= thr, 1, 0), 0)

    @pl.when(qi == 0)
    def _():
        kb_ref[...] = _unpack_bias(kbp_ref[...], sel_ref)

    lane = lax.broadcasted_iota(jnp.int32, (tq, V7X_LANES), 1)
    ones3 = jnp.where(lane < 3, 1.0, 0.0).astype(_BF16)
    q_aug = jnp.concatenate([q_ref[...], ones3], axis=1)
    acc_ref[...] = jnp.zeros_like(acc_ref)

    def logits(j):
        ks = pl.multiple_of(j * tq, tq)
        k_aug = jnp.concatenate([k_ref[pl.ds(ks, tq), :], kb_ref[pl.ds(ks, tq), :]], axis=1)
        return lax.dot_general(k_aug, q_aug, (((1,), (1,)), ((), ())),
                               preferred_element_type=_F32)

    def accumulate(j, s, m_prev, l_prev):
        m_new = jnp.maximum(m_prev, jnp.max(s, axis=0, keepdims=True))
        alpha = jnp.exp2(m_prev - m_new)
        p = jnp.exp2(s - m_new)
        l_new = alpha * l_prev + jnp.sum(p, axis=0, keepdims=True)
        ks = pl.multiple_of(j * tq, tq)
        pv = lax.dot_general(v_ref[pl.ds(ks, tq), :], p.astype(_BF16),
                             (((0,), (0,)), ((), ())), preferred_element_type=_F32)
        acc_ref[...] = alpha * acc_ref[...] + pv
        return m_new, l_new

    def body(j, carry):
        m_prev, l_prev, s = carry
        s_next = logits(j + 1)
        m_new, l_new = accumulate(j, s, m_prev, l_prev)
        return m_new, l_new, s_next

    m0 = jnp.full((1, tq), MASK_VALUE, _F32)
    l0 = jnp.zeros((1, tq), _F32)
    m, l, s = lax.fori_loop(j0, qi, body, (m0, l0, logits(j0)))
    row = lax.broadcasted_iota(jnp.int32, s.shape, 0)
    col = lax.broadcasted_iota(jnp.int32, s.shape, 1)
    m, l = accumulate(qi, jnp.where(row <= col, s, MASK_VALUE), m, l)
    o_ref[...] = (acc_ref[...] / l).T.astype(o_ref.dtype)


def _fox_fast_kernel(bstart_ref, bend_ref, thr_ref, q_ref, qbp_ref, k_ref, kbp_ref, sel_ref, v_ref,
                     o_ref, acc_ref, kb_ref, l_ref, s0_ref, s1_ref, *, tq):
    head = pl.program_id(0)
    qi = pl.program_id(1)
    b_tile = bstart_ref[qi, head]
    thr = thr_ref[0]
    j0 = lax.fori_loop(
        0, qi, lambda j, n: n + jnp.where(b_tile - bend_ref[j, head] >= thr, 1, 0), 0)

    @pl.when(qi == 0)
    def _():
        kb_ref[...] = _unpack_bias(kbp_ref[...], sel_ref)

    q_aug = jnp.concatenate([q_ref[...], _unpack_bias(qbp_ref[...], sel_ref)], axis=1)
    acc_ref[...] = jnp.zeros_like(acc_ref)

    def logits(j):
        ks = pl.multiple_of(j * tq, tq)
        k_aug = jnp.concatenate([k_ref[pl.ds(ks, tq), :], kb_ref[pl.ds(ks, tq), :]], axis=1)
        return _dot_nt(k_aug, q_aug)

    def accumulate(j, s):
        p = jnp.exp2(s)
        ks = pl.multiple_of(j * tq, tq)
        acc_ref[...] += lax.dot_general(v_ref[pl.ds(ks, tq), :], p.astype(_BF16),
                                        (((0,), (0,)), ((), ())), preferred_element_type=_F32)
        l_ref[...] += jnp.sum(p, axis=0, keepdims=True)

    l_ref[...] = jnp.zeros_like(l_ref)
    row = lax.broadcasted_iota(jnp.int32, s1_ref.shape, 0)
    col = lax.broadcasted_iota(jnp.int32, s1_ref.shape, 1)
    s_diag = jnp.where(row <= col, logits(qi), MASK_VALUE)
    s0_ref[...] = logits(j0)
    accumulate(qi, s_diag)
    n_before = qi - j0

    def pair(t, carry):
        j = j0 + 2 * t
        s1_ref[...] = logits(j + 1)
        accumulate(j, s0_ref[...])
        s0_ref[...] = logits(j + 2)
        accumulate(j + 1, s1_ref[...])
        return carry

    lax.fori_loop(0, n_before // 2, pair, 0)

    @pl.when(n_before % 2 == 1)
    def _():
        accumulate(qi - 1, s0_ref[...])

    o_ref[...] = (acc_ref[...] / l_ref[...]).T.astype(o_ref.dtype)


def _fox_attention(qk, kb, qb, b_start, b_end, qk_bound, v, tq):
    s = v.shape[0]
    hd = FOX_HEAD_DIM
    lanes = V7X_LANES
    thr = (2.0 * qk_bound + FOX_SKIP_BITS).reshape(1).astype(_F32)
    sel = np.zeros((FOX_HEADS, lanes, lanes), np.float32)
    for head in range(FOX_HEADS):
        for c in range(FOX_BIAS_LANES):
            sel[head, FOX_BIAS_LANES * head + c, c] = 1.0
    sel = jnp.asarray(sel, _BF16)
    smem = pl.BlockSpec(memory_space=pltpu.SMEM)
    q_spec = pl.BlockSpec((tq, hd), lambda h, i: (i, h))
    k_spec = pl.BlockSpec((s, hd), lambda h, i: (0, FOX_HEADS + h))
    v_spec = pl.BlockSpec((s, hd), lambda h, i: (0, h))
    kb_spec = pl.BlockSpec((s, lanes), lambda h, i: (0, 0))
    qb_spec = pl.BlockSpec((tq, lanes), lambda h, i: (i, 0))
    sel_spec = pl.BlockSpec((None, lanes, lanes), lambda h, i: (h, 0, 0))

    def call(kernel, in_specs, extra_scratch, *args):
        return pl.pallas_call(
            functools.partial(kernel, tq=tq),
            out_shape=jax.ShapeDtypeStruct((s, FOX_WIDTH), _BF16),
            grid=(FOX_HEADS, s // tq),
            in_specs=[smem, smem, smem] + in_specs,
            out_specs=q_spec,
            scratch_shapes=[pltpu.VMEM((hd, tq), _F32),
                            pltpu.VMEM((s, lanes), _BF16)] + extra_scratch,
            compiler_params=_params(("parallel", "arbitrary"), 32),
            name=kernel.__name__.strip("_"),
        )(b_start, b_end, thr, *args)

    fast_scratch = [pltpu.VMEM((1, tq), _F32), pltpu.VMEM((tq, tq), _F32),
                    pltpu.VMEM((tq, tq), _F32)]
    return lax.cond(
        qk_bound <= FOX_FAST_MAX_LOGIT,
        lambda: call(_fox_fast_kernel, [q_spec, qb_spec, k_spec, kb_spec, sel_spec, v_spec],
                     fast_scratch, qk, qb, qk, kb, sel, v),
        lambda: call(_fox_kernel, [q_spec, k_spec, kb_spec, sel_spec, v_spec], [],
                     qk, qk, kb, sel, v))


def _shift_rows(ext, d):
    return pltpu.roll(ext, d, axis=0)[V7X_SUBLANES:]


def _lru_kernel(lx_ref, lg_ref, wc_ref, bc_ref, wg_ref, bg_ref, lam_ref, o_ref,
                halo_ref, carry_ref, a_ref, h_ref):
    @pl.when(pl.program_id(0) == 0)
    def _():
        halo_ref[...] = jnp.zeros_like(halo_ref)
        carry_ref[...] = jnp.zeros_like(carry_ref)

    lx = lx_ref[...]
    ts = lx.shape[0]
    ext = jnp.concatenate([halo_ref[...], lx], axis=0)
    xr = (wc_ref[3:4, :] * lx + wc_ref[2:3, :] * _shift_rows(ext, 1)
          + wc_ref[1:2, :] * _shift_rows(ext, 2) + wc_ref[0:1, :] * _shift_rows(ext, 3)
          + bc_ref[...])
    halo_ref[...] = lx[ts - V7X_SUBLANES:]

    lam = lam_ref[...]
    log_sig_lam = jnp.minimum(lam, 0.0) - jnp.log1p(jnp.exp(-jnp.abs(lam)))
    bd = LRU_BLOCK_DIM
    for nb in range(LRU_BLOCKS):
        sl = slice(nb * bd, (nb + 1) * bd)
        x_nb = xr[:, sl]
        gates = jax.nn.sigmoid(jnp.dot(x_nb.astype(_BF16), wg_ref[nb],
                                       preferred_element_type=_F32) + bg_ref[nb])
        log_a = LRU_C * gates[:, :bd] * log_sig_lam[:, sl]
        a_ref[:, sl] = jnp.exp(log_a)
        t = jnp.tanh(log_a)
        h_ref[:, sl] = jnp.sqrt(-2.0 * t / (1.0 - t)) * (gates[:, bd:] * x_nb)

    sub = lax.broadcasted_iota(jnp.int32, (V7X_SUBLANES, 1), 0)

    def group(g, carry):
        rows = pl.ds(pl.multiple_of(g * V7X_SUBLANES, V7X_SUBLANES), V7X_SUBLANES)
        a = a_ref[rows, :]
        h = h_ref[rows, :]
        for d in (1, 2, 4):
            valid = sub >= d
            h = h + a * jnp.where(valid, pltpu.roll(h, d, axis=0), 0.0)
            a = a * jnp.where(valid, pltpu.roll(a, d, axis=0), 1.0)
        h = h + a * carry
        h_ref[rows, :] = h
        return jnp.broadcast_to(h[V7X_SUBLANES - 1:, :], h.shape)

    carry_ref[...] = lax.fori_loop(0, ts // V7X_SUBLANES, group, carry_ref[...],
                                   unroll=LRU_SCAN_UNROLL)
    o_ref[...] = (h_ref[...] * jax.nn.gelu(lg_ref[...])).astype(o_ref.dtype)


def _lru_branch(lxlg, w_conv, b_conv, w_gate, b_gate, lam, ts):
    s = lxlg.shape[0]
    w = LRU_WIDTH
    bd = LRU_BLOCK_DIM
    return pl.pallas_call(
        _lru_kernel,
        out_shape=jax.ShapeDtypeStruct((s, w), _BF16),
        grid=(s // ts,),
        in_specs=[pl.BlockSpec((ts, w), lambda i: (i, 0)),
                  pl.BlockSpec((ts, w), lambda i: (i, 1)),
                  pl.BlockSpec((LRU_CONV, w), lambda i: (0, 0)),
                  pl.BlockSpec((1, w), lambda i: (0, 0)),
                  pl.BlockSpec((LRU_BLOCKS, bd, 2 * bd), lambda i: (0, 0, 0)),
                  pl.BlockSpec((LRU_BLOCKS, 1, 2 * bd), lambda i: (0, 0, 0)),
                  pl.BlockSpec((1, w), lambda i: (0, 0))],
        out_specs=pl.BlockSpec((ts, w), lambda i: (i, 0)),
        scratch_shapes=[pltpu.VMEM((V7X_SUBLANES, w), _F32), pltpu.VMEM((V7X_SUBLANES, w), _F32),
                        pltpu.VMEM((ts, w), _F32), pltpu.VMEM((ts, w), _F32)],
        compiler_params=_params(("arbitrary",), 40),
        name="conv_rglru",
    )(lxlg, lxlg, w_conv, b_conv.reshape(1, -1), w_gate, b_gate, lam.reshape(1, -1))


def _mem_kv_kernel(mem_ref, g_ref, w_ref, gk_ref, k_ref, v_ref):
    x = mem_ref[...]
    ms = jnp.mean(x * x, axis=-1, keepdims=True)
    hm = (x * lax.rsqrt(ms + EPS) * g_ref[...]).astype(_BF16)
    acc = jnp.dot(hm, w_ref[...].astype(_BF16), preferred_element_type=_F32)

    @pl.when(pl.program_id(0) == 0)
    def _():
        for gi in range(MEM_HEADS):
            sl = slice(gi * MEM_HEAD_DIM, (gi + 1) * MEM_HEAD_DIM)
            blk = acc[:, sl]
            ms_k = jnp.mean(blk * blk, axis=-1, keepdims=True)
            k_ref[:, sl] = (blk * lax.rsqrt(ms_k + EPS) * gk_ref[:, sl]).astype(k_ref.dtype)

    @pl.when(pl.program_id(0) == 1)
    def _():
        v_ref[...] = acc.astype(v_ref.dtype)


def _mem_kv(mem, g_mem, w_kv, g_k):
    m, d = mem.shape
    w = MEM_WIDTH
    const = lambda j: (0, 0)
    return pl.pallas_call(
        _mem_kv_kernel,
        out_shape=(jax.ShapeDtypeStruct((m, w), _BF16), jax.ShapeDtypeStruct((m, w), _BF16)),
        grid=(2,),
        in_specs=[pl.BlockSpec((m, d), const), pl.BlockSpec((1, d), const),
                  pl.BlockSpec((d, w), lambda j: (0, j)), pl.BlockSpec((1, w), const)],
        out_specs=(pl.BlockSpec((m, w), const), pl.BlockSpec((m, w), const)),
        compiler_params=_params(("arbitrary",), 40),
        name="mem_kv",
    )(mem, g_mem.reshape(1, d), w_kv, jnp.tile(g_k, MEM_HEADS).reshape(1, w))


def _mem_attn_kernel(q_ref, k_ref, v_ref, o_ref):
    s = lax.dot_general(q_ref[...], k_ref[...], (((1,), (1,)), ((), ())),
                        preferred_element_type=_F32)
    m = jnp.max(s, axis=-1, keepdims=True)
    p = jnp.exp(s - m)
    l = jnp.sum(p, axis=-1, keepdims=True)
    acc = jnp.dot(p.astype(_BF16), v_ref[...], preferred_element_type=_F32)
    o_ref[...] = (acc / l).astype(o_ref.dtype)


def _mem_attention(q, k, v, ts):
    s = q.shape[0]
    m = k.shape[0]
    hd = MEM_HEAD_DIM
    return pl.pallas_call(
        _mem_attn_kernel,
        out_shape=jax.ShapeDtypeStruct((s, MEM_WIDTH), _BF16),
        grid=(s // ts, MEM_HEADS),
        in_specs=[pl.BlockSpec((ts, hd), lambda i, h: (i, h)),
                  pl.BlockSpec((m, hd), lambda i, h: (0, h)),
                  pl.BlockSpec((m, hd), lambda i, h: (0, h))],
        out_specs=pl.BlockSpec((ts, hd), lambda i, h: (i, h)),
        compiler_params=_params(("parallel", "arbitrary"), 32),
        name="mem_attention",
    )(q, k, v)


def _merge_kernel(h_ref, yf_ref, yl_ref, ym_ref, wg0_ref, wg1_ref, wg2_ref, wb_ref, bg_ref,
                  o_ref):
    h = h_ref[...]
    merged = None
    for n, (y_ref, wg_ref) in enumerate(((yf_ref, wg0_ref), (yl_ref, wg1_ref),
                                         (ym_ref, wg2_ref))):
        gate = jax.nn.sigmoid(_dot_nt(h, wg_ref[...]) + bg_ref[n:n + 1, :])
        term = gate * lax.dot_general(y_ref[...], wb_ref[n], (((1,), (0,)), ((), ())),
                                      preferred_element_type=_F32)
        merged = term if merged is None else merged + term
    o_ref[...] = merged.astype(o_ref.dtype)


def _gated_merge(h, y_fox, y_lru, y_mem, w_t, gate_row0, w_branch, b_gate, tm, tn):
    s, d = h.shape
    nj = d // tn
    bw = y_fox.shape[1]
    y_spec = pl.BlockSpec((tm, bw), lambda i, j: (i, 0))
    assert gate_row0 % V7X_SUBLANES == 0

    def gate_spec(n):
        return pl.BlockSpec(
            (pl.Element(tn), pl.Element(d)),
            lambda i, j: (pl.multiple_of(gate_row0 + n * d + j * tn, V7X_SUBLANES), 0))

    return pl.pallas_call(
        _merge_kernel,
        out_shape=jax.ShapeDtypeStruct((s, d), _BF16),
        grid=(s // tm, nj),
        in_specs=[pl.BlockSpec((tm, d), lambda i, j: (i, 0)), y_spec, y_spec, y_spec,
                  gate_spec(0), gate_spec(1), gate_spec(2),
                  pl.BlockSpec((N_BRANCH, bw, tn), lambda i, j: (0, 0, j)),
                  pl.BlockSpec((N_BRANCH, tn), lambda i, j: (0, j))],
        out_specs=pl.BlockSpec((tm, tn), lambda i, j: (i, j)),
        compiler_params=_params(("parallel", "arbitrary"), 56),
        name="gated_merge",
    )(h, y_fox, y_lru, y_mem, w_t, w_t, w_t, w_branch, b_gate)


def _out_norm_kernel(a_ref, w_ref, x_ref, g_ref, x2_ref, h2_ref):
    x2 = x_ref[...] + jnp.dot(a_ref[...], w_ref[...], preferred_element_type=_F32)
    x2_ref[...] = x2
    ms = jnp.mean(x2 * x2, axis=-1, keepdims=True)
    h2_ref[...] = (x2 * lax.rsqrt(ms + EPS) * g_ref[...]).astype(h2_ref.dtype)


def _out_proj_norm(a, w, x, g, tm):
    m, k = a.shape
    d = w.shape[1]
    row = lambda i: (i, 0)
    return pl.pallas_call(
        _out_norm_kernel,
        out_shape=(jax.ShapeDtypeStruct((m, d), _F32), jax.ShapeDtypeStruct((m, d), _BF16)),
        grid=(m // tm,),
        in_specs=[pl.BlockSpec((tm, k), row), pl.BlockSpec((k, d), lambda i: (0, 0)),
                  pl.BlockSpec((tm, d), row), pl.BlockSpec((1, d), lambda i: (0, 0))],
        out_specs=(pl.BlockSpec((tm, d), row), pl.BlockSpec((tm, d), row)),
        compiler_params=_params(("parallel",), 48),
        name="proj_out_norm",
    )(a, w, x, g.reshape(1, d))


def _ffn_up_kernel(a_ref, wa_ref, wv_ref, wca_ref, wcv_ref, bca_ref, bcv_ref, o_ref,
                   halo_a_ref, halo_v_ref, wa_bf_ref, wv_bf_ref):
    @pl.when(pl.program_id(1) == 0)
    def _():
        halo_a_ref[...] = jnp.zeros_like(halo_a_ref)
        halo_v_ref[...] = jnp.zeros_like(halo_v_ref)
        wa_bf_ref[...] = wa_ref[...].astype(_BF16)
        wv_bf_ref[...] = wv_ref[...].astype(_BF16)

    a = a_ref[...]
    tm = a.shape[0]

    def conv(up, halo_ref, wc_ref, bc_ref):
        ext = jnp.concatenate([halo_ref[...], up], axis=0)
        halo_ref[...] = up[tm - V7X_SUBLANES:]
        return (wc_ref[2:3, :] * up + wc_ref[1:2, :] * _shift_rows(ext, 1)
                + wc_ref[0:1, :] * _shift_rows(ext, 2) + bc_ref[...])

    act = conv(jnp.dot(a, wa_bf_ref[...], preferred_element_type=_F32), halo_a_ref, wca_ref,
               bca_ref)
    val = conv(jnp.dot(a, wv_bf_ref[...], preferred_element_type=_F32), halo_v_ref, wcv_ref,
               bcv_ref)
    o_ref[...] = (jax.nn.gelu(act) * val).astype(o_ref.dtype)


def _ffn_up(h2, w_up, w_conv, b_conv, tm, tn):
    s, d = h2.shape
    f = FFN_HIDDEN
    nj = f // tn
    return pl.pallas_call(
        _ffn_up_kernel,
        out_shape=jax.ShapeDtypeStruct((s, f), _BF16),
        grid=(nj, s // tm),
        in_specs=[pl.BlockSpec((tm, d), lambda j, i: (i, 0)),
                  pl.BlockSpec((d, tn), lambda j, i: (0, j)),
                  pl.BlockSpec((d, tn), lambda j, i: (0, nj + j)),
                  pl.BlockSpec((FFN_CONV, tn), lambda j, i: (0, j)),
                  pl.BlockSpec((FFN_CONV, tn), lambda j, i: (0, nj + j)),
                  pl.BlockSpec((1, tn), lambda j, i: (0, j)),
                  pl.BlockSpec((1, tn), lambda j, i: (0, nj + j))],
        out_specs=pl.BlockSpec((tm, tn), lambda j, i: (i, j)),
        scratch_shapes=[pltpu.VMEM((V7X_SUBLANES, tn), _F32),
                        pltpu.VMEM((V7X_SUBLANES, tn), _F32),
                        pltpu.VMEM((d, tn), _BF16), pltpu.VMEM((d, tn), _BF16)],
        compiler_params=_params(("parallel", "arbitrary"), 56),
        name="ffn_up_conv_geglu",
    )(h2, w_up, w_up, w_conv, w_conv, b_conv.reshape(1, -1), b_conv.reshape(1, -1))


def _layer(x, mem, g_mix, w_in, b_f, g_q_fox, g_k_fox, w_lru_conv, b_lru_conv, w_rg_a, b_rg_a,
           w_rg_x, b_rg_x, lru_lambda, g_mem, w_mem_kv, g_q_mem, g_k_mem, b_gate, w_branch,
           w_out, g_ffn, w_ffn_up, w_ffn_conv, b_ffn_conv, w_ffn_down):
    c_k = 2 * FOX_WIDTH
    c_v = c_k + FOX_WIDTH
    c_f = c_v + FOX_HEADS
    c_l = c_f + 2 * LRU_WIDTH
    c_m = c_l + MEM_WIDTH

    w_in_t = w_in.T

    g_q_scaled = g_q_fox * (LOG2_E * FOX_HEAD_DIM ** -0.5)
    qk_bound = (1.02 * FOX_HEAD_DIM) * jnp.max(jnp.abs(g_q_scaled)) * jnp.max(jnp.abs(g_k_fox))
    b_pad = jnp.pad(b_f.reshape(1, -1), ((0, 0), (0, V7X_LANES - FOX_HEADS)))
    h, kb, qb, b_start, b_end = _norm_forget_bias(x, g_mix, w_in_t, c_v, b_pad, qk_bound,
                                                  FOX_TILE)
    gain_qk = jnp.concatenate([jnp.tile(g_q_scaled, FOX_HEADS),
                               jnp.tile(g_k_fox, FOX_HEADS)]).reshape(1, -1)
    qk = _matmul(h, w_in_t, w_t=True, n=c_k, tm=ROW_TILE, tn=COL_TILE, out_dtype=_BF16,
                 epilogue="gnorm", extra=gain_qk, group=FOX_HEAD_DIM, name="proj_qk")
    v = _matmul(h, w_in_t, w_t=True, n=FOX_WIDTH, w_off=c_k, tm=ROW_TILE, tn=COL_TILE,
                out_dtype=_BF16, name="proj_v")
    y_fox = _fox_attention(qk, kb, qb, b_start, b_end, qk_bound, v, FOX_TILE)

    lxlg = _matmul(h, w_in_t, w_t=True, n=2 * LRU_WIDTH, w_off=c_f, tm=ROW_TILE, tn=COL_TILE,
                   out_dtype=_F32, name="proj_lru")
    w_gate_lru = jnp.concatenate([w_rg_a, w_rg_x], axis=-1).astype(_BF16)
    b_gate_lru = jnp.concatenate([b_rg_a, b_rg_x], axis=-1).reshape(LRU_BLOCKS, 1, -1)
    y_lru = _lru_branch(lxlg, w_lru_conv, b_lru_conv, w_gate_lru, b_gate_lru, lru_lambda,
                        SEQ_TILE)

    gain_mq = (jnp.tile(g_q_mem, MEM_HEADS) * (MEM_HEAD_DIM ** -0.5)).reshape(1, -1)
    mq = _matmul(h, w_in_t, w_t=True, n=MEM_WIDTH, w_off=c_l, tm=ROW_TILE, tn=COL_TILE,
                 out_dtype=_BF16, epilogue="gnorm", extra=gain_mq, group=MEM_HEAD_DIM,
                 name="proj_mq")
    mk, mv = _mem_kv(mem, g_mem, w_mem_kv, g_k_mem)
    y_mem = _mem_attention(mq, mk, mv, ROW_TILE)

    merged = _gated_merge(h, y_fox, y_lru, y_mem, w_in_t, c_m, w_branch, b_gate,
                          ROW_TILE, MERGE_COL_TILE)
    x2, h2 = _out_proj_norm(merged, w_out.astype(_BF16), x, g_ffn, SEQ_TILE)

    g = _ffn_up(h2, w_ffn_up, w_ffn_conv, b_ffn_conv, ROW_TILE, FFN_COL_TILE)
    return _matmul(g, w_ffn_down.astype(_BF16), n=D_MODEL, tm=SEQ_TILE, tn=D_MODEL,
                   rows_outer=True,
                   out_dtype=_F32, epilogue="residual", extra=x2, name="ffn_down")


def kernel(x, mem, g_mix, w_in, b_f, g_q_fox, g_k_fox, w_lru_conv, b_lru_conv, w_rg_a, b_rg_a,
           w_rg_x, b_rg_x, lru_lambda, g_mem, w_mem_kv, g_q_mem, g_k_mem, b_gate, w_branch,
           w_out, g_ffn, w_ffn_up, w_ffn_conv, b_ffn_conv, w_ffn_down):
    depth = g_mix.shape[0]
    outs = []
    for b in range(x.shape[0]):
        xb = x[b]
        for l in range(depth):
            xb = _layer(xb, mem[b], g_mix[l], w_in[l], b_f[l], g_q_fox[l], g_k_fox[l],
                        w_lru_conv[l], b_lru_conv[l], w_rg_a[l], b_rg_a[l], w_rg_x[l],
                        b_rg_x[l], lru_lambda[l], g_mem[l], w_mem_kv[l], g_q_mem[l],
                        g_k_mem[l], b_gate[l], w_branch[l], w_out[l], g_ffn[l], w_ffn_up[l],
                        w_ffn_conv[l], b_ffn_conv[l], w_ffn_down[l])
        outs.append(xb)
    return outs[0][None] if len(outs) == 1 else jnp.stack(outs)
```

```python
import functools

import jax
import jax.numpy as jnp
import numpy as np
from jax import lax
from jax.experimental import pallas as pl
from jax.experimental.pallas import tpu as pltpu

D_MODEL = 2048
FOX_HEADS = 8
FOX_HEAD_DIM = 128
FOX_WIDTH = FOX_HEADS * FOX_HEAD_DIM
LRU_WIDTH = 1024
LRU_BLOCKS = 8
LRU_BLOCK_DIM = LRU_WIDTH // LRU_BLOCKS
LRU_CONV = 4
LRU_C = 8.0
MEM_HEADS = 4
MEM_HEAD_DIM = 256
MEM_WIDTH = MEM_HEADS * MEM_HEAD_DIM
N_BRANCH = 3
FFN_HIDDEN = 5632
FFN_CONV = 3
EPS = 1e-6

V7X_SUBLANES = 8
V7X_LANES = 128
MASK_VALUE = -1e30
LOG2_E = 1.4426950408889634
ROW_TILE = 1024
COL_TILE = 1024
FFN_COL_TILE = 512
MERGE_COL_TILE = 256
SEQ_TILE = 512
FOX_TILE = 512
FOX_SKIP_BITS = 64.0
FOX_BIAS_LANES = 16
FOX_FAST_MAX_LOGIT = 48.0
LRU_SCAN_UNROLL = 8

_BF16 = jnp.bfloat16
_F32 = jnp.float32


def _params(semantics, vmem_mib):
    return pltpu.CompilerParams(dimension_semantics=semantics,
                                vmem_limit_bytes=vmem_mib * 1024 * 1024)


def _dot_nt(a, w_t):
    return lax.dot_general(a, w_t, (((1,), (1,)), ((), ())), preferred_element_type=_F32)


def _mm_kernel(*refs, epilogue, group, cast_w, w_t):
    a_ref, w_ref = refs[0], refs[1]
    if cast_w:
        o_ref, wb_ref = refs[-2], refs[-1]

        @pl.when(pl.program_id(1) == 0)
        def _():
            wb_ref[...] = w_ref[...].astype(_BF16)

        w = wb_ref[...]
    else:
        o_ref = refs[-1]
        w = w_ref[...]
    if w_t:
        acc = _dot_nt(a_ref[...], w)
    else:
        acc = lax.dot_general(a_ref[...], w, (((1,), (0,)), ((), ())),
                              preferred_element_type=_F32)
    if epilogue == "gnorm":
        g_ref = refs[2]
        for gi in range(acc.shape[1] // group):
            sl = slice(gi * group, (gi + 1) * group)
            blk = acc[:, sl]
            ms = jnp.mean(blk * blk, axis=-1, keepdims=True)
            o_ref[:, sl] = (blk * lax.rsqrt(ms + EPS) * g_ref[:, sl]).astype(o_ref.dtype)
    elif epilogue == "residual":
        o_ref[...] = (refs[2][...] + acc).astype(o_ref.dtype)
    else:
        o_ref[...] = acc.astype(o_ref.dtype)


def _matmul(a, w, *, n, tm, tn, out_dtype, w_t=False, w_off=0, rows_outer=False,
            epilogue="plain", extra=None, group=None, name):
    m, k = a.shape
    cast_w = w.dtype != _BF16 and not rows_outer
    assert n % tn == 0 and m % tm == 0
    if rows_outer:
        grid = (m // tm, n // tn)
        ij = lambda i, j: (i, j)
    else:
        grid = (n // tn, m // tm)
        ij = lambda j, i: (i, j)
    if w_t:
        assert w_off % V7X_SUBLANES == 0
        w_block = (tn, k)
        w_spec = pl.BlockSpec((pl.Element(tn), pl.Element(k)),
                              lambda *g: (pl.multiple_of(w_off + ij(*g)[1] * tn, V7X_SUBLANES), 0))
    else:
        assert w_off % tn == 0
        w_block = (k, tn)
        w_mode = {"pipeline_mode": pl.Buffered(1)} if n == tn else {}
        w_spec = pl.BlockSpec(w_block, lambda *g: (0, w_off // tn + ij(*g)[1]), **w_mode)
    in_specs = [pl.BlockSpec((tm, k), lambda *g: (ij(*g)[0], 0)), w_spec]
    args = [a, w]
    block_bytes = (tm * k * 2 + k * tn * w.dtype.itemsize
                   + tm * tn * jnp.dtype(out_dtype).itemsize)
    if epilogue == "gnorm":
        in_specs.append(pl.BlockSpec((1, tn), lambda *g: (0, ij(*g)[1])))
        args.append(extra)
    elif epilogue == "residual":
        in_specs.append(pl.BlockSpec((tm, tn), lambda *g: ij(*g)))
        args.append(extra)
        block_bytes += tm * tn * extra.dtype.itemsize
    scratch = [pltpu.VMEM(w_block, _BF16)] if cast_w else []
    w_single = (not w_t and n == tn) * k * tn * w.dtype.itemsize
    vmem_mib = -(-(2 * block_bytes - w_single + cast_w * k * tn * 2 + 2 * tm * tn * 4)
                 // 2 ** 20) + 2
    return pl.pallas_call(
        functools.partial(_mm_kernel, epilogue=epilogue, group=group, cast_w=cast_w, w_t=w_t),
        out_shape=jax.ShapeDtypeStruct((m, n), out_dtype),
        grid=grid,
        in_specs=in_specs,
        out_specs=pl.BlockSpec((tm, tn), lambda *g: ij(*g)),
        scratch_shapes=scratch,
        compiler_params=_params(("parallel", "arbitrary"), vmem_mib),
        name=name,
    )(*args)


def _split3(x):
    hi = x.astype(_BF16)
    r1 = x - hi.astype(_F32)
    mid = r1.astype(_BF16)
    lo = (r1 - mid.astype(_F32)).astype(_BF16)
    return hi, mid, lo


def _forget_bias_kernel(shift_ref, x_ref, g_ref, w_ref, b_ref, sel_ref, ones_ref, h_ref, kb_ref,
                        qb_ref, edge_ref, carry_ref):
    @pl.when(pl.program_id(0) == 0)
    def _():
        carry_ref[...] = jnp.zeros_like(carry_ref)

    tc = x_ref.shape[0]
    x = x_ref[...]
    ms = jnp.mean(x * x, axis=-1, keepdims=True)
    h = (x * lax.rsqrt(ms + EPS) * g_ref[...]).astype(_BF16)
    h_ref[...] = h
    w = jnp.concatenate([w_ref[...], jnp.zeros((V7X_LANES - FOX_HEADS, w_ref.shape[1]), _F32)],
                        axis=0).astype(_BF16)
    z = _dot_nt(h, w) + b_ref[...]
    neg_log_f = (jnp.log1p(jnp.exp(-jnp.abs(z))) - jnp.minimum(z, 0.0)) * LOG2_E
    chunk = 2 * V7X_LANES
    row = lax.broadcasted_iota(jnp.int32, (chunk, chunk), 0)
    col = lax.broadcasted_iota(jnp.int32, (chunk, chunk), 1)
    tri = jnp.where(col <= row, 1.0, 0.0).astype(_BF16)
    total = carry_ref[0:1, :]
    pieces = []
    for r0 in range(0, tc, chunk):
        c_chunk = total
        for part in _split3(neg_log_f[r0:r0 + chunk, :]):
            c_chunk = c_chunk + jnp.dot(tri, part, preferred_element_type=_F32)
        pieces.append(c_chunk)
        total = c_chunk[chunk - 1:chunk, :]
    c = jnp.concatenate(pieces, axis=0)
    carry_ref[...] = jnp.broadcast_to(c[tc - 1:tc, :], carry_ref.shape)
    edge_ref[...] = jnp.concatenate(
        [c[0:1, :], c[tc - 1:tc, :], jnp.zeros((V7X_SUBLANES - 2, c.shape[1]), _F32)], axis=0)
    parts = jnp.concatenate(_split3(c) + _split3(-(c + shift_ref[0])), axis=1)
    routed = jnp.dot(parts, sel_ref[...], preferred_element_type=_F32) + ones_ref[...]
    half = kb_ref.shape[1]
    kb_ref[...] = routed[:, :half].astype(kb_ref.dtype)
    qb_ref[...] = routed[:, half:].astype(qb_ref.dtype)


def _norm_forget_bias(x, g, w_t, w_row0, b_pad, shift, tc):
    s, d = x.shape
    lanes = V7X_LANES
    width = lanes
    nt = s // tc
    sel = np.zeros((6 * lanes, 2 * width), np.float32)
    ones = np.zeros((1, 2 * width), np.float32)
    for head in range(FOX_HEADS):
        for part in range(3):
            sel[part * lanes + head, FOX_BIAS_LANES * head + part] = 1.0
            sel[(3 + part) * lanes + head, width + FOX_BIAS_LANES * head + 3 + part] = 1.0
            ones[0, FOX_BIAS_LANES * head + 3 + part] = 1.0
            ones[0, width + FOX_BIAS_LANES * head + part] = 1.0
    h, kb, qb, edges = pl.pallas_call(
        _forget_bias_kernel,
        out_shape=(jax.ShapeDtypeStruct((s, d), _BF16),
                   jax.ShapeDtypeStruct((s, width), _BF16),
                   jax.ShapeDtypeStruct((s, width), _BF16),
                   jax.ShapeDtypeStruct((nt * V7X_SUBLANES, lanes), _F32)),
        grid=(nt,),
        in_specs=[pl.BlockSpec(memory_space=pltpu.SMEM),
                  pl.BlockSpec((tc, d), lambda i: (i, 0)),
                  pl.BlockSpec((1, d), lambda i: (0, 0)),
                  pl.BlockSpec((pl.Element(FOX_HEADS), pl.Element(d)), lambda i: (w_row0, 0)),
                  pl.BlockSpec((1, lanes), lambda i: (0, 0)),
                  pl.BlockSpec((6 * lanes, 2 * width), lambda i: (0, 0)),
                  pl.BlockSpec((1, 2 * width), lambda i: (0, 0))],
        out_specs=(pl.BlockSpec((tc, d), lambda i: (i, 0)),
                   pl.BlockSpec((tc, width), lambda i: (i, 0)),
                   pl.BlockSpec((tc, width), lambda i: (i, 0)),
                   pl.BlockSpec((V7X_SUBLANES, lanes), lambda i: (i, 0))),
        scratch_shapes=[pltpu.VMEM((V7X_SUBLANES, lanes), _F32)],
        compiler_params=_params(("arbitrary",), 40),
        name="norm_forget_bias",
    )(shift.reshape(1).astype(_F32), x, g.reshape(1, d), w_t, b_pad, jnp.asarray(sel, _BF16),
      jnp.asarray(ones))
    edges = edges.reshape(nt, V7X_SUBLANES, lanes)
    return h, kb, qb, edges[:, 0, :FOX_HEADS], edges[:, 1, :FOX_HEADS]


def _unpack_bias(packed, sel_ref):
    return jnp.dot(packed, sel_ref[...], preferred_element_type=_F32).astype(_BF16)


def _fox_kernel(bstart_ref, bend_ref, thr_ref, q_ref, k_ref, kbp_ref, sel_ref, v_ref, o_ref,
                acc_ref, kb_ref, *, tq):
    head = pl.program_id(0)
    qi = pl.program_id(1)
    b_tile = bstart_ref[qi, head]
    thr = thr_ref[0]
    j0 = lax.fori_loop(
        0, qi, lambda j, n: n + jnp.where(b_tile - bend_ref[j, head] >= thr, 1, 0), 0)

    @pl.when(qi == 0)
    def _():
        kb_ref[...] = _unpack_bias(kbp_ref[...], sel_ref)

    lane = lax.broadcasted_iota(jnp.int32, (tq, V7X_LANES), 1)
    ones3 = jnp.where(lane < 3, 1.0, 0.0).astype(_BF16)
    q_aug = jnp.concatenate([q_ref[...], ones3], axis=1)
    acc_ref[...] = jnp.zeros_like(acc_ref)

    def logits(j):
        ks = pl.multiple_of(j * tq, tq)
        k_aug = jnp.concatenate([k_ref[pl.ds(ks, tq), :], kb_ref[pl.ds(ks, tq), :]], axis=1)
        return lax.dot_general(k_aug, q_aug, (((1,), (1,)), ((), ())),
                               preferred_element_type=_F32)

    def accumulate(j, s, m_prev, l_prev):
        m_new = jnp.maximum(m_prev, jnp.max(s, axis=0, keepdims=True))
        alpha = jnp.exp2(m_prev - m_new)
        p = jnp.exp2(s - m_new)
        l_new = alpha * l_prev + jnp.sum(p, axis=0, keepdims=True)
        ks = pl.multiple_of(j * tq, tq)
        pv = lax.dot_general(v_ref[pl.ds(ks, tq), :], p.astype(_BF16),
                             (((0,), (0,)), ((), ())), preferred_element_type=_F32)
        acc_ref[...] = alpha * acc_ref[...] + pv
        return m_new, l_new

    def body(j, carry):
        m_prev, l_prev, s = carry
        s_next = logits(j + 1)
        m_new, l_new = accumulate(j, s, m_prev, l_prev)
        return m_new, l_new, s_next

    m0 = jnp.full((1, tq), MASK_VALUE, _F32)
    l0 = jnp.zeros((1, tq), _F32)
    m, l, s = lax.fori_loop(j0, qi, body, (m0, l0, logits(j0)))
    row = lax.broadcasted_iota(jnp.int32, s.shape, 0)
    col = lax.broadcasted_iota(jnp.int32, s.shape, 1)
    m, l = accumulate(qi, jnp.where(row <= col, s, MASK_VALUE), m, l)
    o_ref[...] = (acc_ref[...] / l).T.astype(o_ref.dtype)


def _fox_fast_kernel(bstart_ref, bend_ref, thr_ref, q_ref, qbp_ref, k_ref, kbp_ref, sel_ref, v_ref,
                     o_ref, acc_ref, kb_ref, l_ref, s0_ref, s1_ref, *, tq):
    head = pl.program_id(0)
    qi = pl.program_id(1)
    b_tile = bstart_ref[qi, head]
    thr = thr_ref[0]
    j0 = lax.fori_loop(
        0, qi, lambda j, n: n + jnp.where(b_tile - bend_ref[j, head] >= thr, 1, 0), 0)

    @pl.when(qi == 0)
    def _():
        kb_ref[...] = _unpack_bias(kbp_ref[...], sel_ref)

    q_aug = jnp.concatenate([q_ref[...], _unpack_bias(qbp_ref[...], sel_ref)], axis=1)
    acc_ref[...] = jnp.zeros_like(acc_ref)

    def logits(j):
        ks = pl.multiple_of(j * tq, tq)
        k_aug = jnp.concatenate([k_ref[pl.ds(ks, tq), :], kb_ref[pl.ds(ks, tq), :]], axis=1)
        return _dot_nt(k_aug, q_aug)

    def accumulate(j, s):
        p = jnp.exp2(s)
        ks = pl.multiple_of(j * tq, tq)
        acc_ref[...] += lax.dot_general(v_ref[pl.ds(ks, tq), :], p.astype(_BF16),
                                        (((0,), (0,)), ((), ())), preferred_element_type=_F32)
        l_ref[...] += jnp.sum(p, axis=0, keepdims=True)

    l_ref[...] = jnp.zeros_like(l_ref)
    row = lax.broadcasted_iota(jnp.int32, s1_ref.shape, 0)
    col = lax.broadcasted_iota(jnp.int32, s1_ref.shape, 1)
    s_diag = jnp.where(row <= col, logits(qi), MASK_VALUE)
    s0_ref[...] = logits(j0)
    accumulate(qi, s_diag)
    n_before = qi - j0

    def pair(t, carry):
        j = j0 + 2 * t
        s1_ref[...] = logits(j + 1)
        accumulate(j, s0_ref[...])
        s0_ref[...] = logits(j + 2)
        accumulate(j + 1, s1_ref[...])
        return carry

    lax.fori_loop(0, n_before // 2, pair, 0)

    @pl.when(n_before % 2 == 1)
    def _():
        accumulate(qi - 1, s0_ref[...])

    o_ref[...] = (acc_ref[...] / l_ref[...]).T.astype(o_ref.dtype)


def _fox_attention(qk, kb, qb, b_start, b_end, qk_bound, v, tq):
    s = v.shape[0]
    hd = FOX_HEAD_DIM
    lanes = V7X_LANES
    thr = (2.0 * qk_bound + FOX_SKIP_BITS).reshape(1).astype(_F32)
    sel = np.zeros((FOX_HEADS, lanes, lanes), np.float32)
    for head in range(FOX_HEADS):
        for c in range(FOX_BIAS_LANES):
            sel[head, FOX_BIAS_LANES * head + c, c] = 1.0
    sel = jnp.asarray(sel, _BF16)
    smem = pl.BlockSpec(memory_space=pltpu.SMEM)
    q_spec = pl.BlockSpec((tq, hd), lambda h, i: (i, h))
    k_spec = pl.BlockSpec((s, hd), lambda h, i: (0, FOX_HEADS + h))
    v_spec = pl.BlockSpec((s, hd), lambda h, i: (0, h))
    kb_spec = pl.BlockSpec((s, lanes), lambda h, i: (0, 0))
    qb_spec = pl.BlockSpec((tq, lanes), lambda h, i: (i, 0))
    sel_spec = pl.BlockSpec((None, lanes, lanes), lambda h, i: (h, 0, 0))

    def call(kernel, in_specs, extra_scratch, *args):
        return pl.pallas_call(
            functools.partial(kernel, tq=tq),
            out_shape=jax.ShapeDtypeStruct((s, FOX_WIDTH), _BF16),
            grid=(FOX_HEADS, s // tq),
            in_specs=[smem, smem, smem] + in_specs,
            out_specs=q_spec,
            scratch_shapes=[pltpu.VMEM((hd, tq), _F32),
                            pltpu.VMEM((s, lanes), _BF16)] + extra_scratch,
            compiler_params=_params(("parallel", "arbitrary"), 32),
            name=kernel.__name__.strip("_"),
        )(b_start, b_end, thr, *args)

    fast_scratch = [pltpu.VMEM((1, tq), _F32), pltpu.VMEM((tq, tq), _F32),
                    pltpu.VMEM((tq, tq), _F32)]
    return lax.cond(
        qk_bound <= FOX_FAST_MAX_LOGIT,
        lambda: call(_fox_fast_kernel, [q_spec, qb_spec, k_spec, kb_spec, sel_spec, v_spec],
                     fast_scratch, qk, qb, qk, kb, sel, v),
        lambda: call(_fox_kernel, [q_spec, k_spec, kb_spec, sel_spec, v_spec], [],
                     qk, qk, kb, sel, v))


def _shift_rows(ext, d):
    return pltpu.roll(ext, d, axis=0)[V7X_SUBLANES:]


def _lru_kernel(lx_ref, lg_ref, wc_ref, bc_ref, wg_ref, bg_ref, lam_ref, o_ref,
                halo_ref, carry_ref, a_ref, h_ref):
    @pl.when(pl.program_id(0) == 0)
    def _():
        halo_ref[...] = jnp.zeros_like(halo_ref)
        carry_ref[...] = jnp.zeros_like(carry_ref)

    lx = lx_ref[...]
    ts = lx.shape[0]
    ext = jnp.concatenate([halo_ref[...], lx], axis=0)
    xr = (wc_ref[3:4, :] * lx + wc_ref[2:3, :] * _shift_rows(ext, 1)
          + wc_ref[1:2, :] * _shift_rows(ext, 2) + wc_ref[0:1, :] * _shift_rows(ext, 3)
          + bc_ref[...])
    halo_ref[...] = lx[ts - V7X_SUBLANES:]

    lam = lam_ref[...]
    log_sig_lam = jnp.minimum(lam, 0.0) - jnp.log1p(jnp.exp(-jnp.abs(lam)))
    bd = LRU_BLOCK_DIM
    for nb in range(LRU_BLOCKS):
        sl = slice(nb * bd, (nb + 1) * bd)
        x_nb = xr[:, sl]
        gates = jax.nn.sigmoid(jnp.dot(x_nb.astype(_BF16), wg_ref[nb],
                                       preferred_element_type=_F32) + bg_ref[nb])
        log_a = LRU_C * gates[:, :bd] * log_sig_lam[:, sl]
        a_ref[:, sl] = jnp.exp(log_a)
        t = jnp.tanh(log_a)
        h_ref[:, sl] = jnp.sqrt(-2.0 * t / (1.0 - t)) * (gates[:, bd:] * x_nb)

    sub = lax.broadcasted_iota(jnp.int32, (V7X_SUBLANES, 1), 0)

    def group(g, carry):
        rows = pl.ds(pl.multiple_of(g * V7X_SUBLANES, V7X_SUBLANES), V7X_SUBLANES)
        a = a_ref[rows, :]
        h = h_ref[rows, :]
        for d in (1, 2, 4):
            valid = sub >= d
            h = h + a * jnp.where(valid, pltpu.roll(h, d, axis=0), 0.0)
            a = a * jnp.where(valid, pltpu.roll(a, d, axis=0), 1.0)
        h = h + a * carry
        h_ref[rows, :] = h
        return jnp.broadcast_to(h[V7X_SUBLANES - 1:, :], h.shape)

    carry_ref[...] = lax.fori_loop(0, ts // V7X_SUBLANES, group, carry_ref[...],
                                   unroll=LRU_SCAN_UNROLL)
    o_ref[...] = (h_ref[...] * jax.nn.gelu(lg_ref[...])).astype(o_ref.dtype)


def _lru_branch(lxlg, w_conv, b_conv, w_gate, b_gate, lam, ts):
    s = lxlg.shape[0]
    w = LRU_WIDTH
    bd = LRU_BLOCK_DIM
    return pl.pallas_call(
        _lru_kernel,
        out_shape=jax.ShapeDtypeStruct((s, w), _BF16),
        grid=(s // ts,),
        in_specs=[pl.BlockSpec((ts, w), lambda i: (i, 0)),
                  pl.BlockSpec((ts, w), lambda i: (i, 1)),
                  pl.BlockSpec((LRU_CONV, w), lambda i: (0, 0)),
                  pl.BlockSpec((1, w), lambda i: (0, 0)),
                  pl.BlockSpec((LRU_BLOCKS, bd, 2 * bd), lambda i: (0, 0, 0)),
                  pl.BlockSpec((LRU_BLOCKS, 1, 2 * bd), lambda i: (0, 0, 0)),
                  pl.BlockSpec((1, w), lambda i: (0, 0))],
        out_specs=pl.BlockSpec((ts, w), lambda i: (i, 0)),
        scratch_shapes=[pltpu.VMEM((V7X_SUBLANES, w), _F32), pltpu.VMEM((V7X_SUBLANES, w), _F32),
                        pltpu.VMEM((ts, w), _F32), pltpu.VMEM((ts, w), _F32)],
        compiler_params=_params(("arbitrary",), 40),
        name="conv_rglru",
    )(lxlg, lxlg, w_conv, b_conv.reshape(1, -1), w_gate, b_gate, lam.reshape(1, -1))


def _mem_kv_kernel(mem_ref, g_ref, w_ref, gk_ref, k_ref, v_ref):
    x = mem_ref[...]
    ms = jnp.mean(x * x, axis=-1, keepdims=True)
    hm = (x * lax.rsqrt(ms + EPS) * g_ref[...]).astype(_BF16)
    acc = jnp.dot(hm, w_ref[...].astype(_BF16), preferred_element_type=_F32)

    @pl.when(pl.program_id(0) == 0)
    def _():
        for gi in range(MEM_HEADS):
            sl = slice(gi * MEM_HEAD_DIM, (gi + 1) * MEM_HEAD_DIM)
            blk = acc[:, sl]
            ms_k = jnp.mean(blk * blk, axis=-1, keepdims=True)
            k_ref[:, sl] = (blk * lax.rsqrt(ms_k + EPS) * gk_ref[:, sl]).astype(k_ref.dtype)

    @pl.when(pl.program_id(0) == 1)
    def _():
        v_ref[...] = acc.astype(v_ref.dtype)


def _mem_kv(mem, g_mem, w_kv, g_k):
    m, d = mem.shape
    w = MEM_WIDTH
    const = lambda j: (0, 0)
    return pl.pallas_call(
        _mem_kv_kernel,
        out_shape=(jax.ShapeDtypeStruct((m, w), _BF16), jax.ShapeDtypeStruct((m, w), _BF16)),
        grid=(2,),
        in_specs=[pl.BlockSpec((m, d), const), pl.BlockSpec((1, d), const),
                  pl.BlockSpec((d, w), lambda j: (0, j)), pl.BlockSpec((1, w), const)],
        out_specs=(pl.BlockSpec((m, w), const), pl.BlockSpec((m, w), const)),
        compiler_params=_params(("arbitrary",), 40),
        name="mem_kv",
    )(mem, g_mem.reshape(1, d), w_kv, jnp.tile(g_k, MEM_HEADS).reshape(1, w))


def _mem_attn_kernel(q_ref, k_ref, v_ref, o_ref):
    s = lax.dot_general(q_ref[...], k_ref[...], (((1,), (1,)), ((), ())),
                        preferred_element_type=_F32)
    m = jnp.max(s, axis=-1, keepdims=True)
    p = jnp.exp(s - m)
    l = jnp.sum(p, axis=-1, keepdims=True)
    acc = jnp.dot(p.astype(_BF16), v_ref[...], preferred_element_type=_F32)
    o_ref[...] = (acc / l).astype(o_ref.dtype)


def _mem_attention(q, k, v, ts):
    s = q.shape[0]
    m = k.shape[0]
    hd = MEM_HEAD_DIM
    return pl.pallas_call(
        _mem_attn_kernel,
        out_shape=jax.ShapeDtypeStruct((s, MEM_WIDTH), _BF16),
        grid=(s // ts, MEM_HEADS),
        in_specs=[pl.BlockSpec((ts, hd), lambda i, h: (i, h)),
                  pl.BlockSpec((m, hd), lambda i, h: (0, h)),
                  pl.BlockSpec((m, hd), lambda i, h: (0, h))],
        out_specs=pl.BlockSpec((ts, hd), lambda i, h: (i, h)),
        compiler_params=_params(("parallel", "arbitrary"), 32),
        name="mem_attention",
    )(q, k, v)


def _merge_kernel(h_ref, yf_ref, yl_ref, ym_ref, wg0_ref, wg1_ref, wg2_ref, wb_ref, bg_ref,
                  o_ref):
    h = h_ref[...]
    merged = None
    for n, (y_ref, wg_ref) in enumerate(((yf_ref, wg0_ref), (yl_ref, wg1_ref),
                                         (ym_ref, wg2_ref))):
        gate = jax.nn.sigmoid(_dot_nt(h, wg_ref[...]) + bg_ref[n:n + 1, :])
        term = gate * lax.dot_general(y_ref[...], wb_ref[n], (((1,), (0,)), ((), ())),
                                      preferred_element_type=_F32)
        merged = term if merged is None else merged + term
    o_ref[...] = merged.astype(o_ref.dtype)


def _gated_merge(h, y_fox, y_lru, y_mem, w_t, gate_row0, w_branch, b_gate, tm, tn):
    s, d = h.shape
    nj = d // tn
    bw = y_fox.shape[1]
    y_spec = pl.BlockSpec((tm, bw), lambda i, j: (i, 0))
    assert gate_row0 % V7X_SUBLANES == 0

    def gate_spec(n):
        return pl.BlockSpec(
            (pl.Element(tn), pl.Element(d)),
            lambda i, j: (pl.multiple_of(gate_row0 + n * d + j * tn, V7X_SUBLANES), 0))

    return pl.pallas_call(
        _merge_kernel,
        out_shape=jax.ShapeDtypeStruct((s, d), _BF16),
        grid=(s // tm, nj),
        in_specs=[pl.BlockSpec((tm, d), lambda i, j: (i, 0)), y_spec, y_spec, y_spec,
                  gate_spec(0), gate_spec(1), gate_spec(2),
                  pl.BlockSpec((N_BRANCH, bw, tn), lambda i, j: (0, 0, j)),
                  pl.BlockSpec((N_BRANCH, tn), lambda i, j: (0, j))],
        out_specs=pl.BlockSpec((tm, tn), lambda i, j: (i, j)),
        compiler_params=_params(("parallel", "arbitrary"), 56),
        name="gated_merge",
    )(h, y_fox, y_lru, y_mem, w_t, w_t, w_t, w_branch, b_gate)


def _out_norm_kernel(a_ref, w_ref, x_ref, g_ref, x2_ref, h2_ref):
    x2 = x_ref[...] + lax.dot_general(a_ref[...], w_ref[...], (((1,), (0,)), ((), ())),
                                      preferred_element_type=_F32)
    x2_ref[...] = x2
    ms = jnp.mean(x2 * x2, axis=-1, keepdims=True)
    h2_ref[...] = (x2 * lax.rsqrt(ms + EPS) * g_ref[...]).astype(h2_ref.dtype)


def _out_proj_norm(a, w, x, g, tm):
    m, k = a.shape
    d = w.shape[1]
    row = lambda i: (i, 0)
    return pl.pallas_call(
        _out_norm_kernel,
        out_shape=(jax.ShapeDtypeStruct((m, d), _F32), jax.ShapeDtypeStruct((m, d), _BF16)),
        grid=(m // tm,),
        in_specs=[pl.BlockSpec((tm, k), row),
                  pl.BlockSpec((k, d), lambda i: (0, 0), pipeline_mode=pl.Buffered(1)),
                  pl.BlockSpec((tm, d), row), pl.BlockSpec((1, d), lambda i: (0, 0))],
        out_specs=(pl.BlockSpec((tm, d), row), pl.BlockSpec((tm, d), row)),
        compiler_params=_params(("parallel",), 52),
        name="proj_out_norm",
    )(a, w, x, g.reshape(1, d))


def _ffn_up_kernel(a_ref, wa_ref, wv_ref, wca_ref, wcv_ref, bca_ref, bcv_ref, o_ref,
                   halo_a_ref, halo_v_ref, wa_bf_ref, wv_bf_ref):
    @pl.when(pl.program_id(1) == 0)
    def _():
        halo_a_ref[...] = jnp.zeros_like(halo_a_ref)
        halo_v_ref[...] = jnp.zeros_like(halo_v_ref)
        wa_bf_ref[...] = wa_ref[...].astype(_BF16)
        wv_bf_ref[...] = wv_ref[...].astype(_BF16)

    a = a_ref[...]
    tm = a.shape[0]

    def conv(up, halo_ref, wc_ref, bc_ref):
        ext = jnp.concatenate([halo_ref[...], up], axis=0)
        halo_ref[...] = up[tm - V7X_SUBLANES:]
        return (wc_ref[2:3, :] * up + wc_ref[1:2, :] * _shift_rows(ext, 1)
                + wc_ref[0:1, :] * _shift_rows(ext, 2) + bc_ref[...])

    act = conv(jnp.dot(a, wa_bf_ref[...], preferred_element_type=_F32), halo_a_ref, wca_ref,
               bca_ref)
    val = conv(jnp.dot(a, wv_bf_ref[...], preferred_element_type=_F32), halo_v_ref, wcv_ref,
               bcv_ref)
    o_ref[...] = (jax.nn.gelu(act) * val).astype(o_ref.dtype)


def _ffn_up(h2, w_up, w_conv, b_conv, tm, tn):
    s, d = h2.shape
    f = FFN_HIDDEN
    nj = f // tn
    return pl.pallas_call(
        _ffn_up_kernel,
        out_shape=jax.ShapeDtypeStruct((s, f), _BF16),
        grid=(nj, s // tm),
        in_specs=[pl.BlockSpec((tm, d), lambda j, i: (i, 0)),
                  pl.BlockSpec((d, tn), lambda j, i: (0, j)),
                  pl.BlockSpec((d, tn), lambda j, i: (0, nj + j)),
                  pl.BlockSpec((FFN_CONV, tn), lambda j, i: (0, j)),
                  pl.BlockSpec((FFN_CONV, tn), lambda j, i: (0, nj + j)),
                  pl.BlockSpec((1, tn), lambda j, i: (0, j)),
                  pl.BlockSpec((1, tn), lambda j, i: (0, nj + j))],
        out_specs=pl.BlockSpec((tm, tn), lambda j, i: (i, j)),
        scratch_shapes=[pltpu.VMEM((V7X_SUBLANES, tn), _F32),
                        pltpu.VMEM((V7X_SUBLANES, tn), _F32),
                        pltpu.VMEM((d, tn), _BF16), pltpu.VMEM((d, tn), _BF16)],
        compiler_params=_params(("parallel", "arbitrary"), 56),
        name="ffn_up_conv_geglu",
    )(h2, w_up, w_up, w_conv, w_conv, b_conv.reshape(1, -1), b_conv.reshape(1, -1))


def _layer(x, mem, g_mix, w_in, b_f, g_q_fox, g_k_fox, w_lru_conv, b_lru_conv, w_rg_a, b_rg_a,
           w_rg_x, b_rg_x, lru_lambda, g_mem, w_mem_kv, g_q_mem, g_k_mem, b_gate, w_branch,
           w_out, g_ffn, w_ffn_up, w_ffn_conv, b_ffn_conv, w_ffn_down):
    c_k = 2 * FOX_WIDTH
    c_v = c_k + FOX_WIDTH
    c_f = c_v + FOX_HEADS
    c_l = c_f + 2 * LRU_WIDTH
    c_m = c_l + MEM_WIDTH

    w_in_t = w_in.T

    g_q_scaled = g_q_fox * (LOG2_E * FOX_HEAD_DIM ** -0.5)
    qk_bound = (1.02 * FOX_HEAD_DIM) * jnp.max(jnp.abs(g_q_scaled)) * jnp.max(jnp.abs(g_k_fox))
    b_pad = jnp.pad(b_f.reshape(1, -1), ((0, 0), (0, V7X_LANES - FOX_HEADS)))
    h, kb, qb, b_start, b_end = _norm_forget_bias(x, g_mix, w_in_t, c_v, b_pad, qk_bound,
                                                  FOX_TILE)
    gain_qk = jnp.concatenate([jnp.tile(g_q_scaled, FOX_HEADS),
                               jnp.tile(g_k_fox, FOX_HEADS)]).reshape(1, -1)
    qk = _matmul(h, w_in_t, w_t=True, n=c_k, tm=ROW_TILE, tn=COL_TILE, out_dtype=_BF16,
                 epilogue="gnorm", extra=gain_qk, group=FOX_HEAD_DIM, name="proj_qk")
    v = _matmul(h, w_in_t, w_t=True, n=FOX_WIDTH, w_off=c_k, tm=ROW_TILE, tn=COL_TILE,
                out_dtype=_BF16, name="proj_v")
    y_fox = _fox_attention(qk, kb, qb, b_start, b_end, qk_bound, v, FOX_TILE)

    lxlg = _matmul(h, w_in_t, w_t=True, n=2 * LRU_WIDTH, w_off=c_f, tm=ROW_TILE, tn=COL_TILE,
                   out_dtype=_F32, name="proj_lru")
    w_gate_lru = jnp.concatenate([w_rg_a, w_rg_x], axis=-1).astype(_BF16)
    b_gate_lru = jnp.concatenate([b_rg_a, b_rg_x], axis=-1).reshape(LRU_BLOCKS, 1, -1)
    y_lru = _lru_branch(lxlg, w_lru_conv, b_lru_conv, w_gate_lru, b_gate_lru, lru_lambda,
                        SEQ_TILE)

    gain_mq = (jnp.tile(g_q_mem, MEM_HEADS) * (MEM_HEAD_DIM ** -0.5)).reshape(1, -1)
    mq = _matmul(h, w_in_t, w_t=True, n=MEM_WIDTH, w_off=c_l, tm=ROW_TILE, tn=COL_TILE,
                 out_dtype=_BF16, epilogue="gnorm", extra=gain_mq, group=MEM_HEAD_DIM,
                 name="proj_mq")
    mk, mv = _mem_kv(mem, g_mem, w_mem_kv, g_k_mem)
    y_mem = _mem_attention(mq, mk, mv, ROW_TILE)

    merged = _gated_merge(h, y_fox, y_lru, y_mem, w_in_t, c_m, w_branch, b_gate,
                          ROW_TILE, MERGE_COL_TILE)
    x2, h2 = _out_proj_norm(merged, w_out, x, g_ffn, SEQ_TILE)

    g = _ffn_up(h2, w_ffn_up, w_ffn_conv, b_ffn_conv, ROW_TILE, FFN_COL_TILE)
    return _matmul(g, w_ffn_down.astype(_BF16), n=D_MODEL, tm=SEQ_TILE, tn=D_MODEL,
                   rows_outer=True,
                   out_dtype=_F32, epilogue="residual", extra=x2, name="ffn_down")


def kernel(x, mem, g_mix, w_in, b_f, g_q_fox, g_k_fox, w_lru_conv, b_lru_conv, w_rg_a, b_rg_a,
           w_rg_x, b_rg_x, lru_lambda, g_mem, w_mem_kv, g_q_mem, g_k_mem, b_gate, w_branch,
           w_out, g_ffn, w_ffn_up, w_ffn_conv, b_ffn_conv, w_ffn_down):
    depth = g_mix.shape[0]
    outs = []
    for b in range(x.shape[0]):
        xb = x[b]
        for l in range(depth):
            xb = _layer(xb, mem[b], g_mix[l], w_in[l], b_f[l], g_q_fox[l], g_k_fox[l],
                        w_lru_conv[l], b_lru_conv[l], w_rg_a[l], b_rg_a[l], w_rg_x[l],
                        b_rg_x[l], lru_lambda[l], g_mem[l], w_mem_kv[l], g_q_mem[l],
                        g_k_mem[l], b_gate[l], w_branch[l], w_out[l], g_ffn[l], w_ffn_up[l],
                        w_ffn_conv[l], b_ffn_conv[l], w_ffn_down[l])
        outs.append(xb)
    return outs[0][None] if len(outs) == 1 else jnp.stack(outs)
```

```python
import functools

import jax
import jax.numpy as jnp
import numpy as np
from jax import lax
from jax.experimental import pallas as pl
from jax.experimental.pallas import tpu as pltpu

D_MODEL = 2048
FOX_HEADS = 8
FOX_HEAD_DIM = 128
FOX_WIDTH = FOX_HEADS * FOX_HEAD_DIM
LRU_WIDTH = 1024
LRU_BLOCKS = 8
LRU_BLOCK_DIM = LRU_WIDTH // LRU_BLOCKS
LRU_CONV = 4
LRU_C = 8.0
MEM_HEADS = 4
MEM_HEAD_DIM = 256
MEM_WIDTH = MEM_HEADS * MEM_HEAD_DIM
N_BRANCH = 3
FFN_HIDDEN = 5632
FFN_CONV = 3
EPS = 1e-6

V7X_SUBLANES = 8
V7X_LANES = 128
MASK_VALUE = -1e30
LOG2_E = 1.4426950408889634
ROW_TILE = 1024
COL_TILE = 1024
FFN_COL_TILE = 512
MERGE_COL_TILE = 256
SEQ_TILE = 512
FOX_TILE = 512
FOX_SKIP_BITS = 64.0
FOX_BIAS_LANES = 16
FOX_FAST_MAX_LOGIT = 48.0
LRU_SCAN_UNROLL = 8

_BF16 = jnp.bfloat16
_F32 = jnp.float32


def _params(semantics, vmem_mib):
    return pltpu.CompilerParams(dimension_semantics=semantics,
                                vmem_limit_bytes=vmem_mib * 1024 * 1024)


def _dot_nt(a, w_t):
    return lax.dot_general(a, w_t, (((1,), (1,)), ((), ())), preferred_element_type=_F32)


def _mm_kernel(*refs, epilogue, group, cast_w, w_t):
    a_ref, w_ref = refs[0], refs[1]
    if cast_w:
        o_ref, wb_ref = refs[-2], refs[-1]

        @pl.when(pl.program_id(1) == 0)
        def _():
            wb_ref[...] = w_ref[...].astype(_BF16)

        w = wb_ref[...]
    else:
        o_ref = refs[-1]
        w = w_ref[...]
    if w_t:
        acc = _dot_nt(a_ref[...], w)
    else:
        acc = lax.dot_general(a_ref[...], w, (((1,), (0,)), ((), ())),
                              preferred_element_type=_F32)
    if epilogue == "gnorm":
        g_ref = refs[2]
        for gi in range(acc.shape[1] // group):
            sl = slice(gi * group, (gi + 1) * group)
            blk = acc[:, sl]
            ms = jnp.mean(blk * blk, axis=-1, keepdims=True)
            o_ref[:, sl] = (blk * lax.rsqrt(ms + EPS) * g_ref[:, sl]).astype(o_ref.dtype)
    elif epilogue == "residual":
        o_ref[...] = (refs[2][...] + acc).astype(o_ref.dtype)
    else:
        o_ref[...] = acc.astype(o_ref.dtype)


def _matmul(a, w, *, n, tm, tn, out_dtype, w_t=False, w_off=0, rows_outer=False,
            epilogue="plain", extra=None, group=None, name):
    m, k = a.shape
    cast_w = w.dtype != _BF16 and not rows_outer
    assert n % tn == 0 and m % tm == 0
    if rows_outer:
        grid = (m // tm, n // tn)
        ij = lambda i, j: (i, j)
    else:
        grid = (n // tn, m // tm)
        ij = lambda j, i: (i, j)
    if w_t:
        assert w_off % V7X_SUBLANES == 0
        w_block = (tn, k)
        w_spec = pl.BlockSpec((pl.Element(tn), pl.Element(k)),
                              lambda *g: (pl.multiple_of(w_off + ij(*g)[1] * tn, V7X_SUBLANES), 0))
    else:
        assert w_off % tn == 0
        w_block = (k, tn)
        w_mode = {"pipeline_mode": pl.Buffered(1)} if n == tn else {}
        w_spec = pl.BlockSpec(w_block, lambda *g: (0, w_off // tn + ij(*g)[1]), **w_mode)
    in_specs = [pl.BlockSpec((tm, k), lambda *g: (ij(*g)[0], 0)), w_spec]
    args = [a, w]
    block_bytes = (tm * k * 2 + k * tn * w.dtype.itemsize
                   + tm * tn * jnp.dtype(out_dtype).itemsize)
    if epilogue == "gnorm":
        in_specs.append(pl.BlockSpec((1, tn), lambda *g: (0, ij(*g)[1])))
        args.append(extra)
    elif epilogue == "residual":
        in_specs.append(pl.BlockSpec((tm, tn), lambda *g: ij(*g)))
        args.append(extra)
        block_bytes += tm * tn * extra.dtype.itemsize
    scratch = [pltpu.VMEM(w_block, _BF16)] if cast_w else []
    w_single = (not w_t and n == tn) * k * tn * w.dtype.itemsize
    vmem_mib = -(-(2 * block_bytes - w_single + cast_w * k * tn * 2 + 2 * tm * tn * 4)
                 // 2 ** 20) + 2
    return pl.pallas_call(
        functools.partial(_mm_kernel, epilogue=epilogue, group=group, cast_w=cast_w, w_t=w_t),
        out_shape=jax.ShapeDtypeStruct((m, n), out_dtype),
        grid=grid,
        in_specs=in_specs,
        out_specs=pl.BlockSpec((tm, tn), lambda *g: ij(*g)),
        scratch_shapes=scratch,
        compiler_params=_params(("parallel", "arbitrary"), vmem_mib),
        name=name,
    )(*args)


def _split3(x):
    hi = x.astype(_BF16)
    r1 = x - hi.astype(_F32)
    mid = r1.astype(_BF16)
    lo = (r1 - mid.astype(_F32)).astype(_BF16)
    return hi, mid, lo


def _forget_bias_kernel(shift_ref, x_ref, g_ref, w_ref, b_ref, sel_ref, ones_ref, h_ref, kb_ref,
                        qb_ref, edge_ref, carry_ref):
    @pl.when(pl.program_id(0) == 0)
    def _():
        carry_ref[...] = jnp.zeros_like(carry_ref)

    tc = x_ref.shape[0]
    x = x_ref[...]
    ms = jnp.mean(x * x, axis=-1, keepdims=True)
    h = (x * lax.rsqrt(ms + EPS) * g_ref[...]).astype(_BF16)
    h_ref[...] = h
    w = jnp.concatenate([w_ref[...], jnp.zeros((V7X_LANES - FOX_HEADS, w_ref.shape[1]), _F32)],
                        axis=0).astype(_BF16)
    z = _dot_nt(h, w) + b_ref[...]
    neg_log_f = (jnp.log1p(jnp.exp(-jnp.abs(z))) - jnp.minimum(z, 0.0)) * LOG2_E
    chunk = 2 * V7X_LANES
    row = lax.broadcasted_iota(jnp.int32, (chunk, chunk), 0)
    col = lax.broadcasted_iota(jnp.int32, (chunk, chunk), 1)
    tri = jnp.where(col <= row, 1.0, 0.0).astype(_BF16)
    total = carry_ref[0:1, :]
    pieces = []
    for r0 in range(0, tc, chunk):
        c_chunk = total
        for part in _split3(neg_log_f[r0:r0 + chunk, :]):
            c_chunk = c_chunk + jnp.dot(tri, part, preferred_element_type=_F32)
        pieces.append(c_chunk)
        total = c_chunk[chunk - 1:chunk, :]
    c = jnp.concatenate(pieces, axis=0)
    carry_ref[...] = jnp.broadcast_to(c[tc - 1:tc, :], carry_ref.shape)
    edge_ref[...] = jnp.concatenate(
        [c[0:1, :], c[tc - 1:tc, :], jnp.zeros((V7X_SUBLANES - 2, c.shape[1]), _F32)], axis=0)
    parts = jnp.concatenate(_split3(c) + _split3(-(c + shift_ref[0])), axis=1)
    routed = jnp.dot(parts, sel_ref[...], preferred_element_type=_F32) + ones_ref[...]
    half = kb_ref.shape[1]
    kb_ref[...] = routed[:, :half].astype(kb_ref.dtype)
    qb_ref[...] = routed[:, half:].astype(qb_ref.dtype)


def _norm_forget_bias(x, g, w_t, w_row0, b_pad, shift, tc):
    s, d = x.shape
    lanes = V7X_LANES
    width = lanes
    nt = s // tc
    sel = np.zeros((6 * lanes, 2 * width), np.float32)
    ones = np.zeros((1, 2 * width), np.float32)
    for head in range(FOX_HEADS):
        for part in range(3):
            sel[part * lanes + head, FOX_BIAS_LANES * head + part] = 1.0
            sel[(3 + part) * lanes + head, width + FOX_BIAS_LANES * head + 3 + part] = 1.0
            ones[0, FOX_BIAS_LANES * head + 3 + part] = 1.0
            ones[0, width + FOX_BIAS_LANES * head + part] = 1.0
    h, kb, qb, edges = pl.pallas_call(
        _forget_bias_kernel,
        out_shape=(jax.ShapeDtypeStruct((s, d), _BF16),
                   jax.ShapeDtypeStruct((s, width), _BF16),
                   jax.ShapeDtypeStruct((s, width), _BF16),
                   jax.ShapeDtypeStruct((nt * V7X_SUBLANES, lanes), _F32)),
        grid=(nt,),
        in_specs=[pl.BlockSpec(memory_space=pltpu.SMEM),
                  pl.BlockSpec((tc, d), lambda i: (i, 0)),
                  pl.BlockSpec((1, d), lambda i: (0, 0)),
                  pl.BlockSpec((pl.Element(FOX_HEADS), pl.Element(d)), lambda i: (w_row0, 0)),
                  pl.BlockSpec((1, lanes), lambda i: (0, 0)),
                  pl.BlockSpec((6 * lanes, 2 * width), lambda i: (0, 0)),
                  pl.BlockSpec((1, 2 * width), lambda i: (0, 0))],
        out_specs=(pl.BlockSpec((tc, d), lambda i: (i, 0)),
                   pl.BlockSpec((tc, width), lambda i: (i, 0)),
                   pl.BlockSpec((tc, width), lambda i: (i, 0)),
                   pl.BlockSpec((V7X_SUBLANES, lanes), lambda i: (i, 0))),
        scratch_shapes=[pltpu.VMEM((V7X_SUBLANES, lanes), _F32)],
        compiler_params=_params(("arbitrary",), 40),
        name="norm_forget_bias",
    )(shift.reshape(1).astype(_F32), x, g.reshape(1, d), w_t, b_pad, jnp.asarray(sel, _BF16),
      jnp.asarray(ones))
    edges = edges.reshape(nt, V7X_SUBLANES, lanes)
    return h, kb, qb, edges[:, 0, :FOX_HEADS], edges[:, 1, :FOX_HEADS]


def _head_lanes(shape, head, width):
    lane = lax.broadcasted_iota(jnp.int32, shape, 1)
    first = head * FOX_BIAS_LANES
    return jnp.logical_and(lane >= first, lane < first + width)


def _own_bias_columns(packed, head):
    return jnp.where(_head_lanes(packed.shape, head, FOX_BIAS_LANES), packed,
                     jnp.zeros_like(packed))


def _fox_kernel(bstart_ref, bend_ref, thr_ref, q_ref, k_ref, kbp_ref, v_ref, o_ref,
                acc_ref, kb_ref, *, tq):
    head = pl.program_id(0)
    qi = pl.program_id(1)
    b_tile = bstart_ref[qi, head]
    thr = thr_ref[0]
    j0 = lax.fori_loop(
        0, qi, lambda j, n: n + jnp.where(b_tile - bend_ref[j, head] >= thr, 1, 0), 0)

    @pl.when(qi == 0)
    def _():
        kb_ref[...] = _own_bias_columns(kbp_ref[...], head)

    ones3 = jnp.where(_head_lanes((tq, V7X_LANES), head, 3), 1.0, 0.0).astype(_BF16)
    q_aug = jnp.concatenate([q_ref[...], ones3], axis=1)
    acc_ref[...] = jnp.zeros_like(acc_ref)

    def logits(j):
        ks = pl.multiple_of(j * tq, tq)
        k_aug = jnp.concatenate([k_ref[pl.ds(ks, tq), :], kb_ref[pl.ds(ks, tq), :]], axis=1)
        return lax.dot_general(k_aug, q_aug, (((1,), (1,)), ((), ())),
                               preferred_element_type=_F32)

    def accumulate(j, s, m_prev, l_prev):
        m_new = jnp.maximum(m_prev, jnp.max(s, axis=0, keepdims=True))
        alpha = jnp.exp2(m_prev - m_new)
        p = jnp.exp2(s - m_new)
        l_new = alpha * l_prev + jnp.sum(p, axis=0, keepdims=True)
        ks = pl.multiple_of(j * tq, tq)
        pv = lax.dot_general(v_ref[pl.ds(ks, tq), :], p.astype(_BF16),
                             (((0,), (0,)), ((), ())), preferred_element_type=_F32)
        acc_ref[...] = alpha * acc_ref[...] + pv
        return m_new, l_new

    def body(j, carry):
        m_prev, l_prev, s = carry
        s_next = logits(j + 1)
        m_new, l_new = accumulate(j, s, m_prev, l_prev)
        return m_new, l_new, s_next

    m0 = jnp.full((1, tq), MASK_VALUE, _F32)
    l0 = jnp.zeros((1, tq), _F32)
    m, l, s = lax.fori_loop(j0, qi, body, (m0, l0, logits(j0)))
    row = lax.broadcasted_iota(jnp.int32, s.shape, 0)
    col = lax.broadcasted_iota(jnp.int32, s.shape, 1)
    m, l = accumulate(qi, jnp.where(row <= col, s, MASK_VALUE), m, l)
    o_ref[...] = (acc_ref[...] / l).T.astype(o_ref.dtype)


def _fox_fast_kernel(bstart_ref, bend_ref, thr_ref, q_ref, qbp_ref, k_ref, kbp_ref, v_ref,
                     o_ref, acc_ref, kb_ref, l_ref, s0_ref, s1_ref, *, tq):
    head = pl.program_id(0)
    qi = pl.program_id(1)
    b_tile = bstart_ref[qi, head]
    thr = thr_ref[0]
    j0 = lax.fori_loop(
        0, qi, lambda j, n: n + jnp.where(b_tile - bend_ref[j, head] >= thr, 1, 0), 0)

    @pl.when(qi == 0)
    def _():
        kb_ref[...] = _own_bias_columns(kbp_ref[...], head)

    q_aug = jnp.concatenate([q_ref[...], qbp_ref[...]], axis=1)
    acc_ref[...] = jnp.zeros_like(acc_ref)

    def logits(j):
        ks = pl.multiple_of(j * tq, tq)
        k_aug = jnp.concatenate([k_ref[pl.ds(ks, tq), :], kb_ref[pl.ds(ks, tq), :]], axis=1)
        return _dot_nt(k_aug, q_aug)

    def accumulate(j, s):
        p = jnp.exp2(s)
        ks = pl.multiple_of(j * tq, tq)
        acc_ref[...] += lax.dot_general(v_ref[pl.ds(ks, tq), :], p.astype(_BF16),
                                        (((0,), (0,)), ((), ())), preferred_element_type=_F32)
        l_ref[...] += jnp.sum(p, axis=0, keepdims=True)

    l_ref[...] = jnp.zeros_like(l_ref)
    row = lax.broadcasted_iota(jnp.int32, s1_ref.shape, 0)
    col = lax.broadcasted_iota(jnp.int32, s1_ref.shape, 1)
    s_diag = jnp.where(row <= col, logits(qi), MASK_VALUE)
    s0_ref[...] = logits(j0)
    accumulate(qi, s_diag)
    n_before = qi - j0

    def pair(t, carry):
        j = j0 + 2 * t
        s1_ref[...] = logits(j + 1)
        accumulate(j, s0_ref[...])
        s0_ref[...] = logits(j + 2)
        accumulate(j + 1, s1_ref[...])
        return carry

    lax.fori_loop(0, n_before // 2, pair, 0)

    @pl.when(n_before % 2 == 1)
    def _():
        accumulate(qi - 1, s0_ref[...])

    o_ref[...] = (acc_ref[...] / l_ref[...]).T.astype(o_ref.dtype)


def _fox_attention(qk, kb, qb, b_start, b_end, qk_bound, v, tq):
    s = v.shape[0]
    hd = FOX_HEAD_DIM
    lanes = V7X_LANES
    thr = (2.0 * qk_bound + FOX_SKIP_BITS).reshape(1).astype(_F32)
    smem = pl.BlockSpec(memory_space=pltpu.SMEM)
    q_spec = pl.BlockSpec((tq, hd), lambda h, i: (i, h))
    k_spec = pl.BlockSpec((s, hd), lambda h, i: (0, FOX_HEADS + h))
    v_spec = pl.BlockSpec((s, hd), lambda h, i: (0, h))
    kb_spec = pl.BlockSpec((s, lanes), lambda h, i: (0, 0))
    qb_spec = pl.BlockSpec((tq, lanes), lambda h, i: (i, 0))

    def call(kernel, in_specs, extra_scratch, *args):
        return pl.pallas_call(
            functools.partial(kernel, tq=tq),
            out_shape=jax.ShapeDtypeStruct((s, FOX_WIDTH), _BF16),
            grid=(FOX_HEADS, s // tq),
            in_specs=[smem, smem, smem] + in_specs,
            out_specs=q_spec,
            scratch_shapes=[pltpu.VMEM((hd, tq), _F32),
                            pltpu.VMEM((s, lanes), _BF16)] + extra_scratch,
            compiler_params=_params(("parallel", "arbitrary"), 32),
            name=kernel.__name__.strip("_"),
        )(b_start, b_end, thr, *args)

    fast_scratch = [pltpu.VMEM((1, tq), _F32), pltpu.VMEM((tq, tq), _F32),
                    pltpu.VMEM((tq, tq), _F32)]
    return lax.cond(
        qk_bound <= FOX_FAST_MAX_LOGIT,
        lambda: call(_fox_fast_kernel, [q_spec, qb_spec, k_spec, kb_spec, v_spec],
                     fast_scratch, qk, qb, qk, kb, v),
        lambda: call(_fox_kernel, [q_spec, k_spec, kb_spec, v_spec], [],
                     qk, qk, kb, v))


def _shift_rows(ext, d):
    return pltpu.roll(ext, d, axis=0)[V7X_SUBLANES:]


def _lru_kernel(lx_ref, lg_ref, wc_ref, bc_ref, wg_ref, bg_ref, lam_ref, o_ref,
                halo_ref, carry_ref, a_ref, h_ref):
    @pl.when(pl.program_id(0) == 0)
    def _():
        halo_ref[...] = jnp.zeros_like(halo_ref)
        carry_ref[...] = jnp.zeros_like(carry_ref)

    lx = lx_ref[...]
    ts = lx.shape[0]
    ext = jnp.concatenate([halo_ref[...], lx], axis=0)
    xr = (wc_ref[3:4, :] * lx + wc_ref[2:3, :] * _shift_rows(ext, 1)
          + wc_ref[1:2, :] * _shift_rows(ext, 2) + wc_ref[0:1, :] * _shift_rows(ext, 3)
          + bc_ref[...])
    halo_ref[...] = lx[ts - V7X_SUBLANES:]

    lam = lam_ref[...]
    log_sig_lam = jnp.minimum(lam, 0.0) - jnp.log1p(jnp.exp(-jnp.abs(lam)))
    bd = LRU_BLOCK_DIM
    for nb in range(LRU_BLOCKS):
        sl = slice(nb * bd, (nb + 1) * bd)
        x_nb = xr[:, sl]
        gates = jax.nn.sigmoid(jnp.dot(x_nb.astype(_BF16), wg_ref[nb],
                                       preferred_element_type=_F32) + bg_ref[nb])
        log_a = LRU_C * gates[:, :bd] * log_sig_lam[:, sl]
        a_ref[:, sl] = jnp.exp(log_a)
        t = jnp.tanh(log_a)
        h_ref[:, sl] = jnp.sqrt(-2.0 * t / (1.0 - t)) * (gates[:, bd:] * x_nb)

    sub = lax.broadcasted_iota(jnp.int32, (V7X_SUBLANES, 1), 0)

    def group(g, carry):
        rows = pl.ds(pl.multiple_of(g * V7X_SUBLANES, V7X_SUBLANES), V7X_SUBLANES)
        a = a_ref[rows, :]
        h = h_ref[rows, :]
        for d in (1, 2, 4):
            valid = sub >= d
            h = h + a * jnp.where(valid, pltpu.roll(h, d, axis=0), 0.0)
            a = a * jnp.where(valid, pltpu.roll(a, d, axis=0), 1.0)
        h = h + a * carry
        h_ref[rows, :] = h
        return jnp.broadcast_to(h[V7X_SUBLANES - 1:, :], h.shape)

    carry_ref[...] = lax.fori_loop(0, ts // V7X_SUBLANES, group, carry_ref[...],
                                   unroll=LRU_SCAN_UNROLL)
    o_ref[...] = (h_ref[...] * jax.nn.gelu(lg_ref[...])).astype(o_ref.dtype)


def _lru_branch(lxlg, w_conv, b_conv, w_gate, b_gate, lam, ts):
    s = lxlg.shape[0]
    w = LRU_WIDTH
    bd = LRU_BLOCK_DIM
    return pl.pallas_call(
        _lru_kernel,
        out_shape=jax.ShapeDtypeStruct((s, w), _BF16),
        grid=(s // ts,),
        in_specs=[pl.BlockSpec((ts, w), lambda i: (i, 0)),
                  pl.BlockSpec((ts, w), lambda i: (i, 1)),
                  pl.BlockSpec((LRU_CONV, w), lambda i: (0, 0)),
                  pl.BlockSpec((1, w), lambda i: (0, 0)),
                  pl.BlockSpec((LRU_BLOCKS, bd, 2 * bd), lambda i: (0, 0, 0)),
                  pl.BlockSpec((LRU_BLOCKS, 1, 2 * bd), lambda i: (0, 0, 0)),
                  pl.BlockSpec((1, w), lambda i: (0, 0))],
        out_specs=pl.BlockSpec((ts, w), lambda i: (i, 0)),
        scratch_shapes=[pltpu.VMEM((V7X_SUBLANES, w), _F32), pltpu.VMEM((V7X_SUBLANES, w), _F32),
                        pltpu.VMEM((ts, w), _F32), pltpu.VMEM((ts, w), _F32)],
        compiler_params=_params(("arbitrary",), 40),
        name="conv_rglru",
    )(lxlg, lxlg, w_conv, b_conv.reshape(1, -1), w_gate, b_gate, lam.reshape(1, -1))


def _mem_kv_kernel(mem_ref, g_ref, w_ref, gk_ref, k_ref, v_ref):
    x = mem_ref[...]
    ms = jnp.mean(x * x, axis=-1, keepdims=True)
    hm = (x * lax.rsqrt(ms + EPS) * g_ref[...]).astype(_BF16)
    acc = jnp.dot(hm, w_ref[...].astype(_BF16), preferred_element_type=_F32)

    @pl.when(pl.program_id(0) == 0)
    def _():
        for gi in range(MEM_HEADS):
            sl = slice(gi * MEM_HEAD_DIM, (gi + 1) * MEM_HEAD_DIM)
            blk = acc[:, sl]
            ms_k = jnp.mean(blk * blk, axis=-1, keepdims=True)
            k_ref[:, sl] = (blk * lax.rsqrt(ms_k + EPS) * gk_ref[:, sl]).astype(k_ref.dtype)

    @pl.when(pl.program_id(0) == 1)
    def _():
        v_ref[...] = acc.astype(v_ref.dtype)


def _mem_kv(mem, g_mem, w_kv, g_k):
    m, d = mem.shape
    w = MEM_WIDTH
    const = lambda j: (0, 0)
    return pl.pallas_call(
        _mem_kv_kernel,
        out_shape=(jax.ShapeDtypeStruct((m, w), _BF16), jax.ShapeDtypeStruct((m, w), _BF16)),
        grid=(2,),
        in_specs=[pl.BlockSpec((m, d), const), pl.BlockSpec((1, d), const),
                  pl.BlockSpec((d, w), lambda j: (0, j)), pl.BlockSpec((1, w), const)],
        out_specs=(pl.BlockSpec((m, w), const), pl.BlockSpec((m, w), const)),
        compiler_params=_params(("arbitrary",), 40),
        name="mem_kv",
    )(mem, g_mem.reshape(1, d), w_kv, jnp.tile(g_k, MEM_HEADS).reshape(1, w))


def _mem_attn_kernel(q_ref, k_ref, v_ref, o_ref):
    s = lax.dot_general(q_ref[...], k_ref[...], (((1,), (1,)), ((), ())),
                        preferred_element_type=_F32)
    m = jnp.max(s, axis=-1, keepdims=True)
    p = jnp.exp(s - m)
    l = jnp.sum(p, axis=-1, keepdims=True)
    acc = jnp.dot(p.astype(_BF16), v_ref[...], preferred_element_type=_F32)
    o_ref[...] = (acc / l).astype(o_ref.dtype)


def _mem_attention(q, k, v, ts):
    s = q.shape[0]
    m = k.shape[0]
    hd = MEM_HEAD_DIM
    return pl.pallas_call(
        _mem_attn_kernel,
        out_shape=jax.ShapeDtypeStruct((s, MEM_WIDTH), _BF16),
        grid=(s // ts, MEM_HEADS),
        in_specs=[pl.BlockSpec((ts, hd), lambda i, h: (i, h)),
                  pl.BlockSpec((m, hd), lambda i, h: (0, h)),
                  pl.BlockSpec((m, hd), lambda i, h: (0, h))],
        out_specs=pl.BlockSpec((ts, hd), lambda i, h: (i, h)),
        compiler_params=_params(("parallel", "arbitrary"), 32),
        name="mem_attention",
    )(q, k, v)


def _merge_kernel(h_ref, yf_ref, yl_ref, ym_ref, wg0_ref, wg1_ref, wg2_ref, wb_ref, bg_ref,
                  o_ref):
    h = h_ref[...]
    merged = None
    for n, (y_ref, wg_ref) in enumerate(((yf_ref, wg0_ref), (yl_ref, wg1_ref),
                                         (ym_ref, wg2_ref))):
        gate = jax.nn.sigmoid(_dot_nt(h, wg_ref[...]) + bg_ref[n:n + 1, :])
        term = gate * lax.dot_general(y_ref[...], wb_ref[n], (((1,), (0,)), ((), ())),
                                      preferred_element_type=_F32)
        merged = term if merged is None else merged + term
    o_ref[...] = merged.astype(o_ref.dtype)


def _gated_merge(h, y_fox, y_lru, y_mem, w_t, gate_row0, w_branch, b_gate, tm, tn):
    s, d = h.shape
    nj = d // tn
    bw = y_fox.shape[1]
    y_spec = pl.BlockSpec((tm, bw), lambda i, j: (i, 0))
    assert gate_row0 % V7X_SUBLANES == 0

    def gate_spec(n):
        return pl.BlockSpec(
            (pl.Element(tn), pl.Element(d)),
            lambda i, j: (pl.multiple_of(gate_row0 + n * d + j * tn, V7X_SUBLANES), 0))

    return pl.pallas_call(
        _merge_kernel,
        out_shape=jax.ShapeDtypeStruct((s, d), _BF16),
        grid=(s // tm, nj),
        in_specs=[pl.BlockSpec((tm, d), lambda i, j: (i, 0)), y_spec, y_spec, y_spec,
                  gate_spec(0), gate_spec(1), gate_spec(2),
                  pl.BlockSpec((N_BRANCH, bw, tn), lambda i, j: (0, 0, j)),
                  pl.BlockSpec((N_BRANCH, tn), lambda i, j: (0, j))],
        out_specs=pl.BlockSpec((tm, tn), lambda i, j: (i, j)),
        compiler_params=_params(("parallel", "arbitrary"), 56),
        name="gated_merge",
    )(h, y_fox, y_lru, y_mem, w_t, w_t, w_t, w_branch, b_gate)


def _out_norm_kernel(a_ref, w_ref, x_ref, g_ref, x2_ref, h2_ref):
    x2 = x_ref[...] + jnp.dot(a_ref[...], w_ref[...], preferred_element_type=_F32)
    x2_ref[...] = x2
    ms = jnp.mean(x2 * x2, axis=-1, keepdims=True)
    h2_ref[...] = (x2 * lax.rsqrt(ms + EPS) * g_ref[...]).astype(h2_ref.dtype)


def _out_proj_norm(a, w, x, g, tm):
    m, k = a.shape
    d = w.shape[1]
    row = lambda i: (i, 0)
    return pl.pallas_call(
        _out_norm_kernel,
        out_shape=(jax.ShapeDtypeStruct((m, d), _F32), jax.ShapeDtypeStruct((m, d), _BF16)),
        grid=(m // tm,),
        in_specs=[pl.BlockSpec((tm, k), row), pl.BlockSpec((k, d), lambda i: (0, 0)),
                  pl.BlockSpec((tm, d), row), pl.BlockSpec((1, d), lambda i: (0, 0))],
        out_specs=(pl.BlockSpec((tm, d), row), pl.BlockSpec((tm, d), row)),
        compiler_params=_params(("parallel",), 48),
        name="proj_out_norm",
    )(a, w, x, g.reshape(1, d))


def _ffn_up_kernel(a_ref, wa_ref, wv_ref, wca_ref, wcv_ref, bca_ref, bcv_ref, o_ref,
                   halo_a_ref, halo_v_ref, wa_bf_ref, wv_bf_ref):
    @pl.when(pl.program_id(1) == 0)
    def _():
        halo_a_ref[...] = jnp.zeros_like(halo_a_ref)
        halo_v_ref[...] = jnp.zeros_like(halo_v_ref)
        wa_bf_ref[...] = wa_ref[...].astype(_BF16)
        wv_bf_ref[...] = wv_ref[...].astype(_BF16)

    a = a_ref[...]
    tm = a.shape[0]

    def conv(up, halo_ref, wc_ref, bc_ref):
        ext = jnp.concatenate([halo_ref[...], up], axis=0)
        halo_ref[...] = up[tm - V7X_SUBLANES:]
        return (wc_ref[2:3, :] * up + wc_ref[1:2, :] * _shift_rows(ext, 1)
                + wc_ref[0:1, :] * _shift_rows(ext, 2) + bc_ref[...])

    act = conv(jnp.dot(a, wa_bf_ref[...], preferred_element_type=_F32), halo_a_ref, wca_ref,
               bca_ref)
    val = conv(jnp.dot(a, wv_bf_ref[...], preferred_element_type=_F32), halo_v_ref, wcv_ref,
               bcv_ref)
    o_ref[...] = (jax.nn.gelu(act) * val).astype(o_ref.dtype)


def _ffn_up(h2, w_up, w_conv, b_conv, tm, tn):
    s, d = h2.shape
    f = FFN_HIDDEN
    nj = f // tn
    return pl.pallas_call(
        _ffn_up_kernel,
        out_shape=jax.ShapeDtypeStruct((s, f), _BF16),
        grid=(nj, s // tm),
        in_specs=[pl.BlockSpec((tm, d), lambda j, i: (i, 0)),
                  pl.BlockSpec((d, tn), lambda j, i: (0, j)),
                  pl.BlockSpec((d, tn), lambda j, i: (0, nj + j)),
                  pl.BlockSpec((FFN_CONV, tn), lambda j, i: (0, j)),
                  pl.BlockSpec((FFN_CONV, tn), lambda j, i: (0, nj + j)),
                  pl.BlockSpec((1, tn), lambda j, i: (0, j)),
                  pl.BlockSpec((1, tn), lambda j, i: (0, nj + j))],
        out_specs=pl.BlockSpec((tm, tn), lambda j, i: (i, j)),
        scratch_shapes=[pltpu.VMEM((V7X_SUBLANES, tn), _F32),
                        pltpu.VMEM((V7X_SUBLANES, tn), _F32),
                        pltpu.VMEM((d, tn), _BF16), pltpu.VMEM((d, tn), _BF16)],
        compiler_params=_params(("parallel", "arbitrary"), 56),
        name="ffn_up_conv_geglu",
    )(h2, w_up, w_up, w_conv, w_conv, b_conv.reshape(1, -1), b_conv.reshape(1, -1))


def _layer(x, mem, g_mix, w_in, b_f, g_q_fox, g_k_fox, w_lru_conv, b_lru_conv, w_rg_a, b_rg_a,
           w_rg_x, b_rg_x, lru_lambda, g_mem, w_mem_kv, g_q_mem, g_k_mem, b_gate, w_branch,
           w_out, g_ffn, w_ffn_up, w_ffn_conv, b_ffn_conv, w_ffn_down):
    c_k = 2 * FOX_WIDTH
    c_v = c_k + FOX_WIDTH
    c_f = c_v + FOX_HEADS
    c_l = c_f + 2 * LRU_WIDTH
    c_m = c_l + MEM_WIDTH

    w_in_t = w_in.T

    g_q_scaled = g_q_fox * (LOG2_E * FOX_HEAD_DIM ** -0.5)
    qk_bound = (1.02 * FOX_HEAD_DIM) * jnp.max(jnp.abs(g_q_scaled)) * jnp.max(jnp.abs(g_k_fox))
    b_pad = jnp.pad(b_f.reshape(1, -1), ((0, 0), (0, V7X_LANES - FOX_HEADS)))
    h, kb, qb, b_start, b_end = _norm_forget_bias(x, g_mix, w_in_t, c_v, b_pad, qk_bound,
                                                  FOX_TILE)
    gain_qk = jnp.concatenate([jnp.tile(g_q_scaled, FOX_HEADS),
                               jnp.tile(g_k_fox, FOX_HEADS)]).reshape(1, -1)
    qk = _matmul(h, w_in_t, w_t=True, n=c_k, tm=ROW_TILE, tn=COL_TILE, out_dtype=_BF16,
                 epilogue="gnorm", extra=gain_qk, group=FOX_HEAD_DIM, name="proj_qk")
    v = _matmul(h, w_in_t, w_t=True, n=FOX_WIDTH, w_off=c_k, tm=ROW_TILE, tn=COL_TILE,
                out_dtype=_BF16, name="proj_v")
    y_fox = _fox_attention(qk, kb, qb, b_start, b_end, qk_bound, v, FOX_TILE)

    lxlg = _matmul(h, w_in_t, w_t=True, n=2 * LRU_WIDTH, w_off=c_f, tm=ROW_TILE, tn=COL_TILE,
                   out_dtype=_F32, name="proj_lru")
    w_gate_lru = jnp.concatenate([w_rg_a, w_rg_x], axis=-1).astype(_BF16)
    b_gate_lru = jnp.concatenate([b_rg_a, b_rg_x], axis=-1).reshape(LRU_BLOCKS, 1, -1)
    y_lru = _lru_branch(lxlg, w_lru_conv, b_lru_conv, w_gate_lru, b_gate_lru, lru_lambda,
                        SEQ_TILE)

    gain_mq = (jnp.tile(g_q_mem, MEM_HEADS) * (MEM_HEAD_DIM ** -0.5)).reshape(1, -1)
    mq = _matmul(h, w_in_t, w_t=True, n=MEM_WIDTH, w_off=c_l, tm=ROW_TILE, tn=COL_TILE,
                 out_dtype=_BF16, epilogue="gnorm", extra=gain_mq, group=MEM_HEAD_DIM,
                 name="proj_mq")
    mk, mv = _mem_kv(mem, g_mem, w_mem_kv, g_k_mem)
    y_mem = _mem_attention(mq, mk, mv, ROW_TILE)

    merged = _gated_merge(h, y_fox, y_lru, y_mem, w_in_t, c_m, w_branch, b_gate,
                          ROW_TILE, MERGE_COL_TILE)
    x2, h2 = _out_proj_norm(merged, w_out.astype(_BF16), x, g_ffn, SEQ_TILE)

    g = _ffn_up(h2, w_ffn_up, w_ffn_conv, b_ffn_conv, ROW_TILE, FFN_COL_TILE)
    return _matmul(g, w_ffn_down.astype(_BF16), n=D_MODEL, tm=SEQ_TILE, tn=D_MODEL,
                   rows_outer=True,
                   out_dtype=_F32, epilogue="residual", extra=x2, name="ffn_down")


def kernel(x, mem, g_mix, w_in, b_f, g_q_fox, g_k_fox, w_lru_conv, b_lru_conv, w_rg_a, b_rg_a,
           w_rg_x, b_rg_x, lru_lambda, g_mem, w_mem_kv, g_q_mem, g_k_mem, b_gate, w_branch,
           w_out, g_ffn, w_ffn_up, w_ffn_conv, b_ffn_conv, w_ffn_down):
    depth = g_mix.shape[0]
    outs = []
    for b in range(x.shape[0]):
        xb = x[b]
        for l in range(depth):
            xb = _layer(xb, mem[b], g_mix[l], w_in[l], b_f[l], g_q_fox[l], g_k_fox[l],
                        w_lru_conv[l], b_lru_conv[l], w_rg_a[l], b_rg_a[l], w_rg_x[l],
                        b_rg_x[l], lru_lambda[l], g_mem[l], w_mem_kv[l], g_q_mem[l],
                        g_k_mem[l], b_gate[l], w_branch[l], w_out[l], g_ffn[l], w_ffn_up[l],
                        w_ffn_conv[l], b_ffn_conv[l], w_ffn_down[l])
        outs.append(xb)
    return outs[0][None] if len(outs) == 1 else jnp.stack(outs)
```

```python
import functools

import jax
import jax.numpy as jnp
import numpy as np
from jax import lax
from jax.experimental import pallas as pl
from jax.experimental.pallas import tpu as pltpu

D_MODEL = 2048
FOX_HEADS = 8
FOX_HEAD_DIM = 128
FOX_WIDTH = FOX_HEADS * FOX_HEAD_DIM
LRU_WIDTH = 1024
LRU_BLOCKS = 8
LRU_BLOCK_DIM = LRU_WIDTH // LRU_BLOCKS
LRU_CONV = 4
LRU_C = 8.0
MEM_HEADS = 4
MEM_HEAD_DIM = 256
MEM_WIDTH = MEM_HEADS * MEM_HEAD_DIM
N_BRANCH = 3
FFN_HIDDEN = 5632
FFN_CONV = 3
EPS = 1e-6

V7X_SUBLANES = 8
V7X_LANES = 128
MASK_VALUE = -1e30
LOG2_E = 1.4426950408889634
ROW_TILE = 1024
COL_TILE = 1024
FFN_COL_TILE = 512
MERGE_COL_TILE = 256
SEQ_TILE = 512
FOX_TILE = 512
FOX_DIAG_STRIPS = 2
FOX_SKIP_BITS = 64.0
FOX_BIAS_LANES = 16
FOX_FAST_MAX_LOGIT = 48.0
LRU_SCAN_UNROLL = 8

_BF16 = jnp.bfloat16
_F32 = jnp.float32


def _params(semantics, vmem_mib):
    return pltpu.CompilerParams(dimension_semantics=semantics,
                                vmem_limit_bytes=vmem_mib * 1024 * 1024)


def _dot_nt(a, w_t):
    return lax.dot_general(a, w_t, (((1,), (1,)), ((), ())), preferred_element_type=_F32)


def _mm_kernel(*refs, epilogue, group, cast_w, w_t):
    a_ref, w_ref = refs[0], refs[1]
    if cast_w:
        o_ref, wb_ref = refs[-2], refs[-1]

        @pl.when(pl.program_id(1) == 0)
        def _():
            wb_ref[...] = w_ref[...].astype(_BF16)

        w = wb_ref[...]
    else:
        o_ref = refs[-1]
        w = w_ref[...]
    if w_t:
        acc = _dot_nt(a_ref[...], w)
    else:
        acc = lax.dot_general(a_ref[...], w, (((1,), (0,)), ((), ())),
                              preferred_element_type=_F32)
    if epilogue == "gnorm":
        g_ref = refs[2]
        for gi in range(acc.shape[1] // group):
            sl = slice(gi * group, (gi + 1) * group)
            blk = acc[:, sl]
            ms = jnp.mean(blk * blk, axis=-1, keepdims=True)
            o_ref[:, sl] = (blk * lax.rsqrt(ms + EPS) * g_ref[:, sl]).astype(o_ref.dtype)
    elif epilogue == "residual":
        o_ref[...] = (refs[2][...] + acc).astype(o_ref.dtype)
    else:
        o_ref[...] = acc.astype(o_ref.dtype)


def _matmul(a, w, *, n, tm, tn, out_dtype, w_t=False, w_off=0, rows_outer=False,
            epilogue="plain", extra=None, group=None, name):
    m, k = a.shape
    cast_w = w.dtype != _BF16 and not rows_outer
    assert n % tn == 0 and m % tm == 0
    if rows_outer:
        grid = (m // tm, n // tn)
        ij = lambda i, j: (i, j)
    else:
        grid = (n // tn, m // tm)
        ij = lambda j, i: (i, j)
    if w_t:
        assert w_off % V7X_SUBLANES == 0
        w_block = (tn, k)
        w_spec = pl.BlockSpec((pl.Element(tn), pl.Element(k)),
                              lambda *g: (pl.multiple_of(w_off + ij(*g)[1] * tn, V7X_SUBLANES), 0))
    else:
        assert w_off % tn == 0
        w_block = (k, tn)
        w_mode = {"pipeline_mode": pl.Buffered(1)} if n == tn else {}
        w_spec = pl.BlockSpec(w_block, lambda *g: (0, w_off // tn + ij(*g)[1]), **w_mode)
    in_specs = [pl.BlockSpec((tm, k), lambda *g: (ij(*g)[0], 0)), w_spec]
    args = [a, w]
    block_bytes = (tm * k * 2 + k * tn * w.dtype.itemsize
                   + tm * tn * jnp.dtype(out_dtype).itemsize)
    if epilogue == "gnorm":
        in_specs.append(pl.BlockSpec((1, tn), lambda *g: (0, ij(*g)[1])))
        args.append(extra)
    elif epilogue == "residual":
        in_specs.append(pl.BlockSpec((tm, tn), lambda *g: ij(*g)))
        args.append(extra)
        block_bytes += tm * tn * extra.dtype.itemsize
    scratch = [pltpu.VMEM(w_block, _BF16)] if cast_w else []
    w_single = (not w_t and n == tn) * k * tn * w.dtype.itemsize
    vmem_mib = -(-(2 * block_bytes - w_single + cast_w * k * tn * 2 + 2 * tm * tn * 4)
                 // 2 ** 20) + 2
    return pl.pallas_call(
        functools.partial(_mm_kernel, epilogue=epilogue, group=group, cast_w=cast_w, w_t=w_t),
        out_shape=jax.ShapeDtypeStruct((m, n), out_dtype),
        grid=grid,
        in_specs=in_specs,
        out_specs=pl.BlockSpec((tm, tn), lambda *g: ij(*g)),
        scratch_shapes=scratch,
        compiler_params=_params(("parallel", "arbitrary"), vmem_mib),
        name=name,
    )(*args)


def _split3(x):
    hi = x.astype(_BF16)
    r1 = x - hi.astype(_F32)
    mid = r1.astype(_BF16)
    lo = (r1 - mid.astype(_F32)).astype(_BF16)
    return hi, mid, lo


def _forget_bias_kernel(shift_ref, x_ref, g_ref, w_ref, b_ref, sel_ref, ones_ref, h_ref, kb_ref,
                        qb_ref, edge_ref, carry_ref):
    @pl.when(pl.program_id(0) == 0)
    def _():
        carry_ref[...] = jnp.zeros_like(carry_ref)

    tc = x_ref.shape[0]
    x = x_ref[...]
    ms = jnp.mean(x * x, axis=-1, keepdims=True)
    h = (x * lax.rsqrt(ms + EPS) * g_ref[...]).astype(_BF16)
    h_ref[...] = h
    w = jnp.concatenate([w_ref[...], jnp.zeros((V7X_LANES - FOX_HEADS, w_ref.shape[1]), _F32)],
                        axis=0).astype(_BF16)
    z = _dot_nt(h, w) + b_ref[...]
    neg_log_f = (jnp.log1p(jnp.exp(-jnp.abs(z))) - jnp.minimum(z, 0.0)) * LOG2_E
    chunk = 2 * V7X_LANES
    row = lax.broadcasted_iota(jnp.int32, (chunk, chunk), 0)
    col = lax.broadcasted_iota(jnp.int32, (chunk, chunk), 1)
    tri = jnp.where(col <= row, 1.0, 0.0).astype(_BF16)
    total = carry_ref[0:1, :]
    pieces = []
    for r0 in range(0, tc, chunk):
        c_chunk = total
        for part in _split3(neg_log_f[r0:r0 + chunk, :]):
            c_chunk = c_chunk + jnp.dot(tri, part, preferred_element_type=_F32)
        pieces.append(c_chunk)
        total = c_chunk[chunk - 1:chunk, :]
    c = jnp.concatenate(pieces, axis=0)
    carry_ref[...] = jnp.broadcast_to(c[tc - 1:tc, :], carry_ref.shape)
    edge_ref[...] = jnp.concatenate(
        [c[0:1, :], c[tc - 1:tc, :], jnp.zeros((V7X_SUBLANES - 2, c.shape[1]), _F32)], axis=0)
    parts = jnp.concatenate(_split3(c) + _split3(-(c + shift_ref[0])), axis=1)
    routed = jnp.dot(parts, sel_ref[...], preferred_element_type=_F32) + ones_ref[...]
    half = kb_ref.shape[1]
    kb_ref[...] = routed[:, :half].astype(kb_ref.dtype)
    qb_ref[...] = routed[:, half:].astype(qb_ref.dtype)


def _norm_forget_bias(x, g, w_t, w_row0, b_pad, shift, tc):
    s, d = x.shape
    lanes = V7X_LANES
    width = lanes
    nt = s // tc
    sel = np.zeros((6 * lanes, 2 * width), np.float32)
    ones = np.zeros((1, 2 * width), np.float32)
    for head in range(FOX_HEADS):
        for part in range(3):
            sel[part * lanes + head, FOX_BIAS_LANES * head + part] = 1.0
            sel[(3 + part) * lanes + head, width + FOX_BIAS_LANES * head + 3 + part] = 1.0
            ones[0, FOX_BIAS_LANES * head + 3 + part] = 1.0
            ones[0, width + FOX_BIAS_LANES * head + part] = 1.0
    h, kb, qb, edges = pl.pallas_call(
        _forget_bias_kernel,
        out_shape=(jax.ShapeDtypeStruct((s, d), _BF16),
                   jax.ShapeDtypeStruct((s, width), _BF16),
                   jax.ShapeDtypeStruct((s, width), _BF16),
                   jax.ShapeDtypeStruct((nt * V7X_SUBLANES, lanes), _F32)),
        grid=(nt,),
        in_specs=[pl.BlockSpec(memory_space=pltpu.SMEM),
                  pl.BlockSpec((tc, d), lambda i: (i, 0)),
                  pl.BlockSpec((1, d), lambda i: (0, 0)),
                  pl.BlockSpec((pl.Element(FOX_HEADS), pl.Element(d)), lambda i: (w_row0, 0)),
                  pl.BlockSpec((1, lanes), lambda i: (0, 0)),
                  pl.BlockSpec((6 * lanes, 2 * width), lambda i: (0, 0)),
                  pl.BlockSpec((1, 2 * width), lambda i: (0, 0))],
        out_specs=(pl.BlockSpec((tc, d), lambda i: (i, 0)),
                   pl.BlockSpec((tc, width), lambda i: (i, 0)),
                   pl.BlockSpec((tc, width), lambda i: (i, 0)),
                   pl.BlockSpec((V7X_SUBLANES, lanes), lambda i: (i, 0))),
        scratch_shapes=[pltpu.VMEM((V7X_SUBLANES, lanes), _F32)],
        compiler_params=_params(("arbitrary",), 40),
        name="norm_forget_bias",
    )(shift.reshape(1).astype(_F32), x, g.reshape(1, d), w_t, b_pad, jnp.asarray(sel, _BF16),
      jnp.asarray(ones))
    edges = edges.reshape(nt, V7X_SUBLANES, lanes)
    return h, kb, qb, edges[:, 0, :FOX_HEADS], edges[:, 1, :FOX_HEADS]


def _head_lanes(shape, head, width):
    lane = lax.broadcasted_iota(jnp.int32, shape, 1)
    first = head * FOX_BIAS_LANES
    return jnp.logical_and(lane >= first, lane < first + width)


def _own_bias_columns(packed, head):
    return jnp.where(_head_lanes(packed.shape, head, FOX_BIAS_LANES), packed,
                     jnp.zeros_like(packed))


def _fox_kernel(bstart_ref, bend_ref, thr_ref, q_ref, k_ref, kbp_ref, v_ref, o_ref,
                acc_ref, kb_ref, *, tq):
    head = pl.program_id(0)
    qi = pl.program_id(1)
    b_tile = bstart_ref[qi, head]
    thr = thr_ref[0]
    j0 = lax.fori_loop(
        0, qi, lambda j, n: n + jnp.where(b_tile - bend_ref[j, head] >= thr, 1, 0), 0)

    @pl.when(qi == 0)
    def _():
        kb_ref[...] = _own_bias_columns(kbp_ref[...], head)

    ones3 = jnp.where(_head_lanes((tq, V7X_LANES), head, 3), 1.0, 0.0).astype(_BF16)
    q_aug = jnp.concatenate([q_ref[...], ones3], axis=1)
    acc_ref[...] = jnp.zeros_like(acc_ref)

    def logits(j):
        ks = pl.multiple_of(j * tq, tq)
        k_aug = jnp.concatenate([k_ref[pl.ds(ks, tq), :], kb_ref[pl.ds(ks, tq), :]], axis=1)
        return lax.dot_general(k_aug, q_aug, (((1,), (1,)), ((), ())),
                               preferred_element_type=_F32)

    def accumulate(j, s, m_prev, l_prev):
        m_new = jnp.maximum(m_prev, jnp.max(s, axis=0, keepdims=True))
        alpha = jnp.exp2(m_prev - m_new)
        p = jnp.exp2(s - m_new)
        l_new = alpha * l_prev + jnp.sum(p, axis=0, keepdims=True)
        ks = pl.multiple_of(j * tq, tq)
        pv = lax.dot_general(v_ref[pl.ds(ks, tq), :], p.astype(_BF16),
                             (((0,), (0,)), ((), ())), preferred_element_type=_F32)
        acc_ref[...] = alpha * acc_ref[...] + pv
        return m_new, l_new

    def body(j, carry):
        m_prev, l_prev, s = carry
        s_next = logits(j + 1)
        m_new, l_new = accumulate(j, s, m_prev, l_prev)
        return m_new, l_new, s_next

    m0 = jnp.full((1, tq), MASK_VALUE, _F32)
    l0 = jnp.zeros((1, tq), _F32)
    m, l, s = lax.fori_loop(j0, qi, body, (m0, l0, logits(j0)))
    row = lax.broadcasted_iota(jnp.int32, s.shape, 0)
    col = lax.broadcasted_iota(jnp.int32, s.shape, 1)
    m, l = accumulate(qi, jnp.where(row <= col, s, MASK_VALUE), m, l)
    o_ref[...] = (acc_ref[...] / l).T.astype(o_ref.dtype)


def _fox_fast_kernel(bstart_ref, bend_ref, thr_ref, q_ref, qbp_ref, k_ref, kbp_ref, v_ref,
                     o_ref, acc_ref, kb_ref, l_ref, s0_ref, s1_ref, *, tq):
    head = pl.program_id(0)
    qi = pl.program_id(1)
    b_tile = bstart_ref[qi, head]
    thr = thr_ref[0]
    j0 = lax.fori_loop(
        0, qi, lambda j, n: n + jnp.where(b_tile - bend_ref[j, head] >= thr, 1, 0), 0)

    @pl.when(qi == 0)
    def _():
        kb_ref[...] = _own_bias_columns(kbp_ref[...], head)

    q_aug = jnp.concatenate([q_ref[...], qbp_ref[...]], axis=1)
    acc_ref[...] = jnp.zeros_like(acc_ref)

    def logits(j):
        ks = pl.multiple_of(j * tq, tq)
        k_aug = jnp.concatenate([k_ref[pl.ds(ks, tq), :], kb_ref[pl.ds(ks, tq), :]], axis=1)
        return _dot_nt(k_aug, q_aug)

    def accumulate(j, s):
        p = jnp.exp2(s)
        ks = pl.multiple_of(j * tq, tq)
        acc_ref[...] += lax.dot_general(v_ref[pl.ds(ks, tq), :], p.astype(_BF16),
                                        (((0,), (0,)), ((), ())), preferred_element_type=_F32)
        l_ref[...] += jnp.sum(p, axis=0, keepdims=True)

    l_ref[...] = jnp.zeros_like(l_ref)
    strip = tq // FOX_DIAG_STRIPS

    def diag_strip(r0):
        ks = pl.multiple_of(qi * tq + r0, strip)
        k_aug = jnp.concatenate([k_ref[pl.ds(ks, strip), :], kb_ref[pl.ds(ks, strip), :]],
                                axis=1)
        s = _dot_nt(k_aug, q_aug[r0:, :])
        row = lax.broadcasted_iota(jnp.int32, s.shape, 0)
        col = lax.broadcasted_iota(jnp.int32, s.shape, 1)
        return ks, jnp.where(row <= col, s, MASK_VALUE)

    def diag_accumulate(ks, s, r0):
        p = jnp.exp2(s)
        acc_ref[:, r0:] += lax.dot_general(v_ref[pl.ds(ks, strip), :], p.astype(_BF16),
                                           (((0,), (0,)), ((), ())),
                                           preferred_element_type=_F32)
        l_ref[:, r0:] += jnp.sum(p, axis=0, keepdims=True)

    strips = [diag_strip(r * strip) for r in range(FOX_DIAG_STRIPS)]
    s0_ref[...] = logits(j0)
    for r, (ks, s) in enumerate(strips):
        diag_accumulate(ks, s, r * strip)
    n_before = qi - j0

    def pair(t, carry):
        j = j0 + 2 * t
        s1_ref[...] = logits(j + 1)
        accumulate(j, s0_ref[...])
        s0_ref[...] = logits(j + 2)
        accumulate(j + 1, s1_ref[...])
        return carry

    lax.fori_loop(0, n_before // 2, pair, 0)

    @pl.when(n_before % 2 == 1)
    def _():
        accumulate(qi - 1, s0_ref[...])

    o_ref[...] = (acc_ref[...] / l_ref[...]).T.astype(o_ref.dtype)


def _fox_attention(qk, kb, qb, b_start, b_end, qk_bound, v, tq):
    s = v.shape[0]
    hd = FOX_HEAD_DIM
    lanes = V7X_LANES
    thr = (2.0 * qk_bound + FOX_SKIP_BITS).reshape(1).astype(_F32)
    smem = pl.BlockSpec(memory_space=pltpu.SMEM)
    q_spec = pl.BlockSpec((tq, hd), lambda h, i: (i, h))
    k_spec = pl.BlockSpec((s, hd), lambda h, i: (0, FOX_HEADS + h))
    v_spec = pl.BlockSpec((s, hd), lambda h, i: (0, h))
    kb_spec = pl.BlockSpec((s, lanes), lambda h, i: (0, 0))
    qb_spec = pl.BlockSpec((tq, lanes), lambda h, i: (i, 0))

    def call(kernel, in_specs, extra_scratch, *args):
        return pl.pallas_call(
            functools.partial(kernel, tq=tq),
            out_shape=jax.ShapeDtypeStruct((s, FOX_WIDTH), _BF16),
            grid=(FOX_HEADS, s // tq),
            in_specs=[smem, smem, smem] + in_specs,
            out_specs=q_spec,
            scratch_shapes=[pltpu.VMEM((hd, tq), _F32),
                            pltpu.VMEM((s, lanes), _BF16)] + extra_scratch,
            compiler_params=_params(("parallel", "arbitrary"), 32),
            name=kernel.__name__.strip("_"),
        )(b_start, b_end, thr, *args)

    fast_scratch = [pltpu.VMEM((1, tq), _F32), pltpu.VMEM((tq, tq), _F32),
                    pltpu.VMEM((tq, tq), _F32)]
    return lax.cond(
        qk_bound <= FOX_FAST_MAX_LOGIT,
        lambda: call(_fox_fast_kernel, [q_spec, qb_spec, k_spec, kb_spec, v_spec],
                     fast_scratch, qk, qb, qk, kb, v),
        lambda: call(_fox_kernel, [q_spec, k_spec, kb_spec, v_spec], [],
                     qk, qk, kb, v))


def _shift_rows(ext, d):
    return pltpu.roll(ext, d, axis=0)[V7X_SUBLANES:]


def _lru_kernel(lx_ref, lg_ref, wc_ref, bc_ref, wg_ref, bg_ref, lam_ref, o_ref,
                halo_ref, carry_ref, a_ref, h_ref):
    @pl.when(pl.program_id(0) == 0)
    def _():
        halo_ref[...] = jnp.zeros_like(halo_ref)
        carry_ref[...] = jnp.zeros_like(carry_ref)

    lx = lx_ref[...]
    ts = lx.shape[0]
    ext = jnp.concatenate([halo_ref[...], lx], axis=0)
    xr = (wc_ref[3:4, :] * lx + wc_ref[2:3, :] * _shift_rows(ext, 1)
          + wc_ref[1:2, :] * _shift_rows(ext, 2) + wc_ref[0:1, :] * _shift_rows(ext, 3)
          + bc_ref[...])
    halo_ref[...] = lx[ts - V7X_SUBLANES:]

    lam = lam_ref[...]
    log_sig_lam = jnp.minimum(lam, 0.0) - jnp.log1p(jnp.exp(-jnp.abs(lam)))
    bd = LRU_BLOCK_DIM
    for nb in range(LRU_BLOCKS):
        sl = slice(nb * bd, (nb + 1) * bd)
        x_nb = xr[:, sl]
        gates = jax.nn.sigmoid(jnp.dot(x_nb.astype(_BF16), wg_ref[nb],
                                       preferred_element_type=_F32) + bg_ref[nb])
        log_a = LRU_C * gates[:, :bd] * log_sig_lam[:, sl]
        a_ref[:, sl] = jnp.exp(log_a)
        t = jnp.tanh(log_a)
        h_ref[:, sl] = jnp.sqrt(-2.0 * t / (1.0 - t)) * (gates[:, bd:] * x_nb)

    sub = lax.broadcasted_iota(jnp.int32, (V7X_SUBLANES, 1), 0)

    def group(g, carry):
        rows = pl.ds(pl.multiple_of(g * V7X_SUBLANES, V7X_SUBLANES), V7X_SUBLANES)
        a = a_ref[rows, :]
        h = h_ref[rows, :]
        for d in (1, 2, 4):
            valid = sub >= d
            h = h + a * jnp.where(valid, pltpu.roll(h, d, axis=0), 0.0)
            a = a * jnp.where(valid, pltpu.roll(a, d, axis=0), 1.0)
        h = h + a * carry
        h_ref[rows, :] = h
        return jnp.broadcast_to(h[V7X_SUBLANES - 1:, :], h.shape)

    carry_ref[...] = lax.fori_loop(0, ts // V7X_SUBLANES, group, carry_ref[...],
                                   unroll=LRU_SCAN_UNROLL)
    o_ref[...] = (h_ref[...] * jax.nn.gelu(lg_ref[...])).astype(o_ref.dtype)


def _lru_branch(lxlg, w_conv, b_conv, w_gate, b_gate, lam, ts):
    s = lxlg.shape[0]
    w = LRU_WIDTH
    bd = LRU_BLOCK_DIM
    return pl.pallas_call(
        _lru_kernel,
        out_shape=jax.ShapeDtypeStruct((s, w), _BF16),
        grid=(s // ts,),
        in_specs=[pl.BlockSpec((ts, w), lambda i: (i, 0)),
                  pl.BlockSpec((ts, w), lambda i: (i, 1)),
                  pl.BlockSpec((LRU_CONV, w), lambda i: (0, 0)),
                  pl.BlockSpec((1, w), lambda i: (0, 0)),
                  pl.BlockSpec((LRU_BLOCKS, bd, 2 * bd), lambda i: (0, 0, 0)),
                  pl.BlockSpec((LRU_BLOCKS, 1, 2 * bd), lambda i: (0, 0, 0)),
                  pl.BlockSpec((1, w), lambda i: (0, 0))],
        out_specs=pl.BlockSpec((ts, w), lambda i: (i, 0)),
        scratch_shapes=[pltpu.VMEM((V7X_SUBLANES, w), _F32), pltpu.VMEM((V7X_SUBLANES, w), _F32),
                        pltpu.VMEM((ts, w), _F32), pltpu.VMEM((ts, w), _F32)],
        compiler_params=_params(("arbitrary",), 40),
        name="conv_rglru",
    )(lxlg, lxlg, w_conv, b_conv.reshape(1, -1), w_gate, b_gate, lam.reshape(1, -1))


def _mem_kv_kernel(mem_ref, g_ref, w_ref, gk_ref, k_ref, v_ref):
    x = mem_ref[...]
    ms = jnp.mean(x * x, axis=-1, keepdims=True)
    hm = (x * lax.rsqrt(ms + EPS) * g_ref[...]).astype(_BF16)
    acc = jnp.dot(hm, w_ref[...].astype(_BF16), preferred_element_type=_F32)

    @pl.when(pl.program_id(0) == 0)
    def _():
        for gi in range(MEM_HEADS):
            sl = slice(gi * MEM_HEAD_DIM, (gi + 1) * MEM_HEAD_DIM)
            blk = acc[:, sl]
            ms_k = jnp.mean(blk * blk, axis=-1, keepdims=True)
            k_ref[:, sl] = (blk * lax.rsqrt(ms_k + EPS) * gk_ref[:, sl]).astype(k_ref.dtype)

    @pl.when(pl.program_id(0) == 1)
    def _():
        v_ref[...] = acc.astype(v_ref.dtype)


def _mem_kv(mem, g_mem, w_kv, g_k):
    m, d = mem.shape
    w = MEM_WIDTH
    const = lambda j: (0, 0)
    return pl.pallas_call(
        _mem_kv_kernel,
        out_shape=(jax.ShapeDtypeStruct((m, w), _BF16), jax.ShapeDtypeStruct((m, w), _BF16)),
        grid=(2,),
        in_specs=[pl.BlockSpec((m, d), const), pl.BlockSpec((1, d), const),
                  pl.BlockSpec((d, w), lambda j: (0, j)), pl.BlockSpec((1, w), const)],
        out_specs=(pl.BlockSpec((m, w), const), pl.BlockSpec((m, w), const)),
        compiler_params=_params(("arbitrary",), 40),
        name="mem_kv",
    )(mem, g_mem.reshape(1, d), w_kv, jnp.tile(g_k, MEM_HEADS).reshape(1, w))


def _mem_attn_kernel(q_ref, k_ref, v_ref, o_ref):
    s = lax.dot_general(q_ref[...], k_ref[...], (((1,), (1,)), ((), ())),
                        preferred_element_type=_F32)
    m = jnp.max(s, axis=-1, keepdims=True)
    p = jnp.exp(s - m)
    l = jnp.sum(p, axis=-1, keepdims=True)
    acc = jnp.dot(p.astype(_BF16), v_ref[...], preferred_element_type=_F32)
    o_ref[...] = (acc / l).astype(o_ref.dtype)


def _mem_attention(q, k, v, ts):
    s = q.shape[0]
    m = k.shape[0]
    hd = MEM_HEAD_DIM
    return pl.pallas_call(
        _mem_attn_kernel,
        out_shape=jax.ShapeDtypeStruct((s, MEM_WIDTH), _BF16),
        grid=(s // ts, MEM_HEADS),
        in_specs=[pl.BlockSpec((ts, hd), lambda i, h: (i, h)),
                  pl.BlockSpec((m, hd), lambda i, h: (0, h)),
                  pl.BlockSpec((m, hd), lambda i, h: (0, h))],
        out_specs=pl.BlockSpec((ts, hd), lambda i, h: (i, h)),
        compiler_params=_params(("parallel", "arbitrary"), 32),
        name="mem_attention",
    )(q, k, v)


def _merge_kernel(h_ref, yf_ref, yl_ref, ym_ref, wg0_ref, wg1_ref, wg2_ref, wb_ref, bg_ref,
                  o_ref):
    h = h_ref[...]
    merged = None
    for n, (y_ref, wg_ref) in enumerate(((yf_ref, wg0_ref), (yl_ref, wg1_ref),
                                         (ym_ref, wg2_ref))):
        gate = jax.nn.sigmoid(_dot_nt(h, wg_ref[...]) + bg_ref[n:n + 1, :])
        term = gate * lax.dot_general(y_ref[...], wb_ref[n], (((1,), (0,)), ((), ())),
                                      preferred_element_type=_F32)
        merged = term if merged is None else merged + term
    o_ref[...] = merged.astype(o_ref.dtype)


def _gated_merge(h, y_fox, y_lru, y_mem, w_t, gate_row0, w_branch, b_gate, tm, tn):
    s, d = h.shape
    nj = d // tn
    bw = y_fox.shape[1]
    y_spec = pl.BlockSpec((tm, bw), lambda i, j: (i, 0))
    assert gate_row0 % V7X_SUBLANES == 0

    def gate_spec(n):
        return pl.BlockSpec(
            (pl.Element(tn), pl.Element(d)),
            lambda i, j: (pl.multiple_of(gate_row0 + n * d + j * tn, V7X_SUBLANES), 0))

    return pl.pallas_call(
        _merge_kernel,
        out_shape=jax.ShapeDtypeStruct((s, d), _BF16),
        grid=(s // tm, nj),
        in_specs=[pl.BlockSpec((tm, d), lambda i, j: (i, 0)), y_spec, y_spec, y_spec,
                  gate_spec(0), gate_spec(1), gate_spec(2),
                  pl.BlockSpec((N_BRANCH, bw, tn), lambda i, j: (0, 0, j)),
                  pl.BlockSpec((N_BRANCH, tn), lambda i, j: (0, j))],
        out_specs=pl.BlockSpec((tm, tn), lambda i, j: (i, j)),
        compiler_params=_params(("parallel", "arbitrary"), 56),
        name="gated_merge",
    )(h, y_fox, y_lru, y_mem, w_t, w_t, w_t, w_branch, b_gate)


def _out_norm_kernel(a_ref, w_ref, x_ref, g_ref, x2_ref, h2_ref):
    x2 = x_ref[...] + jnp.dot(a_ref[...], w_ref[...], preferred_element_type=_F32)
    x2_ref[...] = x2
    ms = jnp.mean(x2 * x2, axis=-1, keepdims=True)
    h2_ref[...] = (x2 * lax.rsqrt(ms + EPS) * g_ref[...]).astype(h2_ref.dtype)


def _out_proj_norm(a, w, x, g, tm):
    m, k = a.shape
    d = w.shape[1]
    row = lambda i: (i, 0)
    return pl.pallas_call(
        _out_norm_kernel,
        out_shape=(jax.ShapeDtypeStruct((m, d), _F32), jax.ShapeDtypeStruct((m, d), _BF16)),
        grid=(m // tm,),
        in_specs=[pl.BlockSpec((tm, k), row), pl.BlockSpec((k, d), lambda i: (0, 0)),
                  pl.BlockSpec((tm, d), row), pl.BlockSpec((1, d), lambda i: (0, 0))],
        out_specs=(pl.BlockSpec((tm, d), row), pl.BlockSpec((tm, d), row)),
        compiler_params=_params(("parallel",), 48),
        name="proj_out_norm",
    )(a, w, x, g.reshape(1, d))


def _ffn_up_kernel(a_ref, wa_ref, wv_ref, wca_ref, wcv_ref, bca_ref, bcv_ref, o_ref,
                   halo_a_ref, halo_v_ref, wa_bf_ref, wv_bf_ref):
    @pl.when(pl.program_id(1) == 0)
    def _():
        halo_a_ref[...] = jnp.zeros_like(halo_a_ref)
        halo_v_ref[...] = jnp.zeros_like(halo_v_ref)
        wa_bf_ref[...] = wa_ref[...].astype(_BF16)
        wv_bf_ref[...] = wv_ref[...].astype(_BF16)

    a = a_ref[...]
    tm = a.shape[0]

    def conv(up, halo_ref, wc_ref, bc_ref):
        ext = jnp.concatenate([halo_ref[...], up], axis=0)
        halo_ref[...] = up[tm - V7X_SUBLANES:]
        return (wc_ref[2:3, :] * up + wc_ref[1:2, :] * _shift_rows(ext, 1)
                + wc_ref[0:1, :] * _shift_rows(ext, 2) + bc_ref[...])

    act = conv(jnp.dot(a, wa_bf_ref[...], preferred_element_type=_F32), halo_a_ref, wca_ref,
               bca_ref)
    val = conv(jnp.dot(a, wv_bf_ref[...], preferred_element_type=_F32), halo_v_ref, wcv_ref,
               bcv_ref)
    o_ref[...] = (jax.nn.gelu(act) * val).astype(o_ref.dtype)


def _ffn_up(h2, w_up, w_conv, b_conv, tm, tn):
    s, d = h2.shape
    f = FFN_HIDDEN
    nj = f // tn
    return pl.pallas_call(
        _ffn_up_kernel,
        out_shape=jax.ShapeDtypeStruct((s, f), _BF16),
        grid=(nj, s // tm),
        in_specs=[pl.BlockSpec((tm, d), lambda j, i: (i, 0)),
                  pl.BlockSpec((d, tn), lambda j, i: (0, j)),
                  pl.BlockSpec((d, tn), lambda j, i: (0, nj + j)),
                  pl.BlockSpec((FFN_CONV, tn), lambda j, i: (0, j)),
                  pl.BlockSpec((FFN_CONV, tn), lambda j, i: (0, nj + j)),
                  pl.BlockSpec((1, tn), lambda j, i: (0, j)),
                  pl.BlockSpec((1, tn), lambda j, i: (0, nj + j))],
        out_specs=pl.BlockSpec((tm, tn), lambda j, i: (i, j)),
        scratch_shapes=[pltpu.VMEM((V7X_SUBLANES, tn), _F32),
                        pltpu.VMEM((V7X_SUBLANES, tn), _F32),
                        pltpu.VMEM((d, tn), _BF16), pltpu.VMEM((d, tn), _BF16)],
        compiler_params=_params(("parallel", "arbitrary"), 56),
        name="ffn_up_conv_geglu",
    )(h2, w_up, w_up, w_conv, w_conv, b_conv.reshape(1, -1), b_conv.reshape(1, -1))


def _layer(x, mem, g_mix, w_in, b_f, g_q_fox, g_k_fox, w_lru_conv, b_lru_conv, w_rg_a, b_rg_a,
           w_rg_x, b_rg_x, lru_lambda, g_mem, w_mem_kv, g_q_mem, g_k_mem, b_gate, w_branch,
           w_out, g_ffn, w_ffn_up, w_ffn_conv, b_ffn_conv, w_ffn_down):
    c_k = 2 * FOX_WIDTH
    c_v = c_k + FOX_WIDTH
    c_f = c_v + FOX_HEADS
    c_l = c_f + 2 * LRU_WIDTH
    c_m = c_l + MEM_WIDTH

    w_in_t = w_in.T

    g_q_scaled = g_q_fox * (LOG2_E * FOX_HEAD_DIM ** -0.5)
    qk_bound = (1.02 * FOX_HEAD_DIM) * jnp.max(jnp.abs(g_q_scaled)) * jnp.max(jnp.abs(g_k_fox))
    b_pad = jnp.pad(b_f.reshape(1, -1), ((0, 0), (0, V7X_LANES - FOX_HEADS)))
    h, kb, qb, b_start, b_end = _norm_forget_bias(x, g_mix, w_in_t, c_v, b_pad, qk_bound,
                                                  FOX_TILE)
    gain_qk = jnp.concatenate([jnp.tile(g_q_scaled, FOX_HEADS),
                               jnp.tile(g_k_fox, FOX_HEADS)]).reshape(1, -1)
    qk = _matmul(h, w_in_t, w_t=True, n=c_k, tm=ROW_TILE, tn=COL_TILE, out_dtype=_BF16,
                 epilogue="gnorm", extra=gain_qk, group=FOX_HEAD_DIM, name="proj_qk")
    v = _matmul(h, w_in_t, w_t=True, n=FOX_WIDTH, w_off=c_k, tm=ROW_TILE, tn=COL_TILE,
                out_dtype=_BF16, name="proj_v")
    y_fox = _fox_attention(qk, kb, qb, b_start, b_end, qk_bound, v, FOX_TILE)

    lxlg = _matmul(h, w_in_t, w_t=True, n=2 * LRU_WIDTH, w_off=c_f, tm=ROW_TILE, tn=COL_TILE,
                   out_dtype=_F32, name="proj_lru")
    w_gate_lru = jnp.concatenate([w_rg_a, w_rg_x], axis=-1).astype(_BF16)
    b_gate_lru = jnp.concatenate([b_rg_a, b_rg_x], axis=-1).reshape(LRU_BLOCKS, 1, -1)
    y_lru = _lru_branch(lxlg, w_lru_conv, b_lru_conv, w_gate_lru, b_gate_lru, lru_lambda,
                        SEQ_TILE)

    gain_mq = (jnp.tile(g_q_mem, MEM_HEADS) * (MEM_HEAD_DIM ** -0.5)).reshape(1, -1)
    mq = _matmul(h, w_in_t, w_t=True, n=MEM_WIDTH, w_off=c_l, tm=ROW_TILE, tn=COL_TILE,
                 out_dtype=_BF16, epilogue="gnorm", extra=gain_mq, group=MEM_HEAD_DIM,
                 name="proj_mq")
    mk, mv = _mem_kv(mem, g_mem, w_mem_kv, g_k_mem)
    y_mem = _mem_attention(mq, mk, mv, ROW_TILE)

    merged = _gated_merge(h, y_fox, y_lru, y_mem, w_in_t, c_m, w_branch, b_gate,
                          ROW_TILE, MERGE_COL_TILE)
    x2, h2 = _out_proj_norm(merged, w_out.astype(_BF16), x, g_ffn, SEQ_TILE)

    g = _ffn_up(h2, w_ffn_up, w_ffn_conv, b_ffn_conv, ROW_TILE, FFN_COL_TILE)
    return _matmul(g, w_ffn_down.astype(_BF16), n=D_MODEL, tm=SEQ_TILE, tn=D_MODEL,
                   rows_outer=True,
                   out_dtype=_F32, epilogue="residual", extra=x2, name="ffn_down")


def kernel(x, mem, g_mix, w_in, b_f, g_q_fox, g_k_fox, w_lru_conv, b_lru_conv, w_rg_a, b_rg_a,
           w_rg_x, b_rg_x, lru_lambda, g_mem, w_mem_kv, g_q_mem, g_k_mem, b_gate, w_branch,
           w_out, g_ffn, w_ffn_up, w_ffn_conv, b_ffn_conv, w_ffn_down):
    depth = g_mix.shape[0]
    outs = []
    for b in range(x.shape[0]):
        xb = x[b]
        for l in range(depth):
            xb = _layer(xb, mem[b], g_mix[l], w_in[l], b_f[l], g_q_fox[l], g_k_fox[l],
                        w_lru_conv[l], b_lru_conv[l], w_rg_a[l], b_rg_a[l], w_rg_x[l],
                        b_rg_x[l], lru_lambda[l], g_mem[l], w_mem_kv[l], g_q_mem[l],
                        g_k_mem[l], b_gate[l], w_branch[l], w_out[l], g_ffn[l], w_ffn_up[l],
                        w_ffn_conv[l], b_ffn_conv[l], w_ffn_down[l])
        outs.append(xb)
    return outs[0][None] if len(outs) == 1 else jnp.stack(outs)
```

```python
import functools

import jax
import jax.numpy as jnp
import numpy as np
from jax import lax
from jax.experimental import pallas as pl
from jax.experimental.pallas import tpu as pltpu

D_MODEL = 2048
FOX_HEADS = 8
FOX_HEAD_DIM = 128
FOX_WIDTH = FOX_HEADS * FOX_HEAD_DIM
LRU_WIDTH = 1024
LRU_BLOCKS = 8
LRU_BLOCK_DIM = LRU_WIDTH // LRU_BLOCKS
LRU_CONV = 4
LRU_C = 8.0
MEM_HEADS = 4
MEM_HEAD_DIM = 256
MEM_WIDTH = MEM_HEADS * MEM_HEAD_DIM
N_BRANCH = 3
FFN_HIDDEN = 5632
FFN_CONV = 3
EPS = 1e-6

V7X_SUBLANES = 8
V7X_LANES = 128
MASK_VALUE = -1e30
LOG2_E = 1.4426950408889634
ROW_TILE = 1024
COL_TILE = 1024
FFN_COL_TILE = 512
MERGE_COL_TILE = 256
SEQ_TILE = 512
FOX_TILE = 512
FOX_DIAG_STRIPS = 2
FOX_SKIP_BITS = 64.0
FOX_BIAS_LANES = 16
FOX_FAST_MAX_LOGIT = 48.0
LRU_SCAN_UNROLL = 8

_BF16 = jnp.bfloat16
_F32 = jnp.float32


def _params(semantics, vmem_mib):
    return pltpu.CompilerParams(dimension_semantics=semantics,
                                vmem_limit_bytes=vmem_mib * 1024 * 1024)


def _dot_nt(a, w_t):
    return lax.dot_general(a, w_t, (((1,), (1,)), ((), ())), preferred_element_type=_F32)


def _mm_kernel(*refs, epilogue, group, cast_w, w_t):
    a_ref, w_ref = refs[0], refs[1]
    if cast_w:
        o_ref, wb_ref = refs[-2], refs[-1]

        @pl.when(pl.program_id(1) == 0)
        def _():
            wb_ref[...] = w_ref[...].astype(_BF16)

        w = wb_ref[...]
    else:
        o_ref = refs[-1]
        w = w_ref[...]
    if w_t:
        acc = _dot_nt(a_ref[...], w)
    else:
        acc = lax.dot_general(a_ref[...], w, (((1,), (0,)), ((), ())),
                              preferred_element_type=_F32)
    if epilogue == "gnorm":
        g_ref = refs[2]
        for gi in range(acc.shape[1] // group):
            sl = slice(gi * group, (gi + 1) * group)
            blk = acc[:, sl]
            ms = jnp.mean(blk * blk, axis=-1, keepdims=True)
            o_ref[:, sl] = (blk * lax.rsqrt(ms + EPS) * g_ref[:, sl]).astype(o_ref.dtype)
    elif epilogue == "residual":
        o_ref[...] = (refs[2][...] + acc).astype(o_ref.dtype)
    else:
        o_ref[...] = acc.astype(o_ref.dtype)


def _matmul(a, w, *, n, tm, tn, out_dtype, w_t=False, w_off=0, rows_outer=False,
            epilogue="plain", extra=None, group=None, name):
    m, k = a.shape
    cast_w = w.dtype != _BF16 and not rows_outer
    assert n % tn == 0 and m % tm == 0
    if rows_outer:
        grid = (m // tm, n // tn)
        ij = lambda i, j: (i, j)
    else:
        grid = (n // tn, m // tm)
        ij = lambda j, i: (i, j)
    if w_t:
        assert w_off % V7X_SUBLANES == 0
        w_block = (tn, k)
        w_spec = pl.BlockSpec((pl.Element(tn), pl.Element(k)),
                              lambda *g: (pl.multiple_of(w_off + ij(*g)[1] * tn, V7X_SUBLANES), 0))
    else:
        assert w_off % tn == 0
        w_block = (k, tn)
        w_mode = {"pipeline_mode": pl.Buffered(1)} if n == tn else {}
        w_spec = pl.BlockSpec(w_block, lambda *g: (0, w_off // tn + ij(*g)[1]), **w_mode)
    in_specs = [pl.BlockSpec((tm, k), lambda *g: (ij(*g)[0], 0)), w_spec]
    args = [a, w]
    block_bytes = (tm * k * 2 + k * tn * w.dtype.itemsize
                   + tm * tn * jnp.dtype(out_dtype).itemsize)
    if epilogue == "gnorm":
        in_specs.append(pl.BlockSpec((1, tn), lambda *g: (0, ij(*g)[1])))
        args.append(extra)
    elif epilogue == "residual":
        in_specs.append(pl.BlockSpec((tm, tn), lambda *g: ij(*g)))
        args.append(extra)
        block_bytes += tm * tn * extra.dtype.itemsize
    scratch = [pltpu.VMEM(w_block, _BF16)] if cast_w else []
    w_single = (not w_t and n == tn) * k * tn * w.dtype.itemsize
    vmem_mib = -(-(2 * block_bytes - w_single + cast_w * k * tn * 2 + 2 * tm * tn * 4)
                 // 2 ** 20) + 2
    return pl.pallas_call(
        functools.partial(_mm_kernel, epilogue=epilogue, group=group, cast_w=cast_w, w_t=w_t),
        out_shape=jax.ShapeDtypeStruct((m, n), out_dtype),
        grid=grid,
        in_specs=in_specs,
        out_specs=pl.BlockSpec((tm, tn), lambda *g: ij(*g)),
        scratch_shapes=scratch,
        compiler_params=_params(("parallel", "arbitrary"), vmem_mib),
        name=name,
    )(*args)


def _split3(x):
    hi = x.astype(_BF16)
    r1 = x - hi.astype(_F32)
    mid = r1.astype(_BF16)
    lo = (r1 - mid.astype(_F32)).astype(_BF16)
    return hi, mid, lo


def _forget_bias_kernel(shift_ref, x_ref, g_ref, w_ref, b_ref, sel_ref, ones_ref, h_ref, kb_ref,
                        qb_ref, edge_ref, carry_ref):
    @pl.when(pl.program_id(0) == 0)
    def _():
        carry_ref[...] = jnp.zeros_like(carry_ref)

    tc = x_ref.shape[0]
    x = x_ref[...]
    ms = jnp.mean(x * x, axis=-1, keepdims=True)
    h = (x * lax.rsqrt(ms + EPS) * g_ref[...]).astype(_BF16)
    h_ref[...] = h
    w = jnp.concatenate([w_ref[...], jnp.zeros((V7X_LANES - FOX_HEADS, w_ref.shape[1]), _F32)],
                        axis=0).astype(_BF16)
    z = _dot_nt(h, w) + b_ref[...]
    neg_log_f = (jnp.log1p(jnp.exp(-jnp.abs(z))) - jnp.minimum(z, 0.0)) * LOG2_E
    chunk = 2 * V7X_LANES
    row = lax.broadcasted_iota(jnp.int32, (chunk, chunk), 0)
    col = lax.broadcasted_iota(jnp.int32, (chunk, chunk), 1)
    tri = jnp.where(col <= row, 1.0, 0.0).astype(_BF16)
    total = carry_ref[0:1, :]
    pieces = []
    for r0 in range(0, tc, chunk):
        c_chunk = total
        for part in _split3(neg_log_f[r0:r0 + chunk, :]):
            c_chunk = c_chunk + jnp.dot(tri, part, preferred_element_type=_F32)
        pieces.append(c_chunk)
        total = c_chunk[chunk - 1:chunk, :]
    c = jnp.concatenate(pieces, axis=0)
    carry_ref[...] = jnp.broadcast_to(c[tc - 1:tc, :], carry_ref.shape)
    edge_ref[...] = jnp.concatenate(
        [c[0:1, :], c[tc - 1:tc, :], jnp.zeros((V7X_SUBLANES - 2, c.shape[1]), _F32)], axis=0)
    parts = jnp.concatenate(_split3(c) + _split3(-(c + shift_ref[0])), axis=1)
    routed = jnp.dot(parts, sel_ref[...], preferred_element_type=_F32) + ones_ref[...]
    half = kb_ref.shape[1]
    kb_ref[...] = routed[:, :half].astype(kb_ref.dtype)
    qb_ref[...] = routed[:, half:].astype(qb_ref.dtype)


def _norm_forget_bias(x, g, w_t, w_row0, b_pad, shift, tc):
    s, d = x.shape
    lanes = V7X_LANES
    width = lanes
    nt = s // tc
    sel = np.zeros((6 * lanes, 2 * width), np.float32)
    ones = np.zeros((1, 2 * width), np.float32)
    for head in range(FOX_HEADS):
        for part in range(3):
            sel[part * lanes + head, FOX_BIAS_LANES * head + part] = 1.0
            sel[(3 + part) * lanes + head, width + FOX_BIAS_LANES * head + 3 + part] = 1.0
            ones[0, FOX_BIAS_LANES * head + 3 + part] = 1.0
            ones[0, width + FOX_BIAS_LANES * head + part] = 1.0
    h, kb, qb, edges = pl.pallas_call(
        _forget_bias_kernel,
        out_shape=(jax.ShapeDtypeStruct((s, d), _BF16),
                   jax.ShapeDtypeStruct((s, width), _BF16),
                   jax.ShapeDtypeStruct((s, width), _BF16),
                   jax.ShapeDtypeStruct((nt * V7X_SUBLANES, lanes), _F32)),
        grid=(nt,),
        in_specs=[pl.BlockSpec(memory_space=pltpu.SMEM),
                  pl.BlockSpec((tc, d), lambda i: (i, 0)),
                  pl.BlockSpec((1, d), lambda i: (0, 0)),
                  pl.BlockSpec((pl.Element(FOX_HEADS), pl.Element(d)), lambda i: (w_row0, 0)),
                  pl.BlockSpec((1, lanes), lambda i: (0, 0)),
                  pl.BlockSpec((6 * lanes, 2 * width), lambda i: (0, 0)),
                  pl.BlockSpec((1, 2 * width), lambda i: (0, 0))],
        out_specs=(pl.BlockSpec((tc, d), lambda i: (i, 0)),
                   pl.BlockSpec((tc, width), lambda i: (i, 0)),
                   pl.BlockSpec((tc, width), lambda i: (i, 0)),
                   pl.BlockSpec((V7X_SUBLANES, lanes), lambda i: (i, 0))),
        scratch_shapes=[pltpu.VMEM((V7X_SUBLANES, lanes), _F32)],
        compiler_params=_params(("arbitrary",), 40),
        name="norm_forget_bias",
    )(shift.reshape(1).astype(_F32), x, g.reshape(1, d), w_t, b_pad, jnp.asarray(sel, _BF16),
      jnp.asarray(ones))
    edges = edges.reshape(nt, V7X_SUBLANES, lanes)
    return h, kb, qb, edges[:, 0, :FOX_HEADS], edges[:, 1, :FOX_HEADS]


def _head_lanes(shape, head, width):
    lane = lax.broadcasted_iota(jnp.int32, shape, 1)
    first = head * FOX_BIAS_LANES
    return jnp.logical_and(lane >= first, lane < first + width)


def _own_bias_columns(packed, head):
    return jnp.where(_head_lanes(packed.shape, head, FOX_BIAS_LANES), packed,
                     jnp.zeros_like(packed))


def _fox_kernel(bstart_ref, bend_ref, thr_ref, q_ref, k_ref, kbp_ref, v_ref, o_ref,
                acc_ref, kb_ref, *, tq):
    head = pl.program_id(0)
    qi = pl.program_id(1)
    b_tile = bstart_ref[qi, head]
    thr = thr_ref[0]
    j0 = lax.fori_loop(
        0, qi, lambda j, n: n + jnp.where(b_tile - bend_ref[j, head] >= thr, 1, 0), 0)

    @pl.when(qi == 0)
    def _():
        kb_ref[...] = _own_bias_columns(kbp_ref[...], head)

    ones3 = jnp.where(_head_lanes((tq, V7X_LANES), head, 3), 1.0, 0.0).astype(_BF16)
    q_aug = jnp.concatenate([q_ref[...], ones3], axis=1)
    acc_ref[...] = jnp.zeros_like(acc_ref)

    def logits(j):
        ks = pl.multiple_of(j * tq, tq)
        k_aug = jnp.concatenate([k_ref[pl.ds(ks, tq), :], kb_ref[pl.ds(ks, tq), :]], axis=1)
        return lax.dot_general(k_aug, q_aug, (((1,), (1,)), ((), ())),
                               preferred_element_type=_F32)

    def accumulate(j, s, m_prev, l_prev):
        m_new = jnp.maximum(m_prev, jnp.max(s, axis=0, keepdims=True))
        alpha = jnp.exp2(m_prev - m_new)
        p = jnp.exp2(s - m_new)
        l_new = alpha * l_prev + jnp.sum(p, axis=0, keepdims=True)
        ks = pl.multiple_of(j * tq, tq)
        pv = lax.dot_general(v_ref[pl.ds(ks, tq), :], p.astype(_BF16),
                             (((0,), (0,)), ((), ())), preferred_element_type=_F32)
        acc_ref[...] = alpha * acc_ref[...] + pv
        return m_new, l_new

    def body(j, carry):
        m_prev, l_prev, s = carry
        s_next = logits(j + 1)
        m_new, l_new = accumulate(j, s, m_prev, l_prev)
        return m_new, l_new, s_next

    m0 = jnp.full((1, tq), MASK_VALUE, _F32)
    l0 = jnp.zeros((1, tq), _F32)
    m, l, s = lax.fori_loop(j0, qi, body, (m0, l0, logits(j0)))
    row = lax.broadcasted_iota(jnp.int32, s.shape, 0)
    col = lax.broadcasted_iota(jnp.int32, s.shape, 1)
    m, l = accumulate(qi, jnp.where(row <= col, s, MASK_VALUE), m, l)
    o_ref[...] = (acc_ref[...] / l).T.astype(o_ref.dtype)


def _fox_fast_kernel(bstart_ref, bend_ref, thr_ref, q_ref, qbp_ref, k_ref, kbp_ref, v_ref,
                     o_ref, acc_ref, kb_ref, l_ref, s0_ref, s1_ref, *, tq):
    head = pl.program_id(0)
    qi = pl.program_id(1)
    b_tile = bstart_ref[qi, head]
    thr = thr_ref[0]
    j0 = lax.fori_loop(
        0, qi, lambda j, n: n + jnp.where(b_tile - bend_ref[j, head] >= thr, 1, 0), 0)

    @pl.when(qi == 0)
    def _():
        kb_ref[...] = _own_bias_columns(kbp_ref[...], head)

    q_aug = jnp.concatenate([q_ref[...], qbp_ref[...]], axis=1)
    acc_ref[...] = jnp.zeros_like(acc_ref)

    def logits(j):
        ks = pl.multiple_of(j * tq, tq)
        k_aug = jnp.concatenate([k_ref[pl.ds(ks, tq), :], kb_ref[pl.ds(ks, tq), :]], axis=1)
        return _dot_nt(k_aug, q_aug)

    def accumulate(j, s):
        p = jnp.exp2(s)
        ks = pl.multiple_of(j * tq, tq)
        acc_ref[...] += lax.dot_general(v_ref[pl.ds(ks, tq), :], p.astype(_BF16),
                                        (((0,), (0,)), ((), ())), preferred_element_type=_F32)
        l_ref[...] += jnp.sum(p, axis=0, keepdims=True)

    l_ref[...] = jnp.zeros_like(l_ref)
    strip = tq // FOX_DIAG_STRIPS

    def diag_strip(r0):
        ks = pl.multiple_of(qi * tq + r0, strip)
        k_aug = jnp.concatenate([k_ref[pl.ds(ks, strip), :], kb_ref[pl.ds(ks, strip), :]],
                                axis=1)
        s = _dot_nt(k_aug, q_aug[r0:, :])
        row = lax.broadcasted_iota(jnp.int32, s.shape, 0)
        col = lax.broadcasted_iota(jnp.int32, s.shape, 1)
        return ks, jnp.where(row <= col, s, MASK_VALUE)

    def diag_accumulate(ks, s, r0):
        p = jnp.exp2(s)
        acc_ref[:, r0:] += lax.dot_general(v_ref[pl.ds(ks, strip), :], p.astype(_BF16),
                                           (((0,), (0,)), ((), ())),
                                           preferred_element_type=_F32)
        l_ref[:, r0:] += jnp.sum(p, axis=0, keepdims=True)

    strips = [diag_strip(r * strip) for r in range(FOX_DIAG_STRIPS)]
    s0_ref[...] = logits(j0)
    for r, (ks, s) in enumerate(strips):
        diag_accumulate(ks, s, r * strip)
    n_before = qi - j0

    def pair(t, carry):
        j = j0 + 2 * t
        s1_ref[...] = logits(j + 1)
        accumulate(j, s0_ref[...])
        s0_ref[...] = logits(j + 2)
        accumulate(j + 1, s1_ref[...])
        return carry

    lax.fori_loop(0, n_before // 2, pair, 0)

    @pl.when(n_before % 2 == 1)
    def _():
        accumulate(qi - 1, s0_ref[...])

    o_ref[...] = (acc_ref[...] / l_ref[...]).astype(o_ref.dtype).T


def _fox_attention(qk, kb, qb, b_start, b_end, qk_bound, v, tq):
    s = v.shape[0]
    hd = FOX_HEAD_DIM
    lanes = V7X_LANES
    thr = (2.0 * qk_bound + FOX_SKIP_BITS).reshape(1).astype(_F32)
    smem = pl.BlockSpec(memory_space=pltpu.SMEM)
    q_spec = pl.BlockSpec((tq, hd), lambda h, i: (i, h))
    k_spec = pl.BlockSpec((s, hd), lambda h, i: (0, FOX_HEADS + h))
    v_spec = pl.BlockSpec((s, hd), lambda h, i: (0, h))
    kb_spec = pl.BlockSpec((s, lanes), lambda h, i: (0, 0))
    qb_spec = pl.BlockSpec((tq, lanes), lambda h, i: (i, 0))

    def call(kernel, in_specs, extra_scratch, *args):
        return pl.pallas_call(
            functools.partial(kernel, tq=tq),
            out_shape=jax.ShapeDtypeStruct((s, FOX_WIDTH), _BF16),
            grid=(FOX_HEADS, s // tq),
            in_specs=[smem, smem, smem] + in_specs,
            out_specs=q_spec,
            scratch_shapes=[pltpu.VMEM((hd, tq), _F32),
                            pltpu.VMEM((s, lanes), _BF16)] + extra_scratch,
            compiler_params=_params(("parallel", "arbitrary"), 32),
            name=kernel.__name__.strip("_"),
        )(b_start, b_end, thr, *args)

    fast_scratch = [pltpu.VMEM((1, tq), _F32), pltpu.VMEM((tq, tq), _F32),
                    pltpu.VMEM((tq, tq), _F32)]
    return lax.cond(
        qk_bound <= FOX_FAST_MAX_LOGIT,
        lambda: call(_fox_fast_kernel, [q_spec, qb_spec, k_spec, kb_spec, v_spec],
                     fast_scratch, qk, qb, qk, kb, v),
        lambda: call(_fox_kernel, [q_spec, k_spec, kb_spec, v_spec], [],
                     qk, qk, kb, v))


def _shift_rows(ext, d):
    return pltpu.roll(ext, d, axis=0)[V7X_SUBLANES:]


def _lru_kernel(lx_ref, lg_ref, wc_ref, bc_ref, wg_ref, bg_ref, lam_ref, o_ref,
                halo_ref, carry_ref, a_ref, h_ref):
    @pl.when(pl.program_id(0) == 0)
    def _():
        halo_ref[...] = jnp.zeros_like(halo_ref)
        carry_ref[...] = jnp.zeros_like(carry_ref)

    lx = lx_ref[...]
    ts = lx.shape[0]
    ext = jnp.concatenate([halo_ref[...], lx], axis=0)
    xr = (wc_ref[3:4, :] * lx + wc_ref[2:3, :] * _shift_rows(ext, 1)
          + wc_ref[1:2, :] * _shift_rows(ext, 2) + wc_ref[0:1, :] * _shift_rows(ext, 3)
          + bc_ref[...])
    halo_ref[...] = lx[ts - V7X_SUBLANES:]

    lam = lam_ref[...]
    log_sig_lam = jnp.minimum(lam, 0.0) - jnp.log1p(jnp.exp(-jnp.abs(lam)))
    bd = LRU_BLOCK_DIM
    for nb in range(LRU_BLOCKS):
        sl = slice(nb * bd, (nb + 1) * bd)
        x_nb = xr[:, sl]
        z = jnp.dot(x_nb.astype(_BF16), wg_ref[nb], preferred_element_type=_F32) + bg_ref[nb]
        gates = 0.5 * jnp.tanh(0.5 * z) + 0.5
        log_a = LRU_C * gates[:, :bd] * log_sig_lam[:, sl]
        a_ref[:, sl] = jnp.exp(log_a)
        t = jnp.tanh(log_a)
        h_ref[:, sl] = jnp.sqrt(-2.0 * t / (1.0 - t)) * (gates[:, bd:] * x_nb)

    sub = lax.broadcasted_iota(jnp.int32, (V7X_SUBLANES, 1), 0)

    def group(g, carry):
        rows = pl.ds(pl.multiple_of(g * V7X_SUBLANES, V7X_SUBLANES), V7X_SUBLANES)
        a = a_ref[rows, :]
        h = h_ref[rows, :]
        for d in (1, 2, 4):
            valid = sub >= d
            h = h + a * jnp.where(valid, pltpu.roll(h, d, axis=0), 0.0)
            a = a * jnp.where(valid, pltpu.roll(a, d, axis=0), 1.0)
        h = h + a * carry
        h_ref[rows, :] = h
        return jnp.broadcast_to(h[V7X_SUBLANES - 1:, :], h.shape)

    carry_ref[...] = lax.fori_loop(0, ts // V7X_SUBLANES, group, carry_ref[...],
                                   unroll=LRU_SCAN_UNROLL)
    o_ref[...] = (h_ref[...] * jax.nn.gelu(lg_ref[...])).astype(o_ref.dtype)


def _lru_branch(lxlg, w_conv, b_conv, w_gate, b_gate, lam, ts):
    s = lxlg.shape[0]
    w = LRU_WIDTH
    bd = LRU_BLOCK_DIM
    return pl.pallas_call(
        _lru_kernel,
        out_shape=jax.ShapeDtypeStruct((s, w), _BF16),
        grid=(s // ts,),
        in_specs=[pl.BlockSpec((ts, w), lambda i: (i, 0)),
                  pl.BlockSpec((ts, w), lambda i: (i, 1)),
                  pl.BlockSpec((LRU_CONV, w), lambda i: (0, 0)),
                  pl.BlockSpec((1, w), lambda i: (0, 0)),
                  pl.BlockSpec((LRU_BLOCKS, bd, 2 * bd), lambda i: (0, 0, 0)),
                  pl.BlockSpec((LRU_BLOCKS, 1, 2 * bd), lambda i: (0, 0, 0)),
                  pl.BlockSpec((1, w), lambda i: (0, 0))],
        out_specs=pl.BlockSpec((ts, w), lambda i: (i, 0)),
        scratch_shapes=[pltpu.VMEM((V7X_SUBLANES, w), _F32), pltpu.VMEM((V7X_SUBLANES, w), _F32),
                        pltpu.VMEM((ts, w), _F32), pltpu.VMEM((ts, w), _F32)],
        compiler_params=_params(("arbitrary",), 40),
        name="conv_rglru",
    )(lxlg, lxlg, w_conv, b_conv.reshape(1, -1), w_gate, b_gate, lam.reshape(1, -1))


def _mem_kv_kernel(mem_ref, g_ref, w_ref, gk_ref, k_ref, v_ref):
    x = mem_ref[...]
    ms = jnp.mean(x * x, axis=-1, keepdims=True)
    hm = (x * lax.rsqrt(ms + EPS) * g_ref[...]).astype(_BF16)
    acc = jnp.dot(hm, w_ref[...].astype(_BF16), preferred_element_type=_F32)

    @pl.when(pl.program_id(0) == 0)
    def _():
        for gi in range(MEM_HEADS):
            sl = slice(gi * MEM_HEAD_DIM, (gi + 1) * MEM_HEAD_DIM)
            blk = acc[:, sl]
            ms_k = jnp.mean(blk * blk, axis=-1, keepdims=True)
            k_ref[:, sl] = (blk * lax.rsqrt(ms_k + EPS) * gk_ref[:, sl]).astype(k_ref.dtype)

    @pl.when(pl.program_id(0) == 1)
    def _():
        v_ref[...] = acc.astype(v_ref.dtype)


def _mem_kv(mem, g_mem, w_kv, g_k):
    m, d = mem.shape
    w = MEM_WIDTH
    const = lambda j: (0, 0)
    return pl.pallas_call(
        _mem_kv_kernel,
        out_shape=(jax.ShapeDtypeStruct((m, w), _BF16), jax.ShapeDtypeStruct((m, w), _BF16)),
        grid=(2,),
        in_specs=[pl.BlockSpec((m, d), const), pl.BlockSpec((1, d), const),
                  pl.BlockSpec((d, w), lambda j: (0, j)), pl.BlockSpec((1, w), const)],
        out_specs=(pl.BlockSpec((m, w), const), pl.BlockSpec((m, w), const)),
        compiler_params=_params(("arbitrary",), 40),
        name="mem_kv",
    )(mem, g_mem.reshape(1, d), w_kv, jnp.tile(g_k, MEM_HEADS).reshape(1, w))


def _mem_attn_kernel(q_ref, k_ref, v_ref, o_ref):
    s = lax.dot_general(q_ref[...], k_ref[...], (((1,), (1,)), ((), ())),
                        preferred_element_type=_F32)
    m = jnp.max(s, axis=-1, keepdims=True)
    p = jnp.exp(s - m)
    l = jnp.sum(p, axis=-1, keepdims=True)
    acc = jnp.dot(p.astype(_BF16), v_ref[...], preferred_element_type=_F32)
    o_ref[...] = (acc / l).astype(o_ref.dtype)


def _mem_attention(q, k, v, ts):
    s = q.shape[0]
    m = k.shape[0]
    hd = MEM_HEAD_DIM
    return pl.pallas_call(
        _mem_attn_kernel,
        out_shape=jax.ShapeDtypeStruct((s, MEM_WIDTH), _BF16),
        grid=(s // ts, MEM_HEADS),
        in_specs=[pl.BlockSpec((ts, hd), lambda i, h: (i, h)),
                  pl.BlockSpec((m, hd), lambda i, h: (0, h)),
                  pl.BlockSpec((m, hd), lambda i, h: (0, h))],
        out_specs=pl.BlockSpec((ts, hd), lambda i, h: (i, h)),
        compiler_params=_params(("parallel", "arbitrary"), 32),
        name="mem_attention",
    )(q, k, v)


def _merge_kernel(h_ref, yf_ref, yl_ref, ym_ref, wg0_ref, wg1_ref, wg2_ref, wb_ref, bg_ref,
                  o_ref):
    h = h_ref[...]
    merged = None
    for n, (y_ref, wg_ref) in enumerate(((yf_ref, wg0_ref), (yl_ref, wg1_ref),
                                         (ym_ref, wg2_ref))):
        gate = jax.nn.sigmoid(_dot_nt(h, wg_ref[...]) + bg_ref[n:n + 1, :])
        term = gate * lax.dot_general(y_ref[...], wb_ref[n], (((1,), (0,)), ((), ())),
                                      preferred_element_type=_F32)
        merged = term if merged is None else merged + term
    o_ref[...] = merged.astype(o_ref.dtype)


def _gated_merge(h, y_fox, y_lru, y_mem, w_t, gate_row0, w_branch, b_gate, tm, tn):
    s, d = h.shape
    nj = d // tn
    bw = y_fox.shape[1]
    y_spec = pl.BlockSpec((tm, bw), lambda i, j: (i, 0))
    assert gate_row0 % V7X_SUBLANES == 0

    def gate_spec(n):
        return pl.BlockSpec(
            (pl.Element(tn), pl.Element(d)),
            lambda i, j: (pl.multiple_of(gate_row0 + n * d + j * tn, V7X_SUBLANES), 0))

    return pl.pallas_call(
        _merge_kernel,
        out_shape=jax.ShapeDtypeStruct((s, d), _BF16),
        grid=(s // tm, nj),
        in_specs=[pl.BlockSpec((tm, d), lambda i, j: (i, 0)), y_spec, y_spec, y_spec,
                  gate_spec(0), gate_spec(1), gate_spec(2),
                  pl.BlockSpec((N_BRANCH, bw, tn), lambda i, j: (0, 0, j)),
                  pl.BlockSpec((N_BRANCH, tn), lambda i, j: (0, j))],
        out_specs=pl.BlockSpec((tm, tn), lambda i, j: (i, j)),
        compiler_params=_params(("parallel", "arbitrary"), 56),
        name="gated_merge",
    )(h, y_fox, y_lru, y_mem, w_t, w_t, w_t, w_branch, b_gate)


def _out_norm_kernel(a_ref, w_ref, x_ref, g_ref, x2_ref, h2_ref):
    x2 = x_ref[...] + jnp.dot(a_ref[...], w_ref[...], preferred_element_type=_F32)
    x2_ref[...] = x2
    ms = jnp.mean(x2 * x2, axis=-1, keepdims=True)
    h2_ref[...] = (x2 * lax.rsqrt(ms + EPS) * g_ref[...]).astype(h2_ref.dtype)


def _out_proj_norm(a, w, x, g, tm):
    m, k = a.shape
    d = w.shape[1]
    row = lambda i: (i, 0)
    return pl.pallas_call(
        _out_norm_kernel,
        out_shape=(jax.ShapeDtypeStruct((m, d), _F32), jax.ShapeDtypeStruct((m, d), _BF16)),
        grid=(m // tm,),
        in_specs=[pl.BlockSpec((tm, k), row), pl.BlockSpec((k, d), lambda i: (0, 0)),
                  pl.BlockSpec((tm, d), row), pl.BlockSpec((1, d), lambda i: (0, 0))],
        out_specs=(pl.BlockSpec((tm, d), row), pl.BlockSpec((tm, d), row)),
        compiler_params=_params(("parallel",), 48),
        name="proj_out_norm",
    )(a, w, x, g.reshape(1, d))


def _ffn_up_kernel(a_ref, wa_ref, wv_ref, wca_ref, wcv_ref, bca_ref, bcv_ref, o_ref,
                   halo_a_ref, halo_v_ref, wa_bf_ref, wv_bf_ref):
    @pl.when(pl.program_id(1) == 0)
    def _():
        halo_a_ref[...] = jnp.zeros_like(halo_a_ref)
        halo_v_ref[...] = jnp.zeros_like(halo_v_ref)
        wa_bf_ref[...] = wa_ref[...].astype(_BF16)
        wv_bf_ref[...] = wv_ref[...].astype(_BF16)

    a = a_ref[...]
    tm = a.shape[0]

    def conv(up, halo_ref, wc_ref, bc_ref):
        ext = jnp.concatenate([halo_ref[...], up], axis=0)
        halo_ref[...] = up[tm - V7X_SUBLANES:]
        return (wc_ref[2:3, :] * up + wc_ref[1:2, :] * _shift_rows(ext, 1)
                + wc_ref[0:1, :] * _shift_rows(ext, 2) + bc_ref[...])

    act = conv(jnp.dot(a, wa_bf_ref[...], preferred_element_type=_F32), halo_a_ref, wca_ref,
               bca_ref)
    val = conv(jnp.dot(a, wv_bf_ref[...], preferred_element_type=_F32), halo_v_ref, wcv_ref,
               bcv_ref)
    o_ref[...] = (jax.nn.gelu(act) * val).astype(o_ref.dtype)


def _ffn_up(h2, w_up, w_conv, b_conv, tm, tn):
    s, d = h2.shape
    f = FFN_HIDDEN
    nj = f // tn
    return pl.pallas_call(
        _ffn_up_kernel,
        out_shape=jax.ShapeDtypeStruct((s, f), _BF16),
        grid=(nj, s // tm),
        in_specs=[pl.BlockSpec((tm, d), lambda j, i: (i, 0)),
                  pl.BlockSpec((d, tn), lambda j, i: (0, j)),
                  pl.BlockSpec((d, tn), lambda j, i: (0, nj + j)),
                  pl.BlockSpec((FFN_CONV, tn), lambda j, i: (0, j)),
                  pl.BlockSpec((FFN_CONV, tn), lambda j, i: (0, nj + j)),
                  pl.BlockSpec((1, tn), lambda j, i: (0, j)),
                  pl.BlockSpec((1, tn), lambda j, i: (0, nj + j))],
        out_specs=pl.BlockSpec((tm, tn), lambda j, i: (i, j)),
        scratch_shapes=[pltpu.VMEM((V7X_SUBLANES, tn), _F32),
                        pltpu.VMEM((V7X_SUBLANES, tn), _F32),
                        pltpu.VMEM((d, tn), _BF16), pltpu.VMEM((d, tn), _BF16)],
        compiler_params=_params(("parallel", "arbitrary"), 56),
        name="ffn_up_conv_geglu",
    )(h2, w_up, w_up, w_conv, w_conv, b_conv.reshape(1, -1), b_conv.reshape(1, -1))


def _layer(x, mem, g_mix, w_in, b_f, g_q_fox, g_k_fox, w_lru_conv, b_lru_conv, w_rg_a, b_rg_a,
           w_rg_x, b_rg_x, lru_lambda, g_mem, w_mem_kv, g_q_mem, g_k_mem, b_gate, w_branch,
           w_out, g_ffn, w_ffn_up, w_ffn_conv, b_ffn_conv, w_ffn_down):
    c_k = 2 * FOX_WIDTH
    c_v = c_k + FOX_WIDTH
    c_f = c_v + FOX_HEADS
    c_l = c_f + 2 * LRU_WIDTH
    c_m = c_l + MEM_WIDTH

    w_in_t = w_in.T

    g_q_scaled = g_q_fox * (LOG2_E * FOX_HEAD_DIM ** -0.5)
    qk_bound = (1.02 * FOX_HEAD_DIM) * jnp.max(jnp.abs(g_q_scaled)) * jnp.max(jnp.abs(g_k_fox))
    b_pad = jnp.pad(b_f.reshape(1, -1), ((0, 0), (0, V7X_LANES - FOX_HEADS)))
    h, kb, qb, b_start, b_end = _norm_forget_bias(x, g_mix, w_in_t, c_v, b_pad, qk_bound,
                                                  FOX_TILE)
    gain_qk = jnp.concatenate([jnp.tile(g_q_scaled, FOX_HEADS),
                               jnp.tile(g_k_fox, FOX_HEADS)]).reshape(1, -1)
    qk = _matmul(h, w_in_t, w_t=True, n=c_k, tm=ROW_TILE, tn=COL_TILE, out_dtype=_BF16,
                 epilogue="gnorm", extra=gain_qk, group=FOX_HEAD_DIM, name="proj_qk")
    v = _matmul(h, w_in_t, w_t=True, n=FOX_WIDTH, w_off=c_k, tm=ROW_TILE, tn=COL_TILE,
                out_dtype=_BF16, name="proj_v")
    y_fox = _fox_attention(qk, kb, qb, b_start, b_end, qk_bound, v, FOX_TILE)

    lxlg = _matmul(h, w_in_t, w_t=True, n=2 * LRU_WIDTH, w_off=c_f, tm=ROW_TILE, tn=COL_TILE,
                   out_dtype=_F32, name="proj_lru")
    w_gate_lru = jnp.concatenate([w_rg_a, w_rg_x], axis=-1).astype(_BF16)
    b_gate_lru = jnp.concatenate([b_rg_a, b_rg_x], axis=-1).reshape(LRU_BLOCKS, 1, -1)
    y_lru = _lru_branch(lxlg, w_lru_conv, b_lru_conv, w_gate_lru, b_gate_lru, lru_lambda,
                        SEQ_TILE)

    gain_mq = (jnp.tile(g_q_mem, MEM_HEADS) * (MEM_HEAD_DIM ** -0.5)).reshape(1, -1)
    mq = _matmul(h, w_in_t, w_t=True, n=MEM_WIDTH, w_off=c_l, tm=ROW_TILE, tn=COL_TILE,
                 out_dtype=_BF16, epilogue="gnorm", extra=gain_mq, group=MEM_HEAD_DIM,
                 name="proj_mq")
    mk, mv = _mem_kv(mem, g_mem, w_mem_kv, g_k_mem)
    y_mem = _mem_attention(mq, mk, mv, ROW_TILE)

    merged = _gated_merge(h, y_fox, y_lru, y_mem, w_in_t, c_m, w_branch, b_gate,
                          ROW_TILE, MERGE_COL_TILE)
    x2, h2 = _out_proj_norm(merged, w_out.astype(_BF16), x, g_ffn, SEQ_TILE)

    g = _ffn_up(h2, w_ffn_up, w_ffn_conv, b_ffn_conv, ROW_TILE, FFN_COL_TILE)
    return _matmul(g, w_ffn_down.astype(_BF16), n=D_MODEL, tm=SEQ_TILE, tn=D_MODEL,
                   rows_outer=True,
                   out_dtype=_F32, epilogue="residual", extra=x2, name="ffn_down")


def kernel(x, mem, g_mix, w_in, b_f, g_q_fox, g_k_fox, w_lru_conv, b_lru_conv, w_rg_a, b_rg_a,
           w_rg_x, b_rg_x, lru_lambda, g_mem, w_mem_kv, g_q_mem, g_k_mem, b_gate, w_branch,
           w_out, g_ffn, w_ffn_up, w_ffn_conv, b_ffn_conv, w_ffn_down):
    depth = g_mix.shape[0]
    outs = []
    for b in range(x.shape[0]):
        xb = x[b]
        for l in range(depth):
            xb = _layer(xb, mem[b], g_mix[l], w_in[l], b_f[l], g_q_fox[l], g_k_fox[l],
                        w_lru_conv[l], b_lru_conv[l], w_rg_a[l], b_rg_a[l], w_rg_x[l],
                        b_rg_x[l], lru_lambda[l], g_mem[l], w_mem_kv[l], g_q_mem[l],
                        g_k_mem[l], b_gate[l], w_branch[l], w_out[l], g_ffn[l], w_ffn_up[l],
                        w_ffn_conv[l], b_ffn_conv[l], w_ffn_down[l])
        outs.append(xb)
    return outs[0][None] if len(outs) == 1 else jnp.stack(outs)
```
